```python
import jax, jax.numpy as jnp
from jax import lax
import numpy as np

D_MODEL = 4096
BATCH = 8
SEQ = 2048
DEPTH = 1
DEC_BATCH = 4
DEC_SEQ = 2048
PAST_LEN = 128

HEAD_SIZE = 64
N_HEADS = D_MODEL // HEAD_SIZE
DECAY_LORA = max(32, int(round(1.8 * D_MODEL ** 0.5 / 32)) * 32)
ICLR_LORA = max(32, int(round(1.8 * D_MODEL ** 0.5 / 32)) * 32)
GATE_LORA = max(32, int(round(0.6 * D_MODEL ** 0.8 / 32)) * 32)
GN_EPS = 64e-5
POOL_WIDTH = D_MODEL // 2
POOL_WINDOWS = (2, 4, 8, 16)
POOL_GROUPS = len(POOL_WINDOWS)
POOL_GROUP_IN = POOL_WIDTH // POOL_GROUPS
POOL_GROUP_OUT = D_MODEL // POOL_GROUPS
N_MEM = 256
X_HEADS = 4
X_HEAD_DIM = D_MODEL // X_HEADS
FFN_HIDDEN = ((8 * D_MODEL + 3 * 256 - 1) // (3 * 256)) * 256
NORM_EPS = 1e-6
RWKV_COLS = 3 * D_MODEL + DECAY_LORA + ICLR_LORA + GATE_LORA
IN_COLS = RWKV_COLS + POOL_WIDTH + 2 * D_MODEL

kernel_name = 'hybrid_rwkv7_pool_xattn_encoder'


def rms_norm(x, g):
    xf = x.astype(jnp.float32)
    y = xf * lax.rsqrt(jnp.mean(xf * xf, axis=-1, keepdims=True) + NORM_EPS)
    return (y * g.astype(jnp.float32)).astype(x.dtype)


def centred_shift(z, taps):
    zp = jnp.pad(z, ((0, 0), (1, 1), (0, 0)))
    return zp[:, :-2] * taps[0] + zp[:, 1:-1] * taps[1] + zp[:, 2:] * taps[2]


def to_heads(x):
    return x.reshape(x.shape[0], x.shape[1], N_HEADS, HEAD_SIZE)


def wkv7_scan(r, w, k, v, a, b, reverse):
    xs = tuple(jnp.moveaxis(to_heads(t).astype(jnp.float32), 1, 0) for t in (r, w, k, v, a, b))
    batch = r.shape[0]

    def step(S, inp):
        r_t, w_t, k_t, v_t, a_t, b_t = inp
        sa = jnp.einsum('bhvk,bhk->bhv', S, a_t)
        S = S * w_t[:, :, None, :] + sa[..., None] * b_t[:, :, None, :] + v_t[..., None] * k_t[:, :, None, :]
        return S, jnp.einsum('bhvk,bhk->bhv', S, r_t)

    S0 = jnp.zeros((batch, N_HEADS, HEAD_SIZE, HEAD_SIZE), jnp.float32)
    _, ys = lax.scan(step, S0, xs, reverse=reverse)
    return jnp.moveaxis(ys, 0, 1)


def rwkv7_direction(r, k, v, kk, wd, ad, w0, w_up, a0, a_up, k_a, r_k, reverse):
    w = -jax.nn.softplus(-(w0 + jnp.tanh(wd) @ w_up).astype(jnp.float32)) - 0.5
    decay = jnp.exp(-jnp.exp(w))
    a = jax.nn.sigmoid((a0 + ad @ a_up).astype(jnp.float32))
    kd = k.astype(jnp.float32) * (1.0 + (a - 1.0) * k_a)
    y = wkv7_scan(r, decay, kd, v, -kk, kk * a, reverse)
    bonus = jnp.sum(to_heads(r).astype(jnp.float32) * to_heads(kd) * r_k, axis=-1, keepdims=True) \
        * to_heads(v).astype(jnp.float32)
    return y, bonus


def rwkv7_branch(z, lp):
    B, T, _ = z.shape
    i1, i2, i3 = D_MODEL, 2 * D_MODEL, 3 * D_MODEL
    i4 = i3 + DECAY_LORA
    i5 = i4 + ICLR_LORA
    r, k, v = z[..., :i1], z[..., i1:i2], z[..., i2:i3]
    wd, ad, gd = z[..., i3:i4], z[..., i4:i5], z[..., i5:]
    g = jax.nn.sigmoid(gd) @ lp['g_up']
    kk = to_heads(k * lp['k_k']).astype(jnp.float32)
    kk = kk * lax.rsqrt(jnp.maximum(jnp.sum(kk * kk, axis=-1, keepdims=True), 1e-12))
    kk = kk.reshape(B, T, D_MODEL)
    y_f, bo_f = rwkv7_direction(r, k, v, kk, wd, ad, lp['w0_f'], lp['w_up_f'], lp['a0_f'], lp['a_up_f'],
                                lp['k_a'], lp['r_k'], False)
    y_b, bo_b = rwkv7_direction(r, k, v, kk, wd, ad, lp['w0_b'], lp['w_up_b'], lp['a0_b'], lp['a_up_b'],
                                lp['k_a'], lp['r_k'], True)
    y = y_f + y_b
    mu = jnp.mean(y, axis=-1, keepdims=True)
    var = jnp.mean(jnp.square(y - mu), axis=-1, keepdims=True)
    gn_g = lp['ln_x_g'].astype(jnp.float32).reshape(N_HEADS, HEAD_SIZE)
    gn_b = lp['ln_x_b'].astype(jnp.float32).reshape(N_HEADS, HEAD_SIZE)
    y = (y - mu) * lax.rsqrt(var + GN_EPS) * gn_g + gn_b + bo_f + bo_b
    return y.reshape(B, T, D_MODEL).astype(z.dtype) * g


def centred_window_mean(p, win):
    T = p.shape[1]
    c = jnp.pad(jnp.cumsum(p, axis=1), ((0, 0), (1, 0), (0, 0)))
    t = jnp.arange(T)
    lo = jnp.clip(t - win // 2, 0, T)
    hi = jnp.clip(t + win - win // 2, 0, T)
    s = jnp.take(c, hi, axis=1) - jnp.take(c, lo, axis=1)
    return s / (hi - lo).astype(jnp.float32)[None, :, None]


def pool_branch(p, lp):
    B, T, _ = p.shape
    pg = p.reshape(B, T, POOL_GROUPS, POOL_GROUP_IN).astype(jnp.float32)
    d = jnp.stack([centred_window_mean(pg[:, :, gi], win) - pg[:, :, gi]
                   for gi, win in enumerate(POOL_WINDOWS)], axis=2)
    out = jnp.einsum('btgi,gio->btgo', d.astype(p.dtype), lp['pool_w'])
    return out.reshape(B, T, D_MODEL) * lp['pool_scale']


def cross_attention(hn, mn, lp):
    B, T, _ = hn.shape
    M = mn.shape[1]
    q = (hn @ lp['xq']).reshape(B, T, X_HEADS, X_HEAD_DIM)
    k = (mn @ lp['xk']).reshape(B, M, X_HEADS, X_HEAD_DIM)
    v = (mn @ lp['xv']).reshape(B, M, X_HEADS, X_HEAD_DIM)
    s = jnp.einsum('bqhd,bkhd->bhqk', q, k).astype(jnp.float32) * (X_HEAD_DIM ** -0.5)
    probs = jax.nn.softmax(s, axis=-1).astype(v.dtype)
    o = jnp.einsum('bhqk,bkhd->bqhd', probs, v).reshape(B, T, D_MODEL)
    return o @ lp['xo']


def swiglu(hn, lp):
    u = hn @ lp['ffn_w13']
    gate, up = u[..., :FFN_HIDDEN], u[..., FFN_HIDDEN:]
    return (jax.nn.silu(gate) * up) @ lp['ffn_w2']


def encoder_layer(h, mem, lp):
    xn = rms_norm(h, lp['norm_mix_g'])
    z = xn @ lp['w_in']
    p0 = RWKV_COLS
    p1 = p0 + POOL_WIDTH
    p2 = p1 + D_MODEL
    y_a = rwkv7_branch(centred_shift(z[..., :p0], lp['shift_w']), lp)
    y_b = pool_branch(z[..., p0:p1], lp)
    merged = jax.nn.sigmoid(z[..., p1:p2]) * y_a + jax.nn.sigmoid(z[..., p2:]) * y_b
    h = h + merged @ lp['w_out']
    h = h + cross_attention(rms_norm(h, lp['norm_x_g']), rms_norm(mem, lp['norm_mem_g']), lp)
    h = h + swiglu(rms_norm(h, lp['norm_ffn_g']), lp)
    return h


def encoder_trunk(x, mem, params, norm_final_g):
    h = x
    for l in range(DEPTH):
        lp = {name: arr[l] for name, arr in params.items()}
        h = encoder_layer(h, mem, lp)
    return rms_norm(h, norm_final_g)


def setup_inputs(seed: int = 0) -> dict:
    key = jax.random.key(seed)
    ks = iter(jax.random.split(key, 40))
    f32 = jnp.float32
    L, D = DEPTH, D_MODEL

    def nrm(shape, scale):
        return jax.random.normal(next(ks), shape, f32) * scale

    def gain(shape):
        return 1.0 + nrm(shape, 0.02)

    shift_base = jnp.array([0.25, 0.5, 0.25], f32)[None, :, None]
    return {
        'x_prompt': nrm((BATCH, SEQ, D), 1.0),
        'x_sample': nrm((DEC_BATCH, DEC_SEQ, D), 1.0),
        'mem_prompt': nrm((BATCH, N_MEM, D), 1.0),
        'mem_sample': nrm((DEC_BATCH, N_MEM, D), 1.0),
        'norm_mix_g': gain((L, D)),
        'w_in': nrm((L, D, IN_COLS), D ** -0.5),
        'shift_w': shift_base + nrm((L, 3, RWKV_COLS), 0.05),
        'w0_f': jax.random.uniform(next(ks), (L, D), f32, -6.0, -1.0),
        'w_up_f': nrm((L, DECAY_LORA, D), 0.5 * DECAY_LORA ** -0.5),
        'w0_b': jax.random.uniform(next(ks), (L, D), f32, -6.0, -1.0),
        'w_up_b': nrm((L, DECAY_LORA, D), 0.5 * DECAY_LORA ** -0.5),
        'a0_f': nrm((L, D), 0.02),
        'a_up_f': nrm((L, ICLR_LORA, D), ICLR_LORA ** -0.5),
        'a0_b': nrm((L, D), 0.02),
        'a_up_b': nrm((L, ICLR_LORA, D), ICLR_LORA ** -0.5),
        'g_up': nrm((L, GATE_LORA, D), GATE_LORA ** -0.5),
        'k_k': 0.85 + nrm((L, D), 0.02),
        'k_a': gain((L, D)),
        'r_k': nrm((L, N_HEADS, HEAD_SIZE), 0.1),
        'ln_x_g': gain((L, D)),
        'ln_x_b': nrm((L, D), 0.02),
        'pool_w': nrm((L, POOL_GROUPS, POOL_GROUP_IN, POOL_GROUP_OUT), POOL_GROUP_IN ** -0.5),
        'pool_scale': gain((L, D)),
        'w_out': nrm((L, D, D), D ** -0.5),
        'norm_x_g': gain((L, D)),
        'norm_mem_g': gain((L, D)),
        'xq': nrm((L, D, D), D ** -0.5),
        'xk': nrm((L, D, D), D ** -0.5),
        'xv': nrm((L, D, D), D ** -0.5),
        'xo': nrm((L, D, D), D ** -0.5),
        'norm_ffn_g': gain((L, D)),
        'ffn_w13': nrm((L, D, 2 * FFN_HIDDEN), D ** -0.5),
        'ffn_w2': nrm((L, FFN_HIDDEN, D), FFN_HIDDEN ** -0.5),
        'norm_final_g': gain((D,)),
    }


def reference(x_prompt, x_sample, mem_prompt, mem_sample, norm_mix_g, w_in, shift_w,
              w0_f, w_up_f, w0_b, w_up_b, a0_f, a_up_f, a0_b, a_up_b, g_up, k_k, k_a, r_k,
              ln_x_g, ln_x_b, pool_w, pool_scale, w_out, norm_x_g, norm_mem_g, xq, xk, xv, xo,
              norm_ffn_g, ffn_w13, ffn_w2, norm_final_g):
    params = {
        'norm_mix_g': norm_mix_g, 'w_in': w_in, 'shift_w': shift_w,
        'w0_f': w0_f, 'w_up_f': w_up_f, 'w0_b': w0_b, 'w_up_b': w_up_b,
        'a0_f': a0_f, 'a_up_f': a_up_f, 'a0_b': a0_b, 'a_up_b': a_up_b,
        'g_up': g_up, 'k_k': k_k, 'k_a': k_a, 'r_k': r_k, 'ln_x_g': ln_x_g, 'ln_x_b': ln_x_b,
        'pool_w': pool_w, 'pool_scale': pool_scale, 'w_out': w_out,
        'norm_x_g': norm_x_g, 'norm_mem_g': norm_mem_g, 'xq': xq, 'xk': xk, 'xv': xv, 'xo': xo,
        'norm_ffn_g': norm_ffn_g, 'ffn_w13': ffn_w13, 'ffn_w2': ffn_w2,
    }
    y_prompt = encoder_trunk(x_prompt, mem_prompt, params, norm_final_g)
    y_sample = encoder_trunk(x_sample, mem_sample, params, norm_final_g)
    return (y_prompt, y_sample)
```

```python
import functools

import jax
import jax.numpy as jnp
from jax import lax
from jax.experimental import pallas as pl
from jax.experimental.pallas import tpu as pltpu

F32 = jnp.float32
BF16 = jnp.bfloat16

LANES = 128
SUBLANES = 8
VMEM_LIMIT_BYTES = 56 * 1024 * 1024

HEAD_SIZE = 64
HEAD_SHIFT = 6
HEADS_PER_TILE = LANES // HEAD_SIZE
X_HEADS = 4
N_MEM = 256
POOL_WINDOWS = (2, 4, 8, 16)
POOL_PAD = 16
GN_EPS = 64e-5
NORM_EPS = 1e-6
WKV_CHUNK = 64
LORA_W = 128
GATE_LORA_PAD = 512


def _params(*semantics):
    return pltpu.CompilerParams(dimension_semantics=semantics, vmem_limit_bytes=VMEM_LIMIT_BYTES)


def _rmsnorm_kernel(x_ref, g_ref, o_ref):
    x = x_ref[...]
    ms = jnp.mean(x * x, axis=-1, keepdims=True)
    o_ref[...] = (x * lax.rsqrt(ms + NORM_EPS) * g_ref[...]).astype(o_ref.dtype)


def _rmsnorm(x, g, out_dtype, tm=256):
    m, d = x.shape
    return pl.pallas_call(
        _rmsnorm_kernel,
        grid=(m // tm,),
        in_specs=[pl.BlockSpec((tm, d), lambda i: (i, 0)),
                  pl.BlockSpec((1, d), lambda i: (0, 0))],
        out_specs=pl.BlockSpec((tm, d), lambda i: (i, 0)),
        out_shape=jax.ShapeDtypeStruct((m, d), out_dtype),
        compiler_params=_params("parallel"),
        name="rmsnorm",
    )(x, g.reshape(1, d))


def _matmul_kernel(x_ref, w_ref, o_ref):
    o_ref[...] = jnp.dot(x_ref[...], w_ref[...], preferred_element_type=F32).astype(o_ref.dtype)


def _matmul_res_kernel(x_ref, w_ref, r_ref, o_ref):
    acc = jnp.dot(x_ref[...], w_ref[...], preferred_element_type=F32)
    o_ref[...] = (r_ref[...] + acc).astype(o_ref.dtype)


def _matmul(x, w, out_dtype, tm, tn, residual=None, name="matmul"):
    m, k = x.shape
    n = w.shape[1]
    in_specs = [pl.BlockSpec((tm, k), lambda i, j: (i, 0)),
                pl.BlockSpec((k, tn), lambda i, j: (0, j))]
    args = [x, w]
    body = _matmul_kernel
    if residual is not None:
        in_specs.append(pl.BlockSpec((tm, tn), lambda i, j: (i, j)))
        args.append(residual)
        body = _matmul_res_kernel
    return pl.pallas_call(
        body,
        grid=(m // tm, n // tn),
        in_specs=in_specs,
        out_specs=pl.BlockSpec((tm, tn), lambda i, j: (i, j)),
        out_shape=jax.ShapeDtypeStruct((m, n), out_dtype),
        compiler_params=_params("parallel", "arbitrary"),
        name=name,
    )(*args)


def _swiglu_kernel(x_ref, wg_ref, wu_ref, o_ref):
    x = x_ref[...]
    gate = jnp.dot(x, wg_ref[...], preferred_element_type=F32)
    up = jnp.dot(x, wu_ref[...], preferred_element_type=F32)
    o_ref[...] = (gate * jax.nn.sigmoid(gate) * up).astype(o_ref.dtype)


def _swiglu_up(x, w13, hidden, tm, tn):
    m, k = x.shape
    nb = hidden // tn
    return pl.pallas_call(
        _swiglu_kernel,
        grid=(m // tm, nb),
        in_specs=[pl.BlockSpec((tm, k), lambda i, j: (i, 0)),
                  pl.BlockSpec((k, tn), lambda i, j: (0, j)),
                  pl.BlockSpec((k, tn), lambda i, j: (0, j + nb))],
        out_specs=pl.BlockSpec((tm, tn), lambda i, j: (i, j)),
        out_shape=jax.ShapeDtypeStruct((m, hidden), BF16),
        compiler_params=_params("parallel", "arbitrary"),
        name="swiglu_up",
    )(x, w13, w13)


def _head_sum_matrix():
    r = lax.broadcasted_iota(jnp.int32, (LANES, LANES), 0) >> HEAD_SHIFT
    c = lax.broadcasted_iota(jnp.int32, (LANES, LANES), 1) >> HEAD_SHIFT
    return jnp.where(r == c, 1.0, 0.0).astype(BF16)


def _shifted(zc, prev_row, next_row, taps):
    rows = zc.shape[0]
    ridx = lax.broadcasted_iota(jnp.int32, zc.shape, 0)
    zm1 = jnp.where(ridx == 0, prev_row, pltpu.roll(zc, 1, axis=0))
    zp1 = jnp.where(ridx == rows - 1, next_row, pltpu.roll(zc, rows - 1, axis=0))
    return zm1 * taps[0:1, :] + zc * taps[1:2, :] + zp1 * taps[2:3, :]


def _halo_rows(prev_ref, next_ref, cols, has_prev, has_next):
    prev_row = prev_ref[0, SUBLANES - 1:SUBLANES, cols] * has_prev
    next_row = next_ref[0, 0:1, cols] * has_next
    return prev_row, next_row


def _wkv_kernel(z_ref, zp_ref, zn_ref, lo_ref, lop_ref, lon_ref, taps_ref, ltaps_ref,
                wup_ref, aup_ref, w0_ref, a0_ref, kk_ref, ka_ref, rk_ref,
                y_ref, bo_ref,
                state_ref, lw_ref, cum_ref, icl_ref, *, d_model, n_chunks):
    C = WKV_CHUNK
    d = pl.program_id(1)
    c = pl.program_id(2)
    tc = jnp.where(d == 0, c, n_chunks - 1 - c)
    has_prev = jnp.where(tc > 0, 1.0, 0.0).astype(F32)
    has_next = jnp.where(tc < n_chunks - 1, 1.0, 0.0).astype(F32)
    sgn = 1 - 2 * d

    @pl.when(c == 0)
    def _():
        state_ref[...] = jnp.zeros_like(state_ref)

    wd_cols = slice(0, LORA_W)
    ad_cols = slice(LORA_W, 2 * LORA_W)
    p_row, n_row = _halo_rows(lop_ref, lon_ref, wd_cols, has_prev, has_next)
    wd = _shifted(lo_ref[0, :, wd_cols], p_row, n_row, ltaps_ref[:, wd_cols])
    p_row, n_row = _halo_rows(lop_ref, lon_ref, ad_cols, has_prev, has_next)
    ad = _shifted(lo_ref[0, :, ad_cols], p_row, n_row, ltaps_ref[:, ad_cols])
    wl = w0_ref[0] + jnp.dot(jnp.tanh(wd), wup_ref[0], preferred_element_type=F32,
                             precision=lax.Precision.HIGHEST)
    neg = -wl
    softplus = jnp.maximum(neg, 0.0) + jnp.log(1.0 + jnp.exp(-jnp.abs(neg)))
    lw = -jnp.exp(-softplus - 0.5)
    lw_ref[...] = lw
    r64 = lax.broadcasted_iota(jnp.int32, (C, C), 0)
    c64 = lax.broadcasted_iota(jnp.int32, (C, C), 1)
    tri = jnp.where((r64 - c64) * sgn >= 0, 1.0, 0.0).astype(F32)
    cum_ref[...] = jnp.dot(tri, lw, preferred_element_type=F32, precision=lax.Precision.HIGHEST)
    icl_ref[...] = jax.nn.sigmoid(
        a0_ref[0] + jnp.dot(ad, aup_ref[0], preferred_element_type=F32,
                            precision=lax.Precision.HIGHEST))

    row = lax.broadcasted_iota(jnp.int32, (C, LANES), 0)
    col = lax.broadcasted_iota(jnp.int32, (C, LANES), 1)
    colh = col & (HEAD_SIZE - 1)
    order = (row - colh) * sgn
    strict = order > 0
    incl = order >= 0
    eye2 = row == colh
    lane_lo = col < HEAD_SIZE
    esum = _head_sum_matrix()
    rr = lax.broadcasted_iota(jnp.int32, (LANES, LANES), 0) >> HEAD_SHIFT
    cc = lax.broadcasted_iota(jnp.int32, (LANES, LANES), 1) >> HEAD_SHIFT
    blockdiag = rr == cc

    def bd(x):
        zero = jnp.zeros_like(x)
        return jnp.concatenate([jnp.where(lane_lo, x, zero), jnp.where(lane_lo, zero, x)], axis=0)

    def pmul(x, y):
        return jnp.dot(x.astype(BF16), bd(y.astype(BF16)), preferred_element_type=F32)

    nt_dims = (((1,), (1,)), ((), ()))
    tn_dims = (((0,), (0,)), ((), ()))

    def pair_body(p, carry):
        cols = pl.ds(pl.multiple_of(p * LANES, LANES), LANES)
        kcols = pl.ds(pl.multiple_of(d_model + p * LANES, LANES), LANES)
        vcols = pl.ds(pl.multiple_of(2 * d_model + p * LANES, LANES), LANES)

        def shifted_stream(cs):
            p_row = zp_ref[0, SUBLANES - 1:SUBLANES, cs] * has_prev
            n_row = zn_ref[0, 0:1, cs] * has_next
            return _shifted(z_ref[0, :, cs], p_row, n_row, taps_ref[:, cs])

        r = shifted_stream(cols)
        k = shifted_stream(kcols)
        v = shifted_stream(vcols)
        icl = icl_ref[:, cols]
        lwp = lw_ref[:, cols]
        cum = cum_ref[:, cols]
        tot = jnp.sum(lwp, axis=0, keepdims=True)

        q = k * kk_ref[:, cols]
        n2 = jnp.dot((q * q).astype(BF16), esum, preferred_element_type=F32)
        kk = q * lax.rsqrt(jnp.maximum(n2, 1e-12))
        kd = k * (1.0 + (icl - 1.0) * ka_ref[:, cols])
        rkr = r * kd * rk_ref[:, cols]
        bo_ref[0, 0, :, cols] = jnp.dot(rkr.astype(BF16), esum, preferred_element_type=F32) * v

        b = kk * icl
        e_in = jnp.exp(cum)
        e_out = jnp.exp(-cum)
        e_tot = jnp.exp(tot - cum)
        at = (-kk * jnp.exp(cum - lwp)).astype(BF16)
        rt = (r * e_in).astype(BF16)
        bt = (b * e_out).astype(BF16)
        kt = (kd * e_out).astype(BF16)
        bw = (b * e_tot).astype(BF16)
        kw = (kd * e_tot).astype(BF16)
        vb = v.astype(BF16)

        lhs = jnp.concatenate([at, rt], axis=0)
        rhs_t = jnp.concatenate([bd(bt), bd(kt)], axis=0)
        pmat = lax.dot_general(lhs, rhs_t, nt_dims, preferred_element_type=F32)
        lab = jnp.where(strict, pmat[:C, :LANES], 0.0)
        lak = jnp.where(strict, pmat[:C, LANES:], 0.0)
        rb = jnp.where(incl, pmat[C:, :LANES], 0.0)
        rk = jnp.where(incl, pmat[C:, LANES:], 0.0)

        h = state_ref[p]
        hs = lax.dot_general(lhs, h.astype(BF16), nt_dims, preferred_element_type=F32)
        vst = bd(vb)
        rhs_u = hs[:C] + jnp.dot(lak.astype(BF16), vst, preferred_element_type=F32)

        tinv = jnp.where(eye2, 1.0, 0.0) + jnp.where((row >> 1) == (colh >> 1), lab, 0.0)
        s = 2
        while s < C:
            sh = s.bit_length() - 1
            off = jnp.where(((row >> (sh + 1)) == (colh >> (sh + 1))) & ((row >> sh) != (colh >> sh)), lab, 0.0)
            tinv = tinv + pmul(pmul(tinv, off), tinv)
            s *= 2
        u = pmul(tinv, rhs_u)

        ub = u.astype(BF16)
        y = hs[C:] + jnp.dot(jnp.concatenate([rb.astype(BF16), rk.astype(BF16)], axis=1),
                             jnp.concatenate([bd(ub), vst], axis=0), preferred_element_type=F32)
        y_ref[0, 0, :, cols] = y

        upd = lax.dot_general(jnp.concatenate([ub, vb], axis=0), jnp.concatenate([bw, kw], axis=0),
                              tn_dims, preferred_element_type=F32)
        state_ref[p] = h * jnp.exp(tot) + jnp.where(blockdiag, upd, 0.0)
        return carry

    lax.fori_loop(0, d_model // LANES, pair_body, 0)


def _wkv_scan(z_rkv, z_lora, taps_rkv, taps_lora, w_up, a_up, w0, a0, k_k, k_a, r_k):
    bsz, t, d3 = z_rkv.shape
    dm = d3 // 3
    C = WKV_CHUNK
    nc = t // C
    hb = C // SUBLANES
    n_hblk = t // SUBLANES
    lw = z_lora.shape[-1]

    def tchunk(dd, cc):
        return jnp.where(dd == 0, cc, nc - 1 - cc)

    def main_map(bb, dd, cc):
        return (bb, tchunk(dd, cc), 0)

    def prev_map(bb, dd, cc):
        return (bb, jnp.maximum(tchunk(dd, cc) * hb - 1, 0), 0)

    def next_map(bb, dd, cc):
        return (bb, jnp.minimum(tchunk(dd, cc) * hb + hb, n_hblk - 1), 0)

    def dir_map(bb, dd, cc):
        return (dd, 0, 0)

    def const2(bb, dd, cc):
        return (0, 0)

    def out_map(bb, dd, cc):
        return (dd, bb, tchunk(dd, cc), 0)

    out_sds = jax.ShapeDtypeStruct((2, bsz, t, dm), F32)
    kern = functools.partial(_wkv_kernel, d_model=dm, n_chunks=nc)
    return pl.pallas_call(
        kern,
        grid=(bsz, 2, nc),
        in_specs=[
            pl.BlockSpec((1, C, d3), main_map),
            pl.BlockSpec((1, SUBLANES, d3), prev_map),
            pl.BlockSpec((1, SUBLANES, d3), next_map),
            pl.BlockSpec((1, C, lw), main_map),
            pl.BlockSpec((1, SUBLANES, lw), prev_map),
            pl.BlockSpec((1, SUBLANES, lw), next_map),
            pl.BlockSpec((3, d3), const2),
            pl.BlockSpec((3, lw), const2),
            pl.BlockSpec((1, LORA_W, dm), dir_map),
            pl.BlockSpec((1, LORA_W, dm), dir_map),
            pl.BlockSpec((1, 1, dm), dir_map),
            pl.BlockSpec((1, 1, dm), dir_map),
            pl.BlockSpec((1, dm), const2),
            pl.BlockSpec((1, dm), const2),
            pl.BlockSpec((1, dm), const2),
        ],
        out_specs=[pl.BlockSpec((1, 1, C, dm), out_map), pl.BlockSpec((1, 1, C, dm), out_map)],
        out_shape=[out_sds, out_sds],
        scratch_shapes=[pltpu.VMEM((dm // LANES, LANES, LANES), F32),
                        pltpu.VMEM((C, dm), F32), pltpu.VMEM((C, dm), F32), pltpu.VMEM((C, dm), F32)],
        compiler_params=_params("parallel", "parallel", "arbitrary"),
        name="wkv_scan",
    )(z_rkv, z_rkv, z_rkv, z_lora, z_lora, z_lora, taps_rkv, taps_lora,
      w_up, a_up, w0, a0, k_k, k_a, r_k)


def _pool_kernel(p_ref, gate_ref, w_ref, scale_ref, o_ref, pad_ref, *, rows):
    g = pl.program_id(1)
    t, gi = p_ref.shape[1], p_ref.shape[2]
    zeros = jnp.zeros((POOL_PAD, gi), F32)
    pad_ref[0:POOL_PAD, :] = zeros
    pad_ref[POOL_PAD + t:POOL_PAD + t + POOL_PAD, :] = zeros
    pad_ref[POOL_PAD:POOL_PAD + t, :] = p_ref[0]
    w = w_ref[0]
    scale = scale_ref[...]

    for gidx, win in enumerate(POOL_WINDOWS):
        @pl.when(g == gidx)
        def _(win=win):
            half = win // 2

            def tile_body(i, carry):
                r0 = pl.multiple_of(i * rows, rows)
                n = rows + 2 * SUBLANES
                xt = pad_ref[pl.ds(r0 + POOL_PAD - SUBLANES, n), :]
                acc = xt
                step = 1
                while step < win:
                    acc = acc + pltpu.roll(acc, n - step, axis=0)
                    step *= 2
                if SUBLANES - half:
                    acc = pltpu.roll(acc, n - (SUBLANES - half), axis=0)
                acc = acc[0:rows]
                tt = r0 + lax.broadcasted_iota(jnp.int32, (rows, LANES), 0)
                cnt = (jnp.minimum(tt + (win - half), t) - jnp.maximum(tt - half, 0)).astype(F32)
                inv = 1.0 / cnt
                inv_full = jnp.concatenate([inv] * (gi // LANES), axis=1)
                dlt = acc * inv_full - xt[SUBLANES:SUBLANES + rows]
                out = jnp.dot(dlt.astype(BF16), w, preferred_element_type=F32) * scale
                gate = jax.nn.sigmoid(gate_ref[0, pl.ds(r0, rows), :])
                o_ref[0, pl.ds(r0, rows), :] = (gate * out).astype(o_ref.dtype)
                return carry

            lax.fori_loop(0, t // rows, tile_body, 0)


def _pool_branch(z_pool, z_gate, pool_w, pool_scale, rows=256):
    bsz, t, pw = z_pool.shape
    ng, gi, go = pool_w.shape
    dm = ng * go
    kern = functools.partial(_pool_kernel, rows=rows)
    return pl.pallas_call(
        kern,
        grid=(bsz, ng),
        in_specs=[pl.BlockSpec((1, t, gi), lambda b, g: (b, 0, g)),
                  pl.BlockSpec((1, t, go), lambda b, g: (b, 0, ng + g)),
                  pl.BlockSpec((1, gi, go), lambda b, g: (g, 0, 0)),
                  pl.BlockSpec((1, go), lambda b, g: (0, g))],
        out_specs=pl.BlockSpec((1, t, go), lambda b, g: (b, 0, g)),
        out_shape=jax.ShapeDtypeStruct((bsz, t, dm), BF16),
        scratch_shapes=[pltpu.VMEM((t + 2 * POOL_PAD, gi), F32)],
        compiler_params=_params("parallel", "arbitrary"),
        name="pool_branch",
    )(z_pool, z_gate, pool_w, pool_scale)


def _merge_kernel(y_ref, bo_ref, lo_ref, lop_ref, lon_ref, ltaps_ref, gup_ref, gng_ref, gnb_ref,
                  gate_ref, yb_ref, o_ref, *, n_tiles):
    i = pl.program_id(1)
    has_prev = jnp.where(i > 0, 1.0, 0.0).astype(F32)
    has_next = jnp.where(i < n_tiles - 1, 1.0, 0.0).astype(F32)
    gd_lo = 2 * LORA_W
    esum = _head_sum_matrix()
    inv_n = 1.0 / HEAD_SIZE
    gparts = []
    for j in range(GATE_LORA_PAD // LANES):
        cs = slice(gd_lo + j * LANES, gd_lo + (j + 1) * LANES)
        p_row, n_row = _halo_rows(lop_ref, lon_ref, cs, has_prev, has_next)
        gparts.append(jax.nn.sigmoid(_shifted(lo_ref[0, :, cs], p_row, n_row, ltaps_ref[:, cs])))
    gd = jnp.concatenate(gparts, axis=1).astype(BF16)

    def col_body(p, carry):
        cols = pl.ds(pl.multiple_of(p * LANES, LANES), LANES)
        y = y_ref[0, 0, :, cols] + y_ref[1, 0, :, cols]
        mu = jnp.dot(y.astype(BF16), esum, preferred_element_type=F32) * inv_n
        yc = y - mu
        var = jnp.dot((yc * yc).astype(BF16), esum, preferred_element_type=F32) * inv_n
        yn = yc * lax.rsqrt(var + GN_EPS) * gng_ref[:, cols] + gnb_ref[:, cols]
        yn = yn + bo_ref[0, 0, :, cols] + bo_ref[1, 0, :, cols]
        g = jnp.dot(gd, gup_ref[:, cols], preferred_element_type=F32)
        ya = jax.nn.sigmoid(gate_ref[0, :, cols]) * (yn * g)
        o_ref[0, :, cols] = (ya + yb_ref[0, :, cols].astype(F32)).astype(o_ref.dtype)
        return carry

    lax.fori_loop(0, o_ref.shape[2] // LANES, col_body, 0)


def _merge(y, bo, z_lora, taps_lora, g_up, ln_g, ln_b, z_gate, yb, tt=128):
    _, bsz, t, dm = y.shape
    lw = z_lora.shape[-1]
    nt = t // tt
    hb = tt // SUBLANES
    n_hblk = t // SUBLANES
    kern = functools.partial(_merge_kernel, n_tiles=nt)
    return pl.pallas_call(
        kern,
        grid=(bsz, nt),
        in_specs=[pl.BlockSpec((2, 1, tt, dm), lambda b, i: (0, b, i, 0)),
                  pl.BlockSpec((2, 1, tt, dm), lambda b, i: (0, b, i, 0)),
                  pl.BlockSpec((1, tt, lw), lambda b, i: (b, i, 0)),
                  pl.BlockSpec((1, SUBLANES, lw), lambda b, i: (b, jnp.maximum(i * hb - 1, 0), 0)),
                  pl.BlockSpec((1, SUBLANES, lw), lambda b, i: (b, jnp.minimum(i * hb + hb, n_hblk - 1), 0)),
                  pl.BlockSpec((3, lw), lambda b, i: (0, 0)),
                  pl.BlockSpec((GATE_LORA_PAD, dm), lambda b, i: (0, 0)),
                  pl.BlockSpec((1, dm), lambda b, i: (0, 0)),
                  pl.BlockSpec((1, dm), lambda b, i: (0, 0)),
                  pl.BlockSpec((1, tt, dm), lambda b, i: (b, i, 0)),
                  pl.BlockSpec((1, tt, dm), lambda b, i: (b, i, 0))],
        out_specs=pl.BlockSpec((1, tt, dm), lambda b, i: (b, i, 0)),
        out_shape=jax.ShapeDtypeStruct((bsz, t, dm), BF16),
        compiler_params=_params("parallel", "arbitrary"),
        name="wkv_merge",
    )(y, bo, z_lora, z_lora, z_lora, taps_lora, g_up, ln_g, ln_b, z_gate, yb)


def _xattn_kernel(q_ref, k_ref, v_ref, o_ref, *, head_dim):
    scale = head_dim ** -0.5
    nt_dims = (((1,), (1,)), ((), ()))
    for h in range(X_HEADS):
        cs = slice(h * head_dim, (h + 1) * head_dim)
        s = lax.dot_general(q_ref[0, :, cs], k_ref[0, :, cs], nt_dims, preferred_element_type=F32) * scale
        m = jnp.max(s, axis=-1, keepdims=True)
        e = jnp.exp(s - m)
        p = e / jnp.sum(e, axis=-1, keepdims=True)
        o_ref[0, :, cs] = jnp.dot(p.astype(BF16), v_ref[0, :, cs], preferred_element_type=F32).astype(o_ref.dtype)


def _xattn(q, k, v, tq=512):
    bsz, t, dm = q.shape
    m = k.shape[1]
    kern = functools.partial(_xattn_kernel, head_dim=dm // X_HEADS)
    return pl.pallas_call(
        kern,
        grid=(bsz, t // tq),
        in_specs=[pl.BlockSpec((1, tq, dm), lambda b, i: (b, i, 0)),
                  pl.BlockSpec((1, m, dm), lambda b, i: (b, 0, 0)),
                  pl.BlockSpec((1, m, dm), lambda b, i: (b, 0, 0))],
        out_specs=pl.BlockSpec((1, tq, dm), lambda b, i: (b, i, 0)),
        out_shape=jax.ShapeDtypeStruct((bsz, t, dm), BF16),
        compiler_params=_params("parallel", "arbitrary"),
        name="xattn",
    )(q, k, v)


def _pad_to(x, axis, size):
    pad = [(0, 0)] * x.ndim
    pad[axis] = (0, size - x.shape[axis])
    return jnp.pad(x, pad)


def _round_up(x, m):
    return (x + m - 1) // m * m


def _trunk(x, mem, lp, norm_final_g):
    bsz, t, dm = x.shape
    n_mem = mem.shape[1]
    m_tok = bsz * t
    h = x.reshape(m_tok, dm)
    memf = mem.reshape(bsz * n_mem, dm)
    depth = lp['w_in'].shape[0]
    for l in range(depth):
        p = {name: arr[l] for name, arr in lp.items()}
        gate_lora = p['g_up'].shape[0]
        c_rkv = 3 * dm
        c_lora = c_rkv + 2 * LORA_W + gate_lora
        pool_w = p['pool_w']
        pool_width = pool_w.shape[0] * pool_w.shape[1]
        c_pool = c_lora + pool_width
        lora_pad = 2 * LORA_W + GATE_LORA_PAD
        w_in = p['w_in']
        w_rkv = w_in[:, :c_rkv].astype(BF16)
        w_lora = _pad_to(w_in[:, c_rkv:c_lora], 1, lora_pad).astype(BF16)
        w_pool = w_in[:, c_lora:c_pool].astype(BF16)
        w_gate = w_in[:, c_pool:].astype(BF16)
        taps_rkv = p['shift_w'][:, :c_rkv]
        taps_lora = _pad_to(p['shift_w'][:, c_rkv:c_lora], 1, lora_pad)
        g_up = _pad_to(p['g_up'], 0, GATE_LORA_PAD).astype(BF16)

        xn = _rmsnorm(h, p['norm_mix_g'], BF16)
        z_rkv = _matmul(xn, w_rkv, F32, 1024, 512, name="in_rkv").reshape(bsz, t, c_rkv)
        z_lora = _matmul(xn, w_lora, F32, 1024, lora_pad, name="in_lora").reshape(bsz, t, lora_pad)
        z_pool = _matmul(xn, w_pool, F32, 1024, 512, name="in_pool").reshape(bsz, t, pool_width)
        z_gate = _matmul(xn, w_gate, F32, 1024, 512, name="in_gate").reshape(bsz, t, 2 * dm)

        w_up = jnp.stack([p['w_up_f'], p['w_up_b']])
        a_up = jnp.stack([p['a_up_f'], p['a_up_b']])
        w0 = jnp.stack([p['w0_f'], p['w0_b']]).reshape(2, 1, dm)
        a0 = jnp.stack([p['a0_f'], p['a0_b']]).reshape(2, 1, dm)
        y, bo = _wkv_scan(z_rkv, z_lora, taps_rkv, taps_lora, w_up, a_up, w0, a0,
                          p['k_k'].reshape(1, dm), p['k_a'].reshape(1, dm), p['r_k'].reshape(1, dm))
        yb = _pool_branch(z_pool, z_gate, pool_w.astype(BF16), p['pool_scale'].reshape(1, dm))
        merged = _merge(y, bo, z_lora, taps_lora, g_up, p['ln_x_g'].reshape(1, dm),
                        p['ln_x_b'].reshape(1, dm), z_gate, yb)
        h = _matmul(merged.reshape(m_tok, dm), p['w_out'].astype(BF16), F32, 1024, 512,
                    residual=h, name="out_proj")

        hn = _rmsnorm(h, p['norm_x_g'], BF16)
        mn = _rmsnorm(memf, p['norm_mem_g'], BF16)
        q = _matmul(hn, p['xq'].astype(BF16), BF16, 1024, 512, name="xq")
        kx = _matmul(mn, p['xk'].astype(BF16), BF16, 1024, 512, name="xk")
        vx = _matmul(mn, p['xv'].astype(BF16), BF16, 1024, 512, name="xv")
        o = _xattn(q.reshape(bsz, t, dm), kx.reshape(bsz, n_mem, dm), vx.reshape(bsz, n_mem, dm))
        h = _matmul(o.reshape(m_tok, dm), p['xo'].astype(BF16), F32, 1024, 512, residual=h, name="xo")

        hn = _rmsnorm(h, p['norm_ffn_g'], BF16)
        hidden = p['ffn_w2'].shape[0]
        hidden_pad = _round_up(hidden, 512)
        w13 = p['ffn_w13']
        w13p = jnp.concatenate([_pad_to(w13[:, :hidden], 1, hidden_pad),
                                _pad_to(w13[:, hidden:], 1, hidden_pad)], axis=1).astype(BF16)
        w2p = _pad_to(p['ffn_w2'], 0, hidden_pad).astype(BF16)
        act = _swiglu_up(hn, w13p, hidden_pad, 1024, 512)
        h = _matmul(act, w2p, F32, 512, 256, residual=h, name="ffn_down")
    return _rmsnorm(h, norm_final_g, F32).reshape(bsz, t, dm)


def kernel(x_prompt, x_sample, mem_prompt, mem_sample, norm_mix_g, w_in, shift_w, w0_f, w_up_f, w0_b, w_up_b, a0_f, a_up_f, a0_b, a_up_b, g_up, k_k, k_a, r_k, ln_x_g, ln_x_b, pool_w, pool_scale, w_out, norm_x_g, norm_mem_g, xq, xk, xv, xo, norm_ffn_g, ffn_w13, ffn_w2, norm_final_g):
    assert x_prompt.shape[1:] == x_sample.shape[1:] and mem_prompt.shape[1:] == mem_sample.shape[1:]
    lp = {
        'norm_mix_g': norm_mix_g, 'w_in': w_in, 'shift_w': shift_w,
        'w0_f': w0_f, 'w_up_f': w_up_f, 'w0_b': w0_b, 'w_up_b': w_up_b,
        'a0_f': a0_f, 'a_up_f': a_up_f, 'a0_b': a0_b, 'a_up_b': a_up_b,
        'g_up': g_up, 'k_k': k_k, 'k_a': k_a, 'r_k': r_k.reshape(r_k.shape[0], -1),
        'ln_x_g': ln_x_g, 'ln_x_b': ln_x_b,
        'pool_w': pool_w, 'pool_scale': pool_scale, 'w_out': w_out,
        'norm_x_g': norm_x_g, 'norm_mem_g': norm_mem_g, 'xq': xq, 'xk': xk, 'xv': xv, 'xo': xo,
        'norm_ffn_g': norm_ffn_g, 'ffn_w13': ffn_w13, 'ffn_w2': ffn_w2,
    }
    nb = x_prompt.shape[0]
    x = jnp.concatenate([x_prompt, x_sample], axis=0)
    mem = jnp.concatenate([mem_prompt, mem_sample], axis=0)
    y = _trunk(x, mem, lp, norm_final_g)
    return (y[:nb], y[nb:])
```

```python
import functools

import jax
import jax.numpy as jnp
from jax import lax
from jax.experimental import pallas as pl
from jax.experimental.pallas import tpu as pltpu

F32 = jnp.float32
BF16 = jnp.bfloat16

LANES = 128
SUBLANES = 8
VMEM_LIMIT_BYTES = 56 * 1024 * 1024

HEAD_SIZE = 64
HEAD_SHIFT = 6
HEADS_PER_TILE = LANES // HEAD_SIZE
X_HEADS = 4
N_MEM = 256
POOL_WINDOWS = (2, 4, 8, 16)
POOL_PAD = 16
GN_EPS = 64e-5
NORM_EPS = 1e-6
WKV_CHUNK = 64
WKV_GROUP = 16
LORA_W = 128
GATE_LORA_PAD = 512


def _params(*semantics):
    return pltpu.CompilerParams(dimension_semantics=semantics, vmem_limit_bytes=VMEM_LIMIT_BYTES)


def _rmsnorm_kernel(x_ref, g_ref, o_ref):
    x = x_ref[...]
    ms = jnp.mean(x * x, axis=-1, keepdims=True)
    o_ref[...] = (x * lax.rsqrt(ms + NORM_EPS) * g_ref[...]).astype(o_ref.dtype)


def _rmsnorm(x, g, out_dtype, tm=256):
    m, d = x.shape
    return pl.pallas_call(
        _rmsnorm_kernel,
        grid=(m // tm,),
        in_specs=[pl.BlockSpec((tm, d), lambda i: (i, 0)),
                  pl.BlockSpec((1, d), lambda i: (0, 0))],
        out_specs=pl.BlockSpec((tm, d), lambda i: (i, 0)),
        out_shape=jax.ShapeDtypeStruct((m, d), out_dtype),
        compiler_params=_params("parallel"),
        name="rmsnorm",
    )(x, g.reshape(1, d))


def _matmul_kernel(x_ref, w_ref, o_ref):
    o_ref[...] = jnp.dot(x_ref[...], w_ref[...], preferred_element_type=F32).astype(o_ref.dtype)


def _matmul_res_kernel(x_ref, w_ref, r_ref, o_ref):
    acc = jnp.dot(x_ref[...], w_ref[...], preferred_element_type=F32)
    o_ref[...] = (r_ref[...] + acc).astype(o_ref.dtype)


def _matmul(x, w, out_dtype, tm, tn, residual=None, name="matmul"):
    m, k = x.shape
    n = w.shape[1]
    in_specs = [pl.BlockSpec((tm, k), lambda i, j: (i, 0)),
                pl.BlockSpec((k, tn), lambda i, j: (0, j))]
    args = [x, w]
    body = _matmul_kernel
    if residual is not None:
        in_specs.append(pl.BlockSpec((tm, tn), lambda i, j: (i, j)))
        args.append(residual)
        body = _matmul_res_kernel
    return pl.pallas_call(
        body,
        grid=(m // tm, n // tn),
        in_specs=in_specs,
        out_specs=pl.BlockSpec((tm, tn), lambda i, j: (i, j)),
        out_shape=jax.ShapeDtypeStruct((m, n), out_dtype),
        compiler_params=_params("parallel", "arbitrary"),
        name=name,
    )(*args)


def _swiglu_kernel(x_ref, wg_ref, wu_ref, o_ref):
    x = x_ref[...]
    gate = jnp.dot(x, wg_ref[...], preferred_element_type=F32)
    up = jnp.dot(x, wu_ref[...], preferred_element_type=F32)
    o_ref[...] = (gate * jax.nn.sigmoid(gate) * up).astype(o_ref.dtype)


def _swiglu_up(x, w13, hidden, tm, tn):
    m, k = x.shape
    nb = hidden // tn
    return pl.pallas_call(
        _swiglu_kernel,
        grid=(m // tm, nb),
        in_specs=[pl.BlockSpec((tm, k), lambda i, j: (i, 0)),
                  pl.BlockSpec((k, tn), lambda i, j: (0, j)),
                  pl.BlockSpec((k, tn), lambda i, j: (0, j + nb))],
        out_specs=pl.BlockSpec((tm, tn), lambda i, j: (i, j)),
        out_shape=jax.ShapeDtypeStruct((m, hidden), BF16),
        compiler_params=_params("parallel", "arbitrary"),
        name="swiglu_up",
    )(x, w13, w13)


def _head_sum_matrix():
    r = lax.broadcasted_iota(jnp.int32, (LANES, LANES), 0) >> HEAD_SHIFT
    c = lax.broadcasted_iota(jnp.int32, (LANES, LANES), 1) >> HEAD_SHIFT
    return jnp.where(r == c, 1.0, 0.0).astype(BF16)


def _split2(x):
    hi = x.astype(BF16)
    return hi, (x - hi.astype(F32)).astype(BF16)


def _split_weight(w):
    hi, lo = _split2(w)
    return jnp.concatenate([hi, hi, lo], axis=0)


def _shifted(zc, prev_row, next_row, taps):
    rows = zc.shape[0]
    ridx = lax.broadcasted_iota(jnp.int32, zc.shape, 0)
    zm1 = jnp.where(ridx == 0, prev_row, pltpu.roll(zc, 1, axis=0))
    zp1 = jnp.where(ridx == rows - 1, next_row, pltpu.roll(zc, rows - 1, axis=0))
    return zm1 * taps[0:1, :] + zc * taps[1:2, :] + zp1 * taps[2:3, :]


def _halo_rows(prev_ref, next_ref, cols, has_prev, has_next):
    prev_row = prev_ref[0, SUBLANES - 1:SUBLANES, cols] * has_prev
    next_row = next_ref[0, 0:1, cols] * has_next
    return prev_row, next_row


def _wkv_kernel(z_ref, zp_ref, zn_ref, lo_ref, lop_ref, lon_ref, taps_ref, ltaps_ref,
                wup_ref, aup_ref, w0_ref, a0_ref, kk_ref, ka_ref, rk_ref,
                y_ref, bo_ref,
                state_ref, lw_ref, cum_ref, icl_ref, *, d_model, n_chunks):
    C = WKV_CHUNK
    d = pl.program_id(1)
    c = pl.program_id(2)
    tc = jnp.where(d == 0, c, n_chunks - 1 - c)
    has_prev = jnp.where(tc > 0, 1.0, 0.0).astype(F32)
    has_next = jnp.where(tc < n_chunks - 1, 1.0, 0.0).astype(F32)
    sgn = 1 - 2 * d

    @pl.when(c == 0)
    def _():
        state_ref[...] = jnp.zeros_like(state_ref)

    wd_cols = slice(0, LORA_W)
    ad_cols = slice(LORA_W, 2 * LORA_W)
    p_row, n_row = _halo_rows(lop_ref, lon_ref, wd_cols, has_prev, has_next)
    wd = _shifted(lo_ref[0, :, wd_cols], p_row, n_row, ltaps_ref[:, wd_cols])
    p_row, n_row = _halo_rows(lop_ref, lon_ref, ad_cols, has_prev, has_next)
    ad = _shifted(lo_ref[0, :, ad_cols], p_row, n_row, ltaps_ref[:, ad_cols])
    th_hi, th_lo = _split2(jnp.tanh(wd))
    wl = w0_ref[0] + jnp.dot(jnp.concatenate([th_hi, th_lo, th_hi], axis=1), wup_ref[0],
                             preferred_element_type=F32)
    neg = -wl
    softplus = jnp.maximum(neg, 0.0) + jnp.log(1.0 + jnp.exp(-jnp.abs(neg)))
    lw = -jnp.exp(-softplus - 0.5)
    lw_ref[...] = lw
    r64 = lax.broadcasted_iota(jnp.int32, (C, C), 0)
    c64 = lax.broadcasted_iota(jnp.int32, (C, C), 1)
    tri = jnp.where((r64 - c64) * sgn >= 0, 1.0, 0.0).astype(BF16)
    lw_hi = lw.astype(BF16)
    lw_r = lw - lw_hi.astype(F32)
    lw_mid = lw_r.astype(BF16)
    lw_lo = (lw_r - lw_mid.astype(F32)).astype(BF16)
    cum_ref[...] = (jnp.dot(tri, lw_hi, preferred_element_type=F32)
                    + jnp.dot(tri, lw_mid, preferred_element_type=F32)
                    + jnp.dot(tri, lw_lo, preferred_element_type=F32))
    ad_hi, ad_lo = _split2(ad)
    icl_ref[...] = jax.nn.sigmoid(
        a0_ref[0] + jnp.dot(jnp.concatenate([ad_hi, ad_lo, ad_hi], axis=1), aup_ref[0],
                            preferred_element_type=F32))

    row = lax.broadcasted_iota(jnp.int32, (C, LANES), 0)
    col = lax.broadcasted_iota(jnp.int32, (C, LANES), 1)
    colh = col & (HEAD_SIZE - 1)
    order = (row - colh) * sgn
    strict = order > 0
    incl = order >= 0
    eye2 = row == colh
    lane_lo = col < HEAD_SIZE
    esum = _head_sum_matrix()
    rr = lax.broadcasted_iota(jnp.int32, (LANES, LANES), 0) >> HEAD_SHIFT
    cc = lax.broadcasted_iota(jnp.int32, (LANES, LANES), 1) >> HEAD_SHIFT
    blockdiag = rr == cc

    def bd(x):
        zero = jnp.zeros_like(x)
        return jnp.concatenate([jnp.where(lane_lo, x, zero), jnp.where(lane_lo, zero, x)], axis=0)

    def pmul(x, y):
        return jnp.dot(x.astype(BF16), bd(y.astype(BF16)), preferred_element_type=F32)

    nt_dims = (((1,), (1,)), ((), ()))
    tn_dims = (((0,), (0,)), ((), ()))

    def group_body(g, carry):
        pairs = [g * WKV_GROUP + j for j in range(WKV_GROUP)]
        gmap = lambda f, *lists: [f(*xs) for xs in zip(*lists)]

        def shifted_stream(cs):
            p_row = zp_ref[0, SUBLANES - 1:SUBLANES, cs] * has_prev
            n_row = zn_ref[0, 0:1, cs] * has_next
            return _shifted(z_ref[0, :, cs], p_row, n_row, taps_ref[:, cs])

        def lane_tile(p, offset=0):
            return pl.ds(pl.multiple_of(offset + p * LANES, LANES), LANES)

        cols = [lane_tile(p) for p in pairs]
        r = [shifted_stream(cs) for cs in cols]
        k = [shifted_stream(lane_tile(p, d_model)) for p in pairs]
        v = [shifted_stream(lane_tile(p, 2 * d_model)) for p in pairs]
        icl = [icl_ref[:, cs] for cs in cols]
        lwp = [lw_ref[:, cs] for cs in cols]
        cum = [cum_ref[:, cs] for cs in cols]
        tot = [jnp.sum(x, axis=0, keepdims=True) for x in lwp]

        q = [ki * kk_ref[:, cs] for ki, cs in zip(k, cols)]
        n2 = [jnp.dot((qi * qi).astype(BF16), esum, preferred_element_type=F32) for qi in q]
        kd = [ki * (1.0 + (ic - 1.0) * ka_ref[:, cs]) for ki, ic, cs in zip(k, icl, cols)]
        rkr = [ri * kdi * rk_ref[:, cs] for ri, kdi, cs in zip(r, kd, cols)]
        bsum = [jnp.dot(x.astype(BF16), esum, preferred_element_type=F32) for x in rkr]
        for cs, bs, vi in zip(cols, bsum, v):
            bo_ref[0, 0, :, cs] = bs * vi

        kk = [qi * lax.rsqrt(jnp.maximum(ni, 1e-12)) for qi, ni in zip(q, n2)]
        b = gmap(lambda x, ic: x * ic, kk, icl)
        e_out = [jnp.exp(-x) for x in cum]
        e_tot = gmap(lambda t_, x: jnp.exp(t_ - x), tot, cum)
        at = gmap(lambda x, cm, lw_: (-x * jnp.exp(cm - lw_)).astype(BF16), kk, cum, lwp)
        rt = gmap(lambda x, cm: (x * jnp.exp(cm)).astype(BF16), r, cum)
        bt = gmap(lambda x, e: (x * e).astype(BF16), b, e_out)
        kt = gmap(lambda x, e: (x * e).astype(BF16), kd, e_out)
        bw = gmap(lambda x, e: (x * e).astype(BF16), b, e_tot)
        kw = gmap(lambda x, e: (x * e).astype(BF16), kd, e_tot)
        vb = [x.astype(BF16) for x in v]

        lhs = gmap(lambda a_, r_: jnp.concatenate([a_, r_], axis=0), at, rt)
        rhs_t = gmap(lambda b_, k_: jnp.concatenate([bd(b_), bd(k_)], axis=0), bt, kt)
        pmat = gmap(lambda l_, r_: lax.dot_general(l_, r_, nt_dims, preferred_element_type=F32), lhs, rhs_t)
        h = [state_ref[p] for p in pairs]
        hs = gmap(lambda l_, h_: lax.dot_general(l_, h_.astype(BF16), nt_dims, preferred_element_type=F32),
                  lhs, h)
        lab = [jnp.where(strict, x[:C, :LANES], 0.0) for x in pmat]
        lak = [jnp.where(strict, x[:C, LANES:], 0.0).astype(BF16) for x in pmat]
        rbk = [jnp.concatenate([jnp.where(incl, x[C:, :LANES], 0.0).astype(BF16),
                                jnp.where(incl, x[C:, LANES:], 0.0).astype(BF16)], axis=1) for x in pmat]
        vst = [bd(x) for x in vb]
        lakv = gmap(lambda l_, v_: jnp.dot(l_, v_, preferred_element_type=F32), lak, vst)
        rhs_u = gmap(lambda h_, x: h_[:C] + x, hs, lakv)

        ident = jnp.where(eye2, 1.0, 0.0)
        first = (row >> 1) == (colh >> 1)
        tinv = [ident + jnp.where(first, x, 0.0) for x in lab]
        s = 2
        while s < C:
            sh = s.bit_length() - 1
            level = ((row >> (sh + 1)) == (colh >> (sh + 1))) & ((row >> sh) != (colh >> sh))
            off = [jnp.where(level, x, 0.0) for x in lab]
            tmp = gmap(pmul, tinv, off)
            upd_t = gmap(pmul, tmp, tinv)
            tinv = gmap(lambda t_, x: t_ + x, tinv, upd_t)
            s *= 2
        u = gmap(pmul, tinv, rhs_u)

        ub = [x.astype(BF16) for x in u]
        yv = gmap(lambda rk_, u_, v_: jnp.dot(rk_, jnp.concatenate([bd(u_), v_], axis=0),
                                            preferred_element_type=F32), rbk, ub, vst)
        for cs, h_, x in zip(cols, hs, yv):
            y_ref[0, 0, :, cs] = h_[C:] + x
        upd = gmap(lambda u_, v_, b_, k_: lax.dot_general(
            jnp.concatenate([u_, v_], axis=0), jnp.concatenate([b_, k_], axis=0),
            tn_dims, preferred_element_type=F32), ub, vb, bw, kw)
        for p, h_, t_, x in zip(pairs, h, tot, upd):
            state_ref[p] = h_ * jnp.exp(t_) + jnp.where(blockdiag, x, 0.0)
        return carry

    lax.fori_loop(0, d_model // (LANES * WKV_GROUP), group_body, 0)


def _wkv_scan(z_rkv, z_lora, taps_rkv, taps_lora, w_up, a_up, w0, a0, k_k, k_a, r_k):
    bsz, t, d3 = z_rkv.shape
    dm = d3 // 3
    C = WKV_CHUNK
    nc = t // C
    hb = C // SUBLANES
    n_hblk = t // SUBLANES
    lw = z_lora.shape[-1]

    def tchunk(dd, cc):
        return jnp.where(dd == 0, cc, nc - 1 - cc)

    def main_map(bb, dd, cc):
        return (bb, tchunk(dd, cc), 0)

    def prev_map(bb, dd, cc):
        return (bb, jnp.maximum(tchunk(dd, cc) * hb - 1, 0), 0)

    def next_map(bb, dd, cc):
        return (bb, jnp.minimum(tchunk(dd, cc) * hb + hb, n_hblk - 1), 0)

    def dir_map(bb, dd, cc):
        return (dd, 0, 0)

    def const2(bb, dd, cc):
        return (0, 0)

    def out_map(bb, dd, cc):
        return (dd, bb, tchunk(dd, cc), 0)

    out_sds = jax.ShapeDtypeStruct((2, bsz, t, dm), F32)
    kern = functools.partial(_wkv_kernel, d_model=dm, n_chunks=nc)
    return pl.pallas_call(
        kern,
        grid=(bsz, 2, nc),
        in_specs=[
            pl.BlockSpec((1, C, d3), main_map),
            pl.BlockSpec((1, SUBLANES, d3), prev_map),
            pl.BlockSpec((1, SUBLANES, d3), next_map),
            pl.BlockSpec((1, C, lw), main_map),
            pl.BlockSpec((1, SUBLANES, lw), prev_map),
            pl.BlockSpec((1, SUBLANES, lw), next_map),
            pl.BlockSpec((3, d3), const2),
            pl.BlockSpec((3, lw), const2),
            pl.BlockSpec((1, 3 * LORA_W, dm), dir_map),
            pl.BlockSpec((1, 3 * LORA_W, dm), dir_map),
            pl.BlockSpec((1, 1, dm), dir_map),
            pl.BlockSpec((1, 1, dm), dir_map),
            pl.BlockSpec((1, dm), const2),
            pl.BlockSpec((1, dm), const2),
            pl.BlockSpec((1, dm), const2),
        ],
        out_specs=[pl.BlockSpec((1, 1, C, dm), out_map), pl.BlockSpec((1, 1, C, dm), out_map)],
        out_shape=[out_sds, out_sds],
        scratch_shapes=[pltpu.VMEM((dm // LANES, LANES, LANES), F32),
                        pltpu.VMEM((C, dm), F32), pltpu.VMEM((C, dm), F32), pltpu.VMEM((C, dm), F32)],
        compiler_params=_params("parallel", "parallel", "arbitrary"),
        name="wkv_scan",
    )(z_rkv, z_rkv, z_rkv, z_lora, z_lora, z_lora, taps_rkv, taps_lora,
      w_up, a_up, w0, a0, k_k, k_a, r_k)


def _pool_kernel(p_ref, gate_ref, w_ref, scale_ref, o_ref, pad_ref, *, rows):
    g = pl.program_id(1)
    t, gi = p_ref.shape[1], p_ref.shape[2]
    zeros = jnp.zeros((POOL_PAD, gi), F32)
    pad_ref[0:POOL_PAD, :] = zeros
    pad_ref[POOL_PAD + t:POOL_PAD + t + POOL_PAD, :] = zeros
    pad_ref[POOL_PAD:POOL_PAD + t, :] = p_ref[0]
    w = w_ref[0]
    scale = scale_ref[...]

    for gidx, win in enumerate(POOL_WINDOWS):
        @pl.when(g == gidx)
        def _(win=win):
            half = win // 2

            def tile_body(i, carry):
                r0 = pl.multiple_of(i * rows, rows)
                n = rows + 2 * SUBLANES
                xt = pad_ref[pl.ds(r0 + POOL_PAD - SUBLANES, n), :]
                acc = xt
                step = 1
                while step < win:
                    acc = acc + pltpu.roll(acc, n - step, axis=0)
                    step *= 2
                if SUBLANES - half:
                    acc = pltpu.roll(acc, n - (SUBLANES - half), axis=0)
                acc = acc[0:rows]
                tt = r0 + lax.broadcasted_iota(jnp.int32, (rows, LANES), 0)
                cnt = (jnp.minimum(tt + (win - half), t) - jnp.maximum(tt - half, 0)).astype(F32)
                inv = 1.0 / cnt
                inv_full = jnp.concatenate([inv] * (gi // LANES), axis=1)
                dlt = acc * inv_full - xt[SUBLANES:SUBLANES + rows]
                out = jnp.dot(dlt.astype(BF16), w, preferred_element_type=F32) * scale
                gate = jax.nn.sigmoid(gate_ref[0, pl.ds(r0, rows), :])
                o_ref[0, pl.ds(r0, rows), :] = (gate * out).astype(o_ref.dtype)
                return carry

            lax.fori_loop(0, t // rows, tile_body, 0)


def _pool_branch(z_pool, z_gate, pool_w, pool_scale, rows=256):
    bsz, t, pw = z_pool.shape
    ng, gi, go = pool_w.shape
    dm = ng * go
    kern = functools.partial(_pool_kernel, rows=rows)
    return pl.pallas_call(
        kern,
        grid=(bsz, ng),
        in_specs=[pl.BlockSpec((1, t, gi), lambda b, g: (b, 0, g)),
                  pl.BlockSpec((1, t, go), lambda b, g: (b, 0, ng + g)),
                  pl.BlockSpec((1, gi, go), lambda b, g: (g, 0, 0)),
                  pl.BlockSpec((1, go), lambda b, g: (0, g))],
        out_specs=pl.BlockSpec((1, t, go), lambda b, g: (b, 0, g)),
        out_shape=jax.ShapeDtypeStruct((bsz, t, dm), BF16),
        scratch_shapes=[pltpu.VMEM((t + 2 * POOL_PAD, gi), F32)],
        compiler_params=_params("parallel", "arbitrary"),
        name="pool_branch",
    )(z_pool, z_gate, pool_w, pool_scale)


def _merge_kernel(y_ref, bo_ref, lo_ref, lop_ref, lon_ref, ltaps_ref, gup_ref, gng_ref, gnb_ref,
                  gate_ref, yb_ref, o_ref, *, n_tiles):
    i = pl.program_id(1)
    has_prev = jnp.where(i > 0, 1.0, 0.0).astype(F32)
    has_next = jnp.where(i < n_tiles - 1, 1.0, 0.0).astype(F32)
    gd_lo = 2 * LORA_W
    esum = _head_sum_matrix()
    inv_n = 1.0 / HEAD_SIZE
    gparts = []
    for j in range(GATE_LORA_PAD // LANES):
        cs = slice(gd_lo + j * LANES, gd_lo + (j + 1) * LANES)
        p_row, n_row = _halo_rows(lop_ref, lon_ref, cs, has_prev, has_next)
        gparts.append(jax.nn.sigmoid(_shifted(lo_ref[0, :, cs], p_row, n_row, ltaps_ref[:, cs])))
    gd = jnp.concatenate(gparts, axis=1).astype(BF16)

    def col_body(p, carry):
        cols = pl.ds(pl.multiple_of(p * LANES, LANES), LANES)
        y = y_ref[0, 0, :, cols] + y_ref[1, 0, :, cols]
        mu = jnp.dot(y.astype(BF16), esum, preferred_element_type=F32) * inv_n
        yc = y - mu
        var = jnp.dot((yc * yc).astype(BF16), esum, preferred_element_type=F32) * inv_n
        yn = yc * lax.rsqrt(var + GN_EPS) * gng_ref[:, cols] + gnb_ref[:, cols]
        yn = yn + bo_ref[0, 0, :, cols] + bo_ref[1, 0, :, cols]
        g = jnp.dot(gd, gup_ref[:, cols], preferred_element_type=F32)
        ya = jax.nn.sigmoid(gate_ref[0, :, cols]) * (yn * g)
        o_ref[0, :, cols] = (ya + yb_ref[0, :, cols].astype(F32)).astype(o_ref.dtype)
        return carry

    lax.fori_loop(0, o_ref.shape[2] // LANES, col_body, 0)


def _merge(y, bo, z_lora, taps_lora, g_up, ln_g, ln_b, z_gate, yb, tt=128):
    _, bsz, t, dm = y.shape
    lw = z_lora.shape[-1]
    nt = t // tt
    hb = tt // SUBLANES
    n_hblk = t // SUBLANES
    kern = functools.partial(_merge_kernel, n_tiles=nt)
    return pl.pallas_call(
        kern,
        grid=(bsz, nt),
        in_specs=[pl.BlockSpec((2, 1, tt, dm), lambda b, i: (0, b, i, 0)),
                  pl.BlockSpec((2, 1, tt, dm), lambda b, i: (0, b, i, 0)),
                  pl.BlockSpec((1, tt, lw), lambda b, i: (b, i, 0)),
                  pl.BlockSpec((1, SUBLANES, lw), lambda b, i: (b, jnp.maximum(i * hb - 1, 0), 0)),
                  pl.BlockSpec((1, SUBLANES, lw), lambda b, i: (b, jnp.minimum(i * hb + hb, n_hblk - 1), 0)),
                  pl.BlockSpec((3, lw), lambda b, i: (0, 0)),
                  pl.BlockSpec((GATE_LORA_PAD, dm), lambda b, i: (0, 0)),
                  pl.BlockSpec((1, dm), lambda b, i: (0, 0)),
                  pl.BlockSpec((1, dm), lambda b, i: (0, 0)),
                  pl.BlockSpec((1, tt, dm), lambda b, i: (b, i, 0)),
                  pl.BlockSpec((1, tt, dm), lambda b, i: (b, i, 0))],
        out_specs=pl.BlockSpec((1, tt, dm), lambda b, i: (b, i, 0)),
        out_shape=jax.ShapeDtypeStruct((bsz, t, dm), BF16),
        compiler_params=_params("parallel", "arbitrary"),
        name="wkv_merge",
    )(y, bo, z_lora, z_lora, z_lora, taps_lora, g_up, ln_g, ln_b, z_gate, yb)


def _xattn_kernel(q_ref, k_ref, v_ref, o_ref, *, head_dim):
    scale = head_dim ** -0.5
    nt_dims = (((1,), (1,)), ((), ()))
    for h in range(X_HEADS):
        cs = slice(h * head_dim, (h + 1) * head_dim)
        s = lax.dot_general(q_ref[0, :, cs], k_ref[0, :, cs], nt_dims, preferred_element_type=F32) * scale
        m = jnp.max(s, axis=-1, keepdims=True)
        e = jnp.exp(s - m)
        p = e / jnp.sum(e, axis=-1, keepdims=True)
        o_ref[0, :, cs] = jnp.dot(p.astype(BF16), v_ref[0, :, cs], preferred_element_type=F32).astype(o_ref.dtype)


def _xattn(q, k, v, tq=512):
    bsz, t, dm = q.shape
    m = k.shape[1]
    kern = functools.partial(_xattn_kernel, head_dim=dm // X_HEADS)
    return pl.pallas_call(
        kern,
        grid=(bsz, t // tq),
        in_specs=[pl.BlockSpec((1, tq, dm), lambda b, i: (b, i, 0)),
                  pl.BlockSpec((1, m, dm), lambda b, i: (b, 0, 0)),
                  pl.BlockSpec((1, m, dm), lambda b, i: (b, 0, 0))],
        out_specs=pl.BlockSpec((1, tq, dm), lambda b, i: (b, i, 0)),
        out_shape=jax.ShapeDtypeStruct((bsz, t, dm), BF16),
        compiler_params=_params("parallel", "arbitrary"),
        name="xattn",
    )(q, k, v)


def _pad_to(x, axis, size):
    pad = [(0, 0)] * x.ndim
    pad[axis] = (0, size - x.shape[axis])
    return jnp.pad(x, pad)


def _round_up(x, m):
    return (x + m - 1) // m * m


def _trunk(x, mem, lp, norm_final_g):
    bsz, t, dm = x.shape
    n_mem = mem.shape[1]
    m_tok = bsz * t
    h = x.reshape(m_tok, dm)
    memf = mem.reshape(bsz * n_mem, dm)
    depth = lp['w_in'].shape[0]
    for l in range(depth):
        p = {name: arr[l] for name, arr in lp.items()}
        gate_lora = p['g_up'].shape[0]
        c_rkv = 3 * dm
        c_lora = c_rkv + 2 * LORA_W + gate_lora
        pool_w = p['pool_w']
        pool_width = pool_w.shape[0] * pool_w.shape[1]
        c_pool = c_lora + pool_width
        lora_pad = 2 * LORA_W + GATE_LORA_PAD
        w_in = p['w_in']
        w_rkv = w_in[:, :c_rkv].astype(BF16)
        w_lora = _pad_to(w_in[:, c_rkv:c_lora], 1, lora_pad).astype(BF16)
        w_pool = w_in[:, c_lora:c_pool].astype(BF16)
        w_gate = w_in[:, c_pool:].astype(BF16)
        taps_rkv = p['shift_w'][:, :c_rkv]
        taps_lora = _pad_to(p['shift_w'][:, c_rkv:c_lora], 1, lora_pad)
        g_up = _pad_to(p['g_up'], 0, GATE_LORA_PAD).astype(BF16)

        xn = _rmsnorm(h, p['norm_mix_g'], BF16)
        z_rkv = _matmul(xn, w_rkv, F32, 1024, 512, name="in_rkv").reshape(bsz, t, c_rkv)
        z_lora = _matmul(xn, w_lora, F32, 1024, lora_pad, name="in_lora").reshape(bsz, t, lora_pad)
        z_pool = _matmul(xn, w_pool, F32, 1024, 512, name="in_pool").reshape(bsz, t, pool_width)
        z_gate = _matmul(xn, w_gate, F32, 1024, 512, name="in_gate").reshape(bsz, t, 2 * dm)

        w_up = jnp.stack([_split_weight(p['w_up_f']), _split_weight(p['w_up_b'])])
        a_up = jnp.stack([_split_weight(p['a_up_f']), _split_weight(p['a_up_b'])])
        w0 = jnp.stack([p['w0_f'], p['w0_b']]).reshape(2, 1, dm)
        a0 = jnp.stack([p['a0_f'], p['a0_b']]).reshape(2, 1, dm)
        y, bo = _wkv_scan(z_rkv, z_lora, taps_rkv, taps_lora, w_up, a_up, w0, a0,
                          p['k_k'].reshape(1, dm), p['k_a'].reshape(1, dm), p['r_k'].reshape(1, dm))
        yb = _pool_branch(z_pool, z_gate, pool_w.astype(BF16), p['pool_scale'].reshape(1, dm))
        merged = _merge(y, bo, z_lora, taps_lora, g_up, p['ln_x_g'].reshape(1, dm),
                        p['ln_x_b'].reshape(1, dm), z_gate, yb)
        h = _matmul(merged.reshape(m_tok, dm), p['w_out'].astype(BF16), F32, 1024, 512,
                    residual=h, name="out_proj")

        hn = _rmsnorm(h, p['norm_x_g'], BF16)
        mn = _rmsnorm(memf, p['norm_mem_g'], BF16)
        q = _matmul(hn, p['xq'].astype(BF16), BF16, 1024, 512, name="xq")
        kx = _matmul(mn, p['xk'].astype(BF16), BF16, 1024, 512, name="xk")
        vx = _matmul(mn, p['xv'].astype(BF16), BF16, 1024, 512, name="xv")
        o = _xattn(q.reshape(bsz, t, dm), kx.reshape(bsz, n_mem, dm), vx.reshape(bsz, n_mem, dm))
        h = _matmul(o.reshape(m_tok, dm), p['xo'].astype(BF16), F32, 1024, 512, residual=h, name="xo")

        hn = _rmsnorm(h, p['norm_ffn_g'], BF16)
        hidden = p['ffn_w2'].shape[0]
        hidden_pad = _round_up(hidden, 512)
        w13 = p['ffn_w13']
        w13p = jnp.concatenate([_pad_to(w13[:, :hidden], 1, hidden_pad),
                                _pad_to(w13[:, hidden:], 1, hidden_pad)], axis=1).astype(BF16)
        w2p = _pad_to(p['ffn_w2'], 0, hidden_pad).astype(BF16)
        act = _swiglu_up(hn, w13p, hidden_pad, 1024, 512)
        h = _matmul(act, w2p, F32, 512, 256, residual=h, name="ffn_down")
    return _rmsnorm(h, norm_final_g, F32).reshape(bsz, t, dm)


def kernel(x_prompt, x_sample, mem_prompt, mem_sample, norm_mix_g, w_in, shift_w, w0_f, w_up_f, w0_b, w_up_b, a0_f, a_up_f, a0_b, a_up_b, g_up, k_k, k_a, r_k, ln_x_g, ln_x_b, pool_w, pool_scale, w_out, norm_x_g, norm_mem_g, xq, xk, xv, xo, norm_ffn_g, ffn_w13, ffn_w2, norm_final_g):
    assert x_prompt.shape[1:] == x_sample.shape[1:] and mem_prompt.shape[1:] == mem_sample.shape[1:]
    lp = {
        'norm_mix_g': norm_mix_g, 'w_in': w_in, 'shift_w': shift_w,
        'w0_f': w0_f, 'w_up_f': w_up_f, 'w0_b': w0_b, 'w_up_b': w_up_b,
        'a0_f': a0_f, 'a_up_f': a_up_f, 'a0_b': a0_b, 'a_up_b': a_up_b,
        'g_up': g_up, 'k_k': k_k, 'k_a': k_a, 'r_k': r_k.reshape(r_k.shape[0], -1),
        'ln_x_g': ln_x_g, 'ln_x_b': ln_x_b,
        'pool_w': pool_w, 'pool_scale': pool_scale, 'w_out': w_out,
        'norm_x_g': norm_x_g, 'norm_mem_g': norm_mem_g, 'xq': xq, 'xk': xk, 'xv': xv, 'xo': xo,
        'norm_ffn_g': norm_ffn_g, 'ffn_w13': ffn_w13, 'ffn_w2': ffn_w2,
    }
    nb = x_prompt.shape[0]
    x = jnp.concatenate([x_prompt, x_sample], axis=0)
    mem = jnp.concatenate([mem_prompt, mem_sample], axis=0)
    y = _trunk(x, mem, lp, norm_final_g)
    return (y[:nb], y[nb:])
```

```python
import functools
import math

import jax
import jax.numpy as jnp
from jax import lax
from jax.experimental import pallas as pl
from jax.experimental.pallas import tpu as pltpu

F32 = jnp.float32
BF16 = jnp.bfloat16

LANES = 128
SUBLANES = 8
VMEM_LIMIT_BYTES = 56 * 1024 * 1024

HEAD_SIZE = 64
HEAD_SHIFT = 6
X_HEADS = 4
POOL_WINDOWS = (2, 4, 8, 16)
POOL_PAD = 16
GN_EPS = 64e-5
NORM_EPS = 1e-6
DECAY_SCALE = -math.exp(-0.5)
WKV_CHUNK = 64
WKV_GROUP = 16
MERGE_GROUP = 8
LORA_W = 128
GATE_LORA_PAD = 512
LORA_COLS = 1024
IN_TN = 256


def _params(*semantics):
    return pltpu.CompilerParams(dimension_semantics=semantics, vmem_limit_bytes=VMEM_LIMIT_BYTES)


def _rms(x, g):
    ms = jnp.mean(x * x, axis=-1, keepdims=True)
    return x * lax.rsqrt(ms + NORM_EPS) * g


def _rmsnorm_kernel(x_ref, g_ref, o_ref):
    o_ref[...] = _rms(x_ref[...], g_ref[...]).astype(o_ref.dtype)


def _rmsnorm(x, g, out_dtype, tm=256, row_block_offset=0, n_row_blocks=None):
    m, d = x.shape
    nb = m // tm if n_row_blocks is None else n_row_blocks
    return pl.pallas_call(
        _rmsnorm_kernel,
        grid=(nb,),
        in_specs=[pl.BlockSpec((tm, d), lambda i: (i + row_block_offset, 0)),
                  pl.BlockSpec((1, d), lambda i: (0, 0))],
        out_specs=pl.BlockSpec((tm, d), lambda i: (i, 0)),
        out_shape=jax.ShapeDtypeStruct((nb * tm, d), out_dtype),
        compiler_params=_params("parallel"),
        name="rmsnorm",
    )(x, g.reshape(1, d))


def _rmsnorm2_kernel(xa_ref, xb_ref, g_ref, o_ref, *, na):
    x = jnp.where(pl.program_id(0) < na, xa_ref[...], xb_ref[...])
    o_ref[...] = _rms(x, g_ref[...]).astype(o_ref.dtype)


def _rmsnorm2(xa, xb, g, out_dtype, tm=256):
    d = xa.shape[1]
    na, nb = xa.shape[0] // tm, xb.shape[0] // tm
    return pl.pallas_call(
        functools.partial(_rmsnorm2_kernel, na=na),
        grid=(na + nb,),
        in_specs=[pl.BlockSpec((tm, d), lambda i: (jnp.minimum(i, na - 1), 0)),
                  pl.BlockSpec((tm, d), lambda i: (jnp.maximum(i - na, 0), 0)),
                  pl.BlockSpec((1, d), lambda i: (0, 0))],
        out_specs=pl.BlockSpec((tm, d), lambda i: (i, 0)),
        out_shape=jax.ShapeDtypeStruct(((na + nb) * tm, d), out_dtype),
        compiler_params=_params("arbitrary"),
        name="rmsnorm2",
    )(xa, xb, g.reshape(1, d))


def _matmul_kernel(x_ref, w_ref, o_ref):
    o_ref[...] = jnp.dot(x_ref[...], w_ref[...], preferred_element_type=F32).astype(o_ref.dtype)


def _matmul_res_kernel(x_ref, w_ref, r_ref, o_ref):
    acc = jnp.dot(x_ref[...], w_ref[...], preferred_element_type=F32)
    o_ref[...] = (r_ref[...] + acc).astype(o_ref.dtype)


def _matmul_res2_kernel(x_ref, w_ref, ra_ref, rb_ref, o_ref, *, na):
    acc = jnp.dot(x_ref[...], w_ref[...], preferred_element_type=F32)
    res = jnp.where(pl.program_id(0) < na, ra_ref[...], rb_ref[...])
    o_ref[...] = (res + acc).astype(o_ref.dtype)


def _matmul(x, w, out_dtype, tm, tn, residual=None, name="matmul"):
    m, k = x.shape
    n = w.shape[1]
    in_specs = [pl.BlockSpec((tm, k), lambda i, j: (i, 0)),
                pl.BlockSpec((k, tn), lambda i, j: (0, j))]
    args = [x, w]
    body = _matmul_kernel
    if isinstance(residual, tuple):
        ra, rb = residual
        na = ra.shape[0] // tm
        in_specs += [pl.BlockSpec((tm, tn), lambda i, j: (jnp.minimum(i, na - 1), j)),
                     pl.BlockSpec((tm, tn), lambda i, j: (jnp.maximum(i - na, 0), j))]
        args += [ra, rb]
        body = functools.partial(_matmul_res2_kernel, na=na)
    elif residual is not None:
        in_specs.append(pl.BlockSpec((tm, tn), lambda i, j: (i, j)))
        args.append(residual)
        body = _matmul_res_kernel
    return pl.pallas_call(
        body,
        grid=(m // tm, n // tn),
        in_specs=in_specs,
        out_specs=pl.BlockSpec((tm, tn), lambda i, j: (i, j)),
        out_shape=jax.ShapeDtypeStruct((m, n), out_dtype),
        compiler_params=_params("arbitrary", "arbitrary"),
        name=name,
    )(*args)


def _in_proj_kernel(x_ref, w_ref, taps_ref, o_ref):
    z = jnp.dot(x_ref[...], w_ref[...], preferred_element_type=F32)
    t = z.shape[0]
    ridx = lax.broadcasted_iota(jnp.int32, z.shape, 0)
    zm1 = jnp.where(ridx == 0, 0.0, pltpu.roll(z, 1, axis=0))
    zp1 = jnp.where(ridx == t - 1, 0.0, pltpu.roll(z, t - 1, axis=0))
    taps = taps_ref[...]
    o_ref[0] = zm1 * taps[0:1, :] + z * taps[1:2, :] + zp1 * taps[2:3, :]


def _in_proj(xn, w, taps, bsz, t):
    k = xn.shape[1]
    n = w.shape[1]
    return pl.pallas_call(
        _in_proj_kernel,
        grid=(bsz, n // IN_TN),
        in_specs=[pl.BlockSpec((t, k), lambda b, j: (b, 0)),
                  pl.BlockSpec((k, IN_TN), lambda b, j: (0, j)),
                  pl.BlockSpec((3, IN_TN), lambda b, j: (0, j))],
        out_specs=pl.BlockSpec((1, t, IN_TN), lambda b, j: (b, 0, j)),
        out_shape=jax.ShapeDtypeStruct((bsz, t, n), F32),
        compiler_params=_params("arbitrary", "arbitrary"),
        name="in_proj",
    )(xn, w, taps)


def _swiglu_kernel(x_ref, wg_ref, wu_ref, o_ref):
    x = x_ref[...]
    gate = jnp.dot(x, wg_ref[...], preferred_element_type=F32)
    up = jnp.dot(x, wu_ref[...], preferred_element_type=F32)
    o_ref[...] = (gate * jax.nn.sigmoid(gate) * up).astype(o_ref.dtype)


def _swiglu_up(x, w13, hidden, tm, tn):
    m, k = x.shape
    nb = hidden // tn
    return pl.pallas_call(
        _swiglu_kernel,
        grid=(m // tm, nb),
        in_specs=[pl.BlockSpec((tm, k), lambda i, j: (i, 0)),
                  pl.BlockSpec((k, tn), lambda i, j: (0, j)),
                  pl.BlockSpec((k, tn), lambda i, j: (0, j + nb))],
        out_specs=pl.BlockSpec((tm, tn), lambda i, j: (i, j)),
        out_shape=jax.ShapeDtypeStruct((m, hidden), BF16),
        compiler_params=_params("arbitrary", "arbitrary"),
        name="swiglu_up",
    )(x, w13, w13)


def _head_sum_matrix():
    r = lax.broadcasted_iota(jnp.int32, (LANES, LANES), 0) >> HEAD_SHIFT
    c = lax.broadcasted_iota(jnp.int32, (LANES, LANES), 1) >> HEAD_SHIFT
    return jnp.where(r == c, 1.0, 0.0).astype(BF16)


def _split2(x):
    hi = x.astype(BF16)
    return hi, (x - hi.astype(F32)).astype(BF16)


def _split_weight(w):
    hi, lo = _split2(w)
    return jnp.concatenate([hi, hi, lo], axis=0)


def _gmap(f, *lists):
    return [f(*xs) for xs in zip(*lists)]


def _wkv_kernel(z_ref, lo_ref, wup_ref, aup_ref, w0_ref, a0_ref, kk_ref, ka_ref, rk_ref,
                y_ref, bo_ref, state_ref, lw_ref, cum_ref, icl_ref, *, d_model):
    C = WKV_CHUNK
    d = pl.program_id(1)
    c = pl.program_id(2)
    sgn = 1 - 2 * d

    @pl.when(c == 0)
    def _():
        state_ref[...] = jnp.zeros_like(state_ref)

    th_hi, th_lo = _split2(jnp.tanh(lo_ref[0, :, 0:LORA_W]))
    wl = w0_ref[0] + jnp.dot(jnp.concatenate([th_hi, th_lo, th_hi], axis=1), wup_ref[0],
                             preferred_element_type=F32)
    lw = DECAY_SCALE * jax.nn.sigmoid(wl)
    lw_ref[...] = lw
    r64 = lax.broadcasted_iota(jnp.int32, (C, C), 0)
    c64 = lax.broadcasted_iota(jnp.int32, (C, C), 1)
    tri = jnp.where((r64 - c64) * sgn >= 0, 1.0, 0.0).astype(BF16)
    lw_hi = lw.astype(BF16)
    lw_r = lw - lw_hi.astype(F32)
    lw_mid = lw_r.astype(BF16)
    lw_lo = (lw_r - lw_mid.astype(F32)).astype(BF16)
    cum_ref[...] = (jnp.dot(tri, lw_hi, preferred_element_type=F32)
                    + jnp.dot(tri, lw_mid, preferred_element_type=F32)
                    + jnp.dot(tri, lw_lo, preferred_element_type=F32))
    ad_hi, ad_lo = _split2(lo_ref[0, :, LORA_W:2 * LORA_W])
    icl_ref[...] = jax.nn.sigmoid(
        a0_ref[0] + jnp.dot(jnp.concatenate([ad_hi, ad_lo, ad_hi], axis=1), aup_ref[0],
                            preferred_element_type=F32))

    row = lax.broadcasted_iota(jnp.int32, (C, LANES), 0)
    col = lax.broadcasted_iota(jnp.int32, (C, LANES), 1)
    colh = col & (HEAD_SIZE - 1)
    order = (row - colh) * sgn
    strict = order > 0
    incl = order >= 0
    eye2 = row == colh
    lane_lo = col < HEAD_SIZE
    esum = _head_sum_matrix()
    rr = lax.broadcasted_iota(jnp.int32, (LANES, LANES), 0) >> HEAD_SHIFT
    cc = lax.broadcasted_iota(jnp.int32, (LANES, LANES), 1) >> HEAD_SHIFT
    blockdiag = rr == cc

    def bd(x):
        zero = jnp.zeros_like(x)
        return jnp.concatenate([jnp.where(lane_lo, x, zero), jnp.where(lane_lo, zero, x)], axis=0)

    def pmul(x, y):
        return jnp.dot(x.astype(BF16), bd(y.astype(BF16)), preferred_element_type=F32)

    nt_dims = (((1,), (1,)), ((), ()))
    tn_dims = (((0,), (0,)), ((), ()))

    def group_body(g, carry):
        pairs = [g * WKV_GROUP + j for j in range(WKV_GROUP)]

        def lane_tile(p, offset=0):
            return pl.ds(pl.multiple_of(offset + p * LANES, LANES), LANES)

        cols = [lane_tile(p) for p in pairs]
        r = [z_ref[0, :, cs] for cs in cols]
        k = [z_ref[0, :, lane_tile(p, d_model)] for p in pairs]
        v = [z_ref[0, :, lane_tile(p, 2 * d_model)] for p in pairs]
        icl = [icl_ref[:, cs] for cs in cols]
        lwp = [lw_ref[:, cs] for cs in cols]
        cum = [cum_ref[:, cs] for cs in cols]
        tot = [jnp.sum(x, axis=0, keepdims=True) for x in lwp]

        q = [ki * kk_ref[:, cs] for ki, cs in zip(k, cols)]
        n2 = [jnp.dot((qi * qi).astype(BF16), esum, preferred_element_type=F32) for qi in q]
        kd = [ki * (1.0 + (ic - 1.0) * ka_ref[:, cs]) for ki, ic, cs in zip(k, icl, cols)]
        rkr = [ri * kdi * rk_ref[:, cs] for ri, kdi, cs in zip(r, kd, cols)]
        bsum = [jnp.dot(x.astype(BF16), esum, preferred_element_type=F32) for x in rkr]
        for cs, bs, vi in zip(cols, bsum, v):
            bo_ref[0, 0, :, cs] = bs * vi

        kk = [qi * lax.rsqrt(jnp.maximum(ni, 1e-12)) for qi, ni in zip(q, n2)]
        b = _gmap(lambda x, ic: x * ic, kk, icl)
        e_out = [jnp.exp(-x) for x in cum]
        e_tot = _gmap(lambda t_, x: jnp.exp(t_ - x), tot, cum)
        at = _gmap(lambda x, cm, lw_: (-x * jnp.exp(cm - lw_)).astype(BF16), kk, cum, lwp)
        rt = _gmap(lambda x, cm: (x * jnp.exp(cm)).astype(BF16), r, cum)
        bt = _gmap(lambda x, e: (x * e).astype(BF16), b, e_out)
        kt = _gmap(lambda x, e: (x * e).astype(BF16), kd, e_out)
        bw = _gmap(lambda x, e: (x * e).astype(BF16), b, e_tot)
        kw = _gmap(lambda x, e: (x * e).astype(BF16), kd, e_tot)
        vb = [x.astype(BF16) for x in v]

        lhs = _gmap(lambda a_, r_: jnp.concatenate([a_, r_], axis=0), at, rt)
        rhs_t = _gmap(lambda b_, k_: jnp.concatenate([bd(b_), bd(k_)], axis=0), bt, kt)
        pmat = _gmap(lambda l_, r_: lax.dot_general(l_, r_, nt_dims, preferred_element_type=F32), lhs, rhs_t)
        h = [state_ref[p] for p in pairs]
        hs = _gmap(lambda l_, h_: lax.dot_general(l_, h_.astype(BF16), nt_dims, preferred_element_type=F32),
                   lhs, h)
        lab = [jnp.where(strict, x[:C, :LANES], 0.0) for x in pmat]
        lak = [jnp.where(strict, x[:C, LANES:], 0.0).astype(BF16) for x in pmat]
        rbk = [jnp.concatenate([jnp.where(incl, x[C:, :LANES], 0.0).astype(BF16),
                                jnp.where(incl, x[C:, LANES:], 0.0).astype(BF16)], axis=1) for x in pmat]
        vst = [bd(x) for x in vb]
        lakv = _gmap(lambda l_, v_: jnp.dot(l_, v_, preferred_element_type=F32), lak, vst)
        rhs_u = _gmap(lambda h_, x: h_[:C] + x, hs, lakv)

        ident = jnp.where(eye2, 1.0, 0.0)
        first = (row >> 1) == (colh >> 1)
        tinv = [ident + jnp.where(first, x, 0.0) for x in lab]
        s = 2
        while s < C:
            sh = s.bit_length() - 1
            level = ((row >> (sh + 1)) == (colh >> (sh + 1))) & ((row >> sh) != (colh >> sh))
            off = [jnp.where(level, x, 0.0) for x in lab]
            tmp = _gmap(pmul, tinv, off)
            upd_t = _gmap(pmul, tmp, tinv)
            tinv = _gmap(lambda t_, x: t_ + x, tinv, upd_t)
            s *= 2
        u = _gmap(pmul, tinv, rhs_u)

        ub = [x.astype(BF16) for x in u]
        yv = _gmap(lambda rk_, u_, v_: jnp.dot(rk_, jnp.concatenate([bd(u_), v_], axis=0),
                                             preferred_element_type=F32), rbk, ub, vst)
        for cs, h_, x in zip(cols, hs, yv):
            y_ref[0, 0, :, cs] = h_[C:] + x
        upd = _gmap(lambda u_, v_, b_, k_: lax.dot_general(
            jnp.concatenate([u_, v_], axis=0), jnp.concatenate([b_, k_], axis=0),
            tn_dims, preferred_element_type=F32), ub, vb, bw, kw)
        for p, h_, t_, x in zip(pairs, h, tot, upd):
            state_ref[p] = h_ * jnp.exp(t_) + jnp.where(blockdiag, x, 0.0)
        return carry

    lax.fori_loop(0, d_model // (LANES * WKV_GROUP), group_body, 0)


def _wkv_scan(z, dm, w_up, a_up, w0, a0, k_k, k_a, r_k):
    bsz, t, n = z.shape
    C = WKV_CHUNK
    nc = t // C
    lora_block = (n - LORA_COLS) // LORA_COLS

    def tchunk(dd, cc):
        return jnp.where(dd == 0, cc, nc - 1 - cc)

    def dir_map(bb, dd, cc):
        return (dd, 0, 0)

    def const2(bb, dd, cc):
        return (0, 0)

    def out_map(bb, dd, cc):
        return (dd, bb, tchunk(dd, cc), 0)

    out_sds = jax.ShapeDtypeStruct((2, bsz, t, dm), F32)
    kern = functools.partial(_wkv_kernel, d_model=dm)
    return pl.pallas_call(
        kern,
        grid=(bsz, 2, nc),
        in_specs=[
            pl.BlockSpec((1, C, 3 * dm), lambda bb, dd, cc: (bb, tchunk(dd, cc), 0)),
            pl.BlockSpec((1, C, LORA_COLS), lambda bb, dd, cc: (bb, tchunk(dd, cc), lora_block)),
            pl.BlockSpec((1, 3 * LORA_W, dm), dir_map),
            pl.BlockSpec((1, 3 * LORA_W, dm), dir_map),
            pl.BlockSpec((1, 1, dm), dir_map),
            pl.BlockSpec((1, 1, dm), dir_map),
            pl.BlockSpec((1, dm), const2),
            pl.BlockSpec((1, dm), const2),
            pl.BlockSpec((1, dm), const2),
        ],
        out_specs=[pl.BlockSpec((1, 1, C, dm), out_map), pl.BlockSpec((1, 1, C, dm), out_map)],
        out_shape=[out_sds, out_sds],
        scratch_shapes=[pltpu.VMEM((dm // LANES, LANES, LANES), F32),
                        pltpu.VMEM((C, dm), F32), pltpu.VMEM((C, dm), F32), pltpu.VMEM((C, dm), F32)],
        compiler_params=_params("arbitrary", "arbitrary", "arbitrary"),
        name="wkv_scan",
    )(z, z, w_up, a_up, w0, a0, k_k, k_a, r_k)


def _pool_kernel(p_ref, gate_ref, w_ref, scale_ref, o_ref, pad_ref, *, rows):
    g = pl.program_id(1)
    t, gi = p_ref.shape[1], p_ref.shape[2]
    zeros = jnp.zeros((POOL_PAD, gi), F32)
    pad_ref[0:POOL_PAD, :] = zeros
    pad_ref[POOL_PAD + t:POOL_PAD + t + POOL_PAD, :] = zeros
    pad_ref[POOL_PAD:POOL_PAD + t, :] = p_ref[0]
    w = w_ref[0]
    scale = scale_ref[...]

    for gidx, win in enumerate(POOL_WINDOWS):
        @pl.when(g == gidx)
        def _(win=win):
            half = win // 2

            def tile_body(i, carry):
                r0 = pl.multiple_of(i * rows, rows)
                n = rows + 2 * SUBLANES
                xt = pad_ref[pl.ds(r0 + POOL_PAD - SUBLANES, n), :]
                acc = xt
                step = 1
                while step < win:
                    acc = acc + pltpu.roll(acc, n - step, axis=0)
                    step *= 2
                if SUBLANES - half:
                    acc = pltpu.roll(acc, n - (SUBLANES - half), axis=0)
                acc = acc[0:rows]
                tt = r0 + lax.broadcasted_iota(jnp.int32, (rows, LANES), 0)
                cnt = (jnp.minimum(tt + (win - half), t) - jnp.maximum(tt - half, 0)).astype(F32)
                inv = 1.0 / cnt
                inv_full = jnp.concatenate([inv] * (gi // LANES), axis=1)
                dlt = acc * inv_full - xt[SUBLANES:SUBLANES + rows]
                out = jnp.dot(dlt.astype(BF16), w, preferred_element_type=F32) * scale
                gate = jax.nn.sigmoid(gate_ref[0, pl.ds(r0, rows), :])
                o_ref[0, pl.ds(r0, rows), :] = (gate * out).astype(o_ref.dtype)
                return carry

            lax.fori_loop(0, t // rows, tile_body, 0)


def _pool_branch(z, pool_col, gate_col, pool_w, pool_scale, rows=256):
    bsz, t, _ = z.shape
    ng, gi, go = pool_w.shape
    dm = ng * go
    kern = functools.partial(_pool_kernel, rows=rows)
    return pl.pallas_call(
        kern,
        grid=(bsz, ng),
        in_specs=[pl.BlockSpec((1, t, gi), lambda b, g: (b, 0, pool_col // gi + g)),
                  pl.BlockSpec((1, t, go), lambda b, g: (b, 0, gate_col // go + g)),
                  pl.BlockSpec((1, gi, go), lambda b, g: (g, 0, 0)),
                  pl.BlockSpec((1, go), lambda b, g: (0, g))],
        out_specs=pl.BlockSpec((1, t, go), lambda b, g: (b, 0, g)),
        out_shape=jax.ShapeDtypeStruct((bsz, t, dm), BF16),
        scratch_shapes=[pltpu.VMEM((t + 2 * POOL_PAD, gi), F32)],
        compiler_params=_params("arbitrary", "arbitrary"),
        name="pool_branch",
    )(z, z, pool_w, pool_scale)


def _merge_kernel(y_ref, bo_ref, lo_ref, gup_ref, gng_ref, gnb_ref, gate_ref, yb_ref, o_ref):
    gd_lo = 2 * LORA_W
    esum = _head_sum_matrix()
    inv_n = 1.0 / HEAD_SIZE
    gd = jax.nn.sigmoid(lo_ref[0, :, gd_lo:gd_lo + GATE_LORA_PAD]).astype(BF16)

    def group_body(gidx, carry):
        cols = [pl.ds(pl.multiple_of((gidx * MERGE_GROUP + j) * LANES, LANES), LANES) for j in range(MERGE_GROUP)]
        y = [y_ref[0, 0, :, cs] + y_ref[1, 0, :, cs] for cs in cols]
        mu = [jnp.dot(x.astype(BF16), esum, preferred_element_type=F32) * inv_n for x in y]
        g = [jnp.dot(gd, gup_ref[:, cs], preferred_element_type=F32) for cs in cols]
        yc = _gmap(lambda x, m: x - m, y, mu)
        var = [jnp.dot((x * x).astype(BF16), esum, preferred_element_type=F32) * inv_n for x in yc]
        for cs, x, vr, gi in zip(cols, yc, var, g):
            yn = x * lax.rsqrt(vr + GN_EPS) * gng_ref[:, cs] + gnb_ref[:, cs]
            yn = yn + bo_ref[0, 0, :, cs] + bo_ref[1, 0, :, cs]
            ya = jax.nn.sigmoid(gate_ref[0, :, cs]) * (yn * gi)
            o_ref[0, :, cs] = (ya + yb_ref[0, :, cs].astype(F32)).astype(o_ref.dtype)
        return carry

    lax.fori_loop(0, o_ref.shape[2] // (LANES * MERGE_GROUP), group_body, 0)


def _merge(y, bo, z, gate_col, g_up, ln_g, ln_b, yb, tt=128):
    _, bsz, t, dm = y.shape
    n = z.shape[-1]
    lora_block = (n - LORA_COLS) // LORA_COLS
    return pl.pallas_call(
        _merge_kernel,
        grid=(bsz, t // tt),
        in_specs=[pl.BlockSpec((2, 1, tt, dm), lambda b, i: (0, b, i, 0)),
                  pl.BlockSpec((2, 1, tt, dm), lambda b, i: (0, b, i, 0)),
                  pl.BlockSpec((1, tt, LORA_COLS), lambda b, i: (b, i, lora_block)),
                  pl.BlockSpec((GATE_LORA_PAD, dm), lambda b, i: (0, 0)),
                  pl.BlockSpec((1, dm), lambda b, i: (0, 0)),
                  pl.BlockSpec((1, dm), lambda b, i: (0, 0)),
                  pl.BlockSpec((1, tt, dm), lambda b, i: (b, i, gate_col // dm)),
                  pl.BlockSpec((1, tt, dm), lambda b, i: (b, i, 0))],
        out_specs=pl.BlockSpec((1, tt, dm), lambda b, i: (b, i, 0)),
        out_shape=jax.ShapeDtypeStruct((bsz, t, dm), BF16),
        compiler_params=_params("arbitrary", "arbitrary"),
        name="wkv_merge",
    )(y, bo, z, g_up, ln_g, ln_b, z, yb)


def _xattn_kernel(q_ref, k_ref, v_ref, o_ref, *, head_dim):
    scale = head_dim ** -0.5
    nt_dims = (((1,), (1,)), ((), ()))
    for h in range(X_HEADS):
        cs = slice(h * head_dim, (h + 1) * head_dim)
        s = lax.dot_general(q_ref[0, :, cs], k_ref[0, :, cs], nt_dims, preferred_element_type=F32) * scale
        m = jnp.max(s, axis=-1, keepdims=True)
        e = jnp.exp(s - m)
        p = e / jnp.sum(e, axis=-1, keepdims=True)
        o_ref[0, :, cs] = jnp.dot(p.astype(BF16), v_ref[0, :, cs], preferred_element_type=F32).astype(o_ref.dtype)


def _xattn(q, k, v, tq=512):
    bsz, t, dm = q.shape
    m = k.shape[1]
    kern = functools.partial(_xattn_kernel, head_dim=dm // X_HEADS)
    return pl.pallas_call(
        kern,
        grid=(bsz, t // tq),
        in_specs=[pl.BlockSpec((1, tq, dm), lambda b, i: (b, i, 0)),
                  pl.BlockSpec((1, m, dm), lambda b, i: (b, 0, 0)),
                  pl.BlockSpec((1, m, dm), lambda b, i: (b, 0, 0))],
        out_specs=pl.BlockSpec((1, tq, dm), lambda b, i: (b, i, 0)),
        out_shape=jax.ShapeDtypeStruct((bsz, t, dm), BF16),
        compiler_params=_params("arbitrary", "arbitrary"),
        name="xattn",
    )(q, k, v)


def _pad_to(x, axis, size):
    pad = [(0, 0)] * x.ndim
    pad[axis] = (0, size - x.shape[axis])
    return jnp.pad(x, pad)


def _round_up(x, m):
    return (x + m - 1) // m * m


def _in_proj_operands(p, dm):
    w_in, shift_w = p['w_in'], p['shift_w']
    gate_lora = p['g_up'].shape[0]
    c_rkv = 3 * dm
    c_lora = c_rkv + 2 * LORA_W + gate_lora
    pool_width = p['pool_w'].shape[0] * p['pool_w'].shape[1]
    c_pool = c_lora + pool_width
    w = jnp.concatenate([w_in[:, :c_rkv], w_in[:, c_pool:], w_in[:, c_lora:c_pool],
                         _pad_to(w_in[:, c_rkv:c_lora], 1, LORA_COLS)], axis=1).astype(BF16)
    ident = jnp.zeros((3, 2 * dm + pool_width), F32).at[1].set(1.0)
    taps = jnp.concatenate([shift_w[:, :c_rkv], ident, _pad_to(shift_w[:, c_rkv:c_lora], 1, LORA_COLS)], axis=1)
    return w, taps, c_rkv, c_rkv + dm, c_rkv + 2 * dm


def _trunk(xs, mem, lp, norm_final_g):
    t, dm = xs[0].shape[1:]
    rows = [x.shape[0] * t for x in xs]
    bsz = sum(x.shape[0] for x in xs)
    n_mem = mem.shape[1]
    m_tok = bsz * t
    hs = tuple(x.reshape(-1, dm) for x in xs)
    memf = mem.reshape(bsz * n_mem, dm)
    depth = lp['w_in'].shape[0]
    h = None
    for l in range(depth):
        p = {name: arr[l] for name, arr in lp.items()}
        w_all, taps_all, gate_a_col, gate_b_col, pool_col = _in_proj_operands(p, dm)
        g_up = _pad_to(p['g_up'], 0, GATE_LORA_PAD).astype(BF16)

        if h is None:
            xn = _rmsnorm2(hs[0], hs[1], p['norm_mix_g'], BF16)
            res = hs
        else:
            xn = _rmsnorm(h, p['norm_mix_g'], BF16)
            res = h
        z = _in_proj(xn, w_all, taps_all, bsz, t)

        w_up = jnp.stack([_split_weight(p['w_up_f']), _split_weight(p['w_up_b'])])
        a_up = jnp.stack([_split_weight(p['a_up_f']), _split_weight(p['a_up_b'])])
        w0 = jnp.stack([p['w0_f'], p['w0_b']]).reshape(2, 1, dm)
        a0 = jnp.stack([p['a0_f'], p['a0_b']]).reshape(2, 1, dm)
        y, bo = _wkv_scan(z, dm, w_up, a_up, w0, a0,
                          p['k_k'].reshape(1, dm), p['k_a'].reshape(1, dm), p['r_k'].reshape(1, dm))
        yb = _pool_branch(z, pool_col, gate_b_col, p['pool_w'].astype(BF16), p['pool_scale'].reshape(1, dm))
        merged = _merge(y, bo, z, gate_a_col, g_up, p['ln_x_g'].reshape(1, dm), p['ln_x_b'].reshape(1, dm), yb)
        h = _matmul(merged.reshape(m_tok, dm), p['w_out'].astype(BF16), F32, 1024, 512,
                    residual=res, name="out_proj")

        hn = _rmsnorm(h, p['norm_x_g'], BF16)
        mn = _rmsnorm(memf, p['norm_mem_g'], BF16)
        q = _matmul(hn, p['xq'].astype(BF16), BF16, 1024, 512, name="xq")
        kx = _matmul(mn, p['xk'].astype(BF16), BF16, 1024, 512, name="xk")
        vx = _matmul(mn, p['xv'].astype(BF16), BF16, 1024, 512, name="xv")
        o = _xattn(q.reshape(bsz, t, dm), kx.reshape(bsz, n_mem, dm), vx.reshape(bsz, n_mem, dm))
        h = _matmul(o.reshape(m_tok, dm), p['xo'].astype(BF16), F32, 1024, 512, residual=h, name="xo")

        hn = _rmsnorm(h, p['norm_ffn_g'], BF16)
        hidden = p['ffn_w2'].shape[0]
        hidden_pad = _round_up(hidden, 512)
        w13 = p['ffn_w13']
        w13p = jnp.concatenate([_pad_to(w13[:, :hidden], 1, hidden_pad),
                                _pad_to(w13[:, hidden:], 1, hidden_pad)], axis=1).astype(BF16)
        w2p = _pad_to(p['ffn_w2'], 0, hidden_pad).astype(BF16)
        act = _swiglu_up(hn, w13p, hidden_pad, 1024, 512)
        h = _matmul(act, w2p, F32, 512, 256, residual=h, name="ffn_down")

    tm = 256
    outs, off = [], 0
    for x, nrow in zip(xs, rows):
        y = _rmsnorm(h, norm_final_g, F32, tm=tm, row_block_offset=off // tm, n_row_blocks=nrow // tm)
        outs.append(y.reshape(x.shape))
        off += nrow
    return tuple(outs)


def kernel(x_prompt, x_sample, mem_prompt, mem_sample, norm_mix_g, w_in, shift_w, w0_f, w_up_f, w0_b, w_up_b, a0_f, a_up_f, a0_b, a_up_b, g_up, k_k, k_a, r_k, ln_x_g, ln_x_b, pool_w, pool_scale, w_out, norm_x_g, norm_mem_g, xq, xk, xv, xo, norm_ffn_g, ffn_w13, ffn_w2, norm_final_g):
    assert x_prompt.shape[1:] == x_sample.shape[1:] and mem_prompt.shape[1:] == mem_sample.shape[1:]
    lp = {
        'norm_mix_g': norm_mix_g, 'w_in': w_in, 'shift_w': shift_w,
        'w0_f': w0_f, 'w_up_f': w_up_f, 'w0_b': w0_b, 'w_up_b': w_up_b,
        'a0_f': a0_f, 'a_up_f': a_up_f, 'a0_b': a0_b, 'a_up_b': a_up_b,
        'g_up': g_up, 'k_k': k_k, 'k_a': k_a, 'r_k': r_k.reshape(r_k.shape[0], -1),
        'ln_x_g': ln_x_g, 'ln_x_b': ln_x_b,
        'pool_w': pool_w, 'pool_scale': pool_scale, 'w_out': w_out,
        'norm_x_g': norm_x_g, 'norm_mem_g': norm_mem_g, 'xq': xq, 'xk': xk, 'xv': xv, 'xo': xo,
        'norm_ffn_g': norm_ffn_g, 'ffn_w13': ffn_w13, 'ffn_w2': ffn_w2,
    }
    mem = jnp.concatenate([mem_prompt, mem_sample], axis=0)
    return _trunk((x_prompt, x_sample), mem, lp, norm_final_g)
```

```python
import functools
import math

import jax
import jax.numpy as jnp
from jax import lax
from jax.experimental import pallas as pl
from jax.experimental.pallas import tpu as pltpu

F32 = jnp.float32
BF16 = jnp.bfloat16

LANES = 128
SUBLANES = 8
VMEM_LIMIT_BYTES = 56 * 1024 * 1024

HEAD_SIZE = 64
HEAD_SHIFT = 6
X_HEADS = 4
POOL_WINDOWS = (2, 4, 8, 16)
POOL_PAD = 16
GN_EPS = 64e-5
NORM_EPS = 1e-6
DECAY_SCALE = -math.exp(-0.5)
WKV_CHUNK = 64
WKV_GROUP = 16
MERGE_GROUP = 8
LORA_W = 128
GATE_LORA_PAD = 512
LORA_COLS = 1024
FFN_TM = 2048
FFN_TN = 256
IN_TN = 256
IN_ROW_CHUNK = 512


def _params(*semantics):
    return pltpu.CompilerParams(dimension_semantics=semantics, vmem_limit_bytes=VMEM_LIMIT_BYTES)


def _rms(x, g):
    ms = jnp.mean(x * x, axis=-1, keepdims=True)
    return x * lax.rsqrt(ms + NORM_EPS) * g


def _rmsnorm_kernel(x_ref, g_ref, o_ref):
    o_ref[...] = _rms(x_ref[...], g_ref[...]).astype(o_ref.dtype)


def _rmsnorm(x, g, out_dtype, tm=256, row_block_offset=0, n_row_blocks=None):
    m, d = x.shape
    nb = m // tm if n_row_blocks is None else n_row_blocks
    return pl.pallas_call(
        _rmsnorm_kernel,
        grid=(nb,),
        in_specs=[pl.BlockSpec((tm, d), lambda i: (i + row_block_offset, 0)),
                  pl.BlockSpec((1, d), lambda i: (0, 0))],
        out_specs=pl.BlockSpec((tm, d), lambda i: (i, 0)),
        out_shape=jax.ShapeDtypeStruct((nb * tm, d), out_dtype),
        compiler_params=_params("parallel"),
        name="rmsnorm",
    )(x, g.reshape(1, d))


def _rmsnorm2_kernel(xa_ref, xb_ref, g_ref, o_ref, *, na):
    x = jnp.where(pl.program_id(0) < na, xa_ref[...], xb_ref[...])
    o_ref[...] = _rms(x, g_ref[...]).astype(o_ref.dtype)


def _rmsnorm2(xa, xb, g, out_dtype, tm=256):
    d = xa.shape[1]
    na, nb = xa.shape[0] // tm, xb.shape[0] // tm
    return pl.pallas_call(
        functools.partial(_rmsnorm2_kernel, na=na),
        grid=(na + nb,),
        in_specs=[pl.BlockSpec((tm, d), lambda i: (jnp.minimum(i, na - 1), 0)),
                  pl.BlockSpec((tm, d), lambda i: (jnp.maximum(i - na, 0), 0)),
                  pl.BlockSpec((1, d), lambda i: (0, 0))],
        out_specs=pl.BlockSpec((tm, d), lambda i: (i, 0)),
        out_shape=jax.ShapeDtypeStruct(((na + nb) * tm, d), out_dtype),
        compiler_params=_params("arbitrary"),
        name="rmsnorm2",
    )(xa, xb, g.reshape(1, d))


def _matmul_kernel(x_ref, w_ref, o_ref):
    o_ref[...] = jnp.dot(x_ref[...], w_ref[...], preferred_element_type=F32).astype(o_ref.dtype)


def _matmul_res_kernel(x_ref, w_ref, r_ref, o_ref):
    acc = jnp.dot(x_ref[...], w_ref[...], preferred_element_type=F32)
    o_ref[...] = (r_ref[...] + acc).astype(o_ref.dtype)


def _matmul_res2_kernel(x_ref, w_ref, ra_ref, rb_ref, o_ref, *, na):
    acc = jnp.dot(x_ref[...], w_ref[...], preferred_element_type=F32)
    res = jnp.where(pl.program_id(0) < na, ra_ref[...], rb_ref[...])
    o_ref[...] = (res + acc).astype(o_ref.dtype)


def _matmul(x, w, out_dtype, tm, tn, residual=None, name="matmul"):
    m, k = x.shape
    n = w.shape[1]
    in_specs = [pl.BlockSpec((tm, k), lambda i, j: (i, 0)),
                pl.BlockSpec((k, tn), lambda i, j: (0, j))]
    args = [x, w]
    body = _matmul_kernel
    if isinstance(residual, tuple):
        ra, rb = residual
        na = ra.shape[0] // tm
        in_specs += [pl.BlockSpec((tm, tn), lambda i, j: (jnp.minimum(i, na - 1), j)),
                     pl.BlockSpec((tm, tn), lambda i, j: (jnp.maximum(i - na, 0), j))]
        args += [ra, rb]
        body = functools.partial(_matmul_res2_kernel, na=na)
    elif residual is not None:
        in_specs.append(pl.BlockSpec((tm, tn), lambda i, j: (i, j)))
        args.append(residual)
        body = _matmul_res_kernel
    return pl.pallas_call(
        body,
        grid=(m // tm, n // tn),
        in_specs=in_specs,
        out_specs=pl.BlockSpec((tm, tn), lambda i, j: (i, j)),
        out_shape=jax.ShapeDtypeStruct((m, n), out_dtype),
        compiler_params=_params("arbitrary", "arbitrary"),
        name=name,
    )(*args)


def _in_proj_kernel(x_ref, w_ref, taps_ref, o_ref):
    t = x_ref.shape[0]
    rc = IN_ROW_CHUNK
    w = w_ref[...]
    taps = taps_ref[...]
    ridx = lax.broadcasted_iota(jnp.int32, (rc, w.shape[1]), 0)
    zs = [jnp.dot(x_ref[c * rc:(c + 1) * rc, :], w, preferred_element_type=F32) for c in range(t // rc)]
    zero_row = jnp.zeros((1, w.shape[1]), F32)
    for c, z in enumerate(zs):
        prev_row = zs[c - 1][rc - 1:rc, :] if c > 0 else zero_row
        next_row = zs[c + 1][0:1, :] if c + 1 < len(zs) else zero_row
        zm1 = jnp.where(ridx == 0, prev_row, pltpu.roll(z, 1, axis=0))
        zp1 = jnp.where(ridx == rc - 1, next_row, pltpu.roll(z, rc - 1, axis=0))
        o_ref[0, c * rc:(c + 1) * rc, :] = zm1 * taps[0:1, :] + z * taps[1:2, :] + zp1 * taps[2:3, :]


def _in_proj(xn, w, taps, bsz, t):
    k = xn.shape[1]
    n = w.shape[1]
    return pl.pallas_call(
        _in_proj_kernel,
        grid=(bsz, n // IN_TN),
        in_specs=[pl.BlockSpec((t, k), lambda b, j: (b, 0)),
                  pl.BlockSpec((k, IN_TN), lambda b, j: (0, j)),
                  pl.BlockSpec((3, IN_TN), lambda b, j: (0, j))],
        out_specs=pl.BlockSpec((1, t, IN_TN), lambda b, j: (b, 0, j)),
        out_shape=jax.ShapeDtypeStruct((bsz, t, n), F32),
        compiler_params=_params("arbitrary", "arbitrary"),
        name="in_proj",
    )(xn, w, taps)


def _swiglu_kernel(x_ref, wg_ref, wu_ref, o_ref):
    x = x_ref[...]
    gate = jnp.dot(x, wg_ref[...], preferred_element_type=F32)
    up = jnp.dot(x, wu_ref[...], preferred_element_type=F32)
    o_ref[...] = (gate * jax.nn.sigmoid(gate) * up).astype(o_ref.dtype)


def _swiglu_up(x, w13, hidden, tm, tn):
    m, k = x.shape
    nb = hidden // tn
    return pl.pallas_call(
        _swiglu_kernel,
        grid=(m // tm, nb),
        in_specs=[pl.BlockSpec((tm, k), lambda i, j: (i, 0)),
                  pl.BlockSpec((k, tn), lambda i, j: (0, j)),
                  pl.BlockSpec((k, tn), lambda i, j: (0, j + nb))],
        out_specs=pl.BlockSpec((tm, tn), lambda i, j: (i, j)),
        out_shape=jax.ShapeDtypeStruct((m, hidden), BF16),
        compiler_params=_params("arbitrary", "arbitrary"),
        name="swiglu_up",
    )(x, w13, w13)


def _head_sum_matrix():
    r = lax.broadcasted_iota(jnp.int32, (LANES, LANES), 0) >> HEAD_SHIFT
    c = lax.broadcasted_iota(jnp.int32, (LANES, LANES), 1) >> HEAD_SHIFT
    return jnp.where(r == c, 1.0, 0.0).astype(BF16)


def _split2(x):
    hi = x.astype(BF16)
    return hi, (x - hi.astype(F32)).astype(BF16)


def _gmap(f, *lists):
    return [f(*xs) for xs in zip(*lists)]


def _wkv_kernel(z_ref, lo_ref, wup_ref, aup_ref, w0_ref, a0_ref, kk_ref, ka_ref, rk_ref,
                y_ref, bo_ref, state_ref, lw_ref, cum_ref, icl_ref, *, d_model):
    C = WKV_CHUNK
    d = pl.program_id(1)
    c = pl.program_id(2)
    sgn = 1 - 2 * d

    @pl.when(c == 0)
    def _():
        state_ref[...] = jnp.zeros_like(state_ref)

    wl = w0_ref[0] + jnp.dot(jnp.tanh(lo_ref[0, :, 0:LORA_W]).astype(BF16), wup_ref[0],
                             preferred_element_type=F32)
    lw = DECAY_SCALE * jax.nn.sigmoid(wl)
    lw_ref[...] = lw
    r64 = lax.broadcasted_iota(jnp.int32, (C, C), 0)
    c64 = lax.broadcasted_iota(jnp.int32, (C, C), 1)
    tri = jnp.where((r64 - c64) * sgn >= 0, 1.0, 0.0).astype(BF16)
    lw_hi, lw_lo = _split2(lw)
    cum_ref[...] = (jnp.dot(tri, lw_hi, preferred_element_type=F32)
                    + jnp.dot(tri, lw_lo, preferred_element_type=F32))
    icl_ref[...] = jax.nn.sigmoid(
        a0_ref[0] + jnp.dot(lo_ref[0, :, LORA_W:2 * LORA_W].astype(BF16), aup_ref[0],
                            preferred_element_type=F32))

    row = lax.broadcasted_iota(jnp.int32, (C, LANES), 0)
    col = lax.broadcasted_iota(jnp.int32, (C, LANES), 1)
    colh = col & (HEAD_SIZE - 1)
    order = (row - colh) * sgn
    strict = order > 0
    incl = order >= 0
    eye2 = row == colh
    lane_lo = col < HEAD_SIZE
    esum = _head_sum_matrix()
    rr = lax.broadcasted_iota(jnp.int32, (LANES, LANES), 0) >> HEAD_SHIFT
    cc = lax.broadcasted_iota(jnp.int32, (LANES, LANES), 1) >> HEAD_SHIFT
    blockdiag = rr == cc

    def bd(x):
        zero = jnp.zeros_like(x)
        return jnp.concatenate([jnp.where(lane_lo, x, zero), jnp.where(lane_lo, zero, x)], axis=0)

    def pmul(x, y):
        return jnp.dot(x.astype(BF16), bd(y.astype(BF16)), preferred_element_type=F32)

    nt_dims = (((1,), (1,)), ((), ()))
    tn_dims = (((0,), (0,)), ((), ()))

    def group_body(g, carry):
        pairs = [g * WKV_GROUP + j for j in range(WKV_GROUP)]

        def lane_tile(p, offset=0):
            return pl.ds(pl.multiple_of(offset + p * LANES, LANES), LANES)

        cols = [lane_tile(p) for p in pairs]
        r = [z_ref[0, :, cs] for cs in cols]
        k = [z_ref[0, :, lane_tile(p, d_model)] for p in pairs]
        v = [z_ref[0, :, lane_tile(p, 2 * d_model)] for p in pairs]
        icl = [icl_ref[:, cs] for cs in cols]
        lwp = [lw_ref[:, cs] for cs in cols]
        cum = [cum_ref[:, cs] for cs in cols]
        tot = [jnp.sum(x, axis=0, keepdims=True) for x in lwp]

        def head_sums(xs):
            stacked = jnp.concatenate([x.astype(BF16) for x in xs], axis=0)
            sums = jnp.dot(stacked, esum, preferred_element_type=F32)
            return [sums[j * C:(j + 1) * C] for j in range(len(xs))]

        q = [ki * kk_ref[:, cs] for ki, cs in zip(k, cols)]
        n2 = head_sums([qi * qi for qi in q])
        kd = [ki * (1.0 + (ic - 1.0) * ka_ref[:, cs]) for ki, ic, cs in zip(k, icl, cols)]
        bsum = head_sums([ri * kdi * rk_ref[:, cs] for ri, kdi, cs in zip(r, kd, cols)])
        for cs, bs, vi in zip(cols, bsum, v):
            bo_ref[0, 0, :, cs] = bs * vi

        kk = [qi * lax.rsqrt(jnp.maximum(ni, 1e-12)) for qi, ni in zip(q, n2)]
        b = _gmap(lambda x, ic: x * ic, kk, icl)
        e_out = [jnp.exp(-x) for x in cum]
        e_tot = _gmap(lambda t_, x: jnp.exp(t_ - x), tot, cum)
        at = _gmap(lambda x, cm, lw_: (-x * jnp.exp(cm - lw_)).astype(BF16), kk, cum, lwp)
        rt = _gmap(lambda x, cm: (x * jnp.exp(cm)).astype(BF16), r, cum)
        bt = _gmap(lambda x, e: (x * e).astype(BF16), b, e_out)
        kt = _gmap(lambda x, e: (x * e).astype(BF16), kd, e_out)
        bw = _gmap(lambda x, e: (x * e).astype(BF16), b, e_tot)
        kw = _gmap(lambda x, e: (x * e).astype(BF16), kd, e_tot)
        vb = [x.astype(BF16) for x in v]

        lhs = _gmap(lambda a_, r_: jnp.concatenate([a_, r_], axis=0), at, rt)
        rhs_t = _gmap(lambda b_, k_: jnp.concatenate([bd(b_), bd(k_)], axis=0), bt, kt)
        pmat = _gmap(lambda l_, r_: lax.dot_general(l_, r_, nt_dims, preferred_element_type=F32), lhs, rhs_t)
        h = [state_ref[p] for p in pairs]
        hs = _gmap(lambda l_, h_: jnp.dot(l_, h_.astype(BF16), preferred_element_type=F32),
                   lhs, h)
        lab = [jnp.where(strict, x[:C, :LANES], 0.0) for x in pmat]
        rb = [jnp.where(incl, x[C:, :LANES], 0.0) for x in pmat]
        lrk = [jnp.concatenate([jnp.where(strict, x[:C, LANES:], 0.0).astype(BF16),
                                jnp.where(incl, x[C:, LANES:], 0.0).astype(BF16)], axis=0) for x in pmat]
        vst = [bd(x) for x in vb]
        lrkv = _gmap(lambda l_, v_: jnp.dot(l_, v_, preferred_element_type=F32), lrk, vst)
        rhs_u = _gmap(lambda h_, x: h_[:C] + x[:C], hs, lrkv)

        ident = jnp.where(eye2, 1.0, 0.0)
        first = (row >> 1) == (colh >> 1)
        tinv = [ident + jnp.where(first, x, 0.0) for x in lab]
        s = 2
        while s < C:
            sh = s.bit_length() - 1
            level = ((row >> (sh + 1)) == (colh >> (sh + 1))) & ((row >> sh) != (colh >> sh))
            off = [jnp.where(level, x, 0.0) for x in lab]
            tmp = _gmap(pmul, tinv, off)
            upd_t = _gmap(pmul, tmp, tinv)
            tinv = _gmap(lambda t_, x: t_ + x, tinv, upd_t)
            s *= 2
        u = _gmap(pmul, tinv, rhs_u)

        ub = [x.astype(BF16) for x in u]
        rbu = _gmap(pmul, rb, ub)
        for cs, h_, xv, xu in zip(cols, hs, lrkv, rbu):
            y_ref[0, 0, :, cs] = h_[C:] + xv[C:] + xu
        upd = _gmap(lambda b_, k_, u_, v_: lax.dot_general(
            jnp.concatenate([b_, k_], axis=0), jnp.concatenate([u_, v_], axis=0),
            tn_dims, preferred_element_type=F32), bw, kw, ub, vb)
        for p, h_, t_, x in zip(pairs, h, tot, upd):
            decay_rows = jnp.broadcast_to(jnp.exp(t_), (LANES, LANES)).T
            state_ref[p] = h_ * decay_rows + jnp.where(blockdiag, x, 0.0)
        return carry

    lax.fori_loop(0, d_model // (LANES * WKV_GROUP), group_body, 0)


def _wkv_scan(z, dm, w_up, a_up, w0, a0, k_k, k_a, r_k):
    bsz, t, n = z.shape
    C = WKV_CHUNK
    nc = t // C
    lora_block = (n - LORA_COLS) // LORA_COLS

    def tchunk(dd, cc):
        return jnp.where(dd == 0, cc, nc - 1 - cc)

    def dir_map(bb, dd, cc):
        return (dd, 0, 0)

    def const2(bb, dd, cc):
        return (0, 0)

    def out_map(bb, dd, cc):
        return (dd, bb, tchunk(dd, cc), 0)

    out_sds = jax.ShapeDtypeStruct((2, bsz, t, dm), F32)
    kern = functools.partial(_wkv_kernel, d_model=dm)
    return pl.pallas_call(
        kern,
        grid=(bsz, 2, nc),
        in_specs=[
            pl.BlockSpec((1, C, 3 * dm), lambda bb, dd, cc: (bb, tchunk(dd, cc), 0)),
            pl.BlockSpec((1, C, LORA_COLS), lambda bb, dd, cc: (bb, tchunk(dd, cc), lora_block)),
            pl.BlockSpec((1, LORA_W, dm), dir_map),
            pl.BlockSpec((1, LORA_W, dm), dir_map),
            pl.BlockSpec((1, 1, dm), dir_map),
            pl.BlockSpec((1, 1, dm), dir_map),
            pl.BlockSpec((1, dm), const2),
            pl.BlockSpec((1, dm), const2),
            pl.BlockSpec((1, dm), const2),
        ],
        out_specs=[pl.BlockSpec((1, 1, C, dm), out_map), pl.BlockSpec((1, 1, C, dm), out_map)],
        out_shape=[out_sds, out_sds],
        scratch_shapes=[pltpu.VMEM((dm // LANES, LANES, LANES), F32),
                        pltpu.VMEM((C, dm), F32), pltpu.VMEM((C, dm), F32), pltpu.VMEM((C, dm), F32)],
        compiler_params=_params("arbitrary", "arbitrary", "arbitrary"),
        name="wkv_scan",
    )(z, z, w_up, a_up, w0, a0, k_k, k_a, r_k)


def _pool_kernel(p_ref, gate_ref, w_ref, scale_ref, o_ref, pad_ref, *, rows):
    g = pl.program_id(1)
    t, gi = p_ref.shape[1], p_ref.shape[2]
    zeros = jnp.zeros((POOL_PAD, gi), F32)
    pad_ref[0:POOL_PAD, :] = zeros
    pad_ref[POOL_PAD + t:POOL_PAD + t + POOL_PAD, :] = zeros
    pad_ref[POOL_PAD:POOL_PAD + t, :] = p_ref[0]
    w = w_ref[0]
    scale = scale_ref[...]

    for gidx, win in enumerate(POOL_WINDOWS):
        @pl.when(g == gidx)
        def _(win=win):
            half = win // 2

            def tile_body(i, carry):
                r0 = pl.multiple_of(i * rows, rows)
                n = rows + 2 * SUBLANES
                xt = pad_ref[pl.ds(r0 + POOL_PAD - SUBLANES, n), :]
                acc = xt
                step = 1
                while step < win:
                    acc = acc + pltpu.roll(acc, n - step, axis=0)
                    step *= 2
                if SUBLANES - half:
                    acc = pltpu.roll(acc, n - (SUBLANES - half), axis=0)
                acc = acc[0:rows]
                tt = r0 + lax.broadcasted_iota(jnp.int32, (rows, LANES), 0)
                cnt = (jnp.minimum(tt + (win - half), t) - jnp.maximum(tt - half, 0)).astype(F32)
                inv = 1.0 / cnt
                inv_full = jnp.concatenate([inv] * (gi // LANES), axis=1)
                dlt = acc * inv_full - xt[SUBLANES:SUBLANES + rows]
                out = jnp.dot(dlt.astype(BF16), w, preferred_element_type=F32) * scale
                gate = jax.nn.sigmoid(gate_ref[0, pl.ds(r0, rows), :])
                o_ref[0, pl.ds(r0, rows), :] = (gate * out).astype(o_ref.dtype)
                return carry

            lax.fori_loop(0, t // rows, tile_body, 0)


def _pool_branch(z, pool_col, gate_col, pool_w, pool_scale, rows=256):
    bsz, t, _ = z.shape
    ng, gi, go = pool_w.shape
    dm = ng * go
    kern = functools.partial(_pool_kernel, rows=rows)
    return pl.pallas_call(
        kern,
        grid=(bsz, ng),
        in_specs=[pl.BlockSpec((1, t, gi), lambda b, g: (b, 0, pool_col // gi + g)),
                  pl.BlockSpec((1, t, go), lambda b, g: (b, 0, gate_col // go + g)),
                  pl.BlockSpec((1, gi, go), lambda b, g: (g, 0, 0)),
                  pl.BlockSpec((1, go), lambda b, g: (0, g))],
        out_specs=pl.BlockSpec((1, t, go), lambda b, g: (b, 0, g)),
        out_shape=jax.ShapeDtypeStruct((bsz, t, dm), BF16),
        scratch_shapes=[pltpu.VMEM((t + 2 * POOL_PAD, gi), F32)],
        compiler_params=_params("arbitrary", "arbitrary"),
        name="pool_branch",
    )(z, z, pool_w, pool_scale)


def _merge_kernel(y_ref, bo_ref, lo_ref, gup_ref, gng_ref, gnb_ref, gate_ref, yb_ref, o_ref):
    gd_lo = 2 * LORA_W
    esum = _head_sum_matrix()
    inv_n = 1.0 / HEAD_SIZE
    gd = jax.nn.sigmoid(lo_ref[0, :, gd_lo:gd_lo + GATE_LORA_PAD]).astype(BF16)

    def group_body(gidx, carry):
        cols = [pl.ds(pl.multiple_of((gidx * MERGE_GROUP + j) * LANES, LANES), LANES) for j in range(MERGE_GROUP)]
        y = [y_ref[0, 0, :, cs] + y_ref[1, 0, :, cs] for cs in cols]
        mu = [jnp.dot(x.astype(BF16), esum, preferred_element_type=F32) * inv_n for x in y]
        g = [jnp.dot(gd, gup_ref[:, cs], preferred_element_type=F32) for cs in cols]
        yc = _gmap(lambda x, m: x - m, y, mu)
        var = [jnp.dot((x * x).astype(BF16), esum, preferred_element_type=F32) * inv_n for x in yc]
        for cs, x, vr, gi in zip(cols, yc, var, g):
            yn = x * lax.rsqrt(vr + GN_EPS) * gng_ref[:, cs] + gnb_ref[:, cs]
            yn = yn + bo_ref[0, 0, :, cs] + bo_ref[1, 0, :, cs]
            ya = jax.nn.sigmoid(gate_ref[0, :, cs]) * (yn * gi)
            o_ref[0, :, cs] = (ya + yb_ref[0, :, cs].astype(F32)).astype(o_ref.dtype)
        return carry

    lax.fori_loop(0, o_ref.shape[2] // (LANES * MERGE_GROUP), group_body, 0)


def _merge(y, bo, z, gate_col, g_up, ln_g, ln_b, yb, tt=128):
    _, bsz, t, dm = y.shape
    n = z.shape[-1]
    lora_block = (n - LORA_COLS) // LORA_COLS
    return pl.pallas_call(
        _merge_kernel,
        grid=(bsz, t // tt),
        in_specs=[pl.BlockSpec((2, 1, tt, dm), lambda b, i: (0, b, i, 0)),
                  pl.BlockSpec((2, 1, tt, dm), lambda b, i: (0, b, i, 0)),
                  pl.BlockSpec((1, tt, LORA_COLS), lambda b, i: (b, i, lora_block)),
                  pl.BlockSpec((GATE_LORA_PAD, dm), lambda b, i: (0, 0)),
                  pl.BlockSpec((1, dm), lambda b, i: (0, 0)),
                  pl.BlockSpec((1, dm), lambda b, i: (0, 0)),
                  pl.BlockSpec((1, tt, dm), lambda b, i: (b, i, gate_col // dm)),
                  pl.BlockSpec((1, tt, dm), lambda b, i: (b, i, 0))],
        out_specs=pl.BlockSpec((1, tt, dm), lambda b, i: (b, i, 0)),
        out_shape=jax.ShapeDtypeStruct((bsz, t, dm), BF16),
        compiler_params=_params("arbitrary", "arbitrary"),
        name="wkv_merge",
    )(y, bo, z, g_up, ln_g, ln_b, z, yb)


def _xattn_kernel(q_ref, k_ref, v_ref, o_ref, *, head_dim):
    scale = head_dim ** -0.5
    nt_dims = (((1,), (1,)), ((), ()))
    for h in range(X_HEADS):
        cs = slice(h * head_dim, (h + 1) * head_dim)
        s = lax.dot_general(q_ref[0, :, cs], k_ref[0, :, cs], nt_dims, preferred_element_type=F32) * scale
        m = jnp.max(s, axis=-1, keepdims=True)
        e = jnp.exp(s - m)
        p = e / jnp.sum(e, axis=-1, keepdims=True)
        o_ref[0, :, cs] = jnp.dot(p.astype(BF16), v_ref[0, :, cs], preferred_element_type=F32).astype(o_ref.dtype)


def _xattn(q, k, v, tq=512):
    bsz, t, dm = q.shape
    m = k.shape[1]
    kern = functools.partial(_xattn_kernel, head_dim=dm // X_HEADS)
    return pl.pallas_call(
        kern,
        grid=(bsz, t // tq),
        in_specs=[pl.BlockSpec((1, tq, dm), lambda b, i: (b, i, 0)),
                  pl.BlockSpec((1, m, dm), lambda b, i: (b, 0, 0)),
                  pl.BlockSpec((1, m, dm), lambda b, i: (b, 0, 0))],
        out_specs=pl.BlockSpec((1, tq, dm), lambda b, i: (b, i, 0)),
        out_shape=jax.ShapeDtypeStruct((bsz, t, dm), BF16),
        compiler_params=_params("arbitrary", "arbitrary"),
        name="xattn",
    )(q, k, v)


def _pad_to(x, axis, size):
    pad = [(0, 0)] * x.ndim
    pad[axis] = (0, size - x.shape[axis])
    return jnp.pad(x, pad)


def _round_up(x, m):
    return (x + m - 1) // m * m


def _in_proj_operands(p, dm):
    w_in, shift_w = p['w_in'], p['shift_w']
    gate_lora = p['g_up'].shape[0]
    c_rkv = 3 * dm
    c_lora = c_rkv + 2 * LORA_W + gate_lora
    pool_width = p['pool_w'].shape[0] * p['pool_w'].shape[1]
    c_pool = c_lora + pool_width
    w = jnp.concatenate([w_in[:, :c_rkv], w_in[:, c_pool:], w_in[:, c_lora:c_pool],
                         _pad_to(w_in[:, c_rkv:c_lora], 1, LORA_COLS)], axis=1).astype(BF16)
    ident = jnp.zeros((3, 2 * dm + pool_width), F32).at[1].set(1.0)
    taps = jnp.concatenate([shift_w[:, :c_rkv], ident, _pad_to(shift_w[:, c_rkv:c_lora], 1, LORA_COLS)], axis=1)
    return w, taps, c_rkv, c_rkv + dm, c_rkv + 2 * dm


def _trunk(xs, mem, lp, norm_final_g):
    t, dm = xs[0].shape[1:]
    rows = [x.shape[0] * t for x in xs]
    bsz = sum(x.shape[0] for x in xs)
    n_mem = mem.shape[1]
    m_tok = bsz * t
    hs = tuple(x.reshape(-1, dm) for x in xs)
    memf = mem.reshape(bsz * n_mem, dm)
    depth = lp['w_in'].shape[0]
    h = None
    for l in range(depth):
        p = {name: arr[l] for name, arr in lp.items()}
        w_all, taps_all, gate_a_col, gate_b_col, pool_col = _in_proj_operands(p, dm)
        g_up = _pad_to(p['g_up'], 0, GATE_LORA_PAD).astype(BF16)

        if h is None:
            xn = _rmsnorm2(hs[0], hs[1], p['norm_mix_g'], BF16)
            res = hs
        else:
            xn = _rmsnorm(h, p['norm_mix_g'], BF16)
            res = h
        z = _in_proj(xn, w_all, taps_all, bsz, t)

        w_up = jnp.stack([p['w_up_f'], p['w_up_b']]).astype(BF16)
        a_up = jnp.stack([p['a_up_f'], p['a_up_b']]).astype(BF16)
        w0 = jnp.stack([p['w0_f'], p['w0_b']]).reshape(2, 1, dm)
        a0 = jnp.stack([p['a0_f'], p['a0_b']]).reshape(2, 1, dm)
        y, bo = _wkv_scan(z, dm, w_up, a_up, w0, a0,
                          p['k_k'].reshape(1, dm), p['k_a'].reshape(1, dm), p['r_k'].reshape(1, dm))
        yb = _pool_branch(z, pool_col, gate_b_col, p['pool_w'].astype(BF16), p['pool_scale'].reshape(1, dm))
        merged = _merge(y, bo, z, gate_a_col, g_up, p['ln_x_g'].reshape(1, dm), p['ln_x_b'].reshape(1, dm), yb)
        h = _matmul(merged.reshape(m_tok, dm), p['w_out'].astype(BF16), F32, 1024, 512,
                    residual=res, name="out_proj")

        hn = _rmsnorm(h, p['norm_x_g'], BF16)
        mn = _rmsnorm(memf, p['norm_mem_g'], BF16)
        q = _matmul(hn, p['xq'].astype(BF16), BF16, 1024, 512, name="xq")
        kx = _matmul(mn, p['xk'].astype(BF16), BF16, 1024, 512, name="xk")
        vx = _matmul(mn, p['xv'].astype(BF16), BF16, 1024, 512, name="xv")
        o = _xattn(q.reshape(bsz, t, dm), kx.reshape(bsz, n_mem, dm), vx.reshape(bsz, n_mem, dm))
        h = _matmul(o.reshape(m_tok, dm), p['xo'].astype(BF16), F32, 1024, 512, residual=h, name="xo")

        hn = _rmsnorm(h, p['norm_ffn_g'], BF16)
        hidden = p['ffn_w2'].shape[0]
        act = _swiglu_up(hn, p['ffn_w13'].astype(BF16), hidden, FFN_TM, FFN_TN)
        h = _matmul(act, p['ffn_w2'].astype(BF16), F32, 512, 256, residual=h, name="ffn_down")

    tm = 256
    outs, off = [], 0
    for x, nrow in zip(xs, rows):
        y = _rmsnorm(h, norm_final_g, F32, tm=tm, row_block_offset=off // tm, n_row_blocks=nrow // tm)
        outs.append(y.reshape(x.shape))
        off += nrow
    return tuple(outs)


def kernel(x_prompt, x_sample, mem_prompt, mem_sample, norm_mix_g, w_in, shift_w, w0_f, w_up_f, w0_b, w_up_b, a0_f, a_up_f, a0_b, a_up_b, g_up, k_k, k_a, r_k, ln_x_g, ln_x_b, pool_w, pool_scale, w_out, norm_x_g, norm_mem_g, xq, xk, xv, xo, norm_ffn_g, ffn_w13, ffn_w2, norm_final_g):
    assert x_prompt.shape[1:] == x_sample.shape[1:] and mem_prompt.shape[1:] == mem_sample.shape[1:]
    lp = {
        'norm_mix_g': norm_mix_g, 'w_in': w_in, 'shift_w': shift_w,
        'w0_f': w0_f, 'w_up_f': w_up_f, 'w0_b': w0_b, 'w_up_b': w_up_b,
        'a0_f': a0_f, 'a_up_f': a_up_f, 'a0_b': a0_b, 'a_up_b': a_up_b,
        'g_up': g_up, 'k_k': k_k, 'k_a': k_a, 'r_k': r_k.reshape(r_k.shape[0], -1),
        'ln_x_g': ln_x_g, 'ln_x_b': ln_x_b,
        'pool_w': pool_w, 'pool_scale': pool_scale, 'w_out': w_out,
        'norm_x_g': norm_x_g, 'norm_mem_g': norm_mem_g, 'xq': xq, 'xk': xk, 'xv': xv, 'xo': xo,
        'norm_ffn_g': norm_ffn_g, 'ffn_w13': ffn_w13, 'ffn_w2': ffn_w2,
    }
    mem = jnp.concatenate([mem_prompt, mem_sample], axis=0)
    return _trunk((x_prompt, x_sample), mem, lp, norm_final_g)
```

```python
import functools
import math

import jax
import jax.numpy as jnp
from jax import lax
from jax.experimental import pallas as pl
from jax.experimental.pallas import tpu as pltpu

F32 = jnp.float32
BF16 = jnp.bfloat16

LANES = 128
SUBLANES = 8
VMEM_LIMIT_BYTES = 56 * 1024 * 1024

HEAD_SIZE = 64
HEAD_SHIFT = 6
X_HEADS = 4
POOL_WINDOWS = (2, 4, 8, 16)
POOL_PAD = 16
GN_EPS = 64e-5
NORM_EPS = 1e-6
DECAY_SCALE = -math.exp(-0.5)
WKV_CHUNK = 64
WKV_GROUP = 16
MERGE_GROUP = 8
LORA_W = 128
GATE_LORA_PAD = 512
LORA_COLS = 1024
FFN_TM = 1024
FFN_TN = 256
IN_TN = 256
IN_ROW_CHUNK = 512


def _params(*semantics):
    return pltpu.CompilerParams(dimension_semantics=semantics, vmem_limit_bytes=VMEM_LIMIT_BYTES)


def _rms(x, g):
    ms = jnp.mean(x * x, axis=-1, keepdims=True)
    return x * lax.rsqrt(ms + NORM_EPS) * g


def _rmsnorm_kernel(x_ref, g_ref, o_ref):
    o_ref[...] = _rms(x_ref[...], g_ref[...]).astype(o_ref.dtype)


def _rmsnorm(x, g, out_dtype, tm=256, row_block_offset=0, n_row_blocks=None):
    m, d = x.shape
    nb = m // tm if n_row_blocks is None else n_row_blocks
    return pl.pallas_call(
        _rmsnorm_kernel,
        grid=(nb,),
        in_specs=[pl.BlockSpec((tm, d), lambda i: (i + row_block_offset, 0)),
                  pl.BlockSpec((1, d), lambda i: (0, 0))],
        out_specs=pl.BlockSpec((tm, d), lambda i: (i, 0)),
        out_shape=jax.ShapeDtypeStruct((nb * tm, d), out_dtype),
        compiler_params=_params("parallel"),
        name="rmsnorm",
    )(x, g.reshape(1, d))


def _rmsnorm2_kernel(xa_ref, xb_ref, g_ref, o_ref, *, na):
    x = jnp.where(pl.program_id(0) < na, xa_ref[...], xb_ref[...])
    o_ref[...] = _rms(x, g_ref[...]).astype(o_ref.dtype)


def _rmsnorm2(xa, xb, g, out_dtype, tm=256):
    d = xa.shape[1]
    na, nb = xa.shape[0] // tm, xb.shape[0] // tm
    return pl.pallas_call(
        functools.partial(_rmsnorm2_kernel, na=na),
        grid=(na + nb,),
        in_specs=[pl.BlockSpec((tm, d), lambda i: (jnp.minimum(i, na - 1), 0)),
                  pl.BlockSpec((tm, d), lambda i: (jnp.maximum(i - na, 0), 0)),
                  pl.BlockSpec((1, d), lambda i: (0, 0))],
        out_specs=pl.BlockSpec((tm, d), lambda i: (i, 0)),
        out_shape=jax.ShapeDtypeStruct(((na + nb) * tm, d), out_dtype),
        compiler_params=_params("arbitrary"),
        name="rmsnorm2",
    )(xa, xb, g.reshape(1, d))


def _matmul_kernel(x_ref, w_ref, o_ref):
    o_ref[...] = jnp.dot(x_ref[...], w_ref[...], preferred_element_type=F32).astype(o_ref.dtype)


def _matmul_res_kernel(x_ref, w_ref, r_ref, o_ref):
    acc = jnp.dot(x_ref[...], w_ref[...], preferred_element_type=F32)
    o_ref[...] = (r_ref[...] + acc).astype(o_ref.dtype)


def _matmul(x, w, out_dtype, tm, tn, residual=None, name="matmul"):
    m, k = x.shape
    n = w.shape[1]
    in_specs = [pl.BlockSpec((tm, k), lambda i, j: (i, 0)),
                pl.BlockSpec((k, tn), lambda i, j: (0, j))]
    args = [x, w]
    body = _matmul_kernel
    if residual is not None:
        in_specs.append(pl.BlockSpec((tm, tn), lambda i, j: (i, j)))
        args.append(residual)
        body = _matmul_res_kernel
    return pl.pallas_call(
        body,
        grid=(m // tm, n // tn),
        in_specs=in_specs,
        out_specs=pl.BlockSpec((tm, tn), lambda i, j: (i, j)),
        out_shape=jax.ShapeDtypeStruct((m, n), out_dtype),
        compiler_params=_params("arbitrary", "arbitrary"),
        name=name,
    )(*args)


def _in_proj_kernel(x_ref, w_ref, taps_ref, o_ref):
    t = x_ref.shape[0]
    rc = IN_ROW_CHUNK
    w = w_ref[...]
    taps = taps_ref[...]
    ridx = lax.broadcasted_iota(jnp.int32, (rc, w.shape[1]), 0)
    zs = [jnp.dot(x_ref[c * rc:(c + 1) * rc, :], w, preferred_element_type=F32) for c in range(t // rc)]
    zero_row = jnp.zeros((1, w.shape[1]), F32)
    for c, z in enumerate(zs):
        prev_row = zs[c - 1][rc - 1:rc, :] if c > 0 else zero_row
        next_row = zs[c + 1][0:1, :] if c + 1 < len(zs) else zero_row
        zm1 = jnp.where(ridx == 0, prev_row, pltpu.roll(z, 1, axis=0))
        zp1 = jnp.where(ridx == rc - 1, next_row, pltpu.roll(z, rc - 1, axis=0))
        o_ref[0, c * rc:(c + 1) * rc, :] = zm1 * taps[0:1, :] + z * taps[1:2, :] + zp1 * taps[2:3, :]


def _in_proj(xn, w, taps, bsz, t):
    k = xn.shape[1]
    n = w.shape[1]
    return pl.pallas_call(
        _in_proj_kernel,
        grid=(bsz, n // IN_TN),
        in_specs=[pl.BlockSpec((t, k), lambda b, j: (b, 0)),
                  pl.BlockSpec((k, IN_TN), lambda b, j: (0, j)),
                  pl.BlockSpec((3, IN_TN), lambda b, j: (0, j))],
        out_specs=pl.BlockSpec((1, t, IN_TN), lambda b, j: (b, 0, j)),
        out_shape=jax.ShapeDtypeStruct((bsz, t, n), F32),
        compiler_params=_params("arbitrary", "arbitrary"),
        name="in_proj",
    )(xn, w, taps)


def _lane_tiled(x, width):
    return jnp.concatenate([x] * (width // LANES), axis=1)


def _matmul_stats_kernel(x_ref, w_ref, *rest, na, inv_d):
    *r_refs, o_ref, ob_ref, sc_ref, ssq_ref = rest
    j = pl.program_id(1)
    acc = jnp.dot(x_ref[...], w_ref[...], preferred_element_type=F32)
    res = r_refs[0][...] if len(r_refs) == 1 else jnp.where(pl.program_id(0) < na, r_refs[0][...], r_refs[1][...])
    h = res + acc
    o_ref[...] = h
    ob_ref[...] = h.astype(BF16)
    hh = h * h
    part = hh[:, 0:LANES]
    for c in range(1, hh.shape[1] // LANES):
        part = part + hh[:, c * LANES:(c + 1) * LANES]

    @pl.when(j == 0)
    def _():
        ssq_ref[...] = part

    @pl.when(j > 0)
    def _():
        ssq_ref[...] += part

    @pl.when(j == pl.num_programs(1) - 1)
    def _():
        ms = jnp.sum(ssq_ref[...], axis=-1, keepdims=True) * inv_d
        sc_ref[...] = jnp.broadcast_to(lax.rsqrt(ms + NORM_EPS), sc_ref.shape)


def _matmul_stats(x, w, tm, tn, residual, name):
    m, k = x.shape
    n = w.shape[1]
    in_specs = [pl.BlockSpec((tm, k), lambda i, j: (i, 0)),
                pl.BlockSpec((k, tn), lambda i, j: (0, j))]
    na = 0
    if isinstance(residual, tuple):
        ra, rb = residual
        na = ra.shape[0] // tm
        in_specs += [pl.BlockSpec((tm, tn), lambda i, j: (jnp.minimum(i, na - 1), j)),
                     pl.BlockSpec((tm, tn), lambda i, j: (jnp.maximum(i - na, 0), j))]
        res_args = [ra, rb]
    else:
        in_specs.append(pl.BlockSpec((tm, tn), lambda i, j: (i, j)))
        res_args = [residual]
    return pl.pallas_call(
        functools.partial(_matmul_stats_kernel, na=na, inv_d=1.0 / n),
        grid=(m // tm, n // tn),
        in_specs=in_specs,
        out_specs=[pl.BlockSpec((tm, tn), lambda i, j: (i, j)),
                   pl.BlockSpec((tm, tn), lambda i, j: (i, j)),
                   pl.BlockSpec((tm, LANES), lambda i, j: (i, 0))],
        out_shape=[jax.ShapeDtypeStruct((m, n), F32), jax.ShapeDtypeStruct((m, n), BF16),
                   jax.ShapeDtypeStruct((m, LANES), F32)],
        scratch_shapes=[pltpu.VMEM((tm, LANES), F32)],
        compiler_params=_params("arbitrary", "arbitrary"),
        name=name,
    )(x, w, *res_args)


def _cast_weight(w_ref, wb_ref, g_ref):
    w = w_ref[...]
    if g_ref is not None:
        w = w * _lane_tiled(g_ref[...], w.shape[1])
    wb_ref[...] = w.astype(BF16)


def _wres_kernel(*refs, normed):
    if normed:
        x_ref, sc_ref, g_ref, w_ref, o_ref, wb_ref = refs
    else:
        x_ref, w_ref, o_ref, wb_ref = refs
        sc_ref = g_ref = None

    @pl.when(pl.program_id(1) == 0)
    def _():
        _cast_weight(w_ref, wb_ref, g_ref)

    acc = jnp.dot(x_ref[...], wb_ref[...], preferred_element_type=F32)
    if normed:
        acc = acc * _lane_tiled(sc_ref[...], acc.shape[1])
    o_ref[...] = acc.astype(o_ref.dtype)


def _matmul_wres(x, w, out_dtype, tm, tn, scale=None, gain=None, name="matmul_wres"):
    m, k = x.shape
    n = w.shape[1]
    normed = scale is not None
    in_specs = [pl.BlockSpec((tm, k), lambda j, i: (i, 0))]
    args = [x]
    if normed:
        in_specs += [pl.BlockSpec((tm, LANES), lambda j, i: (i, 0)),
                     pl.BlockSpec((k, LANES), lambda j, i: (0, 0))]
        args += [scale, gain]
    in_specs.append(pl.BlockSpec((k, tn), lambda j, i: (0, j)))
    args.append(w)
    return pl.pallas_call(
        functools.partial(_wres_kernel, normed=normed),
        grid=(n // tn, m // tm),
        in_specs=in_specs,
        out_specs=pl.BlockSpec((tm, tn), lambda j, i: (i, j)),
        out_shape=jax.ShapeDtypeStruct((m, n), out_dtype),
        scratch_shapes=[pltpu.VMEM((k, tn), BF16)],
        compiler_params=_params("arbitrary", "arbitrary"),
        name=name,
    )(*args)


def _swiglu_kernel(x_ref, sc_ref, g_ref, wg_ref, wu_ref, o_ref, wgb_ref, wub_ref):
    @pl.when(pl.program_id(1) == 0)
    def _():
        _cast_weight(wg_ref, wgb_ref, g_ref)
        _cast_weight(wu_ref, wub_ref, g_ref)

    x = x_ref[...]
    sc = _lane_tiled(sc_ref[...], o_ref.shape[1])
    gate = jnp.dot(x, wgb_ref[...], preferred_element_type=F32) * sc
    up = jnp.dot(x, wub_ref[...], preferred_element_type=F32) * sc
    o_ref[...] = (gate * jax.nn.sigmoid(gate) * up).astype(o_ref.dtype)


def _swiglu_up(x, scale, gain, w13, hidden, tm, tn):
    m, k = x.shape
    nb = hidden // tn
    return pl.pallas_call(
        _swiglu_kernel,
        grid=(nb, m // tm),
        in_specs=[pl.BlockSpec((tm, k), lambda j, i: (i, 0)),
                  pl.BlockSpec((tm, LANES), lambda j, i: (i, 0)),
                  pl.BlockSpec((k, LANES), lambda j, i: (0, 0)),
                  pl.BlockSpec((k, tn), lambda j, i: (0, j)),
                  pl.BlockSpec((k, tn), lambda j, i: (0, j + nb))],
        out_specs=pl.BlockSpec((tm, tn), lambda j, i: (i, j)),
        out_shape=jax.ShapeDtypeStruct((m, hidden), BF16),
        scratch_shapes=[pltpu.VMEM((k, tn), BF16), pltpu.VMEM((k, tn), BF16)],
        compiler_params=_params("arbitrary", "arbitrary"),
        name="swiglu_up",
    )(x, scale, gain, w13, w13)


def _head_sum_matrix():
    r = lax.broadcasted_iota(jnp.int32, (LANES, LANES), 0) >> HEAD_SHIFT
    c = lax.broadcasted_iota(jnp.int32, (LANES, LANES), 1) >> HEAD_SHIFT
    return jnp.where(r == c, 1.0, 0.0).astype(BF16)


def _split2(x):
    hi = x.astype(BF16)
    return hi, (x - hi.astype(F32)).astype(BF16)


def _gmap(f, *lists):
    return [f(*xs) for xs in zip(*lists)]


def _wkv_kernel(z_ref, lo_ref, wup_ref, aup_ref, w0_ref, a0_ref, kk_ref, ka_ref, rk_ref,
                y_ref, bo_ref, state_ref, lw_ref, cum_ref, icl_ref, *, d_model):
    C = WKV_CHUNK
    d = pl.program_id(1)
    c = pl.program_id(2)
    sgn = 1 - 2 * d

    @pl.when(c == 0)
    def _():
        state_ref[...] = jnp.zeros_like(state_ref)

    wl = w0_ref[0] + jnp.dot(jnp.tanh(lo_ref[0, :, 0:LORA_W]).astype(BF16), wup_ref[0],
                             preferred_element_type=F32)
    lw = DECAY_SCALE * jax.nn.sigmoid(wl)
    lw_ref[...] = lw
    r64 = lax.broadcasted_iota(jnp.int32, (C, C), 0)
    c64 = lax.broadcasted_iota(jnp.int32, (C, C), 1)
    tri = jnp.where((r64 - c64) * sgn >= 0, 1.0, 0.0).astype(BF16)
    lw_hi, lw_lo = _split2(lw)
    cum_ref[...] = (jnp.dot(tri, lw_hi, preferred_element_type=F32)
                    + jnp.dot(tri, lw_lo, preferred_element_type=F32))
    icl_ref[...] = jax.nn.sigmoid(
        a0_ref[0] + jnp.dot(lo_ref[0, :, LORA_W:2 * LORA_W].astype(BF16), aup_ref[0],
                            preferred_element_type=F32))

    row = lax.broadcasted_iota(jnp.int32, (C, LANES), 0)
    col = lax.broadcasted_iota(jnp.int32, (C, LANES), 1)
    colh = col & (HEAD_SIZE - 1)
    order = (row - colh) * sgn
    strict = order > 0
    incl = order >= 0
    eye2 = row == colh
    lane_lo = col < HEAD_SIZE
    esum = _head_sum_matrix()
    rr = lax.broadcasted_iota(jnp.int32, (LANES, LANES), 0) >> HEAD_SHIFT
    cc = lax.broadcasted_iota(jnp.int32, (LANES, LANES), 1) >> HEAD_SHIFT
    blockdiag = rr == cc

    def bd(x):
        zero = jnp.zeros_like(x)
        return jnp.concatenate([jnp.where(lane_lo, x, zero), jnp.where(lane_lo, zero, x)], axis=0)

    def pmul(x, y):
        return jnp.dot(x.astype(BF16), bd(y.astype(BF16)), preferred_element_type=F32)

    nt_dims = (((1,), (1,)), ((), ()))
    tn_dims = (((0,), (0,)), ((), ()))

    def group_body(g, carry):
        pairs = [g * WKV_GROUP + j for j in range(WKV_GROUP)]

        def lane_tile(p, offset=0):
            return pl.ds(pl.multiple_of(offset + p * LANES, LANES), LANES)

        cols = [lane_tile(p) for p in pairs]
        r = [z_ref[0, :, cs] for cs in cols]
        k = [z_ref[0, :, lane_tile(p, d_model)] for p in pairs]
        v = [z_ref[0, :, lane_tile(p, 2 * d_model)] for p in pairs]
        icl = [icl_ref[:, cs] for cs in cols]
        lwp = [lw_ref[:, cs] for cs in cols]
        cum = [cum_ref[:, cs] for cs in cols]
        tot = [jnp.sum(x, axis=0, keepdims=True) for x in lwp]

        def head_sums(xs):
            stacked = jnp.concatenate([x.astype(BF16) for x in xs], axis=0)
            sums = jnp.dot(stacked, esum, preferred_element_type=F32)
            return [sums[j * C:(j + 1) * C] for j in range(len(xs))]

        q = [ki * kk_ref[:, cs] for ki, cs in zip(k, cols)]
        n2 = head_sums([qi * qi for qi in q])
        kd = [ki * (1.0 + (ic - 1.0) * ka_ref[:, cs]) for ki, ic, cs in zip(k, icl, cols)]
        bsum = head_sums([ri * kdi * rk_ref[:, cs] for ri, kdi, cs in zip(r, kd, cols)])
        for cs, bs, vi in zip(cols, bsum, v):
            bo_ref[0, 0, :, cs] = bs * vi

        kk = [qi * lax.rsqrt(jnp.maximum(ni, 1e-12)) for qi, ni in zip(q, n2)]
        b = _gmap(lambda x, ic: x * ic, kk, icl)
        e_out = [jnp.exp(-x) for x in cum]
        e_tot = _gmap(lambda t_, x: jnp.exp(t_ - x), tot, cum)
        at = _gmap(lambda x, cm, lw_: (-x * jnp.exp(cm - lw_)).astype(BF16), kk, cum, lwp)
        rt = _gmap(lambda x, cm: (x * jnp.exp(cm)).astype(BF16), r, cum)
        bt = _gmap(lambda x, e: (x * e).astype(BF16), b, e_out)
        kt = _gmap(lambda x, e: (x * e).astype(BF16), kd, e_out)
        bw = _gmap(lambda x, e: (x * e).astype(BF16), b, e_tot)
        kw = _gmap(lambda x, e: (x * e).astype(BF16), kd, e_tot)
        vb = [x.astype(BF16) for x in v]

        lhs = _gmap(lambda a_, r_: jnp.concatenate([a_, r_], axis=0), at, rt)
        rhs_t = _gmap(lambda b_, k_: jnp.concatenate([bd(b_), bd(k_)], axis=0), bt, kt)
        pmat = _gmap(lambda l_, r_: lax.dot_general(l_, r_, nt_dims, preferred_element_type=F32), lhs, rhs_t)
        h = [state_ref[p] for p in pairs]
        hs = _gmap(lambda l_, h_: jnp.dot(l_, h_.astype(BF16), preferred_element_type=F32),
                   lhs, h)
        lab = [jnp.where(strict, x[:C, :LANES], 0.0) for x in pmat]
        rb = [jnp.where(incl, x[C:, :LANES], 0.0) for x in pmat]
        lrk = [jnp.concatenate([jnp.where(strict, x[:C, LANES:], 0.0).astype(BF16),
                                jnp.where(incl, x[C:, LANES:], 0.0).astype(BF16)], axis=0) for x in pmat]
        vst = [bd(x) for x in vb]
        lrkv = _gmap(lambda l_, v_: jnp.dot(l_, v_, preferred_element_type=F32), lrk, vst)
        rhs_u = _gmap(lambda h_, x: h_[:C] + x[:C], hs, lrkv)

        ident = jnp.where(eye2, 1.0, 0.0)
        first = (row >> 1) == (colh >> 1)
        tinv = [ident + jnp.where(first, x, 0.0) for x in lab]
        s = 2
        while s < C:
            sh = s.bit_length() - 1
            level = ((row >> (sh + 1)) == (colh >> (sh + 1))) & ((row >> sh) != (colh >> sh))
            off = [jnp.where(level, x, 0.0) for x in lab]
            tmp = _gmap(pmul, tinv, off)
            upd_t = _gmap(pmul, tmp, tinv)
            tinv = _gmap(lambda t_, x: t_ + x, tinv, upd_t)
            s *= 2
        u = _gmap(pmul, tinv, rhs_u)

        ub = [x.astype(BF16) for x in u]
        rbu = _gmap(pmul, rb, ub)
        for cs, h_, xv, xu in zip(cols, hs, lrkv, rbu):
            y_ref[0, 0, :, cs] = h_[C:] + xv[C:] + xu
        upd = _gmap(lambda b_, k_, u_, v_: lax.dot_general(
            jnp.concatenate([b_, k_], axis=0), jnp.concatenate([u_, v_], axis=0),
            tn_dims, preferred_element_type=F32), bw, kw, ub, vb)
        for p, h_, t_, x in zip(pairs, h, tot, upd):
            decay_rows = jnp.broadcast_to(jnp.exp(t_), (LANES, LANES)).T
            state_ref[p] = h_ * decay_rows + jnp.where(blockdiag, x, 0.0)
        return carry

    lax.fori_loop(0, d_model // (LANES * WKV_GROUP), group_body, 0)


def _wkv_scan(z, dm, w_up, a_up, w0, a0, k_k, k_a, r_k):
    bsz, t, n = z.shape
    C = WKV_CHUNK
    nc = t // C
    lora_block = (n - LORA_COLS) // LORA_COLS

    def tchunk(dd, cc):
        return jnp.where(dd == 0, cc, nc - 1 - cc)

    def dir_map(bb, dd, cc):
        return (dd, 0, 0)

    def const2(bb, dd, cc):
        return (0, 0)

    def out_map(bb, dd, cc):
        return (dd, bb, tchunk(dd, cc), 0)

    out_sds = jax.ShapeDtypeStruct((2, bsz, t, dm), F32)
    kern = functools.partial(_wkv_kernel, d_model=dm)
    return pl.pallas_call(
        kern,
        grid=(bsz, 2, nc),
        in_specs=[
            pl.BlockSpec((1, C, 3 * dm), lambda bb, dd, cc: (bb, tchunk(dd, cc), 0)),
            pl.BlockSpec((1, C, LORA_COLS), lambda bb, dd, cc: (bb, tchunk(dd, cc), lora_block)),
            pl.BlockSpec((1, LORA_W, dm), dir_map),
            pl.BlockSpec((1, LORA_W, dm), dir_map),
            pl.BlockSpec((1, 1, dm), dir_map),
            pl.BlockSpec((1, 1, dm), dir_map),
            pl.BlockSpec((1, dm), const2),
            pl.BlockSpec((1, dm), const2),
            pl.BlockSpec((1, dm), const2),
        ],
        out_specs=[pl.BlockSpec((1, 1, C, dm), out_map), pl.BlockSpec((1, 1, C, dm), out_map)],
        out_shape=[out_sds, out_sds],
        scratch_shapes=[pltpu.VMEM((dm // LANES, LANES, LANES), F32),
                        pltpu.VMEM((C, dm), F32), pltpu.VMEM((C, dm), F32), pltpu.VMEM((C, dm), F32)],
        compiler_params=_params("arbitrary", "arbitrary", "arbitrary"),
        name="wkv_scan",
    )(z, z, w_up, a_up, w0, a0, k_k, k_a, r_k)


def _pool_kernel(p_ref, gate_ref, w_ref, scale_ref, o_ref, pad_ref, *, rows):
    g = pl.program_id(1)
    t, gi = p_ref.shape[1], p_ref.shape[2]
    zeros = jnp.zeros((POOL_PAD, gi), F32)
    pad_ref[0:POOL_PAD, :] = zeros
    pad_ref[POOL_PAD + t:POOL_PAD + t + POOL_PAD, :] = zeros
    pad_ref[POOL_PAD:POOL_PAD + t, :] = p_ref[0]
    w = w_ref[0]
    scale = scale_ref[...]

    for gidx, win in enumerate(POOL_WINDOWS):
        @pl.when(g == gidx)
        def _(win=win):
            half = win // 2

            def tile_body(i, carry):
                r0 = pl.multiple_of(i * rows, rows)
                n = rows + 2 * SUBLANES
                xt = pad_ref[pl.ds(r0 + POOL_PAD - SUBLANES, n), :]
                acc = xt
                step = 1
                while step < win:
                    acc = acc + pltpu.roll(acc, n - step, axis=0)
                    step *= 2
                if SUBLANES - half:
                    acc = pltpu.roll(acc, n - (SUBLANES - half), axis=0)
                acc = acc[0:rows]
                tt = r0 + lax.broadcasted_iota(jnp.int32, (rows, LANES), 0)
                cnt = (jnp.minimum(tt + (win - half), t) - jnp.maximum(tt - half, 0)).astype(F32)
                inv = 1.0 / cnt
                inv_full = jnp.concatenate([inv] * (gi // LANES), axis=1)
                dlt = acc * inv_full - xt[SUBLANES:SUBLANES + rows]
                out = jnp.dot(dlt.astype(BF16), w, preferred_element_type=F32) * scale
                gate = jax.nn.sigmoid(gate_ref[0, pl.ds(r0, rows), :])
                o_ref[0, pl.ds(r0, rows), :] = (gate * out).astype(o_ref.dtype)
                return carry

            lax.fori_loop(0, t // rows, tile_body, 0)


def _pool_branch(z, pool_col, gate_col, pool_w, pool_scale, rows=256):
    bsz, t, _ = z.shape
    ng, gi, go = pool_w.shape
    dm = ng * go
    kern = functools.partial(_pool_kernel, rows=rows)
    return pl.pallas_call(
        kern,
        grid=(bsz, ng),
        in_specs=[pl.BlockSpec((1, t, gi), lambda b, g: (b, 0, pool_col // gi + g)),
                  pl.BlockSpec((1, t, go), lambda b, g: (b, 0, gate_col // go + g)),
                  pl.BlockSpec((1, gi, go), lambda b, g: (g, 0, 0)),
                  pl.BlockSpec((1, go), lambda b, g: (0, g))],
        out_specs=pl.BlockSpec((1, t, go), lambda b, g: (b, 0, g)),
        out_shape=jax.ShapeDtypeStruct((bsz, t, dm), BF16),
        scratch_shapes=[pltpu.VMEM((t + 2 * POOL_PAD, gi), F32)],
        compiler_params=_params("arbitrary", "arbitrary"),
        name="pool_branch",
    )(z, z, pool_w, pool_scale)


def _merge_kernel(y_ref, bo_ref, lo_ref, gup_ref, gng_ref, gnb_ref, gate_ref, yb_ref, o_ref):
    gd_lo = 2 * LORA_W
    esum = _head_sum_matrix()
    inv_n = 1.0 / HEAD_SIZE
    gd = jax.nn.sigmoid(lo_ref[0, :, gd_lo:gd_lo + GATE_LORA_PAD]).astype(BF16)

    def group_body(gidx, carry):
        cols = [pl.ds(pl.multiple_of((gidx * MERGE_GROUP + j) * LANES, LANES), LANES) for j in range(MERGE_GROUP)]
        y = [y_ref[0, 0, :, cs] + y_ref[1, 0, :, cs] for cs in cols]
        mu = [jnp.dot(x.astype(BF16), esum, preferred_element_type=F32) * inv_n for x in y]
        g = [jnp.dot(gd, gup_ref[:, cs], preferred_element_type=F32) for cs in cols]
        yc = _gmap(lambda x, m: x - m, y, mu)
        var = [jnp.dot((x * x).astype(BF16), esum, preferred_element_type=F32) * inv_n for x in yc]
        for cs, x, vr, gi in zip(cols, yc, var, g):
            yn = x * lax.rsqrt(vr + GN_EPS) * gng_ref[:, cs] + gnb_ref[:, cs]
            yn = yn + bo_ref[0, 0, :, cs] + bo_ref[1, 0, :, cs]
            ya = jax.nn.sigmoid(gate_ref[0, :, cs]) * (yn * gi)
            o_ref[0, :, cs] = (ya + yb_ref[0, :, cs].astype(F32)).astype(o_ref.dtype)
        return carry

    lax.fori_loop(0, o_ref.shape[2] // (LANES * MERGE_GROUP), group_body, 0)


def _merge(y, bo, z, gate_col, g_up, ln_g, ln_b, yb, tt=128):
    _, bsz, t, dm = y.shape
    n = z.shape[-1]
    lora_block = (n - LORA_COLS) // LORA_COLS
    return pl.pallas_call(
        _merge_kernel,
        grid=(bsz, t // tt),
        in_specs=[pl.BlockSpec((2, 1, tt, dm), lambda b, i: (0, b, i, 0)),
                  pl.BlockSpec((2, 1, tt, dm), lambda b, i: (0, b, i, 0)),
                  pl.BlockSpec((1, tt, LORA_COLS), lambda b, i: (b, i, lora_block)),
                  pl.BlockSpec((GATE_LORA_PAD, dm), lambda b, i: (0, 0)),
                  pl.BlockSpec((1, dm), lambda b, i: (0, 0)),
                  pl.BlockSpec((1, dm), lambda b, i: (0, 0)),
                  pl.BlockSpec((1, tt, dm), lambda b, i: (b, i, gate_col // dm)),
                  pl.BlockSpec((1, tt, dm), lambda b, i: (b, i, 0))],
        out_specs=pl.BlockSpec((1, tt, dm), lambda b, i: (b, i, 0)),
        out_shape=jax.ShapeDtypeStruct((bsz, t, dm), BF16),
        compiler_params=_params("arbitrary", "arbitrary"),
        name="wkv_merge",
    )(y, bo, z, g_up, ln_g, ln_b, z, yb)


def _xattn_kernel(q_ref, k_ref, v_ref, o_ref, *, head_dim):
    scale = head_dim ** -0.5
    nt_dims = (((1,), (1,)), ((), ()))
    for h in range(X_HEADS):
        cs = slice(h * head_dim, (h + 1) * head_dim)
        s = lax.dot_general(q_ref[0, :, cs], k_ref[0, :, cs], nt_dims, preferred_element_type=F32) * scale
        m = jnp.max(s, axis=-1, keepdims=True)
        e = jnp.exp(s - m)
        p = e / jnp.sum(e, axis=-1, keepdims=True)
        o_ref[0, :, cs] = jnp.dot(p.astype(BF16), v_ref[0, :, cs], preferred_element_type=F32).astype(o_ref.dtype)


def _xattn(q, k, v, tq=512):
    bsz, t, dm = q.shape
    m = k.shape[1]
    kern = functools.partial(_xattn_kernel, head_dim=dm // X_HEADS)
    return pl.pallas_call(
        kern,
        grid=(bsz, t // tq),
        in_specs=[pl.BlockSpec((1, tq, dm), lambda b, i: (b, i, 0)),
                  pl.BlockSpec((1, m, dm), lambda b, i: (b, 0, 0)),
                  pl.BlockSpec((1, m, dm), lambda b, i: (b, 0, 0))],
        out_specs=pl.BlockSpec((1, tq, dm), lambda b, i: (b, i, 0)),
        out_shape=jax.ShapeDtypeStruct((bsz, t, dm), BF16),
        compiler_params=_params("arbitrary", "arbitrary"),
        name="xattn",
    )(q, k, v)


def _pad_to(x, axis, size):
    pad = [(0, 0)] * x.ndim
    pad[axis] = (0, size - x.shape[axis])
    return jnp.pad(x, pad)


def _gain_tile(g):
    return jnp.broadcast_to(g[:, None], (g.shape[0], LANES))


def _in_proj_operands(p, dm):
    w_in, shift_w = p['w_in'], p['shift_w']
    gate_lora = p['g_up'].shape[0]
    c_rkv = 3 * dm
    c_lora = c_rkv + 2 * LORA_W + gate_lora
    pool_width = p['pool_w'].shape[0] * p['pool_w'].shape[1]
    c_pool = c_lora + pool_width
    wb = w_in.astype(BF16)
    w = jnp.concatenate([wb[:, :c_rkv], wb[:, c_pool:], wb[:, c_lora:c_pool],
                         _pad_to(wb[:, c_rkv:c_lora], 1, LORA_COLS)], axis=1)
    ident = jnp.zeros((3, 2 * dm + pool_width), F32).at[1].set(1.0)
    taps = jnp.concatenate([shift_w[:, :c_rkv], ident, _pad_to(shift_w[:, c_rkv:c_lora], 1, LORA_COLS)], axis=1)
    return w, taps, c_rkv, c_rkv + dm, c_rkv + 2 * dm


def _trunk(xs, mem, lp, norm_final_g):
    t, dm = xs[0].shape[1:]
    rows = [x.shape[0] * t for x in xs]
    bsz = sum(x.shape[0] for x in xs)
    n_mem = mem.shape[1]
    m_tok = bsz * t
    hs = tuple(x.reshape(-1, dm) for x in xs)
    memf = mem.reshape(bsz * n_mem, dm)
    depth = lp['w_in'].shape[0]
    h = None
    for l in range(depth):
        p = {name: arr[l] for name, arr in lp.items()}
        w_all, taps_all, gate_a_col, gate_b_col, pool_col = _in_proj_operands(p, dm)
        g_up = _pad_to(p['g_up'], 0, GATE_LORA_PAD).astype(BF16)

        if h is None:
            xn = _rmsnorm2(hs[0], hs[1], p['norm_mix_g'], BF16)
            res = hs
        else:
            xn = _rmsnorm(h, p['norm_mix_g'], BF16)
            res = h
        z = _in_proj(xn, w_all, taps_all, bsz, t)

        w_up = jnp.stack([p['w_up_f'], p['w_up_b']]).astype(BF16)
        a_up = jnp.stack([p['a_up_f'], p['a_up_b']]).astype(BF16)
        w0 = jnp.stack([p['w0_f'], p['w0_b']]).reshape(2, 1, dm)
        a0 = jnp.stack([p['a0_f'], p['a0_b']]).reshape(2, 1, dm)
        y, bo = _wkv_scan(z, dm, w_up, a_up, w0, a0,
                          p['k_k'].reshape(1, dm), p['k_a'].reshape(1, dm), p['r_k'].reshape(1, dm))
        yb = _pool_branch(z, pool_col, gate_b_col, p['pool_w'].astype(BF16), p['pool_scale'].reshape(1, dm))
        merged = _merge(y, bo, z, gate_a_col, g_up, p['ln_x_g'].reshape(1, dm), p['ln_x_b'].reshape(1, dm), yb)
        h, h_bf, scale = _matmul_stats(merged.reshape(m_tok, dm), p['w_out'].astype(BF16), 1024, 512,
                                       residual=res, name="out_proj")

        mn = _rmsnorm(memf, p['norm_mem_g'], BF16)
        q = _matmul_wres(h_bf, p['xq'], BF16, 1024, 512, scale=scale, gain=_gain_tile(p['norm_x_g']), name="xq")
        kx = _matmul_wres(mn, p['xk'], BF16, 1024, 512, name="xk")
        vx = _matmul_wres(mn, p['xv'], BF16, 1024, 512, name="xv")
        o = _xattn(q.reshape(bsz, t, dm), kx.reshape(bsz, n_mem, dm), vx.reshape(bsz, n_mem, dm))
        h, h_bf, scale = _matmul_stats(o.reshape(m_tok, dm), p['xo'].astype(BF16), 1024, 512,
                                       residual=h, name="xo")

        hidden = p['ffn_w2'].shape[0]
        act = _swiglu_up(h_bf, scale, _gain_tile(p['norm_ffn_g']), p['ffn_w13'], hidden, FFN_TM, FFN_TN)
        h = _matmul(act, p['ffn_w2'].astype(BF16), F32, 512, 256, residual=h, name="ffn_down")

    tm = 256
    outs, off = [], 0
    for x, nrow in zip(xs, rows):
        y = _rmsnorm(h, norm_final_g, F32, tm=tm, row_block_offset=off // tm, n_row_blocks=nrow // tm)
        outs.append(y.reshape(x.shape))
        off += nrow
    return tuple(outs)


def kernel(x_prompt, x_sample, mem_prompt, mem_sample, norm_mix_g, w_in, shift_w, w0_f, w_up_f, w0_b, w_up_b, a0_f, a_up_f, a0_b, a_up_b, g_up, k_k, k_a, r_k, ln_x_g, ln_x_b, pool_w, pool_scale, w_out, norm_x_g, norm_mem_g, xq, xk, xv, xo, norm_ffn_g, ffn_w13, ffn_w2, norm_final_g):
    assert x_prompt.shape[1:] == x_sample.shape[1:] and mem_prompt.shape[1:] == mem_sample.shape[1:]
    lp = {
        'norm_mix_g': norm_mix_g, 'w_in': w_in, 'shift_w': shift_w,
        'w0_f': w0_f, 'w_up_f': w_up_f, 'w0_b': w0_b, 'w_up_b': w_up_b,
        'a0_f': a0_f, 'a_up_f': a_up_f, 'a0_b': a0_b, 'a_up_b': a_up_b,
        'g_up': g_up, 'k_k': k_k, 'k_a': k_a, 'r_k': r_k.reshape(r_k.shape[0], -1),
        'ln_x_g': ln_x_g, 'ln_x_b': ln_x_b,
        'pool_w': pool_w, 'pool_scale': pool_scale, 'w_out': w_out,
        'norm_x_g': norm_x_g, 'norm_mem_g': norm_mem_g, 'xq': xq, 'xk': xk, 'xv': xv, 'xo': xo,
        'norm_ffn_g': norm_ffn_g, 'ffn_w13': ffn_w13, 'ffn_w2': ffn_w2,
    }
    mem = jnp.concatenate([mem_prompt, mem_sample], axis=0)
    return _trunk((x_prompt, x_sample), mem, lp, norm_final_g)
```

```python
import functools
import math

import jax
import jax.numpy as jnp
from jax import lax
from jax.experimental import pallas as pl
from jax.experimental.pallas import tpu as pltpu

F32 = jnp.float32
BF16 = jnp.bfloat16

LANES = 128
SUBLANES = 8
VMEM_LIMIT_BYTES = 56 * 1024 * 1024

HEAD_SIZE = 64
HEAD_SHIFT = 6
X_HEADS = 4
POOL_WINDOWS = (2, 4, 8, 16)
POOL_PAD = 16
GN_EPS = 64e-5
NORM_EPS = 1e-6
DECAY_SCALE = -math.exp(-0.5)
WKV_CHUNK = 64
WKV_SOLVE_GROUP = 32
WKV_UPDATE_GROUP = 16
MERGE_GROUP = 8
LORA_W = 128
GATE_LORA_PAD = 512
LORA_COLS = 1024
FFN_TM = 1024
FFN_TN = 256
IN_TN = 256
IN_ROW_CHUNK = 512


def _params(*semantics):
    return pltpu.CompilerParams(dimension_semantics=semantics, vmem_limit_bytes=VMEM_LIMIT_BYTES)


def _rms(x, g):
    ms = jnp.mean(x * x, axis=-1, keepdims=True)
    return x * lax.rsqrt(ms + NORM_EPS) * g


def _rmsnorm_kernel(x_ref, g_ref, o_ref):
    o_ref[...] = _rms(x_ref[...], g_ref[...]).astype(o_ref.dtype)


def _rmsnorm(x, g, out_dtype, tm=256, row_block_offset=0, n_row_blocks=None):
    m, d = x.shape
    nb = m // tm if n_row_blocks is None else n_row_blocks
    return pl.pallas_call(
        _rmsnorm_kernel,
        grid=(nb,),
        in_specs=[pl.BlockSpec((tm, d), lambda i: (i + row_block_offset, 0)),
                  pl.BlockSpec((1, d), lambda i: (0, 0))],
        out_specs=pl.BlockSpec((tm, d), lambda i: (i, 0)),
        out_shape=jax.ShapeDtypeStruct((nb * tm, d), out_dtype),
        compiler_params=_params("parallel"),
        name="rmsnorm",
    )(x, g.reshape(1, d))


def _rmsnorm2_kernel(xa_ref, xb_ref, g_ref, o_ref, *, na):
    x = jnp.where(pl.program_id(0) < na, xa_ref[...], xb_ref[...])
    o_ref[...] = _rms(x, g_ref[...]).astype(o_ref.dtype)


def _rmsnorm2(xa, xb, g, out_dtype, tm=256):
    d = xa.shape[1]
    na, nb = xa.shape[0] // tm, xb.shape[0] // tm
    return pl.pallas_call(
        functools.partial(_rmsnorm2_kernel, na=na),
        grid=(na + nb,),
        in_specs=[pl.BlockSpec((tm, d), lambda i: (jnp.minimum(i, na - 1), 0)),
                  pl.BlockSpec((tm, d), lambda i: (jnp.maximum(i - na, 0), 0)),
                  pl.BlockSpec((1, d), lambda i: (0, 0))],
        out_specs=pl.BlockSpec((tm, d), lambda i: (i, 0)),
        out_shape=jax.ShapeDtypeStruct(((na + nb) * tm, d), out_dtype),
        compiler_params=_params("arbitrary"),
        name="rmsnorm2",
    )(xa, xb, g.reshape(1, d))


def _matmul_kernel(x_ref, w_ref, o_ref):
    o_ref[...] = jnp.dot(x_ref[...], w_ref[...], preferred_element_type=F32).astype(o_ref.dtype)


def _matmul_res_kernel(x_ref, w_ref, r_ref, o_ref):
    acc = jnp.dot(x_ref[...], w_ref[...], preferred_element_type=F32)
    o_ref[...] = (r_ref[...] + acc).astype(o_ref.dtype)


def _matmul(x, w, out_dtype, tm, tn, residual=None, name="matmul"):
    m, k = x.shape
    n = w.shape[1]
    in_specs = [pl.BlockSpec((tm, k), lambda i, j: (i, 0)),
                pl.BlockSpec((k, tn), lambda i, j: (0, j))]
    args = [x, w]
    body = _matmul_kernel
    if residual is not None:
        in_specs.append(pl.BlockSpec((tm, tn), lambda i, j: (i, j)))
        args.append(residual)
        body = _matmul_res_kernel
    return pl.pallas_call(
        body,
        grid=(m // tm, n // tn),
        in_specs=in_specs,
        out_specs=pl.BlockSpec((tm, tn), lambda i, j: (i, j)),
        out_shape=jax.ShapeDtypeStruct((m, n), out_dtype),
        compiler_params=_params("arbitrary", "arbitrary"),
        name=name,
    )(*args)


def _in_proj_kernel(x_ref, w_ref, taps_ref, o_ref):
    t = x_ref.shape[0]
    rc = IN_ROW_CHUNK
    w = w_ref[...]
    taps = taps_ref[...]
    ridx = lax.broadcasted_iota(jnp.int32, (rc, w.shape[1]), 0)
    zs = [jnp.dot(x_ref[c * rc:(c + 1) * rc, :], w, preferred_element_type=F32) for c in range(t // rc)]
    zero_row = jnp.zeros((1, w.shape[1]), F32)
    for c, z in enumerate(zs):
        prev_row = zs[c - 1][rc - 1:rc, :] if c > 0 else zero_row
        next_row = zs[c + 1][0:1, :] if c + 1 < len(zs) else zero_row
        zm1 = jnp.where(ridx == 0, prev_row, pltpu.roll(z, 1, axis=0))
        zp1 = jnp.where(ridx == rc - 1, next_row, pltpu.roll(z, rc - 1, axis=0))
        o_ref[0, c * rc:(c + 1) * rc, :] = zm1 * taps[0:1, :] + z * taps[1:2, :] + zp1 * taps[2:3, :]


def _in_proj(xn, w, taps, bsz, t):
    k = xn.shape[1]
    n = w.shape[1]
    return pl.pallas_call(
        _in_proj_kernel,
        grid=(bsz, n // IN_TN),
        in_specs=[pl.BlockSpec((t, k), lambda b, j: (b, 0)),
                  pl.BlockSpec((k, IN_TN), lambda b, j: (0, j)),
                  pl.BlockSpec((3, IN_TN), lambda b, j: (0, j))],
        out_specs=pl.BlockSpec((1, t, IN_TN), lambda b, j: (b, 0, j)),
        out_shape=jax.ShapeDtypeStruct((bsz, t, n), F32),
        compiler_params=_params("arbitrary", "arbitrary"),
        name="in_proj",
    )(xn, w, taps)


def _lane_tiled(x, width):
    return jnp.concatenate([x] * (width // LANES), axis=1)


def _matmul_stats_kernel(x_ref, w_ref, *rest, na, inv_d):
    *r_refs, o_ref, ob_ref, sc_ref, ssq_ref = rest
    j = pl.program_id(1)
    acc = jnp.dot(x_ref[...], w_ref[...], preferred_element_type=F32)
    res = r_refs[0][...] if len(r_refs) == 1 else jnp.where(pl.program_id(0) < na, r_refs[0][...], r_refs[1][...])
    h = res + acc
    o_ref[...] = h
    ob_ref[...] = h.astype(BF16)
    hh = h * h
    part = hh[:, 0:LANES]
    for c in range(1, hh.shape[1] // LANES):
        part = part + hh[:, c * LANES:(c + 1) * LANES]

    @pl.when(j == 0)
    def _():
        ssq_ref[...] = part

    @pl.when(j > 0)
    def _():
        ssq_ref[...] += part

    @pl.when(j == pl.num_programs(1) - 1)
    def _():
        ms = jnp.sum(ssq_ref[...], axis=-1, keepdims=True) * inv_d
        sc_ref[...] = jnp.broadcast_to(lax.rsqrt(ms + NORM_EPS), sc_ref.shape)


def _matmul_stats(x, w, tm, tn, residual, name):
    m, k = x.shape
    n = w.shape[1]
    in_specs = [pl.BlockSpec((tm, k), lambda i, j: (i, 0)),
                pl.BlockSpec((k, tn), lambda i, j: (0, j))]
    na = 0
    if isinstance(residual, tuple):
        ra, rb = residual
        na = ra.shape[0] // tm
        in_specs += [pl.BlockSpec((tm, tn), lambda i, j: (jnp.minimum(i, na - 1), j)),
                     pl.BlockSpec((tm, tn), lambda i, j: (jnp.maximum(i - na, 0), j))]
        res_args = [ra, rb]
    else:
        in_specs.append(pl.BlockSpec((tm, tn), lambda i, j: (i, j)))
        res_args = [residual]
    return pl.pallas_call(
        functools.partial(_matmul_stats_kernel, na=na, inv_d=1.0 / n),
        grid=(m // tm, n // tn),
        in_specs=in_specs,
        out_specs=[pl.BlockSpec((tm, tn), lambda i, j: (i, j)),
                   pl.BlockSpec((tm, tn), lambda i, j: (i, j)),
                   pl.BlockSpec((tm, LANES), lambda i, j: (i, 0))],
        out_shape=[jax.ShapeDtypeStruct((m, n), F32), jax.ShapeDtypeStruct((m, n), BF16),
                   jax.ShapeDtypeStruct((m, LANES), F32)],
        scratch_shapes=[pltpu.VMEM((tm, LANES), F32)],
        compiler_params=_params("arbitrary", "arbitrary"),
        name=name,
    )(x, w, *res_args)


def _cast_weight(w_ref, wb_ref, g_ref):
    w = w_ref[...]
    if g_ref is not None:
        w = w * _lane_tiled(g_ref[...], w.shape[1])
    wb_ref[...] = w.astype(BF16)


def _wres_kernel(*refs, normed):
    if normed:
        x_ref, sc_ref, g_ref, w_ref, o_ref, wb_ref = refs
    else:
        x_ref, w_ref, o_ref, wb_ref = refs
        sc_ref = g_ref = None

    @pl.when(pl.program_id(1) == 0)
    def _():
        _cast_weight(w_ref, wb_ref, g_ref)

    acc = jnp.dot(x_ref[...], wb_ref[...], preferred_element_type=F32)
    if normed:
        acc = acc * _lane_tiled(sc_ref[...], acc.shape[1])
    o_ref[...] = acc.astype(o_ref.dtype)


def _matmul_wres(x, w, out_dtype, tm, tn, scale=None, gain=None, name="matmul_wres"):
    m, k = x.shape
    n = w.shape[1]
    normed = scale is not None
    in_specs = [pl.BlockSpec((tm, k), lambda j, i: (i, 0))]
    args = [x]
    if normed:
        in_specs += [pl.BlockSpec((tm, LANES), lambda j, i: (i, 0)),
                     pl.BlockSpec((k, LANES), lambda j, i: (0, 0))]
        args += [scale, gain]
    in_specs.append(pl.BlockSpec((k, tn), lambda j, i: (0, j)))
    args.append(w)
    return pl.pallas_call(
        functools.partial(_wres_kernel, normed=normed),
        grid=(n // tn, m // tm),
        in_specs=in_specs,
        out_specs=pl.BlockSpec((tm, tn), lambda j, i: (i, j)),
        out_shape=jax.ShapeDtypeStruct((m, n), out_dtype),
        scratch_shapes=[pltpu.VMEM((k, tn), BF16)],
        compiler_params=_params("arbitrary", "arbitrary"),
        name=name,
    )(*args)


def _swiglu_kernel(x_ref, sc_ref, g_ref, wg_ref, wu_ref, o_ref, wgb_ref, wub_ref):
    @pl.when(pl.program_id(1) == 0)
    def _():
        _cast_weight(wg_ref, wgb_ref, g_ref)
        _cast_weight(wu_ref, wub_ref, g_ref)

    x = x_ref[...]
    sc = _lane_tiled(sc_ref[...], o_ref.shape[1])
    gate = jnp.dot(x, wgb_ref[...], preferred_element_type=F32) * sc
    up = jnp.dot(x, wub_ref[...], preferred_element_type=F32) * sc
    o_ref[...] = (gate * jax.nn.sigmoid(gate) * up).astype(o_ref.dtype)


def _swiglu_up(x, scale, gain, w13, hidden, tm, tn):
    m, k = x.shape
    nb = hidden // tn
    return pl.pallas_call(
        _swiglu_kernel,
        grid=(nb, m // tm),
        in_specs=[pl.BlockSpec((tm, k), lambda j, i: (i, 0)),
                  pl.BlockSpec((tm, LANES), lambda j, i: (i, 0)),
                  pl.BlockSpec((k, LANES), lambda j, i: (0, 0)),
                  pl.BlockSpec((k, tn), lambda j, i: (0, j)),
                  pl.BlockSpec((k, tn), lambda j, i: (0, j + nb))],
        out_specs=pl.BlockSpec((tm, tn), lambda j, i: (i, j)),
        out_shape=jax.ShapeDtypeStruct((m, hidden), BF16),
        scratch_shapes=[pltpu.VMEM((k, tn), BF16), pltpu.VMEM((k, tn), BF16)],
        compiler_params=_params("arbitrary", "arbitrary"),
        name="swiglu_up",
    )(x, scale, gain, w13, w13)


def _head_sum_matrix():
    r = lax.broadcasted_iota(jnp.int32, (LANES, LANES), 0) >> HEAD_SHIFT
    c = lax.broadcasted_iota(jnp.int32, (LANES, LANES), 1) >> HEAD_SHIFT
    return jnp.where(r == c, 1.0, 0.0).astype(BF16)


def _split2(x):
    hi = x.astype(BF16)
    return hi, (x - hi.astype(F32)).astype(BF16)


def _gmap(f, *lists):
    return [f(*xs) for xs in zip(*lists)]


def _wkv_kernel(z_ref, lo_ref, wup_ref, aup_ref, w0_ref, a0_ref, kk_ref, ka_ref, rk_ref,
                y_ref, bo_ref, state_ref, lw_ref, cum_ref, icl_ref,
                tinv_ref, lrk_ref, rb_ref, lhs_ref, btk_ref, vb_ref, *, d_model):
    C = WKV_CHUNK
    d = pl.program_id(1)
    c = pl.program_id(2)
    sgn = 1 - 2 * d

    @pl.when(c == 0)
    def _():
        state_ref[...] = jnp.zeros_like(state_ref)

    wl = w0_ref[0] + jnp.dot(jnp.tanh(lo_ref[0, :, 0:LORA_W]).astype(BF16), wup_ref[0],
                             preferred_element_type=F32)
    lw = DECAY_SCALE * jax.nn.sigmoid(wl)
    lw_ref[...] = lw
    r64 = lax.broadcasted_iota(jnp.int32, (C, C), 0)
    c64 = lax.broadcasted_iota(jnp.int32, (C, C), 1)
    tri = jnp.where((r64 - c64) * sgn >= 0, 1.0, 0.0).astype(BF16)
    lw_hi, lw_lo = _split2(lw)
    cum_ref[...] = (jnp.dot(tri, lw_hi, preferred_element_type=F32)
                    + jnp.dot(tri, lw_lo, preferred_element_type=F32))
    icl_ref[...] = jax.nn.sigmoid(
        a0_ref[0] + jnp.dot(lo_ref[0, :, LORA_W:2 * LORA_W].astype(BF16), aup_ref[0],
                            preferred_element_type=F32))

    row = lax.broadcasted_iota(jnp.int32, (C, LANES), 0)
    col = lax.broadcasted_iota(jnp.int32, (C, LANES), 1)
    colh = col & (HEAD_SIZE - 1)
    order = (row - colh) * sgn
    strict = order > 0
    incl = order >= 0
    eye2 = row == colh
    lane_lo = col < HEAD_SIZE
    esum = _head_sum_matrix()
    rr = lax.broadcasted_iota(jnp.int32, (LANES, LANES), 0) >> HEAD_SHIFT
    cc = lax.broadcasted_iota(jnp.int32, (LANES, LANES), 1) >> HEAD_SHIFT
    blockdiag = rr == cc

    def bd(x):
        zero = jnp.zeros_like(x)
        return jnp.concatenate([jnp.where(lane_lo, x, zero), jnp.where(lane_lo, zero, x)], axis=0)

    def pmul(x, y):
        return jnp.dot(x.astype(BF16), bd(y.astype(BF16)), preferred_element_type=F32)

    nt_dims = (((1,), (1,)), ((), ()))
    tn_dims = (((0,), (0,)), ((), ()))

    def lane_tile(p, offset=0):
        return pl.ds(pl.multiple_of(offset + p * LANES, LANES), LANES)

    def head_sums(xs):
        stacked = jnp.concatenate([x.astype(BF16) for x in xs], axis=0)
        sums = jnp.dot(stacked, esum, preferred_element_type=F32)
        return [sums[j * C:(j + 1) * C] for j in range(len(xs))]

    def solve_body(g, carry):
        pairs = [g * WKV_SOLVE_GROUP + j for j in range(WKV_SOLVE_GROUP)]
        cols = [lane_tile(p) for p in pairs]
        r = [z_ref[0, :, cs] for cs in cols]
        k = [z_ref[0, :, lane_tile(p, d_model)] for p in pairs]
        v = [z_ref[0, :, lane_tile(p, 2 * d_model)] for p in pairs]
        icl = [icl_ref[:, cs] for cs in cols]
        lwp = [lw_ref[:, cs] for cs in cols]
        cum = [cum_ref[:, cs] for cs in cols]

        q = [ki * kk_ref[:, cs] for ki, cs in zip(k, cols)]
        n2 = head_sums([qi * qi for qi in q])
        kd = [ki * (1.0 + (ic - 1.0) * ka_ref[:, cs]) for ki, ic, cs in zip(k, icl, cols)]
        bsum = head_sums([ri * kdi * rk_ref[:, cs] for ri, kdi, cs in zip(r, kd, cols)])
        for cs, bs, vi in zip(cols, bsum, v):
            bo_ref[0, 0, :, cs] = bs * vi

        kk = [qi * lax.rsqrt(jnp.maximum(ni, 1e-12)) for qi, ni in zip(q, n2)]
        b = _gmap(lambda x, ic: x * ic, kk, icl)
        e_out = [jnp.exp(-x) for x in cum]
        at = _gmap(lambda x, cm, lw_: (-x * jnp.exp(cm - lw_)).astype(BF16), kk, cum, lwp)
        rt = _gmap(lambda x, cm: (x * jnp.exp(cm)).astype(BF16), r, cum)
        bt = _gmap(lambda x, e: (x * e).astype(BF16), b, e_out)
        kt = _gmap(lambda x, e: (x * e).astype(BF16), kd, e_out)

        lhs = _gmap(lambda a_, r_: jnp.concatenate([a_, r_], axis=0), at, rt)
        rhs_t = _gmap(lambda b_, k_: jnp.concatenate([bd(b_), bd(k_)], axis=0), bt, kt)
        pmat = _gmap(lambda l_, r_: lax.dot_general(l_, r_, nt_dims, preferred_element_type=F32), lhs, rhs_t)
        for p, l_, b_, k_, v_ in zip(pairs, lhs, bt, kt, v):
            lhs_ref[p] = l_
            btk_ref[p] = jnp.concatenate([b_, k_], axis=0)
            vb_ref[p] = v_.astype(BF16)
        lab = [jnp.where(strict, x[:C, :LANES], 0.0).astype(BF16) for x in pmat]
        for p, x in zip(pairs, pmat):
            rb_ref[p] = jnp.where(incl, x[C:, :LANES], 0.0).astype(BF16)
            lrk_ref[p] = jnp.concatenate([jnp.where(strict, x[:C, LANES:], 0.0).astype(BF16),
                                          jnp.where(incl, x[C:, LANES:], 0.0).astype(BF16)], axis=0)

        zero_b = jnp.zeros((C, LANES), BF16)
        ident = jnp.where(eye2, 1.0, 0.0).astype(BF16)
        first = (row >> 1) == (colh >> 1)
        tinv = [ident + jnp.where(first, x, zero_b) for x in lab]
        s = 2
        while s < C:
            sh = s.bit_length() - 1
            level = ((row >> (sh + 1)) == (colh >> (sh + 1))) & ((row >> sh) != (colh >> sh))
            off = [jnp.where(level, x, zero_b) for x in lab]
            tmp = _gmap(pmul, tinv, off)
            upd_t = _gmap(pmul, tmp, tinv)
            tinv = _gmap(lambda t_, x: t_ + x.astype(BF16), tinv, upd_t)
            s *= 2
        for p, x in zip(pairs, tinv):
            tinv_ref[p] = x
        return carry

    def update_body(g, carry):
        pairs = [g * WKV_UPDATE_GROUP + j for j in range(WKV_UPDATE_GROUP)]
        cols = [lane_tile(p) for p in pairs]
        h = [state_ref[p] for p in pairs]
        vb = [vb_ref[p] for p in pairs]
        hs = [jnp.dot(lhs_ref[p], h_.astype(BF16), preferred_element_type=F32)
              for p, h_ in zip(pairs, h)]
        lrkv = [jnp.dot(lrk_ref[p], bd(v_), preferred_element_type=F32) for p, v_ in zip(pairs, vb)]
        rhs_u = _gmap(lambda h_, x: (h_[:C] + x[:C]).astype(BF16), hs, lrkv)
        u = [jnp.dot(tinv_ref[p], bd(x), preferred_element_type=F32) for p, x in zip(pairs, rhs_u)]
        ub = [x.astype(BF16) for x in u]
        rbu = [jnp.dot(rb_ref[p], bd(x), preferred_element_type=F32) for p, x in zip(pairs, ub)]
        for cs, h_, xv, xu in zip(cols, hs, lrkv, rbu):
            y_ref[0, 0, :, cs] = h_[C:] + xv[C:] + xu
        upd = [lax.dot_general(btk_ref[p], jnp.concatenate([u_, v_], axis=0), tn_dims,
                               preferred_element_type=F32) for p, u_, v_ in zip(pairs, ub, vb)]
        for p, cs, h_, x in zip(pairs, cols, h, upd):
            tot = jnp.sum(lw_ref[:, cs], axis=0, keepdims=True)
            decay_rows = jnp.broadcast_to(jnp.exp(tot), (LANES, LANES)).T
            state_ref[p] = decay_rows * (h_ + jnp.where(blockdiag, x, 0.0))
        return carry

    n_pairs = d_model // LANES
    lax.fori_loop(0, n_pairs // WKV_SOLVE_GROUP, solve_body, 0)
    lax.fori_loop(0, n_pairs // WKV_UPDATE_GROUP, update_body, 0)


def _wkv_scan(z, dm, w_up, a_up, w0, a0, k_k, k_a, r_k):
    bsz, t, n = z.shape
    C = WKV_CHUNK
    nc = t // C
    n_pairs = dm // LANES
    lora_block = (n - LORA_COLS) // LORA_COLS

    def tchunk(dd, cc):
        return jnp.where(dd == 0, cc, nc - 1 - cc)

    def dir_map(bb, dd, cc):
        return (dd, 0, 0)

    def const2(bb, dd, cc):
        return (0, 0)

    def out_map(bb, dd, cc):
        return (dd, bb, tchunk(dd, cc), 0)

    out_sds = jax.ShapeDtypeStruct((2, bsz, t, dm), F32)
    kern = functools.partial(_wkv_kernel, d_model=dm)
    return pl.pallas_call(
        kern,
        grid=(bsz, 2, nc),
        in_specs=[
            pl.BlockSpec((1, C, 3 * dm), lambda bb, dd, cc: (bb, tchunk(dd, cc), 0)),
            pl.BlockSpec((1, C, LORA_COLS), lambda bb, dd, cc: (bb, tchunk(dd, cc), lora_block)),
            pl.BlockSpec((1, LORA_W, dm), dir_map),
            pl.BlockSpec((1, LORA_W, dm), dir_map),
            pl.BlockSpec((1, 1, dm), dir_map),
            pl.BlockSpec((1, 1, dm), dir_map),
            pl.BlockSpec((1, dm), const2),
            pl.BlockSpec((1, dm), const2),
            pl.BlockSpec((1, dm), const2),
        ],
        out_specs=[pl.BlockSpec((1, 1, C, dm), out_map), pl.BlockSpec((1, 1, C, dm), out_map)],
        out_shape=[out_sds, out_sds],
        scratch_shapes=[pltpu.VMEM((n_pairs, LANES, LANES), F32),
                        pltpu.VMEM((C, dm), F32), pltpu.VMEM((C, dm), F32), pltpu.VMEM((C, dm), F32),
                        pltpu.VMEM((n_pairs, C, LANES), BF16), pltpu.VMEM((n_pairs, 2 * C, LANES), BF16),
                        pltpu.VMEM((n_pairs, C, LANES), BF16), pltpu.VMEM((n_pairs, 2 * C, LANES), BF16),
                        pltpu.VMEM((n_pairs, 2 * C, LANES), BF16), pltpu.VMEM((n_pairs, C, LANES), BF16)],
        compiler_params=_params("arbitrary", "arbitrary", "arbitrary"),
        name="wkv_scan",
    )(z, z, w_up, a_up, w0, a0, k_k, k_a, r_k)


def _pool_kernel(p_ref, gate_ref, w_ref, scale_ref, o_ref, pad_ref, *, rows):
    g = pl.program_id(1)
    t, gi = p_ref.shape[1], p_ref.shape[2]
    zeros = jnp.zeros((POOL_PAD, gi), F32)
    pad_ref[0:POOL_PAD, :] = zeros
    pad_ref[POOL_PAD + t:POOL_PAD + t + POOL_PAD, :] = zeros
    pad_ref[POOL_PAD:POOL_PAD + t, :] = p_ref[0]
    w = w_ref[0]
    scale = scale_ref[...]

    for gidx, win in enumerate(POOL_WINDOWS):
        @pl.when(g == gidx)
        def _(win=win):
            half = win // 2

            def tile_body(i, carry):
                r0 = pl.multiple_of(i * rows, rows)
                n = rows + 2 * SUBLANES
                xt = pad_ref[pl.ds(r0 + POOL_PAD - SUBLANES, n), :]
                acc = xt
                step = 1
                while step < win:
                    acc = acc + pltpu.roll(acc, n - step, axis=0)
                    step *= 2
                if SUBLANES - half:
                    acc = pltpu.roll(acc, n - (SUBLANES - half), axis=0)
                acc = acc[0:rows]
                tt = r0 + lax.broadcasted_iota(jnp.int32, (rows, LANES), 0)
                cnt = (jnp.minimum(tt + (win - half), t) - jnp.maximum(tt - half, 0)).astype(F32)
                inv = 1.0 / cnt
                inv_full = jnp.concatenate([inv] * (gi // LANES), axis=1)
                dlt = acc * inv_full - xt[SUBLANES:SUBLANES + rows]
                out = jnp.dot(dlt.astype(BF16), w, preferred_element_type=F32) * scale
                gate = jax.nn.sigmoid(gate_ref[0, pl.ds(r0, rows), :])
                o_ref[0, pl.ds(r0, rows), :] = (gate * out).astype(o_ref.dtype)
                return carry

            lax.fori_loop(0, t // rows, tile_body, 0)


def _pool_branch(z, pool_col, gate_col, pool_w, pool_scale, rows=256):
    bsz, t, _ = z.shape
    ng, gi, go = pool_w.shape
    dm = ng * go
    kern = functools.partial(_pool_kernel, rows=rows)
    return pl.pallas_call(
        kern,
        grid=(bsz, ng),
        in_specs=[pl.BlockSpec((1, t, gi), lambda b, g: (b, 0, pool_col // gi + g)),
                  pl.BlockSpec((1, t, go), lambda b, g: (b, 0, gate_col // go + g)),
                  pl.BlockSpec((1, gi, go), lambda b, g: (g, 0, 0)),
                  pl.BlockSpec((1, go), lambda b, g: (0, g))],
        out_specs=pl.BlockSpec((1, t, go), lambda b, g: (b, 0, g)),
        out_shape=jax.ShapeDtypeStruct((bsz, t, dm), BF16),
        scratch_shapes=[pltpu.VMEM((t + 2 * POOL_PAD, gi), F32)],
        compiler_params=_params("arbitrary", "arbitrary"),
        name="pool_branch",
    )(z, z, pool_w, pool_scale)


def _merge_kernel(y_ref, bo_ref, lo_ref, gup_ref, gng_ref, gnb_ref, gate_ref, yb_ref, o_ref):
    gd_lo = 2 * LORA_W
    esum = _head_sum_matrix()
    inv_n = 1.0 / HEAD_SIZE
    gd = jax.nn.sigmoid(lo_ref[0, :, gd_lo:gd_lo + GATE_LORA_PAD]).astype(BF16)

    def group_body(gidx, carry):
        cols = [pl.ds(pl.multiple_of((gidx * MERGE_GROUP + j) * LANES, LANES), LANES) for j in range(MERGE_GROUP)]
        y = [y_ref[0, 0, :, cs] + y_ref[1, 0, :, cs] for cs in cols]
        mu = [jnp.dot(x.astype(BF16), esum, preferred_element_type=F32) * inv_n for x in y]
        g = [jnp.dot(gd, gup_ref[:, cs], preferred_element_type=F32) for cs in cols]
        yc = _gmap(lambda x, m: x - m, y, mu)
        var = [jnp.dot((x * x).astype(BF16), esum, preferred_element_type=F32) * inv_n for x in yc]
        for cs, x, vr, gi in zip(cols, yc, var, g):
            yn = x * lax.rsqrt(vr + GN_EPS) * gng_ref[:, cs] + gnb_ref[:, cs]
            yn = yn + bo_ref[0, 0, :, cs] + bo_ref[1, 0, :, cs]
            ya = jax.nn.sigmoid(gate_ref[0, :, cs]) * (yn * gi)
            o_ref[0, :, cs] = (ya + yb_ref[0, :, cs].astype(F32)).astype(o_ref.dtype)
        return carry

    lax.fori_loop(0, o_ref.shape[2] // (LANES * MERGE_GROUP), group_body, 0)


def _merge(y, bo, z, gate_col, g_up, ln_g, ln_b, yb, tt=128):
    _, bsz, t, dm = y.shape
    n = z.shape[-1]
    lora_block = (n - LORA_COLS) // LORA_COLS
    return pl.pallas_call(
        _merge_kernel,
        grid=(bsz, t // tt),
        in_specs=[pl.BlockSpec((2, 1, tt, dm), lambda b, i: (0, b, i, 0)),
                  pl.BlockSpec((2, 1, tt, dm), lambda b, i: (0, b, i, 0)),
                  pl.BlockSpec((1, tt, LORA_COLS), lambda b, i: (b, i, lora_block)),
                  pl.BlockSpec((GATE_LORA_PAD, dm), lambda b, i: (0, 0)),
                  pl.BlockSpec((1, dm), lambda b, i: (0, 0)),
                  pl.BlockSpec((1, dm), lambda b, i: (0, 0)),
                  pl.BlockSpec((1, tt, dm), lambda b, i: (b, i, gate_col // dm)),
                  pl.BlockSpec((1, tt, dm), lambda b, i: (b, i, 0))],
        out_specs=pl.BlockSpec((1, tt, dm), lambda b, i: (b, i, 0)),
        out_shape=jax.ShapeDtypeStruct((bsz, t, dm), BF16),
        compiler_params=_params("arbitrary", "arbitrary"),
        name="wkv_merge",
    )(y, bo, z, g_up, ln_g, ln_b, z, yb)


def _xattn_kernel(q_ref, k_ref, v_ref, o_ref, *, head_dim):
    scale = head_dim ** -0.5
    nt_dims = (((1,), (1,)), ((), ()))
    for h in range(X_HEADS):
        cs = slice(h * head_dim, (h + 1) * head_dim)
        s = lax.dot_general(q_ref[0, :, cs], k_ref[0, :, cs], nt_dims, preferred_element_type=F32) * scale
        m = jnp.max(s, axis=-1, keepdims=True)
        e = jnp.exp(s - m)
        p = e / jnp.sum(e, axis=-1, keepdims=True)
        o_ref[0, :, cs] = jnp.dot(p.astype(BF16), v_ref[0, :, cs], preferred_element_type=F32).astype(o_ref.dtype)


def _xattn(q, k, v, tq=512):
    bsz, t, dm = q.shape
    m = k.shape[1]
    kern = functools.partial(_xattn_kernel, head_dim=dm // X_HEADS)
    return pl.pallas_call(
        kern,
        grid=(bsz, t // tq),
        in_specs=[pl.BlockSpec((1, tq, dm), lambda b, i: (b, i, 0)),
                  pl.BlockSpec((1, m, dm), lambda b, i: (b, 0, 0)),
                  pl.BlockSpec((1, m, dm), lambda b, i: (b, 0, 0))],
        out_specs=pl.BlockSpec((1, tq, dm), lambda b, i: (b, i, 0)),
        out_shape=jax.ShapeDtypeStruct((bsz, t, dm), BF16),
        compiler_params=_params("arbitrary", "arbitrary"),
        name="xattn",
    )(q, k, v)


def _pad_to(x, axis, size):
    pad = [(0, 0)] * x.ndim
    pad[axis] = (0, size - x.shape[axis])
    return jnp.pad(x, pad)


def _gain_tile(g):
    return jnp.broadcast_to(g[:, None], (g.shape[0], LANES))


def _in_proj_operands(p, dm):
    w_in, shift_w = p['w_in'], p['shift_w']
    gate_lora = p['g_up'].shape[0]
    c_rkv = 3 * dm
    c_lora = c_rkv + 2 * LORA_W + gate_lora
    pool_width = p['pool_w'].shape[0] * p['pool_w'].shape[1]
    c_pool = c_lora + pool_width
    wb = w_in.astype(BF16)
    w = jnp.concatenate([wb[:, :c_rkv], wb[:, c_pool:], wb[:, c_lora:c_pool],
                         _pad_to(wb[:, c_rkv:c_lora], 1, LORA_COLS)], axis=1)
    ident = jnp.zeros((3, 2 * dm + pool_width), F32).at[1].set(1.0)
    taps = jnp.concatenate([shift_w[:, :c_rkv], ident, _pad_to(shift_w[:, c_rkv:c_lora], 1, LORA_COLS)], axis=1)
    return w, taps, c_rkv, c_rkv + dm, c_rkv + 2 * dm


def _trunk(xs, mem, lp, norm_final_g):
    t, dm = xs[0].shape[1:]
    rows = [x.shape[0] * t for x in xs]
    bsz = sum(x.shape[0] for x in xs)
    n_mem = mem.shape[1]
    m_tok = bsz * t
    hs = tuple(x.reshape(-1, dm) for x in xs)
    memf = mem.reshape(bsz * n_mem, dm)
    depth = lp['w_in'].shape[0]
    h = None
    for l in range(depth):
        p = {name: arr[l] for name, arr in lp.items()}
        w_all, taps_all, gate_a_col, gate_b_col, pool_col = _in_proj_operands(p, dm)
        g_up = _pad_to(p['g_up'], 0, GATE_LORA_PAD).astype(BF16)

        if h is None:
            xn = _rmsnorm2(hs[0], hs[1], p['norm_mix_g'], BF16)
            res = hs
        else:
            xn = _rmsnorm(h, p['norm_mix_g'], BF16)
            res = h
        z = _in_proj(xn, w_all, taps_all, bsz, t)

        w_up = jnp.stack([p['w_up_f'], p['w_up_b']]).astype(BF16)
        a_up = jnp.stack([p['a_up_f'], p['a_up_b']]).astype(BF16)
        w0 = jnp.stack([p['w0_f'], p['w0_b']]).reshape(2, 1, dm)
        a0 = jnp.stack([p['a0_f'], p['a0_b']]).reshape(2, 1, dm)
        y, bo = _wkv_scan(z, dm, w_up, a_up, w0, a0,
                          p['k_k'].reshape(1, dm), p['k_a'].reshape(1, dm), p['r_k'].reshape(1, dm))
        yb = _pool_branch(z, pool_col, gate_b_col, p['pool_w'].astype(BF16), p['pool_scale'].reshape(1, dm))
        merged = _merge(y, bo, z, gate_a_col, g_up, p['ln_x_g'].reshape(1, dm), p['ln_x_b'].reshape(1, dm), yb)
        h, h_bf, scale = _matmul_stats(merged.reshape(m_tok, dm), p['w_out'].astype(BF16), 1024, 512,
                                       residual=res, name="out_proj")

        mn = _rmsnorm(memf, p['norm_mem_g'], BF16)
        q = _matmul_wres(h_bf, p['xq'], BF16, 1024, 512, scale=scale, gain=_gain_tile(p['norm_x_g']), name="xq")
        kx = _matmul_wres(mn, p['xk'], BF16, 1024, 512, name="xk")
        vx = _matmul_wres(mn, p['xv'], BF16, 1024, 512, name="xv")
        o = _xattn(q.reshape(bsz, t, dm), kx.reshape(bsz, n_mem, dm), vx.reshape(bsz, n_mem, dm))
        h, h_bf, scale = _matmul_stats(o.reshape(m_tok, dm), p['xo'].astype(BF16), 1024, 512,
                                       residual=h, name="xo")

        hidden = p['ffn_w2'].shape[0]
        act = _swiglu_up(h_bf, scale, _gain_tile(p['norm_ffn_g']), p['ffn_w13'], hidden, FFN_TM, FFN_TN)
        h = _matmul(act, p['ffn_w2'].astype(BF16), F32, 512, 256, residual=h, name="ffn_down")

    tm = 256
    outs, off = [], 0
    for x, nrow in zip(xs, rows):
        y = _rmsnorm(h, norm_final_g, F32, tm=tm, row_block_offset=off // tm, n_row_blocks=nrow // tm)
        outs.append(y.reshape(x.shape))
        off += nrow
    return tuple(outs)


def kernel(x_prompt, x_sample, mem_prompt, mem_sample, norm_mix_g, w_in, shift_w, w0_f, w_up_f, w0_b, w_up_b, a0_f, a_up_f, a0_b, a_up_b, g_up, k_k, k_a, r_k, ln_x_g, ln_x_b, pool_w, pool_scale, w_out, norm_x_g, norm_mem_g, xq, xk, xv, xo, norm_ffn_g, ffn_w13, ffn_w2, norm_final_g):
    assert x_prompt.shape[1:] == x_sample.shape[1:] and mem_prompt.shape[1:] == mem_sample.shape[1:]
    lp = {
        'norm_mix_g': norm_mix_g, 'w_in': w_in, 'shift_w': shift_w,
        'w0_f': w0_f, 'w_up_f': w_up_f, 'w0_b': w0_b, 'w_up_b': w_up_b,
        'a0_f': a0_f, 'a_up_f': a_up_f, 'a0_b': a0_b, 'a_up_b': a_up_b,
        'g_up': g_up, 'k_k': k_k, 'k_a': k_a, 'r_k': r_k.reshape(r_k.shape[0], -1),
        'ln_x_g': ln_x_g, 'ln_x_b': ln_x_b,
        'pool_w': pool_w, 'pool_scale': pool_scale, 'w_out': w_out,
        'norm_x_g': norm_x_g, 'norm_mem_g': norm_mem_g, 'xq': xq, 'xk': xk, 'xv': xv, 'xo': xo,
        'norm_ffn_g': norm_ffn_g, 'ffn_w13': ffn_w13, 'ffn_w2': ffn_w2,
    }
    mem = jnp.concatenate([mem_prompt, mem_sample], axis=0)
    return _trunk((x_prompt, x_sample), mem, lp, norm_final_g)
```

```python
import functools
import math

import jax
import jax.numpy as jnp
from jax import lax
from jax.experimental import pallas as pl
from jax.experimental.pallas import tpu as pltpu

F32 = jnp.float32
BF16 = jnp.bfloat16

LANES = 128
SUBLANES = 8
VMEM_LIMIT_BYTES = 56 * 1024 * 1024
VMEM_LIMIT_BYTES_MAX = 60000 * 1024

HEAD_SIZE = 64
HEAD_SHIFT = 6
X_HEADS = 4
POOL_WINDOWS = (2, 4, 8, 16)
POOL_PAD = 16
GN_EPS = 64e-5
NORM_EPS = 1e-6
DECAY_SCALE = -math.exp(-0.5)
WKV_CHUNK = 64
WKV_SOLVE_GROUP = 32
WKV_UPDATE_GROUP = 16
MERGE_GROUP = 8
LORA_W = 128
GATE_LORA_PAD = 512
LORA_COLS = 1024
FFN_TM = 1024
FFN_TN = 256
FFN_DOWN_TM = 512
FFN_DOWN_TN = 256
IN_TN = 512
IN_ROW_CHUNK = 512


def _params(*semantics):
    return pltpu.CompilerParams(dimension_semantics=semantics, vmem_limit_bytes=VMEM_LIMIT_BYTES)


def _rms(x, g):
    ms = jnp.mean(x * x, axis=-1, keepdims=True)
    return x * lax.rsqrt(ms + NORM_EPS) * g


def _rmsnorm_kernel(x_ref, g_ref, o_ref):
    o_ref[...] = _rms(x_ref[...], g_ref[...]).astype(o_ref.dtype)


def _rmsnorm(x, g, out_dtype, tm=256, row_block_offset=0, n_row_blocks=None):
    m, d = x.shape
    nb = m // tm if n_row_blocks is None else n_row_blocks
    return pl.pallas_call(
        _rmsnorm_kernel,
        grid=(nb,),
        in_specs=[pl.BlockSpec((tm, d), lambda i: (i + row_block_offset, 0)),
                  pl.BlockSpec((1, d), lambda i: (0, 0))],
        out_specs=pl.BlockSpec((tm, d), lambda i: (i, 0)),
        out_shape=jax.ShapeDtypeStruct((nb * tm, d), out_dtype),
        compiler_params=_params("parallel"),
        name="rmsnorm",
    )(x, g.reshape(1, d))


def _rmsnorm2_kernel(xa_ref, xb_ref, g_ref, o_ref, *, na):
    x = jnp.where(pl.program_id(0) < na, xa_ref[...], xb_ref[...])
    o_ref[...] = _rms(x, g_ref[...]).astype(o_ref.dtype)


def _rmsnorm2(xa, xb, g, out_dtype, tm=256):
    d = xa.shape[1]
    na, nb = xa.shape[0] // tm, xb.shape[0] // tm
    return pl.pallas_call(
        functools.partial(_rmsnorm2_kernel, na=na),
        grid=(na + nb,),
        in_specs=[pl.BlockSpec((tm, d), lambda i: (jnp.minimum(i, na - 1), 0)),
                  pl.BlockSpec((tm, d), lambda i: (jnp.maximum(i - na, 0), 0)),
                  pl.BlockSpec((1, d), lambda i: (0, 0))],
        out_specs=pl.BlockSpec((tm, d), lambda i: (i, 0)),
        out_shape=jax.ShapeDtypeStruct(((na + nb) * tm, d), out_dtype),
        compiler_params=_params("arbitrary"),
        name="rmsnorm2",
    )(xa, xb, g.reshape(1, d))


def _matmul_kernel(x_ref, w_ref, o_ref):
    o_ref[...] = jnp.dot(x_ref[...], w_ref[...], preferred_element_type=F32).astype(o_ref.dtype)


def _matmul_res_kernel(x_ref, w_ref, r_ref, o_ref):
    acc = jnp.dot(x_ref[...], w_ref[...], preferred_element_type=F32)
    o_ref[...] = (r_ref[...] + acc).astype(o_ref.dtype)


def _matmul(x, w, out_dtype, tm, tn, residual=None, name="matmul"):
    m, k = x.shape
    n = w.shape[1]
    in_specs = [pl.BlockSpec((tm, k), lambda i, j: (i, 0)),
                pl.BlockSpec((k, tn), lambda i, j: (0, j))]
    args = [x, w]
    body = _matmul_kernel
    if residual is not None:
        in_specs.append(pl.BlockSpec((tm, tn), lambda i, j: (i, j)))
        args.append(residual)
        body = _matmul_res_kernel
    return pl.pallas_call(
        body,
        grid=(m // tm, n // tn),
        in_specs=in_specs,
        out_specs=pl.BlockSpec((tm, tn), lambda i, j: (i, j)),
        out_shape=jax.ShapeDtypeStruct((m, n), out_dtype),
        compiler_params=_params("arbitrary", "arbitrary"),
        name=name,
    )(*args)


def _matmul_res_norm_kernel(x_ref, w_ref, r_ref, g_ref, o_ref):
    j = pl.program_id(1)
    tn = w_ref.shape[1]
    acc = jnp.dot(x_ref[...], w_ref[...], preferred_element_type=F32)
    o_ref[:, pl.ds(pl.multiple_of(j * tn, tn), tn)] = r_ref[...] + acc

    @pl.when(j == pl.num_programs(1) - 1)
    def _():
        o_ref[...] = _rms(o_ref[...], g_ref[...])


def _matmul_res_norm(x, w, residual, g, tm, tn, row_block_offset, n_row_blocks, name):
    k = x.shape[1]
    n = w.shape[1]
    off = row_block_offset
    return pl.pallas_call(
        _matmul_res_norm_kernel,
        grid=(n_row_blocks, n // tn),
        in_specs=[pl.BlockSpec((tm, k), lambda i, j: (i + off, 0)),
                  pl.BlockSpec((k, tn), lambda i, j: (0, j)),
                  pl.BlockSpec((tm, tn), lambda i, j: (i + off, j)),
                  pl.BlockSpec((1, n), lambda i, j: (0, 0))],
        out_specs=pl.BlockSpec((tm, n), lambda i, j: (i, 0)),
        out_shape=jax.ShapeDtypeStruct((n_row_blocks * tm, n), F32),
        compiler_params=pltpu.CompilerParams(dimension_semantics=("arbitrary", "arbitrary"),
                                             vmem_limit_bytes=VMEM_LIMIT_BYTES_MAX),
        name=name,
    )(x, w, residual, g.reshape(1, n))


def _in_proj_kernel(x_ref, w_ref, taps_ref, o_ref):
    t = x_ref.shape[0]
    rc = IN_ROW_CHUNK
    w = w_ref[...]
    taps = taps_ref[...]
    ridx = lax.broadcasted_iota(jnp.int32, (rc, w.shape[1]), 0)
    zs = [jnp.dot(x_ref[c * rc:(c + 1) * rc, :], w, preferred_element_type=F32) for c in range(t // rc)]
    zero_row = jnp.zeros((1, w.shape[1]), F32)
    for c, z in enumerate(zs):
        prev_row = zs[c - 1][rc - 1:rc, :] if c > 0 else zero_row
        next_row = zs[c + 1][0:1, :] if c + 1 < len(zs) else zero_row
        zm1 = jnp.where(ridx == 0, prev_row, pltpu.roll(z, 1, axis=0))
        zp1 = jnp.where(ridx == rc - 1, next_row, pltpu.roll(z, rc - 1, axis=0))
        o_ref[0, c * rc:(c + 1) * rc, :] = zm1 * taps[0:1, :] + z * taps[1:2, :] + zp1 * taps[2:3, :]


def _in_proj(xn, w, taps, bsz, t):
    k = xn.shape[1]
    n = w.shape[1]
    return pl.pallas_call(
        _in_proj_kernel,
        grid=(bsz, n // IN_TN),
        in_specs=[pl.BlockSpec((t, k), lambda b, j: (b, 0), pipeline_mode=pl.Buffered(1)),
                  pl.BlockSpec((k, IN_TN), lambda b, j: (0, j)),
                  pl.BlockSpec((3, IN_TN), lambda b, j: (0, j))],
        out_specs=pl.BlockSpec((1, t, IN_TN), lambda b, j: (b, 0, j)),
        out_shape=jax.ShapeDtypeStruct((bsz, t, n), F32),
        compiler_params=_params("arbitrary", "arbitrary"),
        name="in_proj",
    )(xn, w, taps)


def _lane_tiled(x, width):
    return jnp.concatenate([x] * (width // LANES), axis=1)


def _matmul_stats_kernel(x_ref, w_ref, *rest, na, inv_d):
    *r_refs, o_ref, ob_ref, sc_ref, ssq_ref = rest
    j = pl.program_id(1)
    acc = jnp.dot(x_ref[...], w_ref[...], preferred_element_type=F32)
    res = r_refs[0][...] if len(r_refs) == 1 else jnp.where(pl.program_id(0) < na, r_refs[0][...], r_refs[1][...])
    h = res + acc
    o_ref[...] = h
    ob_ref[...] = h.astype(BF16)
    hh = h * h
    part = hh[:, 0:LANES]
    for c in range(1, hh.shape[1] // LANES):
        part = part + hh[:, c * LANES:(c + 1) * LANES]

    @pl.when(j == 0)
    def _():
        ssq_ref[...] = part

    @pl.when(j > 0)
    def _():
        ssq_ref[...] += part

    @pl.when(j == pl.num_programs(1) - 1)
    def _():
        ms = jnp.sum(ssq_ref[...], axis=-1, keepdims=True) * inv_d
        sc_ref[...] = jnp.broadcast_to(lax.rsqrt(ms + NORM_EPS), sc_ref.shape)


def _matmul_stats(x, w, tm, tn, residual, name):
    m, k = x.shape
    n = w.shape[1]
    in_specs = [pl.BlockSpec((tm, k), lambda i, j: (i, 0)),
                pl.BlockSpec((k, tn), lambda i, j: (0, j))]
    na = 0
    if isinstance(residual, tuple):
        ra, rb = residual
        na = ra.shape[0] // tm
        in_specs += [pl.BlockSpec((tm, tn), lambda i, j: (jnp.minimum(i, na - 1), j)),
                     pl.BlockSpec((tm, tn), lambda i, j: (jnp.maximum(i - na, 0), j))]
        res_args = [ra, rb]
    else:
        in_specs.append(pl.BlockSpec((tm, tn), lambda i, j: (i, j)))
        res_args = [residual]
    return pl.pallas_call(
        functools.partial(_matmul_stats_kernel, na=na, inv_d=1.0 / n),
        grid=(m // tm, n // tn),
        in_specs=in_specs,
        out_specs=[pl.BlockSpec((tm, tn), lambda i, j: (i, j)),
                   pl.BlockSpec((tm, tn), lambda i, j: (i, j)),
                   pl.BlockSpec((tm, LANES), lambda i, j: (i, 0))],
        out_shape=[jax.ShapeDtypeStruct((m, n), F32), jax.ShapeDtypeStruct((m, n), BF16),
                   jax.ShapeDtypeStruct((m, LANES), F32)],
        scratch_shapes=[pltpu.VMEM((tm, LANES), F32)],
        compiler_params=_params("arbitrary", "arbitrary"),
        name=name,
    )(x, w, *res_args)


def _cast_weight(w_ref, wb_ref, g_ref):
    w = w_ref[...]
    if g_ref is not None:
        w = w * _lane_tiled(g_ref[...], w.shape[1])
    wb_ref[...] = w.astype(BF16)


def _wres_kernel(*refs, normed):
    if normed:
        x_ref, sc_ref, g_ref, w_ref, o_ref, wb_ref = refs
    else:
        x_ref, w_ref, o_ref, wb_ref = refs
        sc_ref = g_ref = None

    @pl.when(pl.program_id(1) == 0)
    def _():
        _cast_weight(w_ref, wb_ref, g_ref)

    acc = jnp.dot(x_ref[...], wb_ref[...], preferred_element_type=F32)
    if normed:
        acc = acc * _lane_tiled(sc_ref[...], acc.shape[1])
    o_ref[...] = acc.astype(o_ref.dtype)


def _matmul_wres(x, w, out_dtype, tm, tn, scale=None, gain=None, name="matmul_wres"):
    m, k = x.shape
    n = w.shape[1]
    normed = scale is not None
    in_specs = [pl.BlockSpec((tm, k), lambda j, i: (i, 0))]
    args = [x]
    if normed:
        in_specs += [pl.BlockSpec((tm, LANES), lambda j, i: (i, 0)),
                     pl.BlockSpec((k, LANES), lambda j, i: (0, 0))]
        args += [scale, gain]
    in_specs.append(pl.BlockSpec((k, tn), lambda j, i: (0, j)))
    args.append(w)
    return pl.pallas_call(
        functools.partial(_wres_kernel, normed=normed),
        grid=(n // tn, m // tm),
        in_specs=in_specs,
        out_specs=pl.BlockSpec((tm, tn), lambda j, i: (i, j)),
        out_shape=jax.ShapeDtypeStruct((m, n), out_dtype),
        scratch_shapes=[pltpu.VMEM((k, tn), BF16)],
        compiler_params=_params("arbitrary", "arbitrary"),
        name=name,
    )(*args)


def _swiglu_kernel(x_ref, sc_ref, g_ref, wg_ref, wu_ref, o_ref, wgb_ref, wub_ref):
    @pl.when(pl.program_id(1) == 0)
    def _():
        _cast_weight(wg_ref, wgb_ref, g_ref)
        _cast_weight(wu_ref, wub_ref, g_ref)

    x = x_ref[...]
    sc = _lane_tiled(sc_ref[...], o_ref.shape[1])
    gate = jnp.dot(x, wgb_ref[...], preferred_element_type=F32) * sc
    up = jnp.dot(x, wub_ref[...], preferred_element_type=F32) * sc
    o_ref[...] = (gate * jax.nn.sigmoid(gate) * up).astype(o_ref.dtype)


def _swiglu_up(x, scale, gain, w13, hidden, tm, tn):
    m, k = x.shape
    nb = hidden // tn
    return pl.pallas_call(
        _swiglu_kernel,
        grid=(nb, m // tm),
        in_specs=[pl.BlockSpec((tm, k), lambda j, i: (i, 0)),
                  pl.BlockSpec((tm, LANES), lambda j, i: (i, 0)),
                  pl.BlockSpec((k, LANES), lambda j, i: (0, 0)),
                  pl.BlockSpec((k, tn), lambda j, i: (0, j)),
                  pl.BlockSpec((k, tn), lambda j, i: (0, j + nb))],
        out_specs=pl.BlockSpec((tm, tn), lambda j, i: (i, j)),
        out_shape=jax.ShapeDtypeStruct((m, hidden), BF16),
        scratch_shapes=[pltpu.VMEM((k, tn), BF16), pltpu.VMEM((k, tn), BF16)],
        compiler_params=_params("arbitrary", "arbitrary"),
        name="swiglu_up",
    )(x, scale, gain, w13, w13)


def _head_sum_matrix():
    r = lax.broadcasted_iota(jnp.int32, (LANES, LANES), 0) >> HEAD_SHIFT
    c = lax.broadcasted_iota(jnp.int32, (LANES, LANES), 1) >> HEAD_SHIFT
    return jnp.where(r == c, 1.0, 0.0).astype(BF16)


def _split2(x):
    hi = x.astype(BF16)
    return hi, (x - hi.astype(F32)).astype(BF16)


def _gmap(f, *lists):
    return [f(*xs) for xs in zip(*lists)]


def _wkv_kernel(z_ref, lo_ref, wup_ref, aup_ref, w0_ref, a0_ref, kk_ref, ka_ref, rk_ref,
                y_ref, bo_ref, state_ref, lw_ref, cum_ref, icl_ref,
                tinv_ref, lrk_ref, rb_ref, lhs_ref, btk_ref, vb_ref, *, d_model):
    C = WKV_CHUNK
    d = pl.program_id(1)
    c = pl.program_id(2)
    sgn = 1 - 2 * d

    @pl.when(c == 0)
    def _():
        state_ref[...] = jnp.zeros_like(state_ref)

    wl = w0_ref[0] + jnp.dot(jnp.tanh(lo_ref[0, :, 0:LORA_W]).astype(BF16), wup_ref[0],
                             preferred_element_type=F32)
    lw = DECAY_SCALE * jax.nn.sigmoid(wl)
    lw_ref[...] = lw
    r64 = lax.broadcasted_iota(jnp.int32, (C, C), 0)
    c64 = lax.broadcasted_iota(jnp.int32, (C, C), 1)
    tri = jnp.where((r64 - c64) * sgn >= 0, 1.0, 0.0).astype(BF16)
    lw_hi, lw_lo = _split2(lw)
    cum_ref[...] = (jnp.dot(tri, lw_hi, preferred_element_type=F32)
                    + jnp.dot(tri, lw_lo, preferred_element_type=F32))
    icl_ref[...] = jax.nn.sigmoid(
        a0_ref[0] + jnp.dot(lo_ref[0, :, LORA_W:2 * LORA_W].astype(BF16), aup_ref[0],
                            preferred_element_type=F32))

    row = lax.broadcasted_iota(jnp.int32, (C, LANES), 0)
    col = lax.broadcasted_iota(jnp.int32, (C, LANES), 1)
    colh = col & (HEAD_SIZE - 1)
    order = (row - colh) * sgn
    strict = order > 0
    incl = order >= 0
    eye2 = row == colh
    lane_lo = col < HEAD_SIZE
    esum = _head_sum_matrix()
    rr = lax.broadcasted_iota(jnp.int32, (LANES, LANES), 0) >> HEAD_SHIFT
    cc = lax.broadcasted_iota(jnp.int32, (LANES, LANES), 1) >> HEAD_SHIFT
    blockdiag = rr == cc

    def bd(x):
        zero = jnp.zeros_like(x)
        return jnp.concatenate([jnp.where(lane_lo, x, zero), jnp.where(lane_lo, zero, x)], axis=0)

    def pmul(x, y):
        return jnp.dot(x.astype(BF16), bd(y.astype(BF16)), preferred_element_type=F32)

    nt_dims = (((1,), (1,)), ((), ()))
    tn_dims = (((0,), (0,)), ((), ()))

    def lane_tile(p, offset=0):
        return pl.ds(pl.multiple_of(offset + p * LANES, LANES), LANES)

    def head_sums(xs):
        stacked = jnp.concatenate([x.astype(BF16) for x in xs], axis=0)
        sums = jnp.dot(stacked, esum, preferred_element_type=F32)
        return [sums[j * C:(j + 1) * C] for j in range(len(xs))]

    def solve_body(g, carry):
        pairs = [g * WKV_SOLVE_GROUP + j for j in range(WKV_SOLVE_GROUP)]
        cols = [lane_tile(p) for p in pairs]
        r = [z_ref[0, :, cs] for cs in cols]
        k = [z_ref[0, :, lane_tile(p, d_model)] for p in pairs]
        v = [z_ref[0, :, lane_tile(p, 2 * d_model)] for p in pairs]
        icl = [icl_ref[:, cs] for cs in cols]
        lwp = [lw_ref[:, cs] for cs in cols]
        cum = [cum_ref[:, cs] for cs in cols]

        q = [ki * kk_ref[:, cs] for ki, cs in zip(k, cols)]
        n2 = head_sums([qi * qi for qi in q])
        kd = [ki * (1.0 + (ic - 1.0) * ka_ref[:, cs]) for ki, ic, cs in zip(k, icl, cols)]
        bsum = head_sums([ri * kdi * rk_ref[:, cs] for ri, kdi, cs in zip(r, kd, cols)])
        for cs, bs, vi in zip(cols, bsum, v):
            bo_ref[0, 0, :, cs] = bs * vi

        kk = [qi * lax.rsqrt(jnp.maximum(ni, 1e-12)) for qi, ni in zip(q, n2)]
        b = _gmap(lambda x, ic: x * ic, kk, icl)
        e_out = [jnp.exp(-x) for x in cum]
        at = _gmap(lambda x, cm, lw_: (-x * jnp.exp(cm - lw_)).astype(BF16), kk, cum, lwp)
        rt = _gmap(lambda x, cm: (x * jnp.exp(cm)).astype(BF16), r, cum)
        bt = _gmap(lambda x, e: (x * e).astype(BF16), b, e_out)
        kt = _gmap(lambda x, e: (x * e).astype(BF16), kd, e_out)

        lhs = _gmap(lambda a_, r_: jnp.concatenate([a_, r_], axis=0), at, rt)
        rhs_t = _gmap(lambda b_, k_: jnp.concatenate([bd(b_), bd(k_)], axis=0), bt, kt)
        pmat = _gmap(lambda l_, r_: lax.dot_general(l_, r_, nt_dims, preferred_element_type=F32), lhs, rhs_t)
        for p, l_, b_, k_, v_ in zip(pairs, lhs, bt, kt, v):
            lhs_ref[p] = l_
            btk_ref[p] = jnp.concatenate([b_, k_], axis=0)
            vb_ref[p] = v_.astype(BF16)
        lab = [jnp.where(strict, x[:C, :LANES], 0.0).astype(BF16) for x in pmat]
        for p, x in zip(pairs, pmat):
            rb_ref[p] = jnp.where(incl, x[C:, :LANES], 0.0).astype(BF16)
            lrk_ref[p] = jnp.concatenate([jnp.where(strict, x[:C, LANES:], 0.0).astype(BF16),
                                          jnp.where(incl, x[C:, LANES:], 0.0).astype(BF16)], axis=0)

        zero_b = jnp.zeros((C, LANES), BF16)
        ident = jnp.where(eye2, 1.0, 0.0).astype(BF16)
        first = (row >> 1) == (colh >> 1)
        tinv = [ident + jnp.where(first, x, zero_b) for x in lab]
        s = 2
        while s < C:
            sh = s.bit_length() - 1
            level = ((row >> (sh + 1)) == (colh >> (sh + 1))) & ((row >> sh) != (colh >> sh))
            off = [jnp.where(level, x, zero_b) for x in lab]
            tmp = _gmap(pmul, tinv, off)
            upd_t = _gmap(pmul, tmp, tinv)
            tinv = _gmap(lambda t_, x: t_ + x.astype(BF16), tinv, upd_t)
            s *= 2
        for p, x in zip(pairs, tinv):
            tinv_ref[p] = x
        return carry

    def update_body(g, carry):
        pairs = [g * WKV_UPDATE_GROUP + j for j in range(WKV_UPDATE_GROUP)]
        cols = [lane_tile(p) for p in pairs]
        h = [state_ref[p] for p in pairs]
        vb = [vb_ref[p] for p in pairs]
        hs = [jnp.dot(lhs_ref[p], h_.astype(BF16), preferred_element_type=F32)
              for p, h_ in zip(pairs, h)]
        lrkv = [jnp.dot(lrk_ref[p], bd(v_), preferred_element_type=F32) for p, v_ in zip(pairs, vb)]
        rhs_u = _gmap(lambda h_, x: (h_[:C] + x[:C]).astype(BF16), hs, lrkv)
        u = [jnp.dot(tinv_ref[p], bd(x), preferred_element_type=F32) for p, x in zip(pairs, rhs_u)]
        ub = [x.astype(BF16) for x in u]
        rbu = [jnp.dot(rb_ref[p], bd(x), preferred_element_type=F32) for p, x in zip(pairs, ub)]
        for cs, h_, xv, xu in zip(cols, hs, lrkv, rbu):
            y_ref[0, 0, :, cs] = h_[C:] + xv[C:] + xu
        upd = [lax.dot_general(btk_ref[p], jnp.concatenate([u_, v_], axis=0), tn_dims,
                               preferred_element_type=F32) for p, u_, v_ in zip(pairs, ub, vb)]
        for p, cs, h_, x in zip(pairs, cols, h, upd):
            tot = jnp.sum(lw_ref[:, cs], axis=0, keepdims=True)
            decay_rows = jnp.broadcast_to(jnp.exp(tot), (LANES, LANES)).T
            state_ref[p] = decay_rows * (h_ + jnp.where(blockdiag, x, 0.0))
        return carry

    n_pairs = d_model // LANES
    lax.fori_loop(0, n_pairs // WKV_SOLVE_GROUP, solve_body, 0)
    lax.fori_loop(0, n_pairs // WKV_UPDATE_GROUP, update_body, 0)


def _wkv_scan(z, dm, w_up, a_up, w0, a0, k_k, k_a, r_k):
    bsz, t, n = z.shape
    C = WKV_CHUNK
    nc = t // C
    n_pairs = dm // LANES
    lora_block = (n - LORA_COLS) // LORA_COLS

    def tchunk(dd, cc):
        return jnp.where(dd == 0, cc, nc - 1 - cc)

    def dir_map(bb, dd, cc):
        return (dd, 0, 0)

    def const2(bb, dd, cc):
        return (0, 0)

    def out_map(bb, dd, cc):
        return (dd, bb, tchunk(dd, cc), 0)

    out_sds = jax.ShapeDtypeStruct((2, bsz, t, dm), F32)
    kern = functools.partial(_wkv_kernel, d_model=dm)
    return pl.pallas_call(
        kern,
        grid=(bsz, 2, nc),
        in_specs=[
            pl.BlockSpec((1, C, 3 * dm), lambda bb, dd, cc: (bb, tchunk(dd, cc), 0)),
            pl.BlockSpec((1, C, LORA_COLS), lambda bb, dd, cc: (bb, tchunk(dd, cc), lora_block)),
            pl.BlockSpec((1, LORA_W, dm), dir_map),
            pl.BlockSpec((1, LORA_W, dm), dir_map),
            pl.BlockSpec((1, 1, dm), dir_map),
            pl.BlockSpec((1, 1, dm), dir_map),
            pl.BlockSpec((1, dm), const2),
            pl.BlockSpec((1, dm), const2),
            pl.BlockSpec((1, dm), const2),
        ],
        out_specs=[pl.BlockSpec((1, 1, C, dm), out_map), pl.BlockSpec((1, 1, C, dm), out_map)],
        out_shape=[out_sds, out_sds],
        scratch_shapes=[pltpu.VMEM((n_pairs, LANES, LANES), F32),
                        pltpu.VMEM((C, dm), F32), pltpu.VMEM((C, dm), F32), pltpu.VMEM((C, dm), F32),
                        pltpu.VMEM((n_pairs, C, LANES), BF16), pltpu.VMEM((n_pairs, 2 * C, LANES), BF16),
                        pltpu.VMEM((n_pairs, C, LANES), BF16), pltpu.VMEM((n_pairs, 2 * C, LANES), BF16),
                        pltpu.VMEM((n_pairs, 2 * C, LANES), BF16), pltpu.VMEM((n_pairs, C, LANES), BF16)],
        compiler_params=_params("arbitrary", "arbitrary", "arbitrary"),
        name="wkv_scan",
    )(z, z, w_up, a_up, w0, a0, k_k, k_a, r_k)


def _pool_kernel(p_ref, gate_ref, w_ref, scale_ref, o_ref, pad_ref, *, rows):
    g = pl.program_id(1)
    t, gi = p_ref.shape[1], p_ref.shape[2]
    zeros = jnp.zeros((POOL_PAD, gi), F32)
    pad_ref[0:POOL_PAD, :] = zeros
    pad_ref[POOL_PAD + t:POOL_PAD + t + POOL_PAD, :] = zeros
    pad_ref[POOL_PAD:POOL_PAD + t, :] = p_ref[0]
    w = w_ref[0]
    scale = scale_ref[...]

    for gidx, win in enumerate(POOL_WINDOWS):
        @pl.when(g == gidx)
        def _(win=win):
            half = win // 2

            def tile_body(i, carry):
                r0 = pl.multiple_of(i * rows, rows)
                n = rows + 2 * SUBLANES
                xt = pad_ref[pl.ds(r0 + POOL_PAD - SUBLANES, n), :]
                acc = xt
                step = 1
                while step < win:
                    acc = acc + pltpu.roll(acc, n - step, axis=0)
                    step *= 2
                if SUBLANES - half:
                    acc = pltpu.roll(acc, n - (SUBLANES - half), axis=0)
                acc = acc[0:rows]
                tt = r0 + lax.broadcasted_iota(jnp.int32, (rows, LANES), 0)
                cnt = (jnp.minimum(tt + (win - half), t) - jnp.maximum(tt - half, 0)).astype(F32)
                inv = 1.0 / cnt
                inv_full = jnp.concatenate([inv] * (gi // LANES), axis=1)
                dlt = acc * inv_full - xt[SUBLANES:SUBLANES + rows]
                out = jnp.dot(dlt.astype(BF16), w, preferred_element_type=F32) * scale
                gate = jax.nn.sigmoid(gate_ref[0, pl.ds(r0, rows), :])
                o_ref[0, pl.ds(r0, rows), :] = (gate * out).astype(o_ref.dtype)
                return carry

            lax.fori_loop(0, t // rows, tile_body, 0)


def _pool_branch(z, pool_col, gate_col, pool_w, pool_scale, rows=256):
    bsz, t, _ = z.shape
    ng, gi, go = pool_w.shape
    dm = ng * go
    kern = functools.partial(_pool_kernel, rows=rows)
    return pl.pallas_call(
        kern,
        grid=(bsz, ng),
        in_specs=[pl.BlockSpec((1, t, gi), lambda b, g: (b, 0, pool_col // gi + g)),
                  pl.BlockSpec((1, t, go), lambda b, g: (b, 0, gate_col // go + g)),
                  pl.BlockSpec((1, gi, go), lambda b, g: (g, 0, 0)),
                  pl.BlockSpec((1, go), lambda b, g: (0, g))],
        out_specs=pl.BlockSpec((1, t, go), lambda b, g: (b, 0, g)),
        out_shape=jax.ShapeDtypeStruct((bsz, t, dm), BF16),
        scratch_shapes=[pltpu.VMEM((t + 2 * POOL_PAD, gi), F32)],
        compiler_params=_params("arbitrary", "arbitrary"),
        name="pool_branch",
    )(z, z, pool_w, pool_scale)


def _merge_kernel(y_ref, bo_ref, lo_ref, gup_ref, gng_ref, gnb_ref, gate_ref, yb_ref, o_ref):
    gd_lo = 2 * LORA_W
    esum = _head_sum_matrix()
    inv_n = 1.0 / HEAD_SIZE
    gd = jax.nn.sigmoid(lo_ref[0, :, gd_lo:gd_lo + GATE_LORA_PAD]).astype(BF16)

    def group_body(gidx, carry):
        cols = [pl.ds(pl.multiple_of((gidx * MERGE_GROUP + j) * LANES, LANES), LANES) for j in range(MERGE_GROUP)]
        y = [y_ref[0, 0, :, cs] + y_ref[1, 0, :, cs] for cs in cols]
        mu = [jnp.dot(x.astype(BF16), esum, preferred_element_type=F32) * inv_n for x in y]
        g = [jnp.dot(gd, gup_ref[:, cs], preferred_element_type=F32) for cs in cols]
        yc = _gmap(lambda x, m: x - m, y, mu)
        var = [jnp.dot((x * x).astype(BF16), esum, preferred_element_type=F32) * inv_n for x in yc]
        for cs, x, vr, gi in zip(cols, yc, var, g):
            yn = x * lax.rsqrt(vr + GN_EPS) * gng_ref[:, cs] + gnb_ref[:, cs]
            yn = yn + bo_ref[0, 0, :, cs] + bo_ref[1, 0, :, cs]
            ya = jax.nn.sigmoid(gate_ref[0, :, cs]) * (yn * gi)
            o_ref[0, :, cs] = (ya + yb_ref[0, :, cs].astype(F32)).astype(o_ref.dtype)
        return carry

    lax.fori_loop(0, o_ref.shape[2] // (LANES * MERGE_GROUP), group_body, 0)


def _merge(y, bo, z, gate_col, g_up, ln_g, ln_b, yb, tt=128):
    _, bsz, t, dm = y.shape
    n = z.shape[-1]
    lora_block = (n - LORA_COLS) // LORA_COLS
    return pl.pallas_call(
        _merge_kernel,
        grid=(bsz, t // tt),
        in_specs=[pl.BlockSpec((2, 1, tt, dm), lambda b, i: (0, b, i, 0)),
                  pl.BlockSpec((2, 1, tt, dm), lambda b, i: (0, b, i, 0)),
                  pl.BlockSpec((1, tt, LORA_COLS), lambda b, i: (b, i, lora_block)),
                  pl.BlockSpec((GATE_LORA_PAD, dm), lambda b, i: (0, 0)),
                  pl.BlockSpec((1, dm), lambda b, i: (0, 0)),
                  pl.BlockSpec((1, dm), lambda b, i: (0, 0)),
                  pl.BlockSpec((1, tt, dm), lambda b, i: (b, i, gate_col // dm)),
                  pl.BlockSpec((1, tt, dm), lambda b, i: (b, i, 0))],
        out_specs=pl.BlockSpec((1, tt, dm), lambda b, i: (b, i, 0)),
        out_shape=jax.ShapeDtypeStruct((bsz, t, dm), BF16),
        compiler_params=_params("arbitrary", "arbitrary"),
        name="wkv_merge",
    )(y, bo, z, g_up, ln_g, ln_b, z, yb)


def _xattn_kernel(q_ref, k_ref, v_ref, o_ref, *, head_dim):
    scale = head_dim ** -0.5
    nt_dims = (((1,), (1,)), ((), ()))
    for h in range(X_HEADS):
        cs = slice(h * head_dim, (h + 1) * head_dim)
        s = lax.dot_general(q_ref[0, :, cs], k_ref[0, :, cs], nt_dims, preferred_element_type=F32) * scale
        m = jnp.max(s, axis=-1, keepdims=True)
        e = jnp.exp(s - m)
        p = e / jnp.sum(e, axis=-1, keepdims=True)
        o_ref[0, :, cs] = jnp.dot(p.astype(BF16), v_ref[0, :, cs], preferred_element_type=F32).astype(o_ref.dtype)


def _xattn(q, k, v, tq=512):
    bsz, t, dm = q.shape
    m = k.shape[1]
    kern = functools.partial(_xattn_kernel, head_dim=dm // X_HEADS)
    return pl.pallas_call(
        kern,
        grid=(bsz, t // tq),
        in_specs=[pl.BlockSpec((1, tq, dm), lambda b, i: (b, i, 0)),
                  pl.BlockSpec((1, m, dm), lambda b, i: (b, 0, 0)),
                  pl.BlockSpec((1, m, dm), lambda b, i: (b, 0, 0))],
        out_specs=pl.BlockSpec((1, tq, dm), lambda b, i: (b, i, 0)),
        out_shape=jax.ShapeDtypeStruct((bsz, t, dm), BF16),
        compiler_params=_params("arbitrary", "arbitrary"),
        name="xattn",
    )(q, k, v)


def _pad_to(x, axis, size):
    pad = [(0, 0)] * x.ndim
    pad[axis] = (0, size - x.shape[axis])
    return jnp.pad(x, pad)


def _gain_tile(g):
    return jnp.broadcast_to(g[:, None], (g.shape[0], LANES))


def _in_proj_operands(p, dm):
    w_in, shift_w = p['w_in'], p['shift_w']
    gate_lora = p['g_up'].shape[0]
    c_rkv = 3 * dm
    c_lora = c_rkv + 2 * LORA_W + gate_lora
    pool_width = p['pool_w'].shape[0] * p['pool_w'].shape[1]
    c_pool = c_lora + pool_width
    wb = w_in.astype(BF16)
    w = jnp.concatenate([wb[:, :c_rkv], wb[:, c_pool:], wb[:, c_lora:c_pool],
                         _pad_to(wb[:, c_rkv:c_lora], 1, LORA_COLS)], axis=1)
    ident = jnp.zeros((3, 2 * dm + pool_width), F32).at[1].set(1.0)
    taps = jnp.concatenate([shift_w[:, :c_rkv], ident, _pad_to(shift_w[:, c_rkv:c_lora], 1, LORA_COLS)], axis=1)
    return w, taps, c_rkv, c_rkv + dm, c_rkv + 2 * dm


def _trunk(xs, mem, lp, norm_final_g):
    t, dm = xs[0].shape[1:]
    rows = [x.shape[0] * t for x in xs]
    bsz = sum(x.shape[0] for x in xs)
    n_mem = mem.shape[1]
    m_tok = bsz * t
    hs = tuple(x.reshape(-1, dm) for x in xs)
    memf = mem.reshape(bsz * n_mem, dm)
    depth = lp['w_in'].shape[0]
    h = None
    for l in range(depth):
        p = {name: arr[l] for name, arr in lp.items()}
        w_all, taps_all, gate_a_col, gate_b_col, pool_col = _in_proj_operands(p, dm)
        g_up = _pad_to(p['g_up'], 0, GATE_LORA_PAD).astype(BF16)

        if h is None:
            xn = _rmsnorm2(hs[0], hs[1], p['norm_mix_g'], BF16)
            res = hs
        else:
            xn = _rmsnorm(h, p['norm_mix_g'], BF16)
            res = h
        z = _in_proj(xn, w_all, taps_all, bsz, t)

        w_up = jnp.stack([p['w_up_f'], p['w_up_b']]).astype(BF16)
        a_up = jnp.stack([p['a_up_f'], p['a_up_b']]).astype(BF16)
        w0 = jnp.stack([p['w0_f'], p['w0_b']]).reshape(2, 1, dm)
        a0 = jnp.stack([p['a0_f'], p['a0_b']]).reshape(2, 1, dm)
        y, bo = _wkv_scan(z, dm, w_up, a_up, w0, a0,
                          p['k_k'].reshape(1, dm), p['k_a'].reshape(1, dm), p['r_k'].reshape(1, dm))
        yb = _pool_branch(z, pool_col, gate_b_col, p['pool_w'].astype(BF16), p['pool_scale'].reshape(1, dm))
        merged = _merge(y, bo, z, gate_a_col, g_up, p['ln_x_g'].reshape(1, dm), p['ln_x_b'].reshape(1, dm), yb)
        h, h_bf, scale = _matmul_stats(merged.reshape(m_tok, dm), p['w_out'].astype(BF16), 1024, 512,
                                       residual=res, name="out_proj")

        mn = _rmsnorm(memf, p['norm_mem_g'], BF16)
        q = _matmul_wres(h_bf, p['xq'], BF16, 1024, 512, scale=scale, gain=_gain_tile(p['norm_x_g']), name="xq")
        kx = _matmul_wres(mn, p['xk'], BF16, 1024, 512, name="xk")
        vx = _matmul_wres(mn, p['xv'], BF16, 1024, 512, name="xv")
        o = _xattn(q.reshape(bsz, t, dm), kx.reshape(bsz, n_mem, dm), vx.reshape(bsz, n_mem, dm))
        h, h_bf, scale = _matmul_stats(o.reshape(m_tok, dm), p['xo'].astype(BF16), 1024, 512,
                                       residual=h, name="xo")

        hidden = p['ffn_w2'].shape[0]
        act = _swiglu_up(h_bf, scale, _gain_tile(p['norm_ffn_g']), p['ffn_w13'], hidden, FFN_TM, FFN_TN)
        w2 = p['ffn_w2'].astype(BF16)
        if l + 1 < depth:
            h = _matmul(act, w2, F32, FFN_DOWN_TM, FFN_DOWN_TN, residual=h, name="ffn_down")

    outs, off = [], 0
    for x, nrow in zip(xs, rows):
        y = _matmul_res_norm(act, w2, h, norm_final_g, FFN_DOWN_TM, FFN_DOWN_TN,
                             off // FFN_DOWN_TM, nrow // FFN_DOWN_TM, name="ffn_down_norm")
        outs.append(y.reshape(x.shape))
        off += nrow
    return tuple(outs)


def kernel(x_prompt, x_sample, mem_prompt, mem_sample, norm_mix_g, w_in, shift_w, w0_f, w_up_f, w0_b, w_up_b, a0_f, a_up_f, a0_b, a_up_b, g_up, k_k, k_a, r_k, ln_x_g, ln_x_b, pool_w, pool_scale, w_out, norm_x_g, norm_mem_g, xq, xk, xv, xo, norm_ffn_g, ffn_w13, ffn_w2, norm_final_g):
    assert x_prompt.shape[1:] == x_sample.shape[1:] and mem_prompt.shape[1:] == mem_sample.shape[1:]
    lp = {
        'norm_mix_g': norm_mix_g, 'w_in': w_in, 'shift_w': shift_w,
        'w0_f': w0_f, 'w_up_f': w_up_f, 'w0_b': w0_b, 'w_up_b': w_up_b,
        'a0_f': a0_f, 'a_up_f': a_up_f, 'a0_b': a0_b, 'a_up_b': a_up_b,
        'g_up': g_up, 'k_k': k_k, 'k_a': k_a, 'r_k': r_k.reshape(r_k.shape[0], -1),
        'ln_x_g': ln_x_g, 'ln_x_b': ln_x_b,
        'pool_w': pool_w, 'pool_scale': pool_scale, 'w_out': w_out,
        'norm_x_g': norm_x_g, 'norm_mem_g': norm_mem_g, 'xq': xq, 'xk': xk, 'xv': xv, 'xo': xo,
        'norm_ffn_g': norm_ffn_g, 'ffn_w13': ffn_w13, 'ffn_w2': ffn_w2,
    }
    mem = jnp.concatenate([mem_prompt, mem_sample], axis=0)
    return _trunk((x_prompt, x_sample), mem, lp, norm_final_g)
```

```python
import functools
import math

import jax
import jax.numpy as jnp
from jax import lax
from jax.experimental import pallas as pl
from jax.experimental.pallas import tpu as pltpu

F32 = jnp.float32
BF16 = jnp.bfloat16

LANES = 128
SUBLANES = 8
VMEM_LIMIT_BYTES = 56 * 1024 * 1024
VMEM_LIMIT_BYTES_MAX = 60000 * 1024

HEAD_SIZE = 64
HEAD_SHIFT = 6
X_HEADS = 4
POOL_WINDOWS = (2, 4, 8, 16)
POOL_PAD = 16
GN_EPS = 64e-5
NORM_EPS = 1e-6
DECAY_SCALE = -math.exp(-0.5)
WKV_CHUNK = 64
WKV_SOLVE_GROUP = 32
WKV_UPDATE_GROUP = 16
MERGE_GROUP = 8
LORA_W = 128
GATE_LORA_PAD = 512
LORA_COLS = 1024
PROJ_TM = 1024
PROJ_TN = 512
FFN_TM = 1024
FFN_TN = 256
FFN_DOWN_TM = 512
FFN_DOWN_TN = 256
IN_TN = 512
IN_ROW_CHUNK = 512


def _params(*semantics):
    return pltpu.CompilerParams(dimension_semantics=semantics, vmem_limit_bytes=VMEM_LIMIT_BYTES)


def _rms(x, g):
    ms = jnp.mean(x * x, axis=-1, keepdims=True)
    return x * lax.rsqrt(ms + NORM_EPS) * g


def _rmsnorm_kernel(x_ref, g_ref, o_ref):
    o_ref[...] = _rms(x_ref[...], g_ref[...]).astype(o_ref.dtype)


def _rmsnorm(x, g, out_dtype, tm=256, row_block_offset=0, n_row_blocks=None):
    m, d = x.shape
    nb = m // tm if n_row_blocks is None else n_row_blocks
    return pl.pallas_call(
        _rmsnorm_kernel,
        grid=(nb,),
        in_specs=[pl.BlockSpec((tm, d), lambda i: (i + row_block_offset, 0)),
                  pl.BlockSpec((1, d), lambda i: (0, 0))],
        out_specs=pl.BlockSpec((tm, d), lambda i: (i, 0)),
        out_shape=jax.ShapeDtypeStruct((nb * tm, d), out_dtype),
        compiler_params=_params("parallel"),
        name="rmsnorm",
    )(x, g.reshape(1, d))


def _rmsnorm2_kernel(xa_ref, xb_ref, g_ref, o_ref, *, na):
    x = jnp.where(pl.program_id(0) < na, xa_ref[...], xb_ref[...])
    o_ref[...] = _rms(x, g_ref[...]).astype(o_ref.dtype)


def _rmsnorm2(xa, xb, g, out_dtype, tm=256):
    d = xa.shape[1]
    na, nb = xa.shape[0] // tm, xb.shape[0] // tm
    return pl.pallas_call(
        functools.partial(_rmsnorm2_kernel, na=na),
        grid=(na + nb,),
        in_specs=[pl.BlockSpec((tm, d), lambda i: (jnp.minimum(i, na - 1), 0)),
                  pl.BlockSpec((tm, d), lambda i: (jnp.maximum(i - na, 0), 0)),
                  pl.BlockSpec((1, d), lambda i: (0, 0))],
        out_specs=pl.BlockSpec((tm, d), lambda i: (i, 0)),
        out_shape=jax.ShapeDtypeStruct(((na + nb) * tm, d), out_dtype),
        compiler_params=_params("arbitrary"),
        name="rmsnorm2",
    )(xa, xb, g.reshape(1, d))


def _matmul_kernel(x_ref, w_ref, o_ref):
    o_ref[...] = jnp.dot(x_ref[...], w_ref[...], preferred_element_type=F32).astype(o_ref.dtype)


def _matmul_res_kernel(x_ref, w_ref, r_ref, o_ref):
    acc = jnp.dot(x_ref[...], w_ref[...], preferred_element_type=F32)
    o_ref[...] = (r_ref[...] + acc).astype(o_ref.dtype)


def _matmul(x, w, out_dtype, tm, tn, residual=None, name="matmul"):
    m, k = x.shape
    n = w.shape[1]
    in_specs = [pl.BlockSpec((tm, k), lambda i, j: (i, 0)),
                pl.BlockSpec((k, tn), lambda i, j: (0, j))]
    args = [x, w]
    body = _matmul_kernel
    if residual is not None:
        in_specs.append(pl.BlockSpec((tm, tn), lambda i, j: (i, j)))
        args.append(residual)
        body = _matmul_res_kernel
    return pl.pallas_call(
        body,
        grid=(m // tm, n // tn),
        in_specs=in_specs,
        out_specs=pl.BlockSpec((tm, tn), lambda i, j: (i, j)),
        out_shape=jax.ShapeDtypeStruct((m, n), out_dtype),
        compiler_params=_params("arbitrary", "arbitrary"),
        name=name,
    )(*args)


def _matmul_res_norm_kernel(x_ref, w_ref, r_ref, g_ref, o_ref):
    j = pl.program_id(1)
    tn = w_ref.shape[1]
    acc = jnp.dot(x_ref[...], w_ref[...], preferred_element_type=F32)
    o_ref[:, pl.ds(pl.multiple_of(j * tn, tn), tn)] = r_ref[...] + acc

    @pl.when(j == pl.num_programs(1) - 1)
    def _():
        o_ref[...] = _rms(o_ref[...], g_ref[...])


def _matmul_res_norm(x, w, residual, g, tm, tn, row_block_offset, n_row_blocks, name):
    k = x.shape[1]
    n = w.shape[1]
    off = row_block_offset
    return pl.pallas_call(
        _matmul_res_norm_kernel,
        grid=(n_row_blocks, n // tn),
        in_specs=[pl.BlockSpec((tm, k), lambda i, j: (i + off, 0)),
                  pl.BlockSpec((k, tn), lambda i, j: (0, j)),
                  pl.BlockSpec((tm, tn), lambda i, j: (i + off, j)),
                  pl.BlockSpec((1, n), lambda i, j: (0, 0))],
        out_specs=pl.BlockSpec((tm, n), lambda i, j: (i, 0)),
        out_shape=jax.ShapeDtypeStruct((n_row_blocks * tm, n), F32),
        compiler_params=pltpu.CompilerParams(dimension_semantics=("arbitrary", "arbitrary"),
                                             vmem_limit_bytes=VMEM_LIMIT_BYTES_MAX),
        name=name,
    )(x, w, residual, g.reshape(1, n))


def _in_proj_kernel(x_ref, w_ref, taps_ref, o_ref):
    t = x_ref.shape[0]
    rc = IN_ROW_CHUNK
    w = w_ref[...]
    taps = taps_ref[...]
    ridx = lax.broadcasted_iota(jnp.int32, (rc, w.shape[1]), 0)
    zs = [jnp.dot(x_ref[c * rc:(c + 1) * rc, :], w, preferred_element_type=F32) for c in range(t // rc)]
    zero_row = jnp.zeros((1, w.shape[1]), F32)
    for c, z in enumerate(zs):
        prev_row = zs[c - 1][rc - 1:rc, :] if c > 0 else zero_row
        next_row = zs[c + 1][0:1, :] if c + 1 < len(zs) else zero_row
        zm1 = jnp.where(ridx == 0, prev_row, pltpu.roll(z, 1, axis=0))
        zp1 = jnp.where(ridx == rc - 1, next_row, pltpu.roll(z, rc - 1, axis=0))
        o_ref[0, c * rc:(c + 1) * rc, :] = zm1 * taps[0:1, :] + z * taps[1:2, :] + zp1 * taps[2:3, :]


def _in_proj(xn, w, taps, bsz, t):
    k = xn.shape[1]
    n = w.shape[1]
    return pl.pallas_call(
        _in_proj_kernel,
        grid=(bsz, n // IN_TN),
        in_specs=[pl.BlockSpec((t, k), lambda b, j: (b, 0), pipeline_mode=pl.Buffered(1)),
                  pl.BlockSpec((k, IN_TN), lambda b, j: (0, j)),
                  pl.BlockSpec((3, IN_TN), lambda b, j: (0, j))],
        out_specs=pl.BlockSpec((1, t, IN_TN), lambda b, j: (b, 0, j)),
        out_shape=jax.ShapeDtypeStruct((bsz, t, n), F32),
        compiler_params=_params("arbitrary", "arbitrary"),
        name="in_proj",
    )(xn, w, taps)


def _lane_tiled(x, width):
    return jnp.concatenate([x] * (width // LANES), axis=1)


def _matmul_stats_kernel(x_ref, w_ref, *rest, na, inv_d):
    *r_refs, o_ref, ob_ref, sc_ref, ssq_ref = rest
    j = pl.program_id(1)
    acc = jnp.dot(x_ref[...], w_ref[...], preferred_element_type=F32)
    res = r_refs[0][...] if len(r_refs) == 1 else jnp.where(pl.program_id(0) < na, r_refs[0][...], r_refs[1][...])
    h = res + acc
    o_ref[...] = h
    ob_ref[...] = h.astype(BF16)
    hh = h * h
    part = hh[:, 0:LANES]
    for c in range(1, hh.shape[1] // LANES):
        part = part + hh[:, c * LANES:(c + 1) * LANES]

    @pl.when(j == 0)
    def _():
        ssq_ref[...] = part

    @pl.when(j > 0)
    def _():
        ssq_ref[...] += part

    @pl.when(j == pl.num_programs(1) - 1)
    def _():
        ms = jnp.sum(ssq_ref[...], axis=-1, keepdims=True) * inv_d
        sc_ref[...] = jnp.broadcast_to(lax.rsqrt(ms + NORM_EPS), sc_ref.shape)


def _matmul_stats(x, w, tm, tn, residual, name):
    m, k = x.shape
    n = w.shape[1]
    in_specs = [pl.BlockSpec((tm, k), lambda i, j: (i, 0)),
                pl.BlockSpec((k, tn), lambda i, j: (0, j))]
    na = 0
    if isinstance(residual, tuple):
        ra, rb = residual
        na = ra.shape[0] // tm
        in_specs += [pl.BlockSpec((tm, tn), lambda i, j: (jnp.minimum(i, na - 1), j)),
                     pl.BlockSpec((tm, tn), lambda i, j: (jnp.maximum(i - na, 0), j))]
        res_args = [ra, rb]
    else:
        in_specs.append(pl.BlockSpec((tm, tn), lambda i, j: (i, j)))
        res_args = [residual]
    return pl.pallas_call(
        functools.partial(_matmul_stats_kernel, na=na, inv_d=1.0 / n),
        grid=(m // tm, n // tn),
        in_specs=in_specs,
        out_specs=[pl.BlockSpec((tm, tn), lambda i, j: (i, j)),
                   pl.BlockSpec((tm, tn), lambda i, j: (i, j)),
                   pl.BlockSpec((tm, LANES), lambda i, j: (i, 0))],
        out_shape=[jax.ShapeDtypeStruct((m, n), F32), jax.ShapeDtypeStruct((m, n), BF16),
                   jax.ShapeDtypeStruct((m, LANES), F32)],
        scratch_shapes=[pltpu.VMEM((tm, LANES), F32)],
        compiler_params=_params("arbitrary", "arbitrary"),
        name=name,
    )(x, w, *res_args)


def _cast_weight(w_ref, wb_ref, g_ref):
    w = w_ref[...]
    if g_ref is not None:
        w = w * _lane_tiled(g_ref[...], w.shape[1])
    wb_ref[...] = w.astype(BF16)


def _wres_kernel(*refs, normed):
    if normed:
        x_ref, sc_ref, g_ref, w_ref, o_ref, wb_ref = refs
    else:
        x_ref, w_ref, o_ref, wb_ref = refs
        sc_ref = g_ref = None

    @pl.when(pl.program_id(1) == 0)
    def _():
        _cast_weight(w_ref, wb_ref, g_ref)

    acc = jnp.dot(x_ref[...], wb_ref[...], preferred_element_type=F32)
    if normed:
        acc = acc * _lane_tiled(sc_ref[...], acc.shape[1])
    o_ref[...] = acc.astype(o_ref.dtype)


def _matmul_wres(x, w, out_dtype, tm, tn, scale=None, gain=None, name="matmul_wres"):
    m, k = x.shape
    n = w.shape[1]
    normed = scale is not None
    in_specs = [pl.BlockSpec((tm, k), lambda j, i: (i, 0))]
    args = [x]
    if normed:
        in_specs += [pl.BlockSpec((tm, LANES), lambda j, i: (i, 0)),
                     pl.BlockSpec((k, LANES), lambda j, i: (0, 0))]
        args += [scale, gain]
    in_specs.append(pl.BlockSpec((k, tn), lambda j, i: (0, j)))
    args.append(w)
    return pl.pallas_call(
        functools.partial(_wres_kernel, normed=normed),
        grid=(n // tn, m // tm),
        in_specs=in_specs,
        out_specs=pl.BlockSpec((tm, tn), lambda j, i: (i, j)),
        out_shape=jax.ShapeDtypeStruct((m, n), out_dtype),
        scratch_shapes=[pltpu.VMEM((k, tn), BF16)],
        compiler_params=_params("arbitrary", "arbitrary"),
        name=name,
    )(*args)


def _swiglu_kernel(x_ref, sc_ref, g_ref, wg_ref, wu_ref, o_ref, wgb_ref, wub_ref):
    @pl.when(pl.program_id(1) == 0)
    def _():
        _cast_weight(wg_ref, wgb_ref, g_ref)
        _cast_weight(wu_ref, wub_ref, g_ref)

    x = x_ref[...]
    sc = _lane_tiled(sc_ref[...], o_ref.shape[1])
    gate = jnp.dot(x, wgb_ref[...], preferred_element_type=F32) * sc
    up = jnp.dot(x, wub_ref[...], preferred_element_type=F32) * sc
    o_ref[...] = (gate * jax.nn.sigmoid(gate) * up).astype(o_ref.dtype)


def _swiglu_up(x, scale, gain, w13, hidden, tm, tn):
    m, k = x.shape
    nb = hidden // tn
    return pl.pallas_call(
        _swiglu_kernel,
        grid=(nb, m // tm),
        in_specs=[pl.BlockSpec((tm, k), lambda j, i: (i, 0)),
                  pl.BlockSpec((tm, LANES), lambda j, i: (i, 0)),
                  pl.BlockSpec((k, LANES), lambda j, i: (0, 0)),
                  pl.BlockSpec((k, tn), lambda j, i: (0, j)),
                  pl.BlockSpec((k, tn), lambda j, i: (0, j + nb))],
        out_specs=pl.BlockSpec((tm, tn), lambda j, i: (i, j)),
        out_shape=jax.ShapeDtypeStruct((m, hidden), BF16),
        scratch_shapes=[pltpu.VMEM((k, tn), BF16), pltpu.VMEM((k, tn), BF16)],
        compiler_params=_params("arbitrary", "arbitrary"),
        name="swiglu_up",
    )(x, scale, gain, w13, w13)


def _head_sum_matrix():
    r = lax.broadcasted_iota(jnp.int32, (LANES, LANES), 0) >> HEAD_SHIFT
    c = lax.broadcasted_iota(jnp.int32, (LANES, LANES), 1) >> HEAD_SHIFT
    return jnp.where(r == c, 1.0, 0.0).astype(BF16)


def _split2(x):
    hi = x.astype(BF16)
    return hi, (x - hi.astype(F32)).astype(BF16)


def _gmap(f, *lists):
    return [f(*xs) for xs in zip(*lists)]


def _wkv_kernel(z_ref, lo_ref, wup_ref, aup_ref, w0_ref, a0_ref, kk_ref, ka_ref, rk_ref,
                y_ref, bo_ref, state_ref, lw_ref, cum_ref, icl_ref,
                tinv_ref, lrk_ref, rb_ref, lhs_ref, btk_ref, vb_ref, *, d_model):
    C = WKV_CHUNK
    d = pl.program_id(1)
    c = pl.program_id(2)
    sgn = 1 - 2 * d

    @pl.when(c == 0)
    def _():
        state_ref[...] = jnp.zeros_like(state_ref)

    wl = w0_ref[0] + jnp.dot(jnp.tanh(lo_ref[0, :, 0:LORA_W]).astype(BF16), wup_ref[0],
                             preferred_element_type=F32)
    lw = DECAY_SCALE * jax.nn.sigmoid(wl)
    lw_ref[...] = lw
    r64 = lax.broadcasted_iota(jnp.int32, (C, C), 0)
    c64 = lax.broadcasted_iota(jnp.int32, (C, C), 1)
    tri = jnp.where((r64 - c64) * sgn >= 0, 1.0, 0.0).astype(BF16)
    lw_hi, lw_lo = _split2(lw)
    cum_ref[...] = (jnp.dot(tri, lw_hi, preferred_element_type=F32)
                    + jnp.dot(tri, lw_lo, preferred_element_type=F32))
    icl_ref[...] = jax.nn.sigmoid(
        a0_ref[0] + jnp.dot(lo_ref[0, :, LORA_W:2 * LORA_W].astype(BF16), aup_ref[0],
                            preferred_element_type=F32))

    row = lax.broadcasted_iota(jnp.int32, (C, LANES), 0)
    col = lax.broadcasted_iota(jnp.int32, (C, LANES), 1)
    colh = col & (HEAD_SIZE - 1)
    order = (row - colh) * sgn
    strict = order > 0
    incl = order >= 0
    eye2 = row == colh
    lane_lo = col < HEAD_SIZE
    esum = _head_sum_matrix()
    rr = lax.broadcasted_iota(jnp.int32, (LANES, LANES), 0) >> HEAD_SHIFT
    cc = lax.broadcasted_iota(jnp.int32, (LANES, LANES), 1) >> HEAD_SHIFT
    blockdiag = rr == cc

    def bd(x):
        zero = jnp.zeros_like(x)
        return jnp.concatenate([jnp.where(lane_lo, x, zero), jnp.where(lane_lo, zero, x)], axis=0)

    def pmul(x, y):
        return jnp.dot(x.astype(BF16), bd(y.astype(BF16)), preferred_element_type=F32)

    nt_dims = (((1,), (1,)), ((), ()))
    tn_dims = (((0,), (0,)), ((), ()))

    def lane_tile(p, offset=0):
        return pl.ds(pl.multiple_of(offset + p * LANES, LANES), LANES)

    def head_sums(xs):
        stacked = jnp.concatenate([x.astype(BF16) for x in xs], axis=0)
        sums = jnp.dot(stacked, esum, preferred_element_type=F32)
        return [sums[j * C:(j + 1) * C] for j in range(len(xs))]

    def solve_body(g, carry):
        pairs = [g * WKV_SOLVE_GROUP + j for j in range(WKV_SOLVE_GROUP)]
        cols = [lane_tile(p) for p in pairs]
        r = [z_ref[0, :, cs] for cs in cols]
        k = [z_ref[0, :, lane_tile(p, d_model)] for p in pairs]
        v = [z_ref[0, :, lane_tile(p, 2 * d_model)] for p in pairs]
        icl = [icl_ref[:, cs] for cs in cols]
        lwp = [lw_ref[:, cs] for cs in cols]
        cum = [cum_ref[:, cs] for cs in cols]

        q = [ki * kk_ref[:, cs] for ki, cs in zip(k, cols)]
        n2 = head_sums([qi * qi for qi in q])
        kd = [ki * (1.0 + (ic - 1.0) * ka_ref[:, cs]) for ki, ic, cs in zip(k, icl, cols)]
        bsum = head_sums([ri * kdi * rk_ref[:, cs] for ri, kdi, cs in zip(r, kd, cols)])
        for cs, bs, vi in zip(cols, bsum, v):
            bo_ref[0, 0, :, cs] = (bs * vi).astype(bo_ref.dtype)

        kk = [qi * lax.rsqrt(jnp.maximum(ni, 1e-12)) for qi, ni in zip(q, n2)]
        b = _gmap(lambda x, ic: x * ic, kk, icl)
        e_out = [jnp.exp(-x) for x in cum]
        at = _gmap(lambda x, cm, lw_: (-x * jnp.exp(cm - lw_)).astype(BF16), kk, cum, lwp)
        rt = _gmap(lambda x, cm: (x * jnp.exp(cm)).astype(BF16), r, cum)
        bt = _gmap(lambda x, e: (x * e).astype(BF16), b, e_out)
        kt = _gmap(lambda x, e: (x * e).astype(BF16), kd, e_out)

        lhs = _gmap(lambda a_, r_: jnp.concatenate([a_, r_], axis=0), at, rt)
        rhs_t = _gmap(lambda b_, k_: jnp.concatenate([bd(b_), bd(k_)], axis=0), bt, kt)
        pmat = _gmap(lambda l_, r_: lax.dot_general(l_, r_, nt_dims, preferred_element_type=F32), lhs, rhs_t)
        for p, l_, b_, k_, v_ in zip(pairs, lhs, bt, kt, v):
            lhs_ref[p] = l_
            btk_ref[p] = jnp.concatenate([b_, k_], axis=0)
            vb_ref[p] = v_.astype(BF16)
        lab = [jnp.where(strict, x[:C, :LANES], 0.0).astype(BF16) for x in pmat]
        for p, x in zip(pairs, pmat):
            rb_ref[p] = jnp.where(incl, x[C:, :LANES], 0.0).astype(BF16)
            lrk_ref[p] = jnp.concatenate([jnp.where(strict, x[:C, LANES:], 0.0).astype(BF16),
                                          jnp.where(incl, x[C:, LANES:], 0.0).astype(BF16)], axis=0)

        zero_b = jnp.zeros((C, LANES), BF16)
        ident = jnp.where(eye2, 1.0, 0.0).astype(BF16)
        first = (row >> 1) == (colh >> 1)
        tinv = [ident + jnp.where(first, x, zero_b) for x in lab]
        s = 2
        while s < C:
            sh = s.bit_length() - 1
            level = ((row >> (sh + 1)) == (colh >> (sh + 1))) & ((row >> sh) != (colh >> sh))
            off = [jnp.where(level, x, zero_b) for x in lab]
            tmp = _gmap(pmul, tinv, off)
            upd_t = _gmap(pmul, tmp, tinv)
            tinv = _gmap(lambda t_, x: t_ + x.astype(BF16), tinv, upd_t)
            s *= 2
        for p, x in zip(pairs, tinv):
            tinv_ref[p] = x
        return carry

    def update_body(g, carry):
        pairs = [g * WKV_UPDATE_GROUP + j for j in range(WKV_UPDATE_GROUP)]
        cols = [lane_tile(p) for p in pairs]
        h = [state_ref[p] for p in pairs]
        vb = [vb_ref[p] for p in pairs]
        hs = [jnp.dot(lhs_ref[p], h_.astype(BF16), preferred_element_type=F32)
              for p, h_ in zip(pairs, h)]
        lrkv = [jnp.dot(lrk_ref[p], bd(v_), preferred_element_type=F32) for p, v_ in zip(pairs, vb)]
        rhs_u = _gmap(lambda h_, x: (h_[:C] + x[:C]).astype(BF16), hs, lrkv)
        u = [jnp.dot(tinv_ref[p], bd(x), preferred_element_type=F32) for p, x in zip(pairs, rhs_u)]
        ub = [x.astype(BF16) for x in u]
        rbu = [jnp.dot(rb_ref[p], bd(x), preferred_element_type=F32) for p, x in zip(pairs, ub)]
        for cs, h_, xv, xu in zip(cols, hs, lrkv, rbu):
            y_ref[0, 0, :, cs] = (h_[C:] + xv[C:] + xu).astype(y_ref.dtype)
        upd = [lax.dot_general(btk_ref[p], jnp.concatenate([u_, v_], axis=0), tn_dims,
                               preferred_element_type=F32) for p, u_, v_ in zip(pairs, ub, vb)]
        for p, cs, h_, x in zip(pairs, cols, h, upd):
            tot = jnp.sum(lw_ref[:, cs], axis=0, keepdims=True)
            decay_rows = jnp.broadcast_to(jnp.exp(tot), (LANES, LANES)).T
            state_ref[p] = decay_rows * (h_ + jnp.where(blockdiag, x, 0.0))
        return carry

    n_pairs = d_model // LANES
    lax.fori_loop(0, n_pairs // WKV_SOLVE_GROUP, solve_body, 0)
    lax.fori_loop(0, n_pairs // WKV_UPDATE_GROUP, update_body, 0)


def _wkv_scan(z, dm, w_up, a_up, w0, a0, k_k, k_a, r_k):
    bsz, t, n = z.shape
    C = WKV_CHUNK
    nc = t // C
    n_pairs = dm // LANES
    lora_block = (n - LORA_COLS) // LORA_COLS

    def tchunk(dd, cc):
        return jnp.where(dd == 0, cc, nc - 1 - cc)

    def dir_map(bb, dd, cc):
        return (dd, 0, 0)

    def const2(bb, dd, cc):
        return (0, 0)

    def out_map(bb, dd, cc):
        return (dd, bb, tchunk(dd, cc), 0)

    out_sds = jax.ShapeDtypeStruct((2, bsz, t, dm), BF16)
    kern = functools.partial(_wkv_kernel, d_model=dm)
    return pl.pallas_call(
        kern,
        grid=(bsz, 2, nc),
        in_specs=[
            pl.BlockSpec((1, C, 3 * dm), lambda bb, dd, cc: (bb, tchunk(dd, cc), 0)),
            pl.BlockSpec((1, C, LORA_COLS), lambda bb, dd, cc: (bb, tchunk(dd, cc), lora_block)),
            pl.BlockSpec((1, LORA_W, dm), dir_map),
            pl.BlockSpec((1, LORA_W, dm), dir_map),
            pl.BlockSpec((1, 1, dm), dir_map),
            pl.BlockSpec((1, 1, dm), dir_map),
            pl.BlockSpec((1, dm), const2),
            pl.BlockSpec((1, dm), const2),
            pl.BlockSpec((1, dm), const2),
        ],
        out_specs=[pl.BlockSpec((1, 1, C, dm), out_map), pl.BlockSpec((1, 1, C, dm), out_map)],
        out_shape=[out_sds, out_sds],
        scratch_shapes=[pltpu.VMEM((n_pairs, LANES, LANES), F32),
                        pltpu.VMEM((C, dm), F32), pltpu.VMEM((C, dm), F32), pltpu.VMEM((C, dm), F32),
                        pltpu.VMEM((n_pairs, C, LANES), BF16), pltpu.VMEM((n_pairs, 2 * C, LANES), BF16),
                        pltpu.VMEM((n_pairs, C, LANES), BF16), pltpu.VMEM((n_pairs, 2 * C, LANES), BF16),
                        pltpu.VMEM((n_pairs, 2 * C, LANES), BF16), pltpu.VMEM((n_pairs, C, LANES), BF16)],
        compiler_params=_params("arbitrary", "arbitrary", "arbitrary"),
        name="wkv_scan",
    )(z, z, w_up, a_up, w0, a0, k_k, k_a, r_k)


def _pool_kernel(p_ref, gate_ref, w_ref, scale_ref, o_ref, pad_ref, *, rows):
    g = pl.program_id(1)
    t, gi = p_ref.shape[1], p_ref.shape[2]
    zeros = jnp.zeros((POOL_PAD, gi), F32)
    pad_ref[0:POOL_PAD, :] = zeros
    pad_ref[POOL_PAD + t:POOL_PAD + t + POOL_PAD, :] = zeros
    pad_ref[POOL_PAD:POOL_PAD + t, :] = p_ref[0]
    w = w_ref[0]
    scale = scale_ref[...]

    for gidx, win in enumerate(POOL_WINDOWS):
        @pl.when(g == gidx)
        def _(win=win):
            half = win // 2

            def tile_body(i, carry):
                r0 = pl.multiple_of(i * rows, rows)
                n = rows + 2 * SUBLANES
                xt = pad_ref[pl.ds(r0 + POOL_PAD - SUBLANES, n), :]
                acc = xt
                step = 1
                while step < win:
                    acc = acc + pltpu.roll(acc, n - step, axis=0)
                    step *= 2
                if SUBLANES - half:
                    acc = pltpu.roll(acc, n - (SUBLANES - half), axis=0)
                acc = acc[0:rows]
                tt = r0 + lax.broadcasted_iota(jnp.int32, (rows, LANES), 0)
                cnt = (jnp.minimum(tt + (win - half), t) - jnp.maximum(tt - half, 0)).astype(F32)
                inv = 1.0 / cnt
                inv_full = jnp.concatenate([inv] * (gi // LANES), axis=1)
                dlt = acc * inv_full - xt[SUBLANES:SUBLANES + rows]
                out = jnp.dot(dlt.astype(BF16), w, preferred_element_type=F32) * scale
                gate = jax.nn.sigmoid(gate_ref[0, pl.ds(r0, rows), :])
                o_ref[0, pl.ds(r0, rows), :] = (gate * out).astype(o_ref.dtype)
                return carry

            lax.fori_loop(0, t // rows, tile_body, 0)


def _pool_branch(z, pool_col, gate_col, pool_w, pool_scale, rows=256):
    bsz, t, _ = z.shape
    ng, gi, go = pool_w.shape
    dm = ng * go
    kern = functools.partial(_pool_kernel, rows=rows)
    return pl.pallas_call(
        kern,
        grid=(bsz, ng),
        in_specs=[pl.BlockSpec((1, t, gi), lambda b, g: (b, 0, pool_col // gi + g)),
                  pl.BlockSpec((1, t, go), lambda b, g: (b, 0, gate_col // go + g)),
                  pl.BlockSpec((1, gi, go), lambda b, g: (g, 0, 0)),
                  pl.BlockSpec((1, go), lambda b, g: (0, g))],
        out_specs=pl.BlockSpec((1, t, go), lambda b, g: (b, 0, g)),
        out_shape=jax.ShapeDtypeStruct((bsz, t, dm), BF16),
        scratch_shapes=[pltpu.VMEM((t + 2 * POOL_PAD, gi), F32)],
        compiler_params=_params("arbitrary", "arbitrary"),
        name="pool_branch",
    )(z, z, pool_w, pool_scale)


def _merge_kernel(y_ref, bo_ref, lo_ref, gup_ref, gng_ref, gnb_ref, gate_ref, yb_ref, o_ref):
    gd_lo = 2 * LORA_W
    esum = _head_sum_matrix()
    inv_n = 1.0 / HEAD_SIZE
    gd = jax.nn.sigmoid(lo_ref[0, :, gd_lo:gd_lo + GATE_LORA_PAD]).astype(BF16)

    def group_body(gidx, carry):
        cols = [pl.ds(pl.multiple_of((gidx * MERGE_GROUP + j) * LANES, LANES), LANES) for j in range(MERGE_GROUP)]
        y = [y_ref[0, 0, :, cs].astype(F32) + y_ref[1, 0, :, cs].astype(F32) for cs in cols]
        mu = [jnp.dot(x.astype(BF16), esum, preferred_element_type=F32) * inv_n for x in y]
        g = [jnp.dot(gd, gup_ref[:, cs], preferred_element_type=F32) for cs in cols]
        yc = _gmap(lambda x, m: x - m, y, mu)
        var = [jnp.dot((x * x).astype(BF16), esum, preferred_element_type=F32) * inv_n for x in yc]
        for cs, x, vr, gi in zip(cols, yc, var, g):
            yn = x * lax.rsqrt(vr + GN_EPS) * gng_ref[:, cs] + gnb_ref[:, cs]
            yn = yn + bo_ref[0, 0, :, cs].astype(F32) + bo_ref[1, 0, :, cs].astype(F32)
            ya = jax.nn.sigmoid(gate_ref[0, :, cs]) * (yn * gi)
            o_ref[0, :, cs] = (ya + yb_ref[0, :, cs].astype(F32)).astype(o_ref.dtype)
        return carry

    lax.fori_loop(0, o_ref.shape[2] // (LANES * MERGE_GROUP), group_body, 0)


def _merge(y, bo, zs, zp, g_up, ln_g, ln_b, yb, tt=256):
    _, bsz, t, dm = y.shape
    n = zs.shape[-1]
    lora_block = (n - LORA_COLS) // LORA_COLS
    return pl.pallas_call(
        _merge_kernel,
        grid=(bsz, t // tt),
        in_specs=[pl.BlockSpec((2, 1, tt, dm), lambda b, i: (0, b, i, 0)),
                  pl.BlockSpec((2, 1, tt, dm), lambda b, i: (0, b, i, 0)),
                  pl.BlockSpec((1, tt, LORA_COLS), lambda b, i: (b, i, lora_block)),
                  pl.BlockSpec((GATE_LORA_PAD, dm), lambda b, i: (0, 0)),
                  pl.BlockSpec((1, dm), lambda b, i: (0, 0)),
                  pl.BlockSpec((1, dm), lambda b, i: (0, 0)),
                  pl.BlockSpec((1, tt, dm), lambda b, i: (b, i, 0)),
                  pl.BlockSpec((1, tt, dm), lambda b, i: (b, i, 0))],
        out_specs=pl.BlockSpec((1, tt, dm), lambda b, i: (b, i, 0)),
        out_shape=jax.ShapeDtypeStruct((bsz, t, dm), BF16),
        compiler_params=_params("arbitrary", "arbitrary"),
        name="wkv_merge",
    )(y, bo, zs, g_up, ln_g, ln_b, zp, yb)


def _xattn_kernel(q_ref, k_ref, v_ref, o_ref, *, head_dim):
    scale = head_dim ** -0.5
    nt_dims = (((1,), (1,)), ((), ()))
    for h in range(X_HEADS):
        cs = slice(h * head_dim, (h + 1) * head_dim)
        s = lax.dot_general(q_ref[0, :, cs], k_ref[0, :, cs], nt_dims, preferred_element_type=F32) * scale
        m = jnp.max(s, axis=-1, keepdims=True)
        e = jnp.exp(s - m)
        p = e / jnp.sum(e, axis=-1, keepdims=True)
        o_ref[0, :, cs] = jnp.dot(p.astype(BF16), v_ref[0, :, cs], preferred_element_type=F32).astype(o_ref.dtype)


def _xattn(q, k, v, tq=512):
    bsz, t, dm = q.shape
    m = k.shape[1]
    kern = functools.partial(_xattn_kernel, head_dim=dm // X_HEADS)
    return pl.pallas_call(
        kern,
        grid=(bsz, t // tq),
        in_specs=[pl.BlockSpec((1, tq, dm), lambda b, i: (b, i, 0)),
                  pl.BlockSpec((1, m, dm), lambda b, i: (b, 0, 0)),
                  pl.BlockSpec((1, m, dm), lambda b, i: (b, 0, 0))],
        out_specs=pl.BlockSpec((1, tq, dm), lambda b, i: (b, i, 0)),
        out_shape=jax.ShapeDtypeStruct((bsz, t, dm), BF16),
        compiler_params=_params("arbitrary", "arbitrary"),
        name="xattn",
    )(q, k, v)


def _pad_to(x, axis, size):
    pad = [(0, 0)] * x.ndim
    pad[axis] = (0, size - x.shape[axis])
    return jnp.pad(x, pad)


def _gain_tile(g):
    return jnp.broadcast_to(g[:, None], (g.shape[0], LANES))


def _in_proj_operands(p, dm):
    w_in, shift_w = p['w_in'], p['shift_w']
    gate_lora = p['g_up'].shape[0]
    c_rkv = 3 * dm
    c_lora = c_rkv + 2 * LORA_W + gate_lora
    pool_width = p['pool_w'].shape[0] * p['pool_w'].shape[1]
    c_pool = c_lora + pool_width
    wb = w_in.astype(BF16)
    w_shift = jnp.concatenate([wb[:, :c_rkv], _pad_to(wb[:, c_rkv:c_lora], 1, LORA_COLS)], axis=1)
    taps = jnp.concatenate([shift_w[:, :c_rkv], _pad_to(shift_w[:, c_rkv:c_lora], 1, LORA_COLS)], axis=1)
    w_plain = jnp.concatenate([wb[:, c_pool:], wb[:, c_lora:c_pool]], axis=1)
    return w_shift, taps, w_plain


def _trunk(xs, mem, lp, norm_final_g):
    t, dm = xs[0].shape[1:]
    rows = [x.shape[0] * t for x in xs]
    bsz = sum(x.shape[0] for x in xs)
    n_mem = mem.shape[1]
    m_tok = bsz * t
    hs = tuple(x.reshape(-1, dm) for x in xs)
    memf = mem.reshape(bsz * n_mem, dm)
    depth = lp['w_in'].shape[0]
    h = None
    for l in range(depth):
        p = {name: arr[l] for name, arr in lp.items()}
        w_shift, taps, w_plain = _in_proj_operands(p, dm)
        g_up = _pad_to(p['g_up'], 0, GATE_LORA_PAD).astype(BF16)

        if h is None:
            xn = _rmsnorm2(hs[0], hs[1], p['norm_mix_g'], BF16)
            res = hs
        else:
            xn = _rmsnorm(h, p['norm_mix_g'], BF16)
            res = h
        zs = _in_proj(xn, w_shift, taps, bsz, t)
        zp = _matmul(xn, w_plain, F32, PROJ_TM, PROJ_TN, name="in_plain").reshape(bsz, t, -1)

        w_up = jnp.stack([p['w_up_f'], p['w_up_b']]).astype(BF16)
        a_up = jnp.stack([p['a_up_f'], p['a_up_b']]).astype(BF16)
        w0 = jnp.stack([p['w0_f'], p['w0_b']]).reshape(2, 1, dm)
        a0 = jnp.stack([p['a0_f'], p['a0_b']]).reshape(2, 1, dm)
        y, bo = _wkv_scan(zs, dm, w_up, a_up, w0, a0,
                          p['k_k'].reshape(1, dm), p['k_a'].reshape(1, dm), p['r_k'].reshape(1, dm))
        yb = _pool_branch(zp, 2 * dm, dm, p['pool_w'].astype(BF16), p['pool_scale'].reshape(1, dm))
        merged = _merge(y, bo, zs, zp, g_up, p['ln_x_g'].reshape(1, dm), p['ln_x_b'].reshape(1, dm), yb)
        h, h_bf, scale = _matmul_stats(merged.reshape(m_tok, dm), p['w_out'].astype(BF16), PROJ_TM, PROJ_TN,
                                       residual=res, name="out_proj")

        mn = _rmsnorm(memf, p['norm_mem_g'], BF16)
        q = _matmul_wres(h_bf, p['xq'], BF16, PROJ_TM, PROJ_TN, scale=scale, gain=_gain_tile(p['norm_x_g']), name="xq")
        kx = _matmul_wres(mn, p['xk'], BF16, PROJ_TM, PROJ_TN, name="xk")
        vx = _matmul_wres(mn, p['xv'], BF16, PROJ_TM, PROJ_TN, name="xv")
        o = _xattn(q.reshape(bsz, t, dm), kx.reshape(bsz, n_mem, dm), vx.reshape(bsz, n_mem, dm))
        h, h_bf, scale = _matmul_stats(o.reshape(m_tok, dm), p['xo'].astype(BF16), PROJ_TM, PROJ_TN,
                                       residual=h, name="xo")

        hidden = p['ffn_w2'].shape[0]
        act = _swiglu_up(h_bf, scale, _gain_tile(p['norm_ffn_g']), p['ffn_w13'], hidden, FFN_TM, FFN_TN)
        w2 = p['ffn_w2'].astype(BF16)
        if l + 1 < depth:
            h = _matmul(act, w2, F32, FFN_DOWN_TM, FFN_DOWN_TN, residual=h, name="ffn_down")

    outs, off = [], 0
    for x, nrow in zip(xs, rows):
        y = _matmul_res_norm(act, w2, h, norm_final_g, FFN_DOWN_TM, FFN_DOWN_TN,
                             off // FFN_DOWN_TM, nrow // FFN_DOWN_TM, name="ffn_down_norm")
        outs.append(y.reshape(x.shape))
        off += nrow
    return tuple(outs)


def kernel(x_prompt, x_sample, mem_prompt, mem_sample, norm_mix_g, w_in, shift_w, w0_f, w_up_f, w0_b, w_up_b, a0_f, a_up_f, a0_b, a_up_b, g_up, k_k, k_a, r_k, ln_x_g, ln_x_b, pool_w, pool_scale, w_out, norm_x_g, norm_mem_g, xq, xk, xv, xo, norm_ffn_g, ffn_w13, ffn_w2, norm_final_g):
    assert x_prompt.shape[1:] == x_sample.shape[1:] and mem_prompt.shape[1:] == mem_sample.shape[1:]
    lp = {
        'norm_mix_g': norm_mix_g, 'w_in': w_in, 'shift_w': shift_w,
        'w0_f': w0_f, 'w_up_f': w_up_f, 'w0_b': w0_b, 'w_up_b': w_up_b,
        'a0_f': a0_f, 'a_up_f': a_up_f, 'a0_b': a0_b, 'a_up_b': a_up_b,
        'g_up': g_up, 'k_k': k_k, 'k_a': k_a, 'r_k': r_k.reshape(r_k.shape[0], -1),
        'ln_x_g': ln_x_g, 'ln_x_b': ln_x_b,
        'pool_w': pool_w, 'pool_scale': pool_scale, 'w_out': w_out,
        'norm_x_g': norm_x_g, 'norm_mem_g': norm_mem_g, 'xq': xq, 'xk': xk, 'xv': xv, 'xo': xo,
        'norm_ffn_g': norm_ffn_g, 'ffn_w13': ffn_w13, 'ffn_w2': ffn_w2,
    }
    mem = jnp.concatenate([mem_prompt, mem_sample], axis=0)
    return _trunk((x_prompt, x_sample), mem, lp, norm_final_g)
```

```python
import functools
import math

import jax
import jax.numpy as jnp
from jax import lax
from jax.experimental import pallas as pl
from jax.experimental.pallas import tpu as pltpu

F32 = jnp.float32
BF16 = jnp.bfloat16

LANES = 128
SUBLANES = 8
VMEM_LIMIT_BYTES = 56 * 1024 * 1024
VMEM_LIMIT_BYTES_MAX = 60000 * 1024

HEAD_SIZE = 64
HEAD_SHIFT = 6
X_HEADS = 4
POOL_WINDOWS = (2, 4, 8, 16)
POOL_PAD = 16
GN_EPS = 64e-5
NORM_EPS = 1e-6
DECAY_SCALE = -math.exp(-0.5)
WKV_CHUNK = 64
WKV_SOLVE_GROUP = 32
WKV_UPDATE_GROUP = 16
MERGE_GROUP = 8
LORA_W = 128
GATE_LORA_PAD = 512
LORA_COLS = 1024
PROJ_TM = 1024
PROJ_TN = 512
FFN_TM = 1024
FFN_TN = 256
FFN_DOWN_TM = 512
FFN_DOWN_TN = 256
IN_TN = 512
IN_ROW_CHUNK = 512


def _params(*semantics):
    return pltpu.CompilerParams(dimension_semantics=semantics, vmem_limit_bytes=VMEM_LIMIT_BYTES)


def _rms(x, g):
    ms = jnp.mean(x * x, axis=-1, keepdims=True)
    return x * lax.rsqrt(ms + NORM_EPS) * g


def _rmsnorm_kernel(x_ref, g_ref, o_ref):
    o_ref[...] = _rms(x_ref[...], g_ref[...]).astype(o_ref.dtype)


def _rmsnorm(x, g, out_dtype, tm=256, row_block_offset=0, n_row_blocks=None):
    m, d = x.shape
    nb = m // tm if n_row_blocks is None else n_row_blocks
    return pl.pallas_call(
        _rmsnorm_kernel,
        grid=(nb,),
        in_specs=[pl.BlockSpec((tm, d), lambda i: (i + row_block_offset, 0)),
                  pl.BlockSpec((1, d), lambda i: (0, 0))],
        out_specs=pl.BlockSpec((tm, d), lambda i: (i, 0)),
        out_shape=jax.ShapeDtypeStruct((nb * tm, d), out_dtype),
        compiler_params=_params("parallel"),
        name="rmsnorm",
    )(x, g.reshape(1, d))


def _rmsnorm2_kernel(xa_ref, xb_ref, g_ref, o_ref, *, na):
    x = jnp.where(pl.program_id(0) < na, xa_ref[...], xb_ref[...])
    o_ref[...] = _rms(x, g_ref[...]).astype(o_ref.dtype)


def _rmsnorm2(xa, xb, g, out_dtype, tm=256):
    d = xa.shape[1]
    na, nb = xa.shape[0] // tm, xb.shape[0] // tm
    return pl.pallas_call(
        functools.partial(_rmsnorm2_kernel, na=na),
        grid=(na + nb,),
        in_specs=[pl.BlockSpec((tm, d), lambda i: (jnp.minimum(i, na - 1), 0)),
                  pl.BlockSpec((tm, d), lambda i: (jnp.maximum(i - na, 0), 0)),
                  pl.BlockSpec((1, d), lambda i: (0, 0))],
        out_specs=pl.BlockSpec((tm, d), lambda i: (i, 0)),
        out_shape=jax.ShapeDtypeStruct(((na + nb) * tm, d), out_dtype),
        compiler_params=_params("arbitrary"),
        name="rmsnorm2",
    )(xa, xb, g.reshape(1, d))


def _matmul_kernel(x_ref, w_ref, o_ref):
    o_ref[...] = jnp.dot(x_ref[...], w_ref[...], preferred_element_type=F32).astype(o_ref.dtype)


def _matmul_res_kernel(x_ref, w_ref, r_ref, o_ref):
    acc = jnp.dot(x_ref[...], w_ref[...], preferred_element_type=F32)
    o_ref[...] = (r_ref[...] + acc).astype(o_ref.dtype)


def _matmul(x, w, out_dtype, tm, tn, residual=None, name="matmul"):
    m, k = x.shape
    n = w.shape[1]
    in_specs = [pl.BlockSpec((tm, k), lambda i, j: (i, 0)),
                pl.BlockSpec((k, tn), lambda i, j: (0, j))]
    args = [x, w]
    body = _matmul_kernel
    if residual is not None:
        in_specs.append(pl.BlockSpec((tm, tn), lambda i, j: (i, j)))
        args.append(residual)
        body = _matmul_res_kernel
    return pl.pallas_call(
        body,
        grid=(m // tm, n // tn),
        in_specs=in_specs,
        out_specs=pl.BlockSpec((tm, tn), lambda i, j: (i, j)),
        out_shape=jax.ShapeDtypeStruct((m, n), out_dtype),
        compiler_params=_params("arbitrary", "arbitrary"),
        name=name,
    )(*args)


def _matmul_res_norm_kernel(x_ref, w_ref, r_ref, g_ref, o_ref):
    j = pl.program_id(1)
    tn = w_ref.shape[1]
    acc = jnp.dot(x_ref[...], w_ref[...], preferred_element_type=F32)
    o_ref[:, pl.ds(pl.multiple_of(j * tn, tn), tn)] = r_ref[...] + acc

    @pl.when(j == pl.num_programs(1) - 1)
    def _():
        o_ref[...] = _rms(o_ref[...], g_ref[...])


def _matmul_res_norm(x, w, residual, g, tm, tn, row_block_offset, n_row_blocks, name):
    k = x.shape[1]
    n = w.shape[1]
    off = row_block_offset
    return pl.pallas_call(
        _matmul_res_norm_kernel,
        grid=(n_row_blocks, n // tn),
        in_specs=[pl.BlockSpec((tm, k), lambda i, j: (i + off, 0)),
                  pl.BlockSpec((k, tn), lambda i, j: (0, j)),
                  pl.BlockSpec((tm, tn), lambda i, j: (i + off, j)),
                  pl.BlockSpec((1, n), lambda i, j: (0, 0))],
        out_specs=pl.BlockSpec((tm, n), lambda i, j: (i, 0)),
        out_shape=jax.ShapeDtypeStruct((n_row_blocks * tm, n), F32),
        compiler_params=pltpu.CompilerParams(dimension_semantics=("arbitrary", "arbitrary"),
                                             vmem_limit_bytes=VMEM_LIMIT_BYTES_MAX),
        name=name,
    )(x, w, residual, g.reshape(1, n))


def _in_proj_kernel(x_ref, w_ref, taps_ref, o_ref):
    t = x_ref.shape[0]
    rc = IN_ROW_CHUNK
    w = w_ref[...]
    taps = taps_ref[...]
    ridx = lax.broadcasted_iota(jnp.int32, (rc, w.shape[1]), 0)
    zs = [jnp.dot(x_ref[c * rc:(c + 1) * rc, :], w, preferred_element_type=F32) for c in range(t // rc)]
    zero_row = jnp.zeros((1, w.shape[1]), F32)
    for c, z in enumerate(zs):
        prev_row = zs[c - 1][rc - 1:rc, :] if c > 0 else zero_row
        next_row = zs[c + 1][0:1, :] if c + 1 < len(zs) else zero_row
        zm1 = jnp.where(ridx == 0, prev_row, pltpu.roll(z, 1, axis=0))
        zp1 = jnp.where(ridx == rc - 1, next_row, pltpu.roll(z, rc - 1, axis=0))
        o_ref[0, c * rc:(c + 1) * rc, :] = zm1 * taps[0:1, :] + z * taps[1:2, :] + zp1 * taps[2:3, :]


def _in_proj(xn, w, taps, bsz, t):
    k = xn.shape[1]
    n = w.shape[1]
    return pl.pallas_call(
        _in_proj_kernel,
        grid=(bsz, n // IN_TN),
        in_specs=[pl.BlockSpec((t, k), lambda b, j: (b, 0), pipeline_mode=pl.Buffered(1)),
                  pl.BlockSpec((k, IN_TN), lambda b, j: (0, j)),
                  pl.BlockSpec((3, IN_TN), lambda b, j: (0, j))],
        out_specs=pl.BlockSpec((1, t, IN_TN), lambda b, j: (b, 0, j)),
        out_shape=jax.ShapeDtypeStruct((bsz, t, n), F32),
        compiler_params=_params("arbitrary", "arbitrary"),
        name="in_proj",
    )(xn, w, taps)


def _lane_tiled(x, width):
    return jnp.concatenate([x] * (width // LANES), axis=1)


def _matmul_stats_kernel(x_ref, w_ref, *rest, na, inv_d):
    *r_refs, o_ref, ob_ref, sc_ref, ssq_ref = rest
    j = pl.program_id(1)
    acc = jnp.dot(x_ref[...], w_ref[...], preferred_element_type=F32)
    res = r_refs[0][...] if len(r_refs) == 1 else jnp.where(pl.program_id(0) < na, r_refs[0][...], r_refs[1][...])
    h = res + acc
    o_ref[...] = h
    ob_ref[...] = h.astype(BF16)
    hh = h * h
    part = hh[:, 0:LANES]
    for c in range(1, hh.shape[1] // LANES):
        part = part + hh[:, c * LANES:(c + 1) * LANES]

    @pl.when(j == 0)
    def _():
        ssq_ref[...] = part

    @pl.when(j > 0)
    def _():
        ssq_ref[...] += part

    @pl.when(j == pl.num_programs(1) - 1)
    def _():
        ms = jnp.sum(ssq_ref[...], axis=-1, keepdims=True) * inv_d
        sc_ref[...] = jnp.broadcast_to(lax.rsqrt(ms + NORM_EPS), sc_ref.shape)


def _matmul_stats(x, w, tm, tn, residual, name):
    m, k = x.shape
    n = w.shape[1]
    in_specs = [pl.BlockSpec((tm, k), lambda i, j: (i, 0)),
                pl.BlockSpec((k, tn), lambda i, j: (0, j))]
    na = 0
    if isinstance(residual, tuple):
        ra, rb = residual
        na = ra.shape[0] // tm
        in_specs += [pl.BlockSpec((tm, tn), lambda i, j: (jnp.minimum(i, na - 1), j)),
                     pl.BlockSpec((tm, tn), lambda i, j: (jnp.maximum(i - na, 0), j))]
        res_args = [ra, rb]
    else:
        in_specs.append(pl.BlockSpec((tm, tn), lambda i, j: (i, j)))
        res_args = [residual]
    return pl.pallas_call(
        functools.partial(_matmul_stats_kernel, na=na, inv_d=1.0 / n),
        grid=(m // tm, n // tn),
        in_specs=in_specs,
        out_specs=[pl.BlockSpec((tm, tn), lambda i, j: (i, j)),
                   pl.BlockSpec((tm, tn), lambda i, j: (i, j)),
                   pl.BlockSpec((tm, LANES), lambda i, j: (i, 0))],
        out_shape=[jax.ShapeDtypeStruct((m, n), F32), jax.ShapeDtypeStruct((m, n), BF16),
                   jax.ShapeDtypeStruct((m, LANES), F32)],
        scratch_shapes=[pltpu.VMEM((tm, LANES), F32)],
        compiler_params=_params("arbitrary", "arbitrary"),
        name=name,
    )(x, w, *res_args)


def _cast_weight(w_ref, wb_ref, g_ref):
    w = w_ref[...]
    if g_ref is not None:
        w = w * _lane_tiled(g_ref[...], w.shape[1])
    wb_ref[...] = w.astype(BF16)


def _wres_kernel(*refs, normed):
    if normed:
        x_ref, sc_ref, g_ref, w_ref, o_ref, wb_ref = refs
    else:
        x_ref, w_ref, o_ref, wb_ref = refs
        sc_ref = g_ref = None

    @pl.when(pl.program_id(1) == 0)
    def _():
        _cast_weight(w_ref, wb_ref, g_ref)

    acc = jnp.dot(x_ref[...], wb_ref[...], preferred_element_type=F32)
    if normed:
        acc = acc * _lane_tiled(sc_ref[...], acc.shape[1])
    o_ref[...] = acc.astype(o_ref.dtype)


def _matmul_wres(x, w, out_dtype, tm, tn, scale=None, gain=None, name="matmul_wres"):
    m, k = x.shape
    n = w.shape[1]
    normed = scale is not None
    in_specs = [pl.BlockSpec((tm, k), lambda j, i: (i, 0))]
    args = [x]
    if normed:
        in_specs += [pl.BlockSpec((tm, LANES), lambda j, i: (i, 0)),
                     pl.BlockSpec((k, LANES), lambda j, i: (0, 0))]
        args += [scale, gain]
    in_specs.append(pl.BlockSpec((k, tn), lambda j, i: (0, j)))
    args.append(w)
    return pl.pallas_call(
        functools.partial(_wres_kernel, normed=normed),
        grid=(n // tn, m // tm),
        in_specs=in_specs,
        out_specs=pl.BlockSpec((tm, tn), lambda j, i: (i, j)),
        out_shape=jax.ShapeDtypeStruct((m, n), out_dtype),
        scratch_shapes=[pltpu.VMEM((k, tn), BF16)],
        compiler_params=_params("arbitrary", "arbitrary"),
        name=name,
    )(*args)


def _swiglu_kernel(x_ref, sc_ref, g_ref, wg_ref, wu_ref, o_ref, wgb_ref, wub_ref):
    @pl.when(pl.program_id(1) == 0)
    def _():
        _cast_weight(wg_ref, wgb_ref, g_ref)
        _cast_weight(wu_ref, wub_ref, g_ref)

    x = x_ref[...]
    sc = _lane_tiled(sc_ref[...], o_ref.shape[1])
    gate = jnp.dot(x, wgb_ref[...], preferred_element_type=F32) * sc
    up = jnp.dot(x, wub_ref[...], preferred_element_type=F32) * sc
    o_ref[...] = (gate * jax.nn.sigmoid(gate) * up).astype(o_ref.dtype)


def _swiglu_up(x, scale, gain, w13, hidden, tm, tn):
    m, k = x.shape
    nb = hidden // tn
    return pl.pallas_call(
        _swiglu_kernel,
        grid=(nb, m // tm),
        in_specs=[pl.BlockSpec((tm, k), lambda j, i: (i, 0)),
                  pl.BlockSpec((tm, LANES), lambda j, i: (i, 0)),
                  pl.BlockSpec((k, LANES), lambda j, i: (0, 0)),
                  pl.BlockSpec((k, tn), lambda j, i: (0, j)),
                  pl.BlockSpec((k, tn), lambda j, i: (0, j + nb))],
        out_specs=pl.BlockSpec((tm, tn), lambda j, i: (i, j)),
        out_shape=jax.ShapeDtypeStruct((m, hidden), BF16),
        scratch_shapes=[pltpu.VMEM((k, tn), BF16), pltpu.VMEM((k, tn), BF16)],
        compiler_params=_params("arbitrary", "arbitrary"),
        name="swiglu_up",
    )(x, scale, gain, w13, w13)


def _head_sum_matrix():
    r = lax.broadcasted_iota(jnp.int32, (LANES, LANES), 0) >> HEAD_SHIFT
    c = lax.broadcasted_iota(jnp.int32, (LANES, LANES), 1) >> HEAD_SHIFT
    return jnp.where(r == c, 1.0, 0.0).astype(BF16)


def _split2(x):
    hi = x.astype(BF16)
    return hi, (x - hi.astype(F32)).astype(BF16)


def _gmap(f, *lists):
    return [f(*xs) for xs in zip(*lists)]


def _wkv_kernel(z_ref, lo_ref, wup_ref, aup_ref, w0_ref, a0_ref, kk_ref, ka_ref, rk_ref,
                y_ref, bo_ref, state_ref, lw_ref, cum_ref, icl_ref,
                tinv_ref, lrk_ref, rb_ref, lhs_ref, btk_ref, vb_ref, *, d_model):
    C = WKV_CHUNK
    d = pl.program_id(1)
    c = pl.program_id(2)
    sgn = 1 - 2 * d

    @pl.when(c == 0)
    def _():
        state_ref[...] = jnp.zeros_like(state_ref)

    wl = w0_ref[0] + jnp.dot(jnp.tanh(lo_ref[0, :, 0:LORA_W]).astype(BF16), wup_ref[0],
                             preferred_element_type=F32)
    lw = DECAY_SCALE * jax.nn.sigmoid(wl)
    lw_ref[...] = lw
    r64 = lax.broadcasted_iota(jnp.int32, (C, C), 0)
    c64 = lax.broadcasted_iota(jnp.int32, (C, C), 1)
    tri = jnp.where((r64 - c64) * sgn >= 0, 1.0, 0.0).astype(BF16)
    lw_hi, lw_lo = _split2(lw)
    cum_ref[...] = (jnp.dot(tri, lw_hi, preferred_element_type=F32)
                    + jnp.dot(tri, lw_lo, preferred_element_type=F32))
    icl_ref[...] = jax.nn.sigmoid(
        a0_ref[0] + jnp.dot(lo_ref[0, :, LORA_W:2 * LORA_W].astype(BF16), aup_ref[0],
                            preferred_element_type=F32))

    row = lax.broadcasted_iota(jnp.int32, (C, LANES), 0)
    col = lax.broadcasted_iota(jnp.int32, (C, LANES), 1)
    colh = col & (HEAD_SIZE - 1)
    order = (row - colh) * sgn
    strict = order > 0
    incl = order >= 0
    eye2 = row == colh
    lane_lo = col < HEAD_SIZE
    esum = _head_sum_matrix()
    rr = lax.broadcasted_iota(jnp.int32, (LANES, LANES), 0) >> HEAD_SHIFT
    cc = lax.broadcasted_iota(jnp.int32, (LANES, LANES), 1) >> HEAD_SHIFT
    blockdiag = rr == cc

    def bd(x):
        zero = jnp.zeros_like(x)
        return jnp.concatenate([jnp.where(lane_lo, x, zero), jnp.where(lane_lo, zero, x)], axis=0)

    def pmul(x, y):
        return jnp.dot(x.astype(BF16), bd(y.astype(BF16)), preferred_element_type=F32)

    nt_dims = (((1,), (1,)), ((), ()))
    tn_dims = (((0,), (0,)), ((), ()))

    def lane_tile(p, offset=0):
        return pl.ds(pl.multiple_of(offset + p * LANES, LANES), LANES)

    def head_sums(xs):
        stacked = jnp.concatenate([x.astype(BF16) for x in xs], axis=0)
        sums = jnp.dot(stacked, esum, preferred_element_type=F32)
        return [sums[j * C:(j + 1) * C] for j in range(len(xs))]

    def solve_body(g, carry):
        pairs = [g * WKV_SOLVE_GROUP + j for j in range(WKV_SOLVE_GROUP)]
        cols = [lane_tile(p) for p in pairs]
        r = [z_ref[0, :, cs] for cs in cols]
        k = [z_ref[0, :, lane_tile(p, d_model)] for p in pairs]
        v = [z_ref[0, :, lane_tile(p, 2 * d_model)] for p in pairs]
        icl = [icl_ref[:, cs] for cs in cols]
        lwp = [lw_ref[:, cs] for cs in cols]
        cum = [cum_ref[:, cs] for cs in cols]

        q = [ki * kk_ref[:, cs] for ki, cs in zip(k, cols)]
        n2 = head_sums([qi * qi for qi in q])
        kd = [ki * (1.0 + (ic - 1.0) * ka_ref[:, cs]) for ki, ic, cs in zip(k, icl, cols)]
        bsum = head_sums([ri * kdi * rk_ref[:, cs] for ri, kdi, cs in zip(r, kd, cols)])
        for cs, bs, vi in zip(cols, bsum, v):
            bo_ref[0, 0, :, cs] = (bs * vi).astype(bo_ref.dtype)

        kk = [qi * lax.rsqrt(jnp.maximum(ni, 1e-12)) for qi, ni in zip(q, n2)]
        b = _gmap(lambda x, ic: x * ic, kk, icl)
        e_out = [jnp.exp(-x) for x in cum]
        at = _gmap(lambda x, cm, lw_: (-x * jnp.exp(cm - lw_)).astype(BF16), kk, cum, lwp)
        rt = _gmap(lambda x, cm: (x * jnp.exp(cm)).astype(BF16), r, cum)
        bt = _gmap(lambda x, e: (x * e).astype(BF16), b, e_out)
        kt = _gmap(lambda x, e: (x * e).astype(BF16), kd, e_out)

        lhs = _gmap(lambda a_, r_: jnp.concatenate([a_, r_], axis=0), at, rt)
        rhs_t = _gmap(lambda b_, k_: jnp.concatenate([bd(b_), bd(k_)], axis=0), bt, kt)
        pmat = _gmap(lambda l_, r_: lax.dot_general(l_, r_, nt_dims, preferred_element_type=F32), lhs, rhs_t)
        for p, l_, b_, k_, v_ in zip(pairs, lhs, bt, kt, v):
            lhs_ref[p] = l_
            btk_ref[p] = jnp.concatenate([b_, k_], axis=0)
            vb_ref[p] = v_.astype(BF16)
        lab = [jnp.where(strict, x[:C, :LANES], 0.0).astype(BF16) for x in pmat]
        for p, x in zip(pairs, pmat):
            rb_ref[p] = jnp.where(incl, x[C:, :LANES], 0.0).astype(BF16)
            lrk_ref[p] = jnp.concatenate([jnp.where(strict, x[:C, LANES:], 0.0).astype(BF16),
                                          jnp.where(incl, x[C:, LANES:], 0.0).astype(BF16)], axis=0)

        zero_b = jnp.zeros((C, LANES), BF16)
        ident = jnp.where(eye2, 1.0, 0.0).astype(BF16)
        first = (row >> 1) == (colh >> 1)
        tinv = [ident + jnp.where(first, x, zero_b) for x in lab]
        s = 2
        while s < C:
            sh = s.bit_length() - 1
            level = ((row >> (sh + 1)) == (colh >> (sh + 1))) & ((row >> sh) != (colh >> sh))
            off = [jnp.where(level, x, zero_b) for x in lab]
            tmp = _gmap(pmul, tinv, off)
            upd_t = _gmap(pmul, tmp, tinv)
            tinv = _gmap(lambda t_, x: t_ + x.astype(BF16), tinv, upd_t)
            s *= 2
        for p, x in zip(pairs, tinv):
            tinv_ref[p] = x
        return carry

    def update_body(g, carry):
        pairs = [g * WKV_UPDATE_GROUP + j for j in range(WKV_UPDATE_GROUP)]
        cols = [lane_tile(p) for p in pairs]
        h = [state_ref[p] for p in pairs]
        vb = [vb_ref[p] for p in pairs]
        hs = [jnp.dot(lhs_ref[p], h_.astype(BF16), preferred_element_type=F32)
              for p, h_ in zip(pairs, h)]
        lrkv = [jnp.dot(lrk_ref[p], bd(v_), preferred_element_type=F32) for p, v_ in zip(pairs, vb)]
        rhs_u = _gmap(lambda h_, x: (h_[:C] + x[:C]).astype(BF16), hs, lrkv)
        u = [jnp.dot(tinv_ref[p], bd(x), preferred_element_type=F32) for p, x in zip(pairs, rhs_u)]
        ub = [x.astype(BF16) for x in u]
        rbu = [jnp.dot(rb_ref[p], bd(x), preferred_element_type=F32) for p, x in zip(pairs, ub)]
        for cs, h_, xv, xu in zip(cols, hs, lrkv, rbu):
            y_ref[0, 0, :, cs] = (h_[C:] + xv[C:] + xu).astype(y_ref.dtype)
        upd = [lax.dot_general(btk_ref[p], jnp.concatenate([u_, v_], axis=0), tn_dims,
                               preferred_element_type=F32) for p, u_, v_ in zip(pairs, ub, vb)]
        for p, cs, h_, x in zip(pairs, cols, h, upd):
            tot = jnp.sum(lw_ref[:, cs], axis=0, keepdims=True)
            decay_rows = jnp.broadcast_to(jnp.exp(tot), (LANES, LANES)).T
            state_ref[p] = decay_rows * (h_ + jnp.where(blockdiag, x, 0.0))
        return carry

    n_pairs = d_model // LANES
    lax.fori_loop(0, n_pairs // WKV_SOLVE_GROUP, solve_body, 0)
    lax.fori_loop(0, n_pairs // WKV_UPDATE_GROUP, update_body, 0)


def _wkv_scan(z, dm, w_up, a_up, w0, a0, k_k, k_a, r_k):
    bsz, t, n = z.shape
    C = WKV_CHUNK
    nc = t // C
    n_pairs = dm // LANES
    lora_block = (n - LORA_COLS) // LORA_COLS

    def tchunk(dd, cc):
        return jnp.where(dd == 0, cc, nc - 1 - cc)

    def dir_map(bb, dd, cc):
        return (dd, 0, 0)

    def const2(bb, dd, cc):
        return (0, 0)

    def out_map(bb, dd, cc):
        return (dd, bb, tchunk(dd, cc), 0)

    out_sds = jax.ShapeDtypeStruct((2, bsz, t, dm), BF16)
    kern = functools.partial(_wkv_kernel, d_model=dm)
    return pl.pallas_call(
        kern,
        grid=(bsz, 2, nc),
        in_specs=[
            pl.BlockSpec((1, C, 3 * dm), lambda bb, dd, cc: (bb, tchunk(dd, cc), 0)),
            pl.BlockSpec((1, C, LORA_COLS), lambda bb, dd, cc: (bb, tchunk(dd, cc), lora_block)),
            pl.BlockSpec((1, LORA_W, dm), dir_map),
            pl.BlockSpec((1, LORA_W, dm), dir_map),
            pl.BlockSpec((1, 1, dm), dir_map),
            pl.BlockSpec((1, 1, dm), dir_map),
            pl.BlockSpec((1, dm), const2),
            pl.BlockSpec((1, dm), const2),
            pl.BlockSpec((1, dm), const2),
        ],
        out_specs=[pl.BlockSpec((1, 1, C, dm), out_map), pl.BlockSpec((1, 1, C, dm), out_map)],
        out_shape=[out_sds, out_sds],
        scratch_shapes=[pltpu.VMEM((n_pairs, LANES, LANES), F32),
                        pltpu.VMEM((C, dm), F32), pltpu.VMEM((C, dm), F32), pltpu.VMEM((C, dm), F32),
                        pltpu.VMEM((n_pairs, C, LANES), BF16), pltpu.VMEM((n_pairs, 2 * C, LANES), BF16),
                        pltpu.VMEM((n_pairs, C, LANES), BF16), pltpu.VMEM((n_pairs, 2 * C, LANES), BF16),
                        pltpu.VMEM((n_pairs, 2 * C, LANES), BF16), pltpu.VMEM((n_pairs, C, LANES), BF16)],
        compiler_params=_params("arbitrary", "arbitrary", "arbitrary"),
        name="wkv_scan",
    )(z, z, w_up, a_up, w0, a0, k_k, k_a, r_k)


def _pool_kernel(p_ref, gate_ref, w_ref, scale_ref, o_ref, pad_ref, *, rows):
    g = pl.program_id(1)
    t, gi = p_ref.shape[1], p_ref.shape[2]
    zeros = jnp.zeros((POOL_PAD, gi), F32)
    pad_ref[0:POOL_PAD, :] = zeros
    pad_ref[POOL_PAD + t:POOL_PAD + t + POOL_PAD, :] = zeros
    pad_ref[POOL_PAD:POOL_PAD + t, :] = p_ref[0]
    w = w_ref[0]
    scale = scale_ref[...]

    for gidx, win in enumerate(POOL_WINDOWS):
        @pl.when(g == gidx)
        def _(win=win):
            half = win // 2

            def tile_body(i, carry):
                r0 = pl.multiple_of(i * rows, rows)
                n = rows + 2 * SUBLANES
                xt = pad_ref[pl.ds(r0 + POOL_PAD - SUBLANES, n), :]
                acc = xt
                step = 1
                while step < win:
                    acc = acc + pltpu.roll(acc, n - step, axis=0)
                    step *= 2
                if SUBLANES - half:
                    acc = pltpu.roll(acc, n - (SUBLANES - half), axis=0)
                acc = acc[0:rows]
                tt = r0 + lax.broadcasted_iota(jnp.int32, (rows, LANES), 0)
                cnt = (jnp.minimum(tt + (win - half), t) - jnp.maximum(tt - half, 0)).astype(F32)
                inv = 1.0 / cnt
                inv_full = jnp.concatenate([inv] * (gi // LANES), axis=1)
                dlt = acc * inv_full - xt[SUBLANES:SUBLANES + rows]
                out = jnp.dot(dlt.astype(BF16), w, preferred_element_type=F32) * scale
                gate = jax.nn.sigmoid(gate_ref[0, pl.ds(r0, rows), :])
                o_ref[0, pl.ds(r0, rows), :] = (gate * out).astype(o_ref.dtype)
                return carry

            lax.fori_loop(0, t // rows, tile_body, 0)


def _pool_branch(z, pool_col, gate_col, pool_w, pool_scale, rows=256):
    bsz, t, _ = z.shape
    ng, gi, go = pool_w.shape
    dm = ng * go
    kern = functools.partial(_pool_kernel, rows=rows)
    return pl.pallas_call(
        kern,
        grid=(bsz, ng),
        in_specs=[pl.BlockSpec((1, t, gi), lambda b, g: (b, 0, pool_col // gi + g)),
                  pl.BlockSpec((1, t, go), lambda b, g: (b, 0, gate_col // go + g)),
                  pl.BlockSpec((1, gi, go), lambda b, g: (g, 0, 0)),
                  pl.BlockSpec((1, go), lambda b, g: (0, g))],
        out_specs=pl.BlockSpec((1, t, go), lambda b, g: (b, 0, g)),
        out_shape=jax.ShapeDtypeStruct((bsz, t, dm), BF16),
        scratch_shapes=[pltpu.VMEM((t + 2 * POOL_PAD, gi), F32)],
        compiler_params=_params("arbitrary", "arbitrary"),
        name="pool_branch",
    )(z, z, pool_w, pool_scale)


def _merge_kernel(y_ref, bo_ref, lo_ref, gup_ref, gng_ref, gnb_ref, gate_ref, yb_ref, o_ref):
    gd_lo = 2 * LORA_W
    esum = _head_sum_matrix()
    inv_n = 1.0 / HEAD_SIZE
    gd = jax.nn.sigmoid(lo_ref[0, :, gd_lo:gd_lo + GATE_LORA_PAD]).astype(BF16)

    def group_body(gidx, carry):
        cols = [pl.ds(pl.multiple_of((gidx * MERGE_GROUP + j) * LANES, LANES), LANES) for j in range(MERGE_GROUP)]
        y = [y_ref[0, 0, :, cs].astype(F32) + y_ref[1, 0, :, cs].astype(F32) for cs in cols]
        mu = [jnp.dot(x.astype(BF16), esum, preferred_element_type=F32) * inv_n for x in y]
        g = [jnp.dot(gd, gup_ref[:, cs], preferred_element_type=F32) for cs in cols]
        yc = _gmap(lambda x, m: x - m, y, mu)
        var = [jnp.dot((x * x).astype(BF16), esum, preferred_element_type=F32) * inv_n for x in yc]
        for cs, x, vr, gi in zip(cols, yc, var, g):
            yn = x * lax.rsqrt(vr + GN_EPS) * gng_ref[:, cs] + gnb_ref[:, cs]
            yn = yn + bo_ref[0, 0, :, cs].astype(F32) + bo_ref[1, 0, :, cs].astype(F32)
            ya = jax.nn.sigmoid(gate_ref[0, :, cs]) * (yn * gi)
            o_ref[0, :, cs] = (ya + yb_ref[0, :, cs].astype(F32)).astype(o_ref.dtype)
        return carry

    lax.fori_loop(0, o_ref.shape[2] // (LANES * MERGE_GROUP), group_body, 0)


def _merge(y, bo, zs, zp, gate_col, g_up, ln_g, ln_b, yb, tt=256):
    _, bsz, t, dm = y.shape
    n = zs.shape[-1]
    lora_block = (n - LORA_COLS) // LORA_COLS
    cw = math.gcd(gate_col, dm)
    assert cw % (LANES * MERGE_GROUP) == 0
    gate_block = gate_col // cw
    return pl.pallas_call(
        _merge_kernel,
        grid=(bsz, t // tt, dm // cw),
        in_specs=[pl.BlockSpec((2, 1, tt, cw), lambda b, i, h: (0, b, i, h)),
                  pl.BlockSpec((2, 1, tt, cw), lambda b, i, h: (0, b, i, h)),
                  pl.BlockSpec((1, tt, LORA_COLS), lambda b, i, h: (b, i, lora_block)),
                  pl.BlockSpec((GATE_LORA_PAD, cw), lambda b, i, h: (0, h)),
                  pl.BlockSpec((1, cw), lambda b, i, h: (0, h)),
                  pl.BlockSpec((1, cw), lambda b, i, h: (0, h)),
                  pl.BlockSpec((1, tt, cw), lambda b, i, h: (b, i, gate_block + h)),
                  pl.BlockSpec((1, tt, cw), lambda b, i, h: (b, i, h))],
        out_specs=pl.BlockSpec((1, tt, cw), lambda b, i, h: (b, i, h)),
        out_shape=jax.ShapeDtypeStruct((bsz, t, dm), BF16),
        compiler_params=_params("arbitrary", "arbitrary", "arbitrary"),
        name="wkv_merge",
    )(y, bo, zs, g_up, ln_g, ln_b, zp, yb)


def _xattn_kernel(q_ref, k_ref, v_ref, o_ref, *, head_dim):
    scale = head_dim ** -0.5
    nt_dims = (((1,), (1,)), ((), ()))
    for h in range(X_HEADS):
        cs = slice(h * head_dim, (h + 1) * head_dim)
        s = lax.dot_general(q_ref[0, :, cs], k_ref[0, :, cs], nt_dims, preferred_element_type=F32) * scale
        m = jnp.max(s, axis=-1, keepdims=True)
        e = jnp.exp(s - m)
        p = e / jnp.sum(e, axis=-1, keepdims=True)
        o_ref[0, :, cs] = jnp.dot(p.astype(BF16), v_ref[0, :, cs], preferred_element_type=F32).astype(o_ref.dtype)


def _xattn(q, k, v, tq=512):
    bsz, t, dm = q.shape
    m = k.shape[1]
    kern = functools.partial(_xattn_kernel, head_dim=dm // X_HEADS)
    return pl.pallas_call(
        kern,
        grid=(bsz, t // tq),
        in_specs=[pl.BlockSpec((1, tq, dm), lambda b, i: (b, i, 0)),
                  pl.BlockSpec((1, m, dm), lambda b, i: (b, 0, 0)),
                  pl.BlockSpec((1, m, dm), lambda b, i: (b, 0, 0))],
        out_specs=pl.BlockSpec((1, tq, dm), lambda b, i: (b, i, 0)),
        out_shape=jax.ShapeDtypeStruct((bsz, t, dm), BF16),
        compiler_params=_params("arbitrary", "arbitrary"),
        name="xattn",
    )(q, k, v)


def _pad_to(x, axis, size):
    pad = [(0, 0)] * x.ndim
    pad[axis] = (0, size - x.shape[axis])
    return jnp.pad(x, pad)


def _gain_tile(g):
    return jnp.broadcast_to(g[:, None], (g.shape[0], LANES))


def _in_proj_operands(p, dm):
    w_in, shift_w = p['w_in'], p['shift_w']
    gate_lora = p['g_up'].shape[0]
    c_rkv = 3 * dm
    c_lora = c_rkv + 2 * LORA_W + gate_lora
    pool_width = p['pool_w'].shape[0] * p['pool_w'].shape[1]
    w_shift = jnp.concatenate([w_in[:, :c_rkv], _pad_to(w_in[:, c_rkv:c_lora], 1, LORA_COLS)], axis=1).astype(BF16)
    taps = jnp.concatenate([shift_w[:, :c_rkv], _pad_to(shift_w[:, c_rkv:c_lora], 1, LORA_COLS)], axis=1)
    w_plain = w_in[:, c_lora:].astype(BF16)
    return w_shift, taps, w_plain, pool_width


def _trunk(xs, mem, lp, norm_final_g):
    t, dm = xs[0].shape[1:]
    rows = [x.shape[0] * t for x in xs]
    bsz = sum(x.shape[0] for x in xs)
    n_mem = mem.shape[1]
    m_tok = bsz * t
    hs = tuple(x.reshape(-1, dm) for x in xs)
    memf = mem.reshape(bsz * n_mem, dm)
    depth = lp['w_in'].shape[0]
    h = None
    for l in range(depth):
        p = {name: arr[l] for name, arr in lp.items()}
        w_shift, taps, w_plain, pool_width = _in_proj_operands(p, dm)
        g_up = _pad_to(p['g_up'], 0, GATE_LORA_PAD).astype(BF16)

        if h is None:
            xn = _rmsnorm2(hs[0], hs[1], p['norm_mix_g'], BF16)
            res = hs
        else:
            xn = _rmsnorm(h, p['norm_mix_g'], BF16)
            res = h
        zs = _in_proj(xn, w_shift, taps, bsz, t)
        zp = _matmul(xn, w_plain, F32, PROJ_TM, PROJ_TN, name="in_plain").reshape(bsz, t, -1)

        w_up = jnp.stack([p['w_up_f'], p['w_up_b']]).astype(BF16)
        a_up = jnp.stack([p['a_up_f'], p['a_up_b']]).astype(BF16)
        w0 = jnp.stack([p['w0_f'], p['w0_b']]).reshape(2, 1, dm)
        a0 = jnp.stack([p['a0_f'], p['a0_b']]).reshape(2, 1, dm)
        y, bo = _wkv_scan(zs, dm, w_up, a_up, w0, a0,
                          p['k_k'].reshape(1, dm), p['k_a'].reshape(1, dm), p['r_k'].reshape(1, dm))
        yb = _pool_branch(zp, 0, pool_width + dm, p['pool_w'].astype(BF16), p['pool_scale'].reshape(1, dm))
        merged = _merge(y, bo, zs, zp, pool_width, g_up, p['ln_x_g'].reshape(1, dm), p['ln_x_b'].reshape(1, dm), yb)
        h, h_bf, scale = _matmul_stats(merged.reshape(m_tok, dm), p['w_out'].astype(BF16), PROJ_TM, PROJ_TN,
                                       residual=res, name="out_proj")

        mn = _rmsnorm(memf, p['norm_mem_g'], BF16)
        q = _matmul_wres(h_bf, p['xq'], BF16, PROJ_TM, PROJ_TN, scale=scale, gain=_gain_tile(p['norm_x_g']), name="xq")
        kx = _matmul_wres(mn, p['xk'], BF16, PROJ_TM, PROJ_TN, name="xk")
        vx = _matmul_wres(mn, p['xv'], BF16, PROJ_TM, PROJ_TN, name="xv")
        o = _xattn(q.reshape(bsz, t, dm), kx.reshape(bsz, n_mem, dm), vx.reshape(bsz, n_mem, dm))
        h, h_bf, scale = _matmul_stats(o.reshape(m_tok, dm), p['xo'].astype(BF16), PROJ_TM, PROJ_TN,
                                       residual=h, name="xo")

        hidden = p['ffn_w2'].shape[0]
        act = _swiglu_up(h_bf, scale, _gain_tile(p['norm_ffn_g']), p['ffn_w13'], hidden, FFN_TM, FFN_TN)
        w2 = p['ffn_w2'].astype(BF16)
        if l + 1 < depth:
            h = _matmul(act, w2, F32, FFN_DOWN_TM, FFN_DOWN_TN, residual=h, name="ffn_down")

    outs, off = [], 0
    for x, nrow in zip(xs, rows):
        y = _matmul_res_norm(act, w2, h, norm_final_g, FFN_DOWN_TM, FFN_DOWN_TN,
                             off // FFN_DOWN_TM, nrow // FFN_DOWN_TM, name="ffn_down_norm")
        outs.append(y.reshape(x.shape))
        off += nrow
    return tuple(outs)


def kernel(x_prompt, x_sample, mem_prompt, mem_sample, norm_mix_g, w_in, shift_w, w0_f, w_up_f, w0_b, w_up_b, a0_f, a_up_f, a0_b, a_up_b, g_up, k_k, k_a, r_k, ln_x_g, ln_x_b, pool_w, pool_scale, w_out, norm_x_g, norm_mem_g, xq, xk, xv, xo, norm_ffn_g, ffn_w13, ffn_w2, norm_final_g):
    assert x_prompt.shape[1:] == x_sample.shape[1:] and mem_prompt.shape[1:] == mem_sample.shape[1:]
    t, dm = x_prompt.shape[1:]
    hidden = ffn_w2.shape[1]
    assert t % max(WKV_CHUNK, IN_ROW_CHUNK, 256) == 0 and dm % (LANES * WKV_SOLVE_GROUP) == 0
    assert all(x.shape[0] * t % max(PROJ_TM, FFN_TM) == 0 for x in (x_prompt, x_sample))
    assert hidden % FFN_TN == 0 and dm % max(PROJ_TN, FFN_DOWN_TN) == 0
    lp = {
        'norm_mix_g': norm_mix_g, 'w_in': w_in, 'shift_w': shift_w,
        'w0_f': w0_f, 'w_up_f': w_up_f, 'w0_b': w0_b, 'w_up_b': w_up_b,
        'a0_f': a0_f, 'a_up_f': a_up_f, 'a0_b': a0_b, 'a_up_b': a_up_b,
        'g_up': g_up, 'k_k': k_k, 'k_a': k_a, 'r_k': r_k.reshape(r_k.shape[0], -1),
        'ln_x_g': ln_x_g, 'ln_x_b': ln_x_b,
        'pool_w': pool_w, 'pool_scale': pool_scale, 'w_out': w_out,
        'norm_x_g': norm_x_g, 'norm_mem_g': norm_mem_g, 'xq': xq, 'xk': xk, 'xv': xv, 'xo': xo,
        'norm_ffn_g': norm_ffn_g, 'ffn_w13': ffn_w13, 'ffn_w2': ffn_w2,
    }
    mem = jnp.concatenate([mem_prompt, mem_sample], axis=0)
    return _trunk((x_prompt, x_sample), mem, lp, norm_final_g)
```

```python
import functools
import math

import jax
import jax.numpy as jnp
from jax import lax
from jax.experimental import pallas as pl
from jax.experimental.pallas import tpu as pltpu

F32 = jnp.float32
BF16 = jnp.bfloat16

LANES = 128
SUBLANES = 8
VMEM_LIMIT_BYTES = 56 * 1024 * 1024
VMEM_LIMIT_BYTES_MAX = 60000 * 1024

HEAD_SIZE = 64
HEAD_SHIFT = 6
X_HEADS = 4
POOL_WINDOWS = (2, 4, 8, 16)
POOL_PAD = 16
GN_EPS = 64e-5
NORM_EPS = 1e-6
DECAY_SCALE = -math.exp(-0.5)
WKV_CHUNK = 64
WKV_SOLVE_GROUP = 32
WKV_UPDATE_GROUP = 16
MERGE_GROUP = 8
LORA_W = 128
GATE_LORA_PAD = 512
LORA_COLS = 1024
PROJ_TM = 1024
PROJ_TN = 512
FFN_TM = 1024
FFN_TN = 256
FFN_DOWN_TM = 512
FFN_DOWN_TN = 256
IN_TN = 512
IN_ROW_CHUNK = 512


def _params(*semantics):
    return pltpu.CompilerParams(dimension_semantics=semantics, vmem_limit_bytes=VMEM_LIMIT_BYTES)


def _rms(x, g):
    ms = jnp.mean(x * x, axis=-1, keepdims=True)
    return x * lax.rsqrt(ms + NORM_EPS) * g


def _rmsnorm_kernel(x_ref, g_ref, o_ref):
    o_ref[...] = _rms(x_ref[...], g_ref[...]).astype(o_ref.dtype)


def _rmsnorm(x, g, out_dtype, tm=256, row_block_offset=0, n_row_blocks=None):
    m, d = x.shape
    nb = m // tm if n_row_blocks is None else n_row_blocks
    return pl.pallas_call(
        _rmsnorm_kernel,
        grid=(nb,),
        in_specs=[pl.BlockSpec((tm, d), lambda i: (i + row_block_offset, 0)),
                  pl.BlockSpec((1, d), lambda i: (0, 0))],
        out_specs=pl.BlockSpec((tm, d), lambda i: (i, 0)),
        out_shape=jax.ShapeDtypeStruct((nb * tm, d), out_dtype),
        compiler_params=_params("parallel"),
        name="rmsnorm",
    )(x, g.reshape(1, d))


def _rmsnorm2_kernel(xa_ref, xb_ref, g_ref, o_ref, *, na):
    x = jnp.where(pl.program_id(0) < na, xa_ref[...], xb_ref[...])
    o_ref[...] = _rms(x, g_ref[...]).astype(o_ref.dtype)


def _rmsnorm2(xa, xb, g, out_dtype, tm=256):
    d = xa.shape[1]
    na, nb = xa.shape[0] // tm, xb.shape[0] // tm
    return pl.pallas_call(
        functools.partial(_rmsnorm2_kernel, na=na),
        grid=(na + nb,),
        in_specs=[pl.BlockSpec((tm, d), lambda i: (jnp.minimum(i, na - 1), 0)),
                  pl.BlockSpec((tm, d), lambda i: (jnp.maximum(i - na, 0), 0)),
                  pl.BlockSpec((1, d), lambda i: (0, 0))],
        out_specs=pl.BlockSpec((tm, d), lambda i: (i, 0)),
        out_shape=jax.ShapeDtypeStruct(((na + nb) * tm, d), out_dtype),
        compiler_params=_params("arbitrary"),
        name="rmsnorm2",
    )(xa, xb, g.reshape(1, d))


NT_DIMS = (((1,), (1,)), ((), ()))


def _matmul_kernel(x_ref, w_ref, o_ref):
    o_ref[...] = jnp.dot(x_ref[...], w_ref[...], preferred_element_type=F32).astype(o_ref.dtype)


def _matmul_nt_kernel(x_ref, w_ref, o_ref):
    o_ref[...] = lax.dot_general(x_ref[...], w_ref[...], NT_DIMS, preferred_element_type=F32).astype(o_ref.dtype)


def _matmul_nt(x, w, out_dtype, tm, tn, name):
    m, k = x.shape
    n = w.shape[0]
    return pl.pallas_call(
        _matmul_nt_kernel,
        grid=(m // tm, n // tn),
        in_specs=[pl.BlockSpec((tm, k), lambda i, j: (i, 0)),
                  pl.BlockSpec((tn, k), lambda i, j: (j, 0))],
        out_specs=pl.BlockSpec((tm, tn), lambda i, j: (i, j)),
        out_shape=jax.ShapeDtypeStruct((m, n), out_dtype),
        compiler_params=_params("arbitrary", "arbitrary"),
        name=name,
    )(x, w)


def _matmul_res_kernel(x_ref, w_ref, r_ref, o_ref):
    acc = jnp.dot(x_ref[...], w_ref[...], preferred_element_type=F32)
    o_ref[...] = (r_ref[...] + acc).astype(o_ref.dtype)


def _matmul(x, w, out_dtype, tm, tn, residual=None, name="matmul"):
    m, k = x.shape
    n = w.shape[1]
    in_specs = [pl.BlockSpec((tm, k), lambda i, j: (i, 0)),
                pl.BlockSpec((k, tn), lambda i, j: (0, j))]
    args = [x, w]
    body = _matmul_kernel
    if residual is not None:
        in_specs.append(pl.BlockSpec((tm, tn), lambda i, j: (i, j)))
        args.append(residual)
        body = _matmul_res_kernel
    return pl.pallas_call(
        body,
        grid=(m // tm, n // tn),
        in_specs=in_specs,
        out_specs=pl.BlockSpec((tm, tn), lambda i, j: (i, j)),
        out_shape=jax.ShapeDtypeStruct((m, n), out_dtype),
        compiler_params=_params("arbitrary", "arbitrary"),
        name=name,
    )(*args)


def _matmul_res_norm_kernel(x_ref, w_ref, r_ref, g_ref, o_ref):
    j = pl.program_id(1)
    tn = w_ref.shape[1]
    acc = jnp.dot(x_ref[...], w_ref[...], preferred_element_type=F32)
    o_ref[:, pl.ds(pl.multiple_of(j * tn, tn), tn)] = r_ref[...] + acc

    @pl.when(j == pl.num_programs(1) - 1)
    def _():
        o_ref[...] = _rms(o_ref[...], g_ref[...])


def _matmul_res_norm(x, w, residual, g, tm, tn, row_block_offset, n_row_blocks, name):
    k = x.shape[1]
    n = w.shape[1]
    off = row_block_offset
    return pl.pallas_call(
        _matmul_res_norm_kernel,
        grid=(n_row_blocks, n // tn),
        in_specs=[pl.BlockSpec((tm, k), lambda i, j: (i + off, 0)),
                  pl.BlockSpec((k, tn), lambda i, j: (0, j)),
                  pl.BlockSpec((tm, tn), lambda i, j: (i + off, j)),
                  pl.BlockSpec((1, n), lambda i, j: (0, 0))],
        out_specs=pl.BlockSpec((tm, n), lambda i, j: (i, 0)),
        out_shape=jax.ShapeDtypeStruct((n_row_blocks * tm, n), F32),
        compiler_params=pltpu.CompilerParams(dimension_semantics=("arbitrary", "arbitrary"),
                                             vmem_limit_bytes=VMEM_LIMIT_BYTES_MAX),
        name=name,
    )(x, w, residual, g.reshape(1, n))


def _in_proj_kernel(x_ref, w_ref, taps_ref, o_ref):
    t = x_ref.shape[0]
    rc = IN_ROW_CHUNK
    w = w_ref[...]
    tn = w.shape[0]
    taps = taps_ref[...]
    ridx = lax.broadcasted_iota(jnp.int32, (rc, tn), 0)
    zs = [lax.dot_general(x_ref[c * rc:(c + 1) * rc, :], w, NT_DIMS, preferred_element_type=F32)
          for c in range(t // rc)]
    zero_row = jnp.zeros((1, tn), F32)
    for c, z in enumerate(zs):
        prev_row = zs[c - 1][rc - 1:rc, :] if c > 0 else zero_row
        next_row = zs[c + 1][0:1, :] if c + 1 < len(zs) else zero_row
        zm1 = jnp.where(ridx == 0, prev_row, pltpu.roll(z, 1, axis=0))
        zp1 = jnp.where(ridx == rc - 1, next_row, pltpu.roll(z, rc - 1, axis=0))
        o_ref[0, c * rc:(c + 1) * rc, :] = zm1 * taps[0:1, :] + z * taps[1:2, :] + zp1 * taps[2:3, :]


def _in_proj(xn, w, taps, bsz, t):
    k = xn.shape[1]
    n = w.shape[0]
    return pl.pallas_call(
        _in_proj_kernel,
        grid=(bsz, n // IN_TN),
        in_specs=[pl.BlockSpec((t, k), lambda b, j: (b, 0), pipeline_mode=pl.Buffered(1)),
                  pl.BlockSpec((IN_TN, k), lambda b, j: (j, 0)),
                  pl.BlockSpec((3, IN_TN), lambda b, j: (0, j))],
        out_specs=pl.BlockSpec((1, t, IN_TN), lambda b, j: (b, 0, j)),
        out_shape=jax.ShapeDtypeStruct((bsz, t, n), F32),
        compiler_params=_params("arbitrary", "arbitrary"),
        name="in_proj",
    )(xn, w, taps)


def _lane_tiled(x, width):
    return jnp.concatenate([x] * (width // LANES), axis=1)


def _matmul_stats_kernel(x_ref, w_ref, *rest, na, inv_d):
    *r_refs, o_ref, ob_ref, sc_ref, ssq_ref = rest
    j = pl.program_id(1)
    acc = jnp.dot(x_ref[...], w_ref[...], preferred_element_type=F32)
    res = r_refs[0][...] if len(r_refs) == 1 else jnp.where(pl.program_id(0) < na, r_refs[0][...], r_refs[1][...])
    h = res + acc
    o_ref[...] = h
    ob_ref[...] = h.astype(BF16)
    hh = h * h
    part = hh[:, 0:LANES]
    for c in range(1, hh.shape[1] // LANES):
        part = part + hh[:, c * LANES:(c + 1) * LANES]

    @pl.when(j == 0)
    def _():
        ssq_ref[...] = part

    @pl.when(j > 0)
    def _():
        ssq_ref[...] += part

    @pl.when(j == pl.num_programs(1) - 1)
    def _():
        ms = jnp.sum(ssq_ref[...], axis=-1, keepdims=True) * inv_d
        sc_ref[...] = jnp.broadcast_to(lax.rsqrt(ms + NORM_EPS), sc_ref.shape)


def _matmul_stats(x, w, tm, tn, residual, name):
    m, k = x.shape
    n = w.shape[1]
    in_specs = [pl.BlockSpec((tm, k), lambda i, j: (i, 0)),
                pl.BlockSpec((k, tn), lambda i, j: (0, j))]
    na = 0
    if isinstance(residual, tuple):
        ra, rb = residual
        na = ra.shape[0] // tm
        in_specs += [pl.BlockSpec((tm, tn), lambda i, j: (jnp.minimum(i, na - 1), j)),
                     pl.BlockSpec((tm, tn), lambda i, j: (jnp.maximum(i - na, 0), j))]
        res_args = [ra, rb]
    else:
        in_specs.append(pl.BlockSpec((tm, tn), lambda i, j: (i, j)))
        res_args = [residual]
    return pl.pallas_call(
        functools.partial(_matmul_stats_kernel, na=na, inv_d=1.0 / n),
        grid=(m // tm, n // tn),
        in_specs=in_specs,
        out_specs=[pl.BlockSpec((tm, tn), lambda i, j: (i, j)),
                   pl.BlockSpec((tm, tn), lambda i, j: (i, j)),
                   pl.BlockSpec((tm, LANES), lambda i, j: (i, 0))],
        out_shape=[jax.ShapeDtypeStruct((m, n), F32), jax.ShapeDtypeStruct((m, n), BF16),
                   jax.ShapeDtypeStruct((m, LANES), F32)],
        scratch_shapes=[pltpu.VMEM((tm, LANES), F32)],
        compiler_params=_params("arbitrary", "arbitrary"),
        name=name,
    )(x, w, *res_args)


def _cast_weight(w_ref, wb_ref, g_ref):
    w = w_ref[...]
    if g_ref is not None:
        w = w * _lane_tiled(g_ref[...], w.shape[1])
    wb_ref[...] = w.astype(BF16)


def _wres_kernel(*refs, normed):
    if normed:
        x_ref, sc_ref, g_ref, w_ref, o_ref, wb_ref = refs
    else:
        x_ref, w_ref, o_ref, wb_ref = refs
        sc_ref = g_ref = None

    @pl.when(pl.program_id(1) == 0)
    def _():
        _cast_weight(w_ref, wb_ref, g_ref)

    acc = jnp.dot(x_ref[...], wb_ref[...], preferred_element_type=F32)
    if normed:
        acc = acc * _lane_tiled(sc_ref[...], acc.shape[1])
    o_ref[...] = acc.astype(o_ref.dtype)


def _matmul_wres(x, w, out_dtype, tm, tn, scale=None, gain=None, name="matmul_wres"):
    m, k = x.shape
    n = w.shape[1]
    normed = scale is not None
    in_specs = [pl.BlockSpec((tm, k), lambda j, i: (i, 0))]
    args = [x]
    if normed:
        in_specs += [pl.BlockSpec((tm, LANES), lambda j, i: (i, 0)),
                     pl.BlockSpec((k, LANES), lambda j, i: (0, 0))]
        args += [scale, gain]
    in_specs.append(pl.BlockSpec((k, tn), lambda j, i: (0, j)))
    args.append(w)
    return pl.pallas_call(
        functools.partial(_wres_kernel, normed=normed),
        grid=(n // tn, m // tm),
        in_specs=in_specs,
        out_specs=pl.BlockSpec((tm, tn), lambda j, i: (i, j)),
        out_shape=jax.ShapeDtypeStruct((m, n), out_dtype),
        scratch_shapes=[pltpu.VMEM((k, tn), BF16)],
        compiler_params=_params("arbitrary", "arbitrary"),
        name=name,
    )(*args)


def _swiglu_kernel(x_ref, sc_ref, g_ref, wg_ref, wu_ref, o_ref, wgb_ref, wub_ref):
    @pl.when(pl.program_id(1) == 0)
    def _():
        _cast_weight(wg_ref, wgb_ref, g_ref)
        _cast_weight(wu_ref, wub_ref, g_ref)

    x = x_ref[...]
    sc = _lane_tiled(sc_ref[...], o_ref.shape[1])
    gate = jnp.dot(x, wgb_ref[...], preferred_element_type=F32) * sc
    up = jnp.dot(x, wub_ref[...], preferred_element_type=F32) * sc
    o_ref[...] = (gate * jax.nn.sigmoid(gate) * up).astype(o_ref.dtype)


def _swiglu_up(x, scale, gain, w13, hidden, tm, tn):
    m, k = x.shape
    nb = hidden // tn
    return pl.pallas_call(
        _swiglu_kernel,
        grid=(nb, m // tm),
        in_specs=[pl.BlockSpec((tm, k), lambda j, i: (i, 0)),
                  pl.BlockSpec((tm, LANES), lambda j, i: (i, 0)),
                  pl.BlockSpec((k, LANES), lambda j, i: (0, 0)),
                  pl.BlockSpec((k, tn), lambda j, i: (0, j)),
                  pl.BlockSpec((k, tn), lambda j, i: (0, j + nb))],
        out_specs=pl.BlockSpec((tm, tn), lambda j, i: (i, j)),
        out_shape=jax.ShapeDtypeStruct((m, hidden), BF16),
        scratch_shapes=[pltpu.VMEM((k, tn), BF16), pltpu.VMEM((k, tn), BF16)],
        compiler_params=_params("arbitrary", "arbitrary"),
        name="swiglu_up",
    )(x, scale, gain, w13, w13)


def _head_sum_matrix():
    r = lax.broadcasted_iota(jnp.int32, (LANES, LANES), 0) >> HEAD_SHIFT
    c = lax.broadcasted_iota(jnp.int32, (LANES, LANES), 1) >> HEAD_SHIFT
    return jnp.where(r == c, 1.0, 0.0).astype(BF16)


def _split2(x):
    hi = x.astype(BF16)
    return hi, (x - hi.astype(F32)).astype(BF16)


def _gmap(f, *lists):
    return [f(*xs) for xs in zip(*lists)]


def _wkv_kernel(z_ref, lo_ref, wup_ref, aup_ref, w0_ref, a0_ref, kk_ref, ka_ref, rk_ref,
                y_ref, bo_ref, state_ref, lw_ref, cum_ref, icl_ref,
                tinv_ref, lrk_ref, rb_ref, lhs_ref, btk_ref, vb_ref, *, d_model):
    C = WKV_CHUNK
    d = pl.program_id(1)
    c = pl.program_id(2)
    sgn = 1 - 2 * d

    @pl.when(c == 0)
    def _():
        state_ref[...] = jnp.zeros_like(state_ref)

    wl = w0_ref[0] + jnp.dot(jnp.tanh(lo_ref[0, :, 0:LORA_W]).astype(BF16), wup_ref[0],
                             preferred_element_type=F32)
    lw = DECAY_SCALE * jax.nn.sigmoid(wl)
    lw_ref[...] = lw
    r64 = lax.broadcasted_iota(jnp.int32, (C, C), 0)
    c64 = lax.broadcasted_iota(jnp.int32, (C, C), 1)
    tri = jnp.where((r64 - c64) * sgn >= 0, 1.0, 0.0).astype(BF16)
    lw_hi, lw_lo = _split2(lw)
    cum_ref[...] = (jnp.dot(tri, lw_hi, preferred_element_type=F32)
                    + jnp.dot(tri, lw_lo, preferred_element_type=F32))
    icl_ref[...] = jax.nn.sigmoid(
        a0_ref[0] + jnp.dot(lo_ref[0, :, LORA_W:2 * LORA_W].astype(BF16), aup_ref[0],
                            preferred_element_type=F32))

    row = lax.broadcasted_iota(jnp.int32, (C, LANES), 0)
    col = lax.broadcasted_iota(jnp.int32, (C, LANES), 1)
    colh = col & (HEAD_SIZE - 1)
    order = (row - colh) * sgn
    strict = order > 0
    incl = order >= 0
    eye2 = row == colh
    lane_lo = col < HEAD_SIZE
    esum = _head_sum_matrix()
    rr = lax.broadcasted_iota(jnp.int32, (LANES, LANES), 0) >> HEAD_SHIFT
    cc = lax.broadcasted_iota(jnp.int32, (LANES, LANES), 1) >> HEAD_SHIFT
    blockdiag = rr == cc

    def bd(x):
        zero = jnp.zeros_like(x)
        return jnp.concatenate([jnp.where(lane_lo, x, zero), jnp.where(lane_lo, zero, x)], axis=0)

    def pmul(x, y):
        return jnp.dot(x.astype(BF16), bd(y.astype(BF16)), preferred_element_type=F32)

    nt_dims = (((1,), (1,)), ((), ()))
    tn_dims = (((0,), (0,)), ((), ()))

    def lane_tile(p, offset=0):
        return pl.ds(pl.multiple_of(offset + p * LANES, LANES), LANES)

    def head_sums(xs):
        stacked = jnp.concatenate([x.astype(BF16) for x in xs], axis=0)
        sums = jnp.dot(stacked, esum, preferred_element_type=F32)
        return [sums[j * C:(j + 1) * C] for j in range(len(xs))]

    def solve_body(g, carry):
        pairs = [g * WKV_SOLVE_GROUP + j for j in range(WKV_SOLVE_GROUP)]
        cols = [lane_tile(p) for p in pairs]
        r = [z_ref[0, :, cs] for cs in cols]
        k = [z_ref[0, :, lane_tile(p, d_model)] for p in pairs]
        v = [z_ref[0, :, lane_tile(p, 2 * d_model)] for p in pairs]
        icl = [icl_ref[:, cs] for cs in cols]
        lwp = [lw_ref[:, cs] for cs in cols]
        cum = [cum_ref[:, cs] for cs in cols]

        q = [ki * kk_ref[:, cs] for ki, cs in zip(k, cols)]
        n2 = head_sums([qi * qi for qi in q])
        kd = [ki * (1.0 + (ic - 1.0) * ka_ref[:, cs]) for ki, ic, cs in zip(k, icl, cols)]
        bsum = head_sums([ri * kdi * rk_ref[:, cs] for ri, kdi, cs in zip(r, kd, cols)])
        for cs, bs, vi in zip(cols, bsum, v):
            bo_ref[0, 0, :, cs] = (bs * vi).astype(bo_ref.dtype)

        kk = [qi * lax.rsqrt(jnp.maximum(ni, 1e-12)) for qi, ni in zip(q, n2)]
        b = _gmap(lambda x, ic: x * ic, kk, icl)
        e_out = [jnp.exp(-x) for x in cum]
        at = _gmap(lambda x, cm, lw_: (-x * jnp.exp(cm - lw_)).astype(BF16), kk, cum, lwp)
        rt = _gmap(lambda x, cm: (x * jnp.exp(cm)).astype(BF16), r, cum)
        bt = _gmap(lambda x, e: (x * e).astype(BF16), b, e_out)
        kt = _gmap(lambda x, e: (x * e).astype(BF16), kd, e_out)

        lhs = _gmap(lambda a_, r_: jnp.concatenate([a_, r_], axis=0), at, rt)
        rhs_t = _gmap(lambda b_, k_: jnp.concatenate([bd(b_), bd(k_)], axis=0), bt, kt)
        pmat = _gmap(lambda l_, r_: lax.dot_general(l_, r_, nt_dims, preferred_element_type=F32), lhs, rhs_t)
        for p, l_, b_, k_, v_ in zip(pairs, lhs, bt, kt, v):
            lhs_ref[p] = l_
            btk_ref[p] = jnp.concatenate([b_, k_], axis=0)
            vb_ref[p] = v_.astype(BF16)
        lab = [jnp.where(strict, x[:C, :LANES], 0.0).astype(BF16) for x in pmat]
        for p, x in zip(pairs, pmat):
            rb_ref[p] = jnp.where(incl, x[C:, :LANES], 0.0).astype(BF16)
            lrk_ref[p] = jnp.concatenate([jnp.where(strict, x[:C, LANES:], 0.0).astype(BF16),
                                          jnp.where(incl, x[C:, LANES:], 0.0).astype(BF16)], axis=0)

        zero_b = jnp.zeros((C, LANES), BF16)
        ident = jnp.where(eye2, 1.0, 0.0).astype(BF16)
        first = (row >> 1) == (colh >> 1)
        tinv = [ident + jnp.where(first, x, zero_b) for x in lab]
        s = 2
        while s < C:
            sh = s.bit_length() - 1
            level = ((row >> (sh + 1)) == (colh >> (sh + 1))) & ((row >> sh) != (colh >> sh))
            off = [jnp.where(level, x, zero_b) for x in lab]
            tmp = _gmap(pmul, tinv, off)
            upd_t = _gmap(pmul, tmp, tinv)
            tinv = _gmap(lambda t_, x: t_ + x.astype(BF16), tinv, upd_t)
            s *= 2
        for p, x in zip(pairs, tinv):
            tinv_ref[p] = x
        return carry

    def update_body(g, carry):
        pairs = [g * WKV_UPDATE_GROUP + j for j in range(WKV_UPDATE_GROUP)]
        cols = [lane_tile(p) for p in pairs]
        h = [state_ref[p] for p in pairs]
        vb = [vb_ref[p] for p in pairs]
        hs = [jnp.dot(lhs_ref[p], h_.astype(BF16), preferred_element_type=F32)
              for p, h_ in zip(pairs, h)]
        lrkv = [jnp.dot(lrk_ref[p], bd(v_), preferred_element_type=F32) for p, v_ in zip(pairs, vb)]
        rhs_u = _gmap(lambda h_, x: (h_[:C] + x[:C]).astype(BF16), hs, lrkv)
        u = [jnp.dot(tinv_ref[p], bd(x), preferred_element_type=F32) for p, x in zip(pairs, rhs_u)]
        ub = [x.astype(BF16) for x in u]
        rbu = [jnp.dot(rb_ref[p], bd(x), preferred_element_type=F32) for p, x in zip(pairs, ub)]
        for cs, h_, xv, xu in zip(cols, hs, lrkv, rbu):
            y_ref[0, 0, :, cs] = (h_[C:] + xv[C:] + xu).astype(y_ref.dtype)
        upd = [lax.dot_general(btk_ref[p], jnp.concatenate([u_, v_], axis=0), tn_dims,
                               preferred_element_type=F32) for p, u_, v_ in zip(pairs, ub, vb)]
        for p, cs, h_, x in zip(pairs, cols, h, upd):
            tot = jnp.sum(lw_ref[:, cs], axis=0, keepdims=True)
            decay_rows = jnp.broadcast_to(jnp.exp(tot), (LANES, LANES)).T
            state_ref[p] = decay_rows * (h_ + jnp.where(blockdiag, x, 0.0))
        return carry

    n_pairs = d_model // LANES
    lax.fori_loop(0, n_pairs // WKV_SOLVE_GROUP, solve_body, 0)
    lax.fori_loop(0, n_pairs // WKV_UPDATE_GROUP, update_body, 0)


def _wkv_scan(z, dm, w_up, a_up, w0, a0, k_k, k_a, r_k):
    bsz, t, n = z.shape
    C = WKV_CHUNK
    nc = t // C
    n_pairs = dm // LANES
    lora_block = (n - LORA_COLS) // LORA_COLS

    def tchunk(dd, cc):
        return jnp.where(dd == 0, cc, nc - 1 - cc)

    def dir_map(bb, dd, cc):
        return (dd, 0, 0)

    def const2(bb, dd, cc):
        return (0, 0)

    def out_map(bb, dd, cc):
        return (dd, bb, tchunk(dd, cc), 0)

    out_sds = jax.ShapeDtypeStruct((2, bsz, t, dm), BF16)
    kern = functools.partial(_wkv_kernel, d_model=dm)
    return pl.pallas_call(
        kern,
        grid=(bsz, 2, nc),
        in_specs=[
            pl.BlockSpec((1, C, 3 * dm), lambda bb, dd, cc: (bb, tchunk(dd, cc), 0)),
            pl.BlockSpec((1, C, LORA_COLS), lambda bb, dd, cc: (bb, tchunk(dd, cc), lora_block)),
            pl.BlockSpec((1, LORA_W, dm), dir_map),
            pl.BlockSpec((1, LORA_W, dm), dir_map),
            pl.BlockSpec((1, 1, dm), dir_map),
            pl.BlockSpec((1, 1, dm), dir_map),
            pl.BlockSpec((1, dm), const2),
            pl.BlockSpec((1, dm), const2),
            pl.BlockSpec((1, dm), const2),
        ],
        out_specs=[pl.BlockSpec((1, 1, C, dm), out_map), pl.BlockSpec((1, 1, C, dm), out_map)],
        out_shape=[out_sds, out_sds],
        scratch_shapes=[pltpu.VMEM((n_pairs, LANES, LANES), F32),
                        pltpu.VMEM((C, dm), F32), pltpu.VMEM((C, dm), F32), pltpu.VMEM((C, dm), F32),
                        pltpu.VMEM((n_pairs, C, LANES), BF16), pltpu.VMEM((n_pairs, 2 * C, LANES), BF16),
                        pltpu.VMEM((n_pairs, C, LANES), BF16), pltpu.VMEM((n_pairs, 2 * C, LANES), BF16),
                        pltpu.VMEM((n_pairs, 2 * C, LANES), BF16), pltpu.VMEM((n_pairs, C, LANES), BF16)],
        compiler_params=_params("arbitrary", "arbitrary", "arbitrary"),
        name="wkv_scan",
    )(z, z, w_up, a_up, w0, a0, k_k, k_a, r_k)


def _pool_kernel(p_ref, gate_ref, w_ref, scale_ref, o_ref, pad_ref, *, rows):
    g = pl.program_id(1)
    t, gi = p_ref.shape[1], p_ref.shape[2]
    zeros = jnp.zeros((POOL_PAD, gi), F32)
    pad_ref[0:POOL_PAD, :] = zeros
    pad_ref[POOL_PAD + t:POOL_PAD + t + POOL_PAD, :] = zeros
    pad_ref[POOL_PAD:POOL_PAD + t, :] = p_ref[0]
    w = w_ref[0]
    scale = scale_ref[...]

    for gidx, win in enumerate(POOL_WINDOWS):
        @pl.when(g == gidx)
        def _(win=win):
            half = win // 2

            def tile_body(i, carry):
                r0 = pl.multiple_of(i * rows, rows)
                n = rows + 2 * SUBLANES
                xt = pad_ref[pl.ds(r0 + POOL_PAD - SUBLANES, n), :]
                acc = xt
                step = 1
                while step < win:
                    acc = acc + pltpu.roll(acc, n - step, axis=0)
                    step *= 2
                if SUBLANES - half:
                    acc = pltpu.roll(acc, n - (SUBLANES - half), axis=0)
                acc = acc[0:rows]
                tt = r0 + lax.broadcasted_iota(jnp.int32, (rows, LANES), 0)
                cnt = (jnp.minimum(tt + (win - half), t) - jnp.maximum(tt - half, 0)).astype(F32)
                inv = 1.0 / cnt
                inv_full = jnp.concatenate([inv] * (gi // LANES), axis=1)
                dlt = acc * inv_full - xt[SUBLANES:SUBLANES + rows]
                out = jnp.dot(dlt.astype(BF16), w, preferred_element_type=F32) * scale
                gate = jax.nn.sigmoid(gate_ref[0, pl.ds(r0, rows), :])
                o_ref[0, pl.ds(r0, rows), :] = (gate * out).astype(o_ref.dtype)
                return carry

            lax.fori_loop(0, t // rows, tile_body, 0)


def _pool_branch(z, pool_col, gate_col, pool_w, pool_scale, rows=256):
    bsz, t, _ = z.shape
    ng, gi, go = pool_w.shape
    dm = ng * go
    kern = functools.partial(_pool_kernel, rows=rows)
    return pl.pallas_call(
        kern,
        grid=(bsz, ng),
        in_specs=[pl.BlockSpec((1, t, gi), lambda b, g: (b, 0, pool_col // gi + g)),
                  pl.BlockSpec((1, t, go), lambda b, g: (b, 0, gate_col // go + g)),
                  pl.BlockSpec((1, gi, go), lambda b, g: (g, 0, 0)),
                  pl.BlockSpec((1, go), lambda b, g: (0, g))],
        out_specs=pl.BlockSpec((1, t, go), lambda b, g: (b, 0, g)),
        out_shape=jax.ShapeDtypeStruct((bsz, t, dm), BF16),
        scratch_shapes=[pltpu.VMEM((t + 2 * POOL_PAD, gi), F32)],
        compiler_params=_params("arbitrary", "arbitrary"),
        name="pool_branch",
    )(z, z, pool_w, pool_scale)


def _merge_kernel(y_ref, bo_ref, lo_ref, gup_ref, gng_ref, gnb_ref, gate_ref, yb_ref, o_ref):
    gd_lo = 2 * LORA_W
    esum = _head_sum_matrix()
    inv_n = 1.0 / HEAD_SIZE
    gd = jax.nn.sigmoid(lo_ref[0, :, gd_lo:gd_lo + GATE_LORA_PAD]).astype(BF16)

    def group_body(gidx, carry):
        cols = [pl.ds(pl.multiple_of((gidx * MERGE_GROUP + j) * LANES, LANES), LANES) for j in range(MERGE_GROUP)]
        y = [y_ref[0, 0, :, cs].astype(F32) + y_ref[1, 0, :, cs].astype(F32) for cs in cols]
        mu = [jnp.dot(x.astype(BF16), esum, preferred_element_type=F32) * inv_n for x in y]
        g = [jnp.dot(gd, gup_ref[:, cs], preferred_element_type=F32) for cs in cols]
        yc = _gmap(lambda x, m: x - m, y, mu)
        var = [jnp.dot((x * x).astype(BF16), esum, preferred_element_type=F32) * inv_n for x in yc]
        for cs, x, vr, gi in zip(cols, yc, var, g):
            yn = x * lax.rsqrt(vr + GN_EPS) * gng_ref[:, cs] + gnb_ref[:, cs]
            yn = yn + bo_ref[0, 0, :, cs].astype(F32) + bo_ref[1, 0, :, cs].astype(F32)
            ya = jax.nn.sigmoid(gate_ref[0, :, cs]) * (yn * gi)
            o_ref[0, :, cs] = (ya + yb_ref[0, :, cs].astype(F32)).astype(o_ref.dtype)
        return carry

    lax.fori_loop(0, o_ref.shape[2] // (LANES * MERGE_GROUP), group_body, 0)


def _merge(y, bo, zs, zp, g_up, ln_g, ln_b, yb, tt=256):
    _, bsz, t, dm = y.shape
    n = zs.shape[-1]
    lora_block = (n - LORA_COLS) // LORA_COLS
    return pl.pallas_call(
        _merge_kernel,
        grid=(bsz, t // tt),
        in_specs=[pl.BlockSpec((2, 1, tt, dm), lambda b, i: (0, b, i, 0)),
                  pl.BlockSpec((2, 1, tt, dm), lambda b, i: (0, b, i, 0)),
                  pl.BlockSpec((1, tt, LORA_COLS), lambda b, i: (b, i, lora_block)),
                  pl.BlockSpec((GATE_LORA_PAD, dm), lambda b, i: (0, 0)),
                  pl.BlockSpec((1, dm), lambda b, i: (0, 0)),
                  pl.BlockSpec((1, dm), lambda b, i: (0, 0)),
                  pl.BlockSpec((1, tt, dm), lambda b, i: (b, i, 0)),
                  pl.BlockSpec((1, tt, dm), lambda b, i: (b, i, 0))],
        out_specs=pl.BlockSpec((1, tt, dm), lambda b, i: (b, i, 0)),
        out_shape=jax.ShapeDtypeStruct((bsz, t, dm), BF16),
        compiler_params=_params("arbitrary", "arbitrary"),
        name="wkv_merge",
    )(y, bo, zs, g_up, ln_g, ln_b, zp, yb)


def _xattn_kernel(q_ref, k_ref, v_ref, o_ref, *, head_dim):
    scale = head_dim ** -0.5
    nt_dims = (((1,), (1,)), ((), ()))
    for h in range(X_HEADS):
        cs = slice(h * head_dim, (h + 1) * head_dim)
        s = lax.dot_general(q_ref[0, :, cs], k_ref[0, :, cs], nt_dims, preferred_element_type=F32) * scale
        m = jnp.max(s, axis=-1, keepdims=True)
        e = jnp.exp(s - m)
        p = e / jnp.sum(e, axis=-1, keepdims=True)
        o_ref[0, :, cs] = jnp.dot(p.astype(BF16), v_ref[0, :, cs], preferred_element_type=F32).astype(o_ref.dtype)


def _xattn(q, k, v, tq=512):
    bsz, t, dm = q.shape
    m = k.shape[1]
    kern = functools.partial(_xattn_kernel, head_dim=dm // X_HEADS)
    return pl.pallas_call(
        kern,
        grid=(bsz, t // tq),
        in_specs=[pl.BlockSpec((1, tq, dm), lambda b, i: (b, i, 0)),
                  pl.BlockSpec((1, m, dm), lambda b, i: (b, 0, 0)),
                  pl.BlockSpec((1, m, dm), lambda b, i: (b, 0, 0))],
        out_specs=pl.BlockSpec((1, tq, dm), lambda b, i: (b, i, 0)),
        out_shape=jax.ShapeDtypeStruct((bsz, t, dm), BF16),
        compiler_params=_params("arbitrary", "arbitrary"),
        name="xattn",
    )(q, k, v)


def _pad_to(x, axis, size):
    pad = [(0, 0)] * x.ndim
    pad[axis] = (0, size - x.shape[axis])
    return jnp.pad(x, pad)


def _gain_tile(g):
    return jnp.broadcast_to(g[:, None], (g.shape[0], LANES))


def _in_proj_operands(p, dm):
    w_in, shift_w = p['w_in'], p['shift_w']
    gate_lora = p['g_up'].shape[0]
    c_rkv = 3 * dm
    c_lora = c_rkv + 2 * LORA_W + gate_lora
    pool_width = p['pool_w'].shape[0] * p['pool_w'].shape[1]
    c_pool = c_lora + pool_width
    wt = jnp.swapaxes(w_in, 0, 1).astype(BF16)
    w_shift = jnp.concatenate([wt[:c_rkv], _pad_to(wt[c_rkv:c_lora], 0, LORA_COLS)], axis=0)
    taps = jnp.concatenate([shift_w[:, :c_rkv], _pad_to(shift_w[:, c_rkv:c_lora], 1, LORA_COLS)], axis=1)
    w_plain = jnp.concatenate([wt[c_pool:], wt[c_lora:c_pool]], axis=0)
    return w_shift, taps, w_plain


def _trunk(xs, mem, lp, norm_final_g):
    t, dm = xs[0].shape[1:]
    rows = [x.shape[0] * t for x in xs]
    bsz = sum(x.shape[0] for x in xs)
    n_mem = mem.shape[1]
    m_tok = bsz * t
    hs = tuple(x.reshape(-1, dm) for x in xs)
    memf = mem.reshape(bsz * n_mem, dm)
    depth = lp['w_in'].shape[0]
    h = None
    for l in range(depth):
        p = {name: arr[l] for name, arr in lp.items()}
        w_shift, taps, w_plain = _in_proj_operands(p, dm)
        g_up = _pad_to(p['g_up'], 0, GATE_LORA_PAD).astype(BF16)

        if h is None:
            xn = _rmsnorm2(hs[0], hs[1], p['norm_mix_g'], BF16)
            res = hs
        else:
            xn = _rmsnorm(h, p['norm_mix_g'], BF16)
            res = h
        zs = _in_proj(xn, w_shift, taps, bsz, t)
        zp = _matmul_nt(xn, w_plain, F32, PROJ_TM, PROJ_TN, name="in_plain").reshape(bsz, t, -1)

        w_up = jnp.stack([p['w_up_f'], p['w_up_b']]).astype(BF16)
        a_up = jnp.stack([p['a_up_f'], p['a_up_b']]).astype(BF16)
        w0 = jnp.stack([p['w0_f'], p['w0_b']]).reshape(2, 1, dm)
        a0 = jnp.stack([p['a0_f'], p['a0_b']]).reshape(2, 1, dm)
        y, bo = _wkv_scan(zs, dm, w_up, a_up, w0, a0,
                          p['k_k'].reshape(1, dm), p['k_a'].reshape(1, dm), p['r_k'].reshape(1, dm))
        yb = _pool_branch(zp, 2 * dm, dm, p['pool_w'].astype(BF16), p['pool_scale'].reshape(1, dm))
        merged = _merge(y, bo, zs, zp, g_up, p['ln_x_g'].reshape(1, dm), p['ln_x_b'].reshape(1, dm), yb)
        h, h_bf, scale = _matmul_stats(merged.reshape(m_tok, dm), p['w_out'].astype(BF16), PROJ_TM, PROJ_TN,
                                       residual=res, name="out_proj")

        mn = _rmsnorm(memf, p['norm_mem_g'], BF16)
        q = _matmul_wres(h_bf, p['xq'], BF16, PROJ_TM, PROJ_TN, scale=scale, gain=_gain_tile(p['norm_x_g']), name="xq")
        kx = _matmul_wres(mn, p['xk'], BF16, PROJ_TM, PROJ_TN, name="xk")
        vx = _matmul_wres(mn, p['xv'], BF16, PROJ_TM, PROJ_TN, name="xv")
        o = _xattn(q.reshape(bsz, t, dm), kx.reshape(bsz, n_mem, dm), vx.reshape(bsz, n_mem, dm))
        h, h_bf, scale = _matmul_stats(o.reshape(m_tok, dm), p['xo'].astype(BF16), PROJ_TM, PROJ_TN,
                                       residual=h, name="xo")

        hidden = p['ffn_w2'].shape[0]
        act = _swiglu_up(h_bf, scale, _gain_tile(p['norm_ffn_g']), p['ffn_w13'], hidden, FFN_TM, FFN_TN)
        w2 = p['ffn_w2'].astype(BF16)
        if l + 1 < depth:
            h = _matmul(act, w2, F32, FFN_DOWN_TM, FFN_DOWN_TN, residual=h, name="ffn_down")

    outs, off = [], 0
    for x, nrow in zip(xs, rows):
        y = _matmul_res_norm(act, w2, h, norm_final_g, FFN_DOWN_TM, FFN_DOWN_TN,
                             off // FFN_DOWN_TM, nrow // FFN_DOWN_TM, name="ffn_down_norm")
        outs.append(y.reshape(x.shape))
        off += nrow
    return tuple(outs)


def kernel(x_prompt, x_sample, mem_prompt, mem_sample, norm_mix_g, w_in, shift_w, w0_f, w_up_f, w0_b, w_up_b, a0_f, a_up_f, a0_b, a_up_b, g_up, k_k, k_a, r_k, ln_x_g, ln_x_b, pool_w, pool_scale, w_out, norm_x_g, norm_mem_g, xq, xk, xv, xo, norm_ffn_g, ffn_w13, ffn_w2, norm_final_g):
    assert x_prompt.shape[1:] == x_sample.shape[1:] and mem_prompt.shape[1:] == mem_sample.shape[1:]
    t, dm = x_prompt.shape[1:]
    hidden = ffn_w2.shape[1]
    assert t % max(WKV_CHUNK, IN_ROW_CHUNK, 256) == 0 and dm % (LANES * WKV_SOLVE_GROUP) == 0
    assert all(x.shape[0] * t % max(PROJ_TM, FFN_TM) == 0 for x in (x_prompt, x_sample))
    assert hidden % FFN_TN == 0 and dm % max(PROJ_TN, FFN_DOWN_TN) == 0
    lp = {
        'norm_mix_g': norm_mix_g, 'w_in': w_in, 'shift_w': shift_w,
        'w0_f': w0_f, 'w_up_f': w_up_f, 'w0_b': w0_b, 'w_up_b': w_up_b,
        'a0_f': a0_f, 'a_up_f': a_up_f, 'a0_b': a0_b, 'a_up_b': a_up_b,
        'g_up': g_up, 'k_k': k_k, 'k_a': k_a, 'r_k': r_k.reshape(r_k.shape[0], -1),
        'ln_x_g': ln_x_g, 'ln_x_b': ln_x_b,
        'pool_w': pool_w, 'pool_scale': pool_scale, 'w_out': w_out,
        'norm_x_g': norm_x_g, 'norm_mem_g': norm_mem_g, 'xq': xq, 'xk': xk, 'xv': xv, 'xo': xo,
        'norm_ffn_g': norm_ffn_g, 'ffn_w13': ffn_w13, 'ffn_w2': ffn_w2,
    }
    mem = jnp.concatenate([mem_prompt, mem_sample], axis=0)
    return _trunk((x_prompt, x_sample), mem, lp, norm_final_g)
```

```python
import functools
import math

import jax
import jax.numpy as jnp
from jax import lax
from jax.experimental import pallas as pl
from jax.experimental.pallas import tpu as pltpu

F32 = jnp.float32
BF16 = jnp.bfloat16

LANES = 128
SUBLANES = 8
VMEM_LIMIT_BYTES = 56 * 1024 * 1024
VMEM_LIMIT_BYTES_MAX = 60000 * 1024

HEAD_SIZE = 64
HEAD_SHIFT = 6
X_HEADS = 4
POOL_WINDOWS = (2, 4, 8, 16)
POOL_PAD = 16
GN_EPS = 64e-5
NORM_EPS = 1e-6
DECAY_SCALE = -math.exp(-0.5)
WKV_CHUNK = 64
WKV_SOLVE_GROUP = 32
WKV_UPDATE_GROUP = 16
MERGE_GROUP = 8
LORA_W = 128
GATE_LORA_PAD = 512
LORA_COLS = 1024
PROJ_TM = 1024
PROJ_TN = 512
FFN_TM = 1024
FFN_TN = 256
FFN_DOWN_TM = 512
FFN_DOWN_TN = 256
IN_TN = 512
IN_ROW_CHUNK = 512


def _params(*semantics):
    return pltpu.CompilerParams(dimension_semantics=semantics, vmem_limit_bytes=VMEM_LIMIT_BYTES)


def _rms(x, g):
    ms = jnp.mean(x * x, axis=-1, keepdims=True)
    return x * lax.rsqrt(ms + NORM_EPS) * g


def _rmsnorm_kernel(x_ref, g_ref, o_ref):
    o_ref[...] = _rms(x_ref[...], g_ref[...]).astype(o_ref.dtype)


def _rmsnorm(x, g, out_dtype, tm=256, row_block_offset=0, n_row_blocks=None):
    m, d = x.shape
    nb = m // tm if n_row_blocks is None else n_row_blocks
    return pl.pallas_call(
        _rmsnorm_kernel,
        grid=(nb,),
        in_specs=[pl.BlockSpec((tm, d), lambda i: (i + row_block_offset, 0)),
                  pl.BlockSpec((1, d), lambda i: (0, 0))],
        out_specs=pl.BlockSpec((tm, d), lambda i: (i, 0)),
        out_shape=jax.ShapeDtypeStruct((nb * tm, d), out_dtype),
        compiler_params=_params("parallel"),
        name="rmsnorm",
    )(x, g.reshape(1, d))


def _rmsnorm2_kernel(xa_ref, xb_ref, g_ref, o_ref, *, na):
    x = jnp.where(pl.program_id(0) < na, xa_ref[...], xb_ref[...])
    o_ref[...] = _rms(x, g_ref[...]).astype(o_ref.dtype)


def _rmsnorm2(xa, xb, g, out_dtype, tm=256):
    d = xa.shape[1]
    na, nb = xa.shape[0] // tm, xb.shape[0] // tm
    return pl.pallas_call(
        functools.partial(_rmsnorm2_kernel, na=na),
        grid=(na + nb,),
        in_specs=[pl.BlockSpec((tm, d), lambda i: (jnp.minimum(i, na - 1), 0)),
                  pl.BlockSpec((tm, d), lambda i: (jnp.maximum(i - na, 0), 0)),
                  pl.BlockSpec((1, d), lambda i: (0, 0))],
        out_specs=pl.BlockSpec((tm, d), lambda i: (i, 0)),
        out_shape=jax.ShapeDtypeStruct(((na + nb) * tm, d), out_dtype),
        compiler_params=_params("arbitrary"),
        name="rmsnorm2",
    )(xa, xb, g.reshape(1, d))


def _matmul_kernel(x_ref, w_ref, o_ref):
    o_ref[...] = jnp.dot(x_ref[...], w_ref[...], preferred_element_type=F32).astype(o_ref.dtype)


def _matmul_res_kernel(x_ref, w_ref, r_ref, o_ref):
    acc = jnp.dot(x_ref[...], w_ref[...], preferred_element_type=F32)
    o_ref[...] = (r_ref[...] + acc).astype(o_ref.dtype)


def _matmul(x, w, out_dtype, tm, tn, residual=None, name="matmul"):
    m, k = x.shape
    n = w.shape[1]
    in_specs = [pl.BlockSpec((tm, k), lambda i, j: (i, 0)),
                pl.BlockSpec((k, tn), lambda i, j: (0, j))]
    args = [x, w]
    body = _matmul_kernel
    if residual is not None:
        in_specs.append(pl.BlockSpec((tm, tn), lambda i, j: (i, j)))
        args.append(residual)
        body = _matmul_res_kernel
    return pl.pallas_call(
        body,
        grid=(m // tm, n // tn),
        in_specs=in_specs,
        out_specs=pl.BlockSpec((tm, tn), lambda i, j: (i, j)),
        out_shape=jax.ShapeDtypeStruct((m, n), out_dtype),
        compiler_params=_params("arbitrary", "arbitrary"),
        name=name,
    )(*args)


def _matmul_res_norm_kernel(x_ref, w_ref, r_ref, g_ref, o_ref):
    j = pl.program_id(1)
    tn = w_ref.shape[1]
    acc = jnp.dot(x_ref[...], w_ref[...], preferred_element_type=F32)
    o_ref[:, pl.ds(pl.multiple_of(j * tn, tn), tn)] = r_ref[...] + acc

    @pl.when(j == pl.num_programs(1) - 1)
    def _():
        o_ref[...] = _rms(o_ref[...], g_ref[...])


def _matmul_res_norm(x, w, residual, g, tm, tn, row_block_offset, n_row_blocks, name):
    k = x.shape[1]
    n = w.shape[1]
    off = row_block_offset
    return pl.pallas_call(
        _matmul_res_norm_kernel,
        grid=(n_row_blocks, n // tn),
        in_specs=[pl.BlockSpec((tm, k), lambda i, j: (i + off, 0)),
                  pl.BlockSpec((k, tn), lambda i, j: (0, j)),
                  pl.BlockSpec((tm, tn), lambda i, j: (i + off, j)),
                  pl.BlockSpec((1, n), lambda i, j: (0, 0))],
        out_specs=pl.BlockSpec((tm, n), lambda i, j: (i, 0)),
        out_shape=jax.ShapeDtypeStruct((n_row_blocks * tm, n), F32),
        compiler_params=pltpu.CompilerParams(dimension_semantics=("arbitrary", "arbitrary"),
                                             vmem_limit_bytes=VMEM_LIMIT_BYTES_MAX),
        name=name,
    )(x, w, residual, g.reshape(1, n))


def _in_proj_kernel(x_ref, w_ref, taps_ref, o_ref):
    t = x_ref.shape[0]
    rc = IN_ROW_CHUNK
    w = w_ref[...]
    taps = taps_ref[...]
    ridx = lax.broadcasted_iota(jnp.int32, (rc, w.shape[1]), 0)
    zs = [jnp.dot(x_ref[c * rc:(c + 1) * rc, :], w, preferred_element_type=F32) for c in range(t // rc)]
    zero_row = jnp.zeros((1, w.shape[1]), F32)
    for c, z in enumerate(zs):
        prev_row = zs[c - 1][rc - 1:rc, :] if c > 0 else zero_row
        next_row = zs[c + 1][0:1, :] if c + 1 < len(zs) else zero_row
        zm1 = jnp.where(ridx == 0, prev_row, pltpu.roll(z, 1, axis=0))
        zp1 = jnp.where(ridx == rc - 1, next_row, pltpu.roll(z, rc - 1, axis=0))
        o_ref[0, c * rc:(c + 1) * rc, :] = zm1 * taps[0:1, :] + z * taps[1:2, :] + zp1 * taps[2:3, :]


def _in_proj(xn, w, taps, bsz, t):
    k = xn.shape[1]
    n = w.shape[1]
    return pl.pallas_call(
        _in_proj_kernel,
        grid=(bsz, n // IN_TN),
        in_specs=[pl.BlockSpec((t, k), lambda b, j: (b, 0), pipeline_mode=pl.Buffered(1)),
                  pl.BlockSpec((k, IN_TN), lambda b, j: (0, j)),
                  pl.BlockSpec((3, IN_TN), lambda b, j: (0, j))],
        out_specs=pl.BlockSpec((1, t, IN_TN), lambda b, j: (b, 0, j)),
        out_shape=jax.ShapeDtypeStruct((bsz, t, n), F32),
        compiler_params=_params("arbitrary", "arbitrary"),
        name="in_proj",
    )(xn, w, taps)


def _lane_tiled(x, width):
    return jnp.concatenate([x] * (width // LANES), axis=1)


def _matmul_stats_kernel(x_ref, w_ref, *rest, na, inv_d):
    *r_refs, o_ref, ob_ref, sc_ref, ssq_ref = rest
    j = pl.program_id(1)
    acc = jnp.dot(x_ref[...], w_ref[...], preferred_element_type=F32)
    res = r_refs[0][...] if len(r_refs) == 1 else jnp.where(pl.program_id(0) < na, r_refs[0][...], r_refs[1][...])
    h = res + acc
    o_ref[...] = h
    ob_ref[...] = h.astype(BF16)
    hh = h * h
    part = hh[:, 0:LANES]
    for c in range(1, hh.shape[1] // LANES):
        part = part + hh[:, c * LANES:(c + 1) * LANES]

    @pl.when(j == 0)
    def _():
        ssq_ref[...] = part

    @pl.when(j > 0)
    def _():
        ssq_ref[...] += part

    @pl.when(j == pl.num_programs(1) - 1)
    def _():
        ms = jnp.sum(ssq_ref[...], axis=-1, keepdims=True) * inv_d
        sc_ref[...] = jnp.broadcast_to(lax.rsqrt(ms + NORM_EPS), sc_ref.shape)


def _matmul_stats(x, w, tm, tn, residual, name):
    m, k = x.shape
    n = w.shape[1]
    in_specs = [pl.BlockSpec((tm, k), lambda i, j: (i, 0)),
                pl.BlockSpec((k, tn), lambda i, j: (0, j))]
    na = 0
    if isinstance(residual, tuple):
        ra, rb = residual
        na = ra.shape[0] // tm
        in_specs += [pl.BlockSpec((tm, tn), lambda i, j: (jnp.minimum(i, na - 1), j)),
                     pl.BlockSpec((tm, tn), lambda i, j: (jnp.maximum(i - na, 0), j))]
        res_args = [ra, rb]
    else:
        in_specs.append(pl.BlockSpec((tm, tn), lambda i, j: (i, j)))
        res_args = [residual]
    return pl.pallas_call(
        functools.partial(_matmul_stats_kernel, na=na, inv_d=1.0 / n),
        grid=(m // tm, n // tn),
        in_specs=in_specs,
        out_specs=[pl.BlockSpec((tm, tn), lambda i, j: (i, j)),
                   pl.BlockSpec((tm, tn), lambda i, j: (i, j)),
                   pl.BlockSpec((tm, LANES), lambda i, j: (i, 0))],
        out_shape=[jax.ShapeDtypeStruct((m, n), F32), jax.ShapeDtypeStruct((m, n), BF16),
                   jax.ShapeDtypeStruct((m, LANES), F32)],
        scratch_shapes=[pltpu.VMEM((tm, LANES), F32)],
        compiler_params=_params("arbitrary", "arbitrary"),
        name=name,
    )(x, w, *res_args)


def _cast_weight(w_ref, wb_ref, g_ref):
    w = w_ref[...]
    if g_ref is not None:
        w = w * _lane_tiled(g_ref[...], w.shape[1])
    wb_ref[...] = w.astype(BF16)


def _wres_kernel(*refs, normed):
    if normed:
        x_ref, sc_ref, g_ref, w_ref, o_ref, wb_ref = refs
    else:
        x_ref, w_ref, o_ref, wb_ref = refs
        sc_ref = g_ref = None

    @pl.when(pl.program_id(1) == 0)
    def _():
        _cast_weight(w_ref, wb_ref, g_ref)

    acc = jnp.dot(x_ref[...], wb_ref[...], preferred_element_type=F32)
    if normed:
        acc = acc * _lane_tiled(sc_ref[...], acc.shape[1])
    o_ref[...] = acc.astype(o_ref.dtype)


def _matmul_wres(x, w, out_dtype, tm, tn, scale=None, gain=None, name="matmul_wres"):
    m, k = x.shape
    n = w.shape[1]
    normed = scale is not None
    in_specs = [pl.BlockSpec((tm, k), lambda j, i: (i, 0))]
    args = [x]
    if normed:
        in_specs += [pl.BlockSpec((tm, LANES), lambda j, i: (i, 0)),
                     pl.BlockSpec((k, LANES), lambda j, i: (0, 0))]
        args += [scale, gain]
    in_specs.append(pl.BlockSpec((k, tn), lambda j, i: (0, j)))
    args.append(w)
    return pl.pallas_call(
        functools.partial(_wres_kernel, normed=normed),
        grid=(n // tn, m // tm),
        in_specs=in_specs,
        out_specs=pl.BlockSpec((tm, tn), lambda j, i: (i, j)),
        out_shape=jax.ShapeDtypeStruct((m, n), out_dtype),
        scratch_shapes=[pltpu.VMEM((k, tn), BF16)],
        compiler_params=_params("arbitrary", "arbitrary"),
        name=name,
    )(*args)


def _swiglu_kernel(x_ref, sc_ref, g_ref, wg_ref, wu_ref, o_ref, wgb_ref, wub_ref):
    @pl.when(pl.program_id(1) == 0)
    def _():
        _cast_weight(wg_ref, wgb_ref, g_ref)
        _cast_weight(wu_ref, wub_ref, g_ref)

    x = x_ref[...]
    sc = _lane_tiled(sc_ref[...], o_ref.shape[1])
    gate = jnp.dot(x, wgb_ref[...], preferred_element_type=F32) * sc
    up = jnp.dot(x, wub_ref[...], preferred_element_type=F32) * sc
    o_ref[...] = (gate * jax.nn.sigmoid(gate) * up).astype(o_ref.dtype)


def _swiglu_up(x, scale, gain, w13, hidden, tm, tn):
    m, k = x.shape
    nb = hidden // tn
    return pl.pallas_call(
        _swiglu_kernel,
        grid=(nb, m // tm),
        in_specs=[pl.BlockSpec((tm, k), lambda j, i: (i, 0)),
                  pl.BlockSpec((tm, LANES), lambda j, i: (i, 0)),
                  pl.BlockSpec((k, LANES), lambda j, i: (0, 0)),
                  pl.BlockSpec((k, tn), lambda j, i: (0, j)),
                  pl.BlockSpec((k, tn), lambda j, i: (0, j + nb))],
        out_specs=pl.BlockSpec((tm, tn), lambda j, i: (i, j)),
        out_shape=jax.ShapeDtypeStruct((m, hidden), BF16),
        scratch_shapes=[pltpu.VMEM((k, tn), BF16), pltpu.VMEM((k, tn), BF16)],
        compiler_params=_params("arbitrary", "arbitrary"),
        name="swiglu_up",
    )(x, scale, gain, w13, w13)


def _head_sum_matrix():
    r = lax.broadcasted_iota(jnp.int32, (LANES, LANES), 0) >> HEAD_SHIFT
    c = lax.broadcasted_iota(jnp.int32, (LANES, LANES), 1) >> HEAD_SHIFT
    return jnp.where(r == c, 1.0, 0.0).astype(BF16)


def _split2(x):
    hi = x.astype(BF16)
    return hi, (x - hi.astype(F32)).astype(BF16)


def _gmap(f, *lists):
    return [f(*xs) for xs in zip(*lists)]


def _wkv_kernel(z_ref, lo_ref, wup_ref, aup_ref, w0_ref, a0_ref, kk_ref, ka_ref, rk_ref,
                y_ref, bo_ref, state_ref, lw_ref, cum_ref, icl_ref,
                tinv_ref, lrk_ref, rb_ref, lhs_ref, btk_ref, vb_ref, *, d_model):
    C = WKV_CHUNK
    d = pl.program_id(1)
    c = pl.program_id(2)
    sgn = 1 - 2 * d

    @pl.when(c == 0)
    def _():
        state_ref[...] = jnp.zeros_like(state_ref)

    wl = w0_ref[0] + jnp.dot(jnp.tanh(lo_ref[0, :, 0:LORA_W]).astype(BF16), wup_ref[0],
                             preferred_element_type=F32)
    lw = DECAY_SCALE * jax.nn.sigmoid(wl)
    lw_ref[...] = lw
    r64 = lax.broadcasted_iota(jnp.int32, (C, C), 0)
    c64 = lax.broadcasted_iota(jnp.int32, (C, C), 1)
    tri = jnp.where((r64 - c64) * sgn >= 0, 1.0, 0.0).astype(BF16)
    lw_hi, lw_lo = _split2(lw)
    cum_ref[...] = (jnp.dot(tri, lw_hi, preferred_element_type=F32)
                    + jnp.dot(tri, lw_lo, preferred_element_type=F32))
    icl_ref[...] = jax.nn.sigmoid(
        a0_ref[0] + jnp.dot(lo_ref[0, :, LORA_W:2 * LORA_W].astype(BF16), aup_ref[0],
                            preferred_element_type=F32))

    row = lax.broadcasted_iota(jnp.int32, (C, LANES), 0)
    col = lax.broadcasted_iota(jnp.int32, (C, LANES), 1)
    colh = col & (HEAD_SIZE - 1)
    order = (row - colh) * sgn
    strict = order > 0
    incl = order >= 0
    eye2 = row == colh
    lane_lo = col < HEAD_SIZE
    esum = _head_sum_matrix()
    rr = lax.broadcasted_iota(jnp.int32, (LANES, LANES), 0) >> HEAD_SHIFT
    cc = lax.broadcasted_iota(jnp.int32, (LANES, LANES), 1) >> HEAD_SHIFT
    blockdiag = rr == cc

    def bd(x):
        zero = jnp.zeros_like(x)
        return jnp.concatenate([jnp.where(lane_lo, x, zero), jnp.where(lane_lo, zero, x)], axis=0)

    def pmul(x, y):
        return jnp.dot(x.astype(BF16), bd(y.astype(BF16)), preferred_element_type=F32)

    nt_dims = (((1,), (1,)), ((), ()))
    tn_dims = (((0,), (0,)), ((), ()))

    def lane_tile(p, offset=0):
        return pl.ds(pl.multiple_of(offset + p * LANES, LANES), LANES)

    def head_sums(xs):
        stacked = jnp.concatenate([x.astype(BF16) for x in xs], axis=0)
        sums = jnp.dot(stacked, esum, preferred_element_type=F32)
        return [sums[j * C:(j + 1) * C] for j in range(len(xs))]

    def solve_body(g, carry):
        pairs = [g * WKV_SOLVE_GROUP + j for j in range(WKV_SOLVE_GROUP)]
        cols = [lane_tile(p) for p in pairs]
        r = [z_ref[0, :, cs] for cs in cols]
        k = [z_ref[0, :, lane_tile(p, d_model)] for p in pairs]
        v = [z_ref[0, :, lane_tile(p, 2 * d_model)] for p in pairs]
        icl = [icl_ref[:, cs] for cs in cols]
        lwp = [lw_ref[:, cs] for cs in cols]
        cum = [cum_ref[:, cs] for cs in cols]

        q = [ki * kk_ref[:, cs] for ki, cs in zip(k, cols)]
        n2 = head_sums([qi * qi for qi in q])
        kd = [ki * (1.0 + (ic - 1.0) * ka_ref[:, cs]) for ki, ic, cs in zip(k, icl, cols)]
        bsum = head_sums([ri * kdi * rk_ref[:, cs] for ri, kdi, cs in zip(r, kd, cols)])
        for cs, bs, vi in zip(cols, bsum, v):
            bo_ref[0, 0, :, cs] = (bs * vi).astype(bo_ref.dtype)

        kk = [qi * lax.rsqrt(jnp.maximum(ni, 1e-12)) for qi, ni in zip(q, n2)]
        b = _gmap(lambda x, ic: x * ic, kk, icl)
        e_out = [jnp.exp(-x) for x in cum]
        at = _gmap(lambda x, cm, lw_: (-x * jnp.exp(cm - lw_)).astype(BF16), kk, cum, lwp)
        rt = _gmap(lambda x, cm: (x * jnp.exp(cm)).astype(BF16), r, cum)
        bt = _gmap(lambda x, e: (x * e).astype(BF16), b, e_out)
        kt = _gmap(lambda x, e: (x * e).astype(BF16), kd, e_out)

        lhs = _gmap(lambda a_, r_: jnp.concatenate([a_, r_], axis=0), at, rt)
        rhs_t = _gmap(lambda b_, k_: jnp.concatenate([bd(b_), bd(k_)], axis=0), bt, kt)
        pmat = _gmap(lambda l_, r_: lax.dot_general(l_, r_, nt_dims, preferred_element_type=F32), lhs, rhs_t)
        for p, l_, b_, k_, v_ in zip(pairs, lhs, bt, kt, v):
            lhs_ref[p] = l_
            btk_ref[p] = jnp.concatenate([b_, k_], axis=0)
            vb_ref[p] = v_.astype(BF16)
        lab = [jnp.where(strict, x[:C, :LANES], 0.0).astype(BF16) for x in pmat]
        for p, x in zip(pairs, pmat):
            rb_ref[p] = jnp.where(incl, x[C:, :LANES], 0.0).astype(BF16)
            lrk_ref[p] = jnp.concatenate([jnp.where(strict, x[:C, LANES:], 0.0).astype(BF16),
                                          jnp.where(incl, x[C:, LANES:], 0.0).astype(BF16)], axis=0)

        zero_b = jnp.zeros((C, LANES), BF16)
        ident = jnp.where(eye2, 1.0, 0.0).astype(BF16)
        first = (row >> 1) == (colh >> 1)
        tinv = [ident + jnp.where(first, x, zero_b) for x in lab]
        s = 2
        while s < C:
            sh = s.bit_length() - 1
            level = ((row >> (sh + 1)) == (colh >> (sh + 1))) & ((row >> sh) != (colh >> sh))
            off = [jnp.where(level, x, zero_b) for x in lab]
            tmp = _gmap(pmul, tinv, off)
            upd_t = _gmap(pmul, tmp, tinv)
            tinv = _gmap(lambda t_, x: t_ + x.astype(BF16), tinv, upd_t)
            s *= 2
        for p, x in zip(pairs, tinv):
            tinv_ref[p] = x
        return carry

    def update_body(g, carry):
        pairs = [g * WKV_UPDATE_GROUP + j for j in range(WKV_UPDATE_GROUP)]
        cols = [lane_tile(p) for p in pairs]
        h = [state_ref[p] for p in pairs]
        vb = [vb_ref[p] for p in pairs]
        hs = [jnp.dot(lhs_ref[p], h_.astype(BF16), preferred_element_type=F32)
              for p, h_ in zip(pairs, h)]
        lrkv = [jnp.dot(lrk_ref[p], bd(v_), preferred_element_type=F32) for p, v_ in zip(pairs, vb)]
        rhs_u = _gmap(lambda h_, x: (h_[:C] + x[:C]).astype(BF16), hs, lrkv)
        u = [jnp.dot(tinv_ref[p], bd(x), preferred_element_type=F32) for p, x in zip(pairs, rhs_u)]
        ub = [x.astype(BF16) for x in u]
        rbu = [jnp.dot(rb_ref[p], bd(x), preferred_element_type=F32) for p, x in zip(pairs, ub)]
        for cs, h_, xv, xu in zip(cols, hs, lrkv, rbu):
            y_ref[0, 0, :, cs] = (h_[C:] + xv[C:] + xu).astype(y_ref.dtype)
        upd = [lax.dot_general(btk_ref[p], jnp.concatenate([u_, v_], axis=0), tn_dims,
                               preferred_element_type=F32) for p, u_, v_ in zip(pairs, ub, vb)]
        for p, cs, h_, x in zip(pairs, cols, h, upd):
            tot = jnp.sum(lw_ref[:, cs], axis=0, keepdims=True)
            decay_rows = jnp.broadcast_to(jnp.exp(tot), (LANES, LANES)).T
            state_ref[p] = decay_rows * (h_ + jnp.where(blockdiag, x, 0.0))
        return carry

    n_pairs = d_model // LANES
    lax.fori_loop(0, n_pairs // WKV_SOLVE_GROUP, solve_body, 0)
    lax.fori_loop(0, n_pairs // WKV_UPDATE_GROUP, update_body, 0)


def _wkv_scan(z, dm, w_up, a_up, w0, a0, k_k, k_a, r_k):
    bsz, t, n = z.shape
    C = WKV_CHUNK
    nc = t // C
    n_pairs = dm // LANES
    lora_block = (n - LORA_COLS) // LORA_COLS

    def tchunk(dd, cc):
        return jnp.where(dd == 0, cc, nc - 1 - cc)

    def dir_map(bb, dd, cc):
        return (dd, 0, 0)

    def const2(bb, dd, cc):
        return (0, 0)

    def out_map(bb, dd, cc):
        return (dd, bb, tchunk(dd, cc), 0)

    out_sds = jax.ShapeDtypeStruct((2, bsz, t, dm), BF16)
    kern = functools.partial(_wkv_kernel, d_model=dm)
    return pl.pallas_call(
        kern,
        grid=(bsz, 2, nc),
        in_specs=[
            pl.BlockSpec((1, C, 3 * dm), lambda bb, dd, cc: (bb, tchunk(dd, cc), 0)),
            pl.BlockSpec((1, C, LORA_COLS), lambda bb, dd, cc: (bb, tchunk(dd, cc), lora_block)),
            pl.BlockSpec((1, LORA_W, dm), dir_map),
            pl.BlockSpec((1, LORA_W, dm), dir_map),
            pl.BlockSpec((1, 1, dm), dir_map),
            pl.BlockSpec((1, 1, dm), dir_map),
            pl.BlockSpec((1, dm), const2),
            pl.BlockSpec((1, dm), const2),
            pl.BlockSpec((1, dm), const2),
        ],
        out_specs=[pl.BlockSpec((1, 1, C, dm), out_map), pl.BlockSpec((1, 1, C, dm), out_map)],
        out_shape=[out_sds, out_sds],
        scratch_shapes=[pltpu.VMEM((n_pairs, LANES, LANES), F32),
                        pltpu.VMEM((C, dm), F32), pltpu.VMEM((C, dm), F32), pltpu.VMEM((C, dm), F32),
                        pltpu.VMEM((n_pairs, C, LANES), BF16), pltpu.VMEM((n_pairs, 2 * C, LANES), BF16),
                        pltpu.VMEM((n_pairs, C, LANES), BF16), pltpu.VMEM((n_pairs, 2 * C, LANES), BF16),
                        pltpu.VMEM((n_pairs, 2 * C, LANES), BF16), pltpu.VMEM((n_pairs, C, LANES), BF16)],
        compiler_params=_params("arbitrary", "arbitrary", "arbitrary"),
        name="wkv_scan",
    )(z, z, w_up, a_up, w0, a0, k_k, k_a, r_k)


def _pool_kernel(p_ref, gate_ref, w_ref, scale_ref, o_ref, pad_ref, *, rows):
    g = pl.program_id(1)
    t, gi = p_ref.shape[1], p_ref.shape[2]
    zeros = jnp.zeros((POOL_PAD, gi), F32)
    pad_ref[0:POOL_PAD, :] = zeros
    pad_ref[POOL_PAD + t:POOL_PAD + t + POOL_PAD, :] = zeros
    pad_ref[POOL_PAD:POOL_PAD + t, :] = p_ref[0]
    w = w_ref[0]
    scale = scale_ref[...]

    for gidx, win in enumerate(POOL_WINDOWS):
        @pl.when(g == gidx)
        def _(win=win):
            half = win // 2

            def tile_body(i, carry):
                r0 = pl.multiple_of(i * rows, rows)
                n = rows + 2 * SUBLANES
                xt = pad_ref[pl.ds(r0 + POOL_PAD - SUBLANES, n), :]
                acc = xt
                step = 1
                while step < win:
                    acc = acc + pltpu.roll(acc, n - step, axis=0)
                    step *= 2
                if SUBLANES - half:
                    acc = pltpu.roll(acc, n - (SUBLANES - half), axis=0)
                acc = acc[0:rows]
                tt = r0 + lax.broadcasted_iota(jnp.int32, (rows, LANES), 0)
                cnt = (jnp.minimum(tt + (win - half), t) - jnp.maximum(tt - half, 0)).astype(F32)
                inv = 1.0 / cnt
                inv_full = jnp.concatenate([inv] * (gi // LANES), axis=1)
                dlt = acc * inv_full - xt[SUBLANES:SUBLANES + rows]
                out = jnp.dot(dlt.astype(BF16), w, preferred_element_type=F32) * scale
                gate = jax.nn.sigmoid(gate_ref[0, pl.ds(r0, rows), :])
                o_ref[0, pl.ds(r0, rows), :] = (gate * out).astype(o_ref.dtype)
                return carry

            lax.fori_loop(0, t // rows, tile_body, 0)


def _pool_branch(z, pool_col, gate_col, pool_w, pool_scale, rows=256):
    bsz, t, _ = z.shape
    ng, gi, go = pool_w.shape
    dm = ng * go
    kern = functools.partial(_pool_kernel, rows=rows)
    return pl.pallas_call(
        kern,
        grid=(bsz, ng),
        in_specs=[pl.BlockSpec((1, t, gi), lambda b, g: (b, 0, pool_col // gi + g)),
                  pl.BlockSpec((1, t, go), lambda b, g: (b, 0, gate_col // go + g)),
                  pl.BlockSpec((1, gi, go), lambda b, g: (g, 0, 0)),
                  pl.BlockSpec((1, go), lambda b, g: (0, g))],
        out_specs=pl.BlockSpec((1, t, go), lambda b, g: (b, 0, g)),
        out_shape=jax.ShapeDtypeStruct((bsz, t, dm), BF16),
        scratch_shapes=[pltpu.VMEM((t + 2 * POOL_PAD, gi), F32)],
        compiler_params=_params("arbitrary", "arbitrary"),
        name="pool_branch",
    )(z, z, pool_w, pool_scale)


def _merge_kernel(y_ref, bo_ref, lo_ref, gup_ref, gng_ref, gnb_ref, gate_ref, yb_ref, o_ref):
    gd_lo = 2 * LORA_W
    esum = _head_sum_matrix()
    inv_n = 1.0 / HEAD_SIZE
    gd = jax.nn.sigmoid(lo_ref[0, :, gd_lo:gd_lo + GATE_LORA_PAD]).astype(BF16)

    def group_body(gidx, carry):
        cols = [pl.ds(pl.multiple_of((gidx * MERGE_GROUP + j) * LANES, LANES), LANES) for j in range(MERGE_GROUP)]
        y = [y_ref[0, 0, :, cs].astype(F32) + y_ref[1, 0, :, cs].astype(F32) for cs in cols]
        mu = [jnp.dot(x.astype(BF16), esum, preferred_element_type=F32) * inv_n for x in y]
        g = [jnp.dot(gd, gup_ref[:, cs], preferred_element_type=F32) for cs in cols]
        yc = _gmap(lambda x, m: x - m, y, mu)
        var = [jnp.dot((x * x).astype(BF16), esum, preferred_element_type=F32) * inv_n for x in yc]
        for cs, x, vr, gi in zip(cols, yc, var, g):
            yn = x * lax.rsqrt(vr + GN_EPS) * gng_ref[:, cs] + gnb_ref[:, cs]
            yn = yn + bo_ref[0, 0, :, cs].astype(F32) + bo_ref[1, 0, :, cs].astype(F32)
            ya = jax.nn.sigmoid(gate_ref[0, :, cs]) * (yn * gi)
            o_ref[0, :, cs] = (ya + yb_ref[0, :, cs].astype(F32)).astype(o_ref.dtype)
        return carry

    lax.fori_loop(0, o_ref.shape[2] // (LANES * MERGE_GROUP), group_body, 0)


def _merge(y, bo, zs, zp, g_up, ln_g, ln_b, yb, tt=256):
    _, bsz, t, dm = y.shape
    n = zs.shape[-1]
    lora_block = (n - LORA_COLS) // LORA_COLS
    return pl.pallas_call(
        _merge_kernel,
        grid=(bsz, t // tt),
        in_specs=[pl.BlockSpec((2, 1, tt, dm), lambda b, i: (0, b, i, 0)),
                  pl.BlockSpec((2, 1, tt, dm), lambda b, i: (0, b, i, 0)),
                  pl.BlockSpec((1, tt, LORA_COLS), lambda b, i: (b, i, lora_block)),
                  pl.BlockSpec((GATE_LORA_PAD, dm), lambda b, i: (0, 0)),
                  pl.BlockSpec((1, dm), lambda b, i: (0, 0)),
                  pl.BlockSpec((1, dm), lambda b, i: (0, 0)),
                  pl.BlockSpec((1, tt, dm), lambda b, i: (b, i, 0)),
                  pl.BlockSpec((1, tt, dm), lambda b, i: (b, i, 0))],
        out_specs=pl.BlockSpec((1, tt, dm), lambda b, i: (b, i, 0)),
        out_shape=jax.ShapeDtypeStruct((bsz, t, dm), BF16),
        compiler_params=_params("arbitrary", "arbitrary"),
        name="wkv_merge",
    )(y, bo, zs, g_up, ln_g, ln_b, zp, yb)


def _xattn_kernel(q_ref, k_ref, v_ref, o_ref, *, head_dim):
    scale = head_dim ** -0.5
    nt_dims = (((1,), (1,)), ((), ()))
    for h in range(X_HEADS):
        cs = slice(h * head_dim, (h + 1) * head_dim)
        s = lax.dot_general(q_ref[0, :, cs], k_ref[0, :, cs], nt_dims, preferred_element_type=F32) * scale
        m = jnp.max(s, axis=-1, keepdims=True)
        e = jnp.exp(s - m)
        p = e / jnp.sum(e, axis=-1, keepdims=True)
        o_ref[0, :, cs] = jnp.dot(p.astype(BF16), v_ref[0, :, cs], preferred_element_type=F32).astype(o_ref.dtype)


def _xattn(q, k, v, tq=512):
    bsz, t, dm = q.shape
    m = k.shape[1]
    kern = functools.partial(_xattn_kernel, head_dim=dm // X_HEADS)
    return pl.pallas_call(
        kern,
        grid=(bsz, t // tq),
        in_specs=[pl.BlockSpec((1, tq, dm), lambda b, i: (b, i, 0)),
                  pl.BlockSpec((1, m, dm), lambda b, i: (b, 0, 0)),
                  pl.BlockSpec((1, m, dm), lambda b, i: (b, 0, 0))],
        out_specs=pl.BlockSpec((1, tq, dm), lambda b, i: (b, i, 0)),
        out_shape=jax.ShapeDtypeStruct((bsz, t, dm), BF16),
        compiler_params=_params("arbitrary", "arbitrary"),
        name="xattn",
    )(q, k, v)


def _pad_to(x, axis, size):
    pad = [(0, 0)] * x.ndim
    pad[axis] = (0, size - x.shape[axis])
    return jnp.pad(x, pad)


def _gain_tile(g):
    return jnp.broadcast_to(g[:, None], (g.shape[0], LANES))


def _in_proj_operands(p, dm):
    w_in, shift_w = p['w_in'], p['shift_w']
    gate_lora = p['g_up'].shape[0]
    c_rkv = 3 * dm
    c_lora = c_rkv + 2 * LORA_W + gate_lora
    pool_width = p['pool_w'].shape[0] * p['pool_w'].shape[1]
    c_pool = c_lora + pool_width
    wb = w_in.astype(BF16)
    w_shift = jnp.concatenate([wb[:, :c_rkv], _pad_to(wb[:, c_rkv:c_lora], 1, LORA_COLS)], axis=1)
    taps = jnp.concatenate([shift_w[:, :c_rkv], _pad_to(shift_w[:, c_rkv:c_lora], 1, LORA_COLS)], axis=1)
    w_plain = jnp.concatenate([wb[:, c_pool:], wb[:, c_lora:c_pool]], axis=1)
    return w_shift, taps, w_plain


def _trunk(xs, mem, lp, norm_final_g):
    t, dm = xs[0].shape[1:]
    rows = [x.shape[0] * t for x in xs]
    bsz = sum(x.shape[0] for x in xs)
    n_mem = mem.shape[1]
    m_tok = bsz * t
    hs = tuple(x.reshape(-1, dm) for x in xs)
    memf = mem.reshape(bsz * n_mem, dm)
    depth = lp['w_in'].shape[0]
    h = None
    for l in range(depth):
        p = {name: arr[l] for name, arr in lp.items()}
        w_shift, taps, w_plain = _in_proj_operands(p, dm)
        g_up = _pad_to(p['g_up'], 0, GATE_LORA_PAD).astype(BF16)

        if h is None:
            xn = _rmsnorm2(hs[0], hs[1], p['norm_mix_g'], BF16)
            res = hs
        else:
            xn = _rmsnorm(h, p['norm_mix_g'], BF16)
            res = h
        zs = _in_proj(xn, w_shift, taps, bsz, t)
        zp = _matmul(xn, w_plain, F32, PROJ_TM, PROJ_TN, name="in_plain").reshape(bsz, t, -1)

        w_up = jnp.stack([p['w_up_f'], p['w_up_b']]).astype(BF16)
        a_up = jnp.stack([p['a_up_f'], p['a_up_b']]).astype(BF16)
        w0 = jnp.stack([p['w0_f'], p['w0_b']]).reshape(2, 1, dm)
        a0 = jnp.stack([p['a0_f'], p['a0_b']]).reshape(2, 1, dm)
        y, bo = _wkv_scan(zs, dm, w_up, a_up, w0, a0,
                          p['k_k'].reshape(1, dm), p['k_a'].reshape(1, dm), p['r_k'].reshape(1, dm))
        yb = _pool_branch(zp, 2 * dm, dm, p['pool_w'].astype(BF16), p['pool_scale'].reshape(1, dm))
        merged = _merge(y, bo, zs, zp, g_up, p['ln_x_g'].reshape(1, dm), p['ln_x_b'].reshape(1, dm), yb)
        h, h_bf, scale = _matmul_stats(merged.reshape(m_tok, dm), p['w_out'].astype(BF16), PROJ_TM, PROJ_TN,
                                       residual=res, name="out_proj")

        mn = _rmsnorm(memf, p['norm_mem_g'], BF16)
        q = _matmul_wres(h_bf, p['xq'], BF16, PROJ_TM, PROJ_TN, scale=scale, gain=_gain_tile(p['norm_x_g']), name="xq")
        kx = _matmul_wres(mn, p['xk'], BF16, PROJ_TM, PROJ_TN, name="xk")
        vx = _matmul_wres(mn, p['xv'], BF16, PROJ_TM, PROJ_TN, name="xv")
        o = _xattn(q.reshape(bsz, t, dm), kx.reshape(bsz, n_mem, dm), vx.reshape(bsz, n_mem, dm))
        h, h_bf, scale = _matmul_stats(o.reshape(m_tok, dm), p['xo'].astype(BF16), PROJ_TM, PROJ_TN,
                                       residual=h, name="xo")

        hidden = p['ffn_w2'].shape[0]
        act = _swiglu_up(h_bf, scale, _gain_tile(p['norm_ffn_g']), p['ffn_w13'], hidden, FFN_TM, FFN_TN)
        w2 = p['ffn_w2'].astype(BF16)
        if l + 1 < depth:
            h = _matmul(act, w2, F32, FFN_DOWN_TM, FFN_DOWN_TN, residual=h, name="ffn_down")

    outs, off = [], 0
    for x, nrow in zip(xs, rows):
        y = _matmul_res_norm(act, w2, h, norm_final_g, FFN_DOWN_TM, FFN_DOWN_TN,
                             off // FFN_DOWN_TM, nrow // FFN_DOWN_TM, name="ffn_down_norm")
        outs.append(y.reshape(x.shape))
        off += nrow
    return tuple(outs)


def kernel(x_prompt, x_sample, mem_prompt, mem_sample, norm_mix_g, w_in, shift_w, w0_f, w_up_f, w0_b, w_up_b, a0_f, a_up_f, a0_b, a_up_b, g_up, k_k, k_a, r_k, ln_x_g, ln_x_b, pool_w, pool_scale, w_out, norm_x_g, norm_mem_g, xq, xk, xv, xo, norm_ffn_g, ffn_w13, ffn_w2, norm_final_g):
    assert x_prompt.shape[1:] == x_sample.shape[1:] and mem_prompt.shape[1:] == mem_sample.shape[1:]
    t, dm = x_prompt.shape[1:]
    hidden = ffn_w2.shape[1]
    assert t % max(WKV_CHUNK, IN_ROW_CHUNK, 256) == 0 and dm % (LANES * WKV_SOLVE_GROUP) == 0
    assert all(x.shape[0] * t % max(PROJ_TM, FFN_TM) == 0 for x in (x_prompt, x_sample))
    assert hidden % FFN_TN == 0 and dm % max(PROJ_TN, FFN_DOWN_TN) == 0
    lp = {
        'norm_mix_g': norm_mix_g, 'w_in': w_in, 'shift_w': shift_w,
        'w0_f': w0_f, 'w_up_f': w_up_f, 'w0_b': w0_b, 'w_up_b': w_up_b,
        'a0_f': a0_f, 'a_up_f': a_up_f, 'a0_b': a0_b, 'a_up_b': a_up_b,
        'g_up': g_up, 'k_k': k_k, 'k_a': k_a, 'r_k': r_k.reshape(r_k.shape[0], -1),
        'ln_x_g': ln_x_g, 'ln_x_b': ln_x_b,
        'pool_w': pool_w, 'pool_scale': pool_scale, 'w_out': w_out,
        'norm_x_g': norm_x_g, 'norm_mem_g': norm_mem_g, 'xq': xq, 'xk': xk, 'xv': xv, 'xo': xo,
        'norm_ffn_g': norm_ffn_g, 'ffn_w13': ffn_w13, 'ffn_w2': ffn_w2,
    }
    mem = jnp.concatenate([mem_prompt, mem_sample], axis=0)
    return _trunk((x_prompt, x_sample), mem, lp, norm_final_g)
```

```python
import functools
import math

import jax
import jax.numpy as jnp
from jax import lax
from jax.experimental import pallas as pl
from jax.experimental.pallas import tpu as pltpu

F32 = jnp.float32
BF16 = jnp.bfloat16

LANES = 128
SUBLANES = 8
VMEM_LIMIT_BYTES = 56 * 1024 * 1024
VMEM_LIMIT_BYTES_MAX = 60000 * 1024

HEAD_SIZE = 64
HEAD_SHIFT = 6
X_HEADS = 4
POOL_WINDOWS = (2, 4, 8, 16)
POOL_PAD = 16
GN_EPS = 64e-5
NORM_EPS = 1e-6
DECAY_SCALE = -math.exp(-0.5)
WKV_CHUNK = 64
WKV_SOLVE_GROUP = 32
WKV_UPDATE_GROUP = 32
MERGE_GROUP = 8
LORA_W = 128
GATE_LORA_PAD = 512
LORA_COLS = 1024
PROJ_TM = 1024
PROJ_TN = 512
FFN_TM = 2048
FFN_TN = 256
FFN_DOWN_TM = 512
FFN_DOWN_TN = 256
IN_TN = 512
IN_ROW_CHUNK = 512


def _params(*semantics):
    return pltpu.CompilerParams(dimension_semantics=semantics, vmem_limit_bytes=VMEM_LIMIT_BYTES)


def _rms(x, g):
    ms = jnp.mean(x * x, axis=-1, keepdims=True)
    return x * lax.rsqrt(ms + NORM_EPS) * g


def _rmsnorm_kernel(x_ref, g_ref, o_ref):
    o_ref[...] = _rms(x_ref[...], g_ref[...]).astype(o_ref.dtype)


def _rmsnorm(x, g, out_dtype, tm=256, row_block_offset=0, n_row_blocks=None):
    m, d = x.shape
    nb = m // tm if n_row_blocks is None else n_row_blocks
    return pl.pallas_call(
        _rmsnorm_kernel,
        grid=(nb,),
        in_specs=[pl.BlockSpec((tm, d), lambda i: (i + row_block_offset, 0)),
                  pl.BlockSpec((1, d), lambda i: (0, 0))],
        out_specs=pl.BlockSpec((tm, d), lambda i: (i, 0)),
        out_shape=jax.ShapeDtypeStruct((nb * tm, d), out_dtype),
        compiler_params=_params("parallel"),
        name="rmsnorm",
    )(x, g.reshape(1, d))


def _rmsnorm2_kernel(xa_ref, xb_ref, g_ref, o_ref, *, na):
    x = jnp.where(pl.program_id(0) < na, xa_ref[...], xb_ref[...])
    o_ref[...] = _rms(x, g_ref[...]).astype(o_ref.dtype)


def _rmsnorm2(xa, xb, g, out_dtype, tm=256):
    d = xa.shape[1]
    na, nb = xa.shape[0] // tm, xb.shape[0] // tm
    return pl.pallas_call(
        functools.partial(_rmsnorm2_kernel, na=na),
        grid=(na + nb,),
        in_specs=[pl.BlockSpec((tm, d), lambda i: (jnp.minimum(i, na - 1), 0)),
                  pl.BlockSpec((tm, d), lambda i: (jnp.maximum(i - na, 0), 0)),
                  pl.BlockSpec((1, d), lambda i: (0, 0))],
        out_specs=pl.BlockSpec((tm, d), lambda i: (i, 0)),
        out_shape=jax.ShapeDtypeStruct(((na + nb) * tm, d), out_dtype),
        compiler_params=_params("arbitrary"),
        name="rmsnorm2",
    )(xa, xb, g.reshape(1, d))


def _matmul_kernel(x_ref, w_ref, o_ref):
    o_ref[...] = jnp.dot(x_ref[...], w_ref[...], preferred_element_type=F32).astype(o_ref.dtype)


def _matmul_res_kernel(x_ref, w_ref, r_ref, o_ref):
    acc = jnp.dot(x_ref[...], w_ref[...], preferred_element_type=F32)
    o_ref[...] = (r_ref[...] + acc).astype(o_ref.dtype)


def _matmul(x, w, out_dtype, tm, tn, residual=None, name="matmul"):
    m, k = x.shape
    n = w.shape[1]
    in_specs = [pl.BlockSpec((tm, k), lambda i, j: (i, 0)),
                pl.BlockSpec((k, tn), lambda i, j: (0, j))]
    args = [x, w]
    body = _matmul_kernel
    if residual is not None:
        in_specs.append(pl.BlockSpec((tm, tn), lambda i, j: (i, j)))
        args.append(residual)
        body = _matmul_res_kernel
    return pl.pallas_call(
        body,
        grid=(m // tm, n // tn),
        in_specs=in_specs,
        out_specs=pl.BlockSpec((tm, tn), lambda i, j: (i, j)),
        out_shape=jax.ShapeDtypeStruct((m, n), out_dtype),
        compiler_params=_params("arbitrary", "arbitrary"),
        name=name,
    )(*args)


def _matmul_res_norm_kernel(x_ref, w_ref, r_ref, g_ref, o_ref):
    j = pl.program_id(1)
    tn = w_ref.shape[1]
    acc = jnp.dot(x_ref[...], w_ref[...], preferred_element_type=F32)
    o_ref[:, pl.ds(pl.multiple_of(j * tn, tn), tn)] = r_ref[...] + acc

    @pl.when(j == pl.num_programs(1) - 1)
    def _():
        o_ref[...] = _rms(o_ref[...], g_ref[...])


def _matmul_res_norm(x, w, residual, g, tm, tn, row_block_offset, n_row_blocks, name):
    k = x.shape[1]
    n = w.shape[1]
    off = row_block_offset
    return pl.pallas_call(
        _matmul_res_norm_kernel,
        grid=(n_row_blocks, n // tn),
        in_specs=[pl.BlockSpec((tm, k), lambda i, j: (i + off, 0)),
                  pl.BlockSpec((k, tn), lambda i, j: (0, j)),
                  pl.BlockSpec((tm, tn), lambda i, j: (i + off, j)),
                  pl.BlockSpec((1, n), lambda i, j: (0, 0))],
        out_specs=pl.BlockSpec((tm, n), lambda i, j: (i, 0)),
        out_shape=jax.ShapeDtypeStruct((n_row_blocks * tm, n), F32),
        compiler_params=pltpu.CompilerParams(dimension_semantics=("arbitrary", "arbitrary"),
                                             vmem_limit_bytes=VMEM_LIMIT_BYTES_MAX),
        name=name,
    )(x, w, residual, g.reshape(1, n))


def _in_proj_kernel(x_ref, w_ref, taps_ref, o_ref):
    t = x_ref.shape[0]
    rc = IN_ROW_CHUNK
    w = w_ref[...]
    taps = taps_ref[...]
    ridx = lax.broadcasted_iota(jnp.int32, (rc, w.shape[1]), 0)
    zs = [jnp.dot(x_ref[c * rc:(c + 1) * rc, :], w, preferred_element_type=F32) for c in range(t // rc)]
    zero_row = jnp.zeros((1, w.shape[1]), F32)
    for c, z in enumerate(zs):
        prev_row = zs[c - 1][rc - 1:rc, :] if c > 0 else zero_row
        next_row = zs[c + 1][0:1, :] if c + 1 < len(zs) else zero_row
        zm1 = jnp.where(ridx == 0, prev_row, pltpu.roll(z, 1, axis=0))
        zp1 = jnp.where(ridx == rc - 1, next_row, pltpu.roll(z, rc - 1, axis=0))
        o_ref[0, c * rc:(c + 1) * rc, :] = zm1 * taps[0:1, :] + z * taps[1:2, :] + zp1 * taps[2:3, :]


def _in_proj(xn, w, taps, bsz, t):
    k = xn.shape[1]
    n = w.shape[1]
    return pl.pallas_call(
        _in_proj_kernel,
        grid=(bsz, n // IN_TN),
        in_specs=[pl.BlockSpec((t, k), lambda b, j: (b, 0), pipeline_mode=pl.Buffered(1)),
                  pl.BlockSpec((k, IN_TN), lambda b, j: (0, j)),
                  pl.BlockSpec((3, IN_TN), lambda b, j: (0, j))],
        out_specs=pl.BlockSpec((1, t, IN_TN), lambda b, j: (b, 0, j)),
        out_shape=jax.ShapeDtypeStruct((bsz, t, n), F32),
        compiler_params=_params("arbitrary", "arbitrary"),
        name="in_proj",
    )(xn, w, taps)


def _lane_tiled(x, width):
    return jnp.concatenate([x] * (width // LANES), axis=1)


def _matmul_stats_kernel(x_ref, w_ref, *rest, na, inv_d):
    *r_refs, o_ref, ob_ref, sc_ref, ssq_ref = rest
    j = pl.program_id(1)
    acc = jnp.dot(x_ref[...], w_ref[...], preferred_element_type=F32)
    res = r_refs[0][...] if len(r_refs) == 1 else jnp.where(pl.program_id(0) < na, r_refs[0][...], r_refs[1][...])
    h = res + acc
    o_ref[...] = h
    ob_ref[...] = h.astype(BF16)
    hh = h * h
    part = hh[:, 0:LANES]
    for c in range(1, hh.shape[1] // LANES):
        part = part + hh[:, c * LANES:(c + 1) * LANES]

    @pl.when(j == 0)
    def _():
        ssq_ref[...] = part

    @pl.when(j > 0)
    def _():
        ssq_ref[...] += part

    @pl.when(j == pl.num_programs(1) - 1)
    def _():
        ms = jnp.sum(ssq_ref[...], axis=-1, keepdims=True) * inv_d
        sc_ref[...] = jnp.broadcast_to(lax.rsqrt(ms + NORM_EPS), sc_ref.shape)


def _matmul_stats(x, w, tm, tn, residual, name):
    m, k = x.shape
    n = w.shape[1]
    in_specs = [pl.BlockSpec((tm, k), lambda i, j: (i, 0)),
                pl.BlockSpec((k, tn), lambda i, j: (0, j))]
    na = 0
    if isinstance(residual, tuple):
        ra, rb = residual
        na = ra.shape[0] // tm
        in_specs += [pl.BlockSpec((tm, tn), lambda i, j: (jnp.minimum(i, na - 1), j)),
                     pl.BlockSpec((tm, tn), lambda i, j: (jnp.maximum(i - na, 0), j))]
        res_args = [ra, rb]
    else:
        in_specs.append(pl.BlockSpec((tm, tn), lambda i, j: (i, j)))
        res_args = [residual]
    return pl.pallas_call(
        functools.partial(_matmul_stats_kernel, na=na, inv_d=1.0 / n),
        grid=(m // tm, n // tn),
        in_specs=in_specs,
        out_specs=[pl.BlockSpec((tm, tn), lambda i, j: (i, j)),
                   pl.BlockSpec((tm, tn), lambda i, j: (i, j)),
                   pl.BlockSpec((tm, LANES), lambda i, j: (i, 0))],
        out_shape=[jax.ShapeDtypeStruct((m, n), F32), jax.ShapeDtypeStruct((m, n), BF16),
                   jax.ShapeDtypeStruct((m, LANES), F32)],
        scratch_shapes=[pltpu.VMEM((tm, LANES), F32)],
        compiler_params=_params("arbitrary", "arbitrary"),
        name=name,
    )(x, w, *res_args)


def _cast_weight(w_ref, wb_ref, g_ref):
    w = w_ref[...]
    if g_ref is not None:
        w = w * _lane_tiled(g_ref[...], w.shape[1])
    wb_ref[...] = w.astype(BF16)


def _wres_kernel(*refs, normed):
    if normed:
        x_ref, sc_ref, g_ref, w_ref, o_ref, wb_ref = refs
    else:
        x_ref, w_ref, o_ref, wb_ref = refs
        sc_ref = g_ref = None

    @pl.when(pl.program_id(1) == 0)
    def _():
        _cast_weight(w_ref, wb_ref, g_ref)

    acc = jnp.dot(x_ref[...], wb_ref[...], preferred_element_type=F32)
    if normed:
        acc = acc * _lane_tiled(sc_ref[...], acc.shape[1])
    o_ref[...] = acc.astype(o_ref.dtype)


def _matmul_wres(x, w, out_dtype, tm, tn, scale=None, gain=None, name="matmul_wres"):
    m, k = x.shape
    n = w.shape[1]
    normed = scale is not None
    in_specs = [pl.BlockSpec((tm, k), lambda j, i: (i, 0))]
    args = [x]
    if normed:
        in_specs += [pl.BlockSpec((tm, LANES), lambda j, i: (i, 0)),
                     pl.BlockSpec((k, LANES), lambda j, i: (0, 0))]
        args += [scale, gain]
    in_specs.append(pl.BlockSpec((k, tn), lambda j, i: (0, j)))
    args.append(w)
    return pl.pallas_call(
        functools.partial(_wres_kernel, normed=normed),
        grid=(n // tn, m // tm),
        in_specs=in_specs,
        out_specs=pl.BlockSpec((tm, tn), lambda j, i: (i, j)),
        out_shape=jax.ShapeDtypeStruct((m, n), out_dtype),
        scratch_shapes=[pltpu.VMEM((k, tn), BF16)],
        compiler_params=_params("arbitrary", "arbitrary"),
        name=name,
    )(*args)


def _swiglu_kernel(x_ref, sc_ref, wg_ref, wu_ref, o_ref):
    x = x_ref[...]
    sc = _lane_tiled(sc_ref[...], o_ref.shape[1])
    gate = jnp.dot(x, wg_ref[...], preferred_element_type=F32) * sc
    up = jnp.dot(x, wu_ref[...], preferred_element_type=F32) * sc
    o_ref[...] = (gate * jax.nn.sigmoid(gate) * up).astype(o_ref.dtype)


def _swiglu_up(x, scale, w13, hidden, tm, tn):
    m, k = x.shape
    nb = hidden // tn
    return pl.pallas_call(
        _swiglu_kernel,
        grid=(m // tm, nb),
        in_specs=[pl.BlockSpec((tm, k), lambda i, j: (i, 0)),
                  pl.BlockSpec((tm, LANES), lambda i, j: (i, 0)),
                  pl.BlockSpec((k, tn), lambda i, j: (0, j)),
                  pl.BlockSpec((k, tn), lambda i, j: (0, j + nb))],
        out_specs=pl.BlockSpec((tm, tn), lambda i, j: (i, j)),
        out_shape=jax.ShapeDtypeStruct((m, hidden), BF16),
        compiler_params=_params("arbitrary", "arbitrary"),
        name="swiglu_up",
    )(x, scale, w13, w13)


def _head_sum_matrix():
    r = lax.broadcasted_iota(jnp.int32, (LANES, LANES), 0) >> HEAD_SHIFT
    c = lax.broadcasted_iota(jnp.int32, (LANES, LANES), 1) >> HEAD_SHIFT
    return jnp.where(r == c, 1.0, 0.0).astype(BF16)


def _split2(x):
    hi = x.astype(BF16)
    return hi, (x - hi.astype(F32)).astype(BF16)


def _gmap(f, *lists):
    return [f(*xs) for xs in zip(*lists)]


def _wkv_kernel(z_ref, lo_ref, wup_ref, aup_ref, w0_ref, a0_ref, kk_ref, ka_ref, rk_ref,
                y_ref, bo_ref, state_ref, lw_ref, cum_ref, icl_ref,
                tinv_ref, lrk_ref, rb_ref, lhs_ref, btk_ref, vb_ref, *, d_model):
    C = WKV_CHUNK
    d = pl.program_id(1)
    c = pl.program_id(2)
    sgn = 1 - 2 * d

    @pl.when(c == 0)
    def _():
        state_ref[...] = jnp.zeros_like(state_ref)

    wl = w0_ref[0] + jnp.dot(jnp.tanh(lo_ref[0, :, 0:LORA_W]).astype(BF16), wup_ref[0],
                             preferred_element_type=F32)
    lw = DECAY_SCALE * jax.nn.sigmoid(wl)
    lw_ref[...] = lw
    r2c = lax.broadcasted_iota(jnp.int32, (C, 2 * C), 0)
    c2c = lax.broadcasted_iota(jnp.int32, (C, 2 * C), 1) & (C - 1)
    tri2 = jnp.where((r2c - c2c) * sgn >= 0, 1.0, 0.0).astype(BF16)
    lw_hi, lw_lo = _split2(lw)
    cum_ref[...] = jnp.dot(tri2, jnp.concatenate([lw_hi, lw_lo], axis=0), preferred_element_type=F32)
    icl_ref[...] = jax.nn.sigmoid(
        a0_ref[0] + jnp.dot(lo_ref[0, :, LORA_W:2 * LORA_W].astype(BF16), aup_ref[0],
                            preferred_element_type=F32))

    row = lax.broadcasted_iota(jnp.int32, (C, LANES), 0)
    col = lax.broadcasted_iota(jnp.int32, (C, LANES), 1)
    colh = col & (HEAD_SIZE - 1)
    order = (row - colh) * sgn
    strict = order > 0
    incl = order >= 0
    eye2 = row == colh
    lane_lo = col < HEAD_SIZE
    esum = _head_sum_matrix()
    rr = lax.broadcasted_iota(jnp.int32, (LANES, LANES), 0) >> HEAD_SHIFT
    cc = lax.broadcasted_iota(jnp.int32, (LANES, LANES), 1) >> HEAD_SHIFT
    blockdiag = rr == cc

    def bd(x):
        zero = jnp.zeros_like(x)
        return jnp.concatenate([jnp.where(lane_lo, x, zero), jnp.where(lane_lo, zero, x)], axis=0)

    def pmul(x, y):
        return jnp.dot(x.astype(BF16), bd(y.astype(BF16)), preferred_element_type=F32)

    nt_dims = (((1,), (1,)), ((), ()))
    tn_dims = (((0,), (0,)), ((), ()))

    def lane_tile(p, offset=0):
        return pl.ds(pl.multiple_of(offset + p * LANES, LANES), LANES)

    def head_sums(xs):
        stacked = jnp.concatenate([x.astype(BF16) for x in xs], axis=0)
        sums = jnp.dot(stacked, esum, preferred_element_type=F32)
        return [sums[j * C:(j + 1) * C] for j in range(len(xs))]

    def solve_body(g, carry):
        pairs = [g * WKV_SOLVE_GROUP + j for j in range(WKV_SOLVE_GROUP)]
        cols = [lane_tile(p) for p in pairs]
        r = [z_ref[0, :, cs] for cs in cols]
        k = [z_ref[0, :, lane_tile(p, d_model)] for p in pairs]
        v = [z_ref[0, :, lane_tile(p, 2 * d_model)] for p in pairs]
        icl = [icl_ref[:, cs] for cs in cols]
        lwp = [lw_ref[:, cs] for cs in cols]
        cum = [cum_ref[:, cs] for cs in cols]

        q = [ki * kk_ref[:, cs] for ki, cs in zip(k, cols)]
        n2 = head_sums([qi * qi for qi in q])
        kd = [ki * (1.0 + (ic - 1.0) * ka_ref[:, cs]) for ki, ic, cs in zip(k, icl, cols)]
        bsum = head_sums([ri * kdi * rk_ref[:, cs] for ri, kdi, cs in zip(r, kd, cols)])
        for cs, bs, vi in zip(cols, bsum, v):
            bo_ref[0, 0, :, cs] = (bs * vi).astype(bo_ref.dtype)

        kk = [qi * lax.rsqrt(jnp.maximum(ni, 1e-12)) for qi, ni in zip(q, n2)]
        b = _gmap(lambda x, ic: x * ic, kk, icl)
        e_out = [jnp.exp(-x) for x in cum]
        at = _gmap(lambda x, cm, lw_: (-x * jnp.exp(cm - lw_)).astype(BF16), kk, cum, lwp)
        rt = _gmap(lambda x, cm: (x * jnp.exp(cm)).astype(BF16), r, cum)
        bt = _gmap(lambda x, e: (x * e).astype(BF16), b, e_out)
        kt = _gmap(lambda x, e: (x * e).astype(BF16), kd, e_out)

        lhs = _gmap(lambda a_, r_: jnp.concatenate([a_, r_], axis=0), at, rt)
        rhs_t = _gmap(lambda b_, k_: jnp.concatenate([bd(b_), bd(k_)], axis=0), bt, kt)
        pmat = _gmap(lambda l_, r_: lax.dot_general(l_, r_, nt_dims, preferred_element_type=F32), lhs, rhs_t)
        for p, l_, b_, k_, v_ in zip(pairs, lhs, bt, kt, v):
            lhs_ref[p] = l_
            btk_ref[p] = jnp.concatenate([b_, k_], axis=0)
            vb_ref[p] = v_.astype(BF16)
        lab = [jnp.where(strict, x[:C, :LANES], 0.0).astype(BF16) for x in pmat]
        for p, x in zip(pairs, pmat):
            rb_ref[p] = jnp.where(incl, x[C:, :LANES], 0.0).astype(BF16)
            lrk_ref[p] = jnp.concatenate([jnp.where(strict, x[:C, LANES:], 0.0).astype(BF16),
                                          jnp.where(incl, x[C:, LANES:], 0.0).astype(BF16)], axis=0)

        zero_b = jnp.zeros((C, LANES), BF16)
        ident = jnp.where(eye2, 1.0, 0.0).astype(BF16)
        first = (row >> 1) == (colh >> 1)
        tinv = [ident + jnp.where(first, x, zero_b) for x in lab]
        s = 2
        while s < C:
            sh = s.bit_length() - 1
            level = ((row >> (sh + 1)) == (colh >> (sh + 1))) & ((row >> sh) != (colh >> sh))
            off = [jnp.where(level, x, zero_b) for x in lab]
            tmp = _gmap(pmul, tinv, off)
            upd_t = _gmap(pmul, tmp, tinv)
            tinv = _gmap(lambda t_, x: t_ + x.astype(BF16), tinv, upd_t)
            s *= 2
        for p, x in zip(pairs, tinv):
            tinv_ref[p] = x
        return carry

    def update_body(g, carry):
        pairs = [g * WKV_UPDATE_GROUP + j for j in range(WKV_UPDATE_GROUP)]
        cols = [lane_tile(p) for p in pairs]
        h = [state_ref[p] for p in pairs]
        vb = [vb_ref[p] for p in pairs]
        hs = [jnp.dot(lhs_ref[p], h_.astype(BF16), preferred_element_type=F32)
              for p, h_ in zip(pairs, h)]
        lrkv = [jnp.dot(lrk_ref[p], bd(v_), preferred_element_type=F32) for p, v_ in zip(pairs, vb)]
        rhs_u = _gmap(lambda h_, x: (h_[:C] + x[:C]).astype(BF16), hs, lrkv)
        u = [jnp.dot(tinv_ref[p], bd(x), preferred_element_type=F32) for p, x in zip(pairs, rhs_u)]
        ub = [x.astype(BF16) for x in u]
        rbu = [jnp.dot(rb_ref[p], bd(x), preferred_element_type=F32) for p, x in zip(pairs, ub)]
        for cs, h_, xv, xu in zip(cols, hs, lrkv, rbu):
            y_ref[0, 0, :, cs] = (h_[C:] + xv[C:] + xu).astype(y_ref.dtype)
        upd = [lax.dot_general(btk_ref[p], jnp.concatenate([u_, v_], axis=0), tn_dims,
                               preferred_element_type=F32) for p, u_, v_ in zip(pairs, ub, vb)]
        for p, cs, h_, x in zip(pairs, cols, h, upd):
            tot = jnp.sum(lw_ref[:, cs], axis=0, keepdims=True)
            decay_rows = jnp.broadcast_to(jnp.exp(tot), (LANES, LANES)).T
            state_ref[p] = decay_rows * (h_ + jnp.where(blockdiag, x, 0.0))
        return carry

    n_pairs = d_model // LANES
    lax.fori_loop(0, n_pairs // WKV_SOLVE_GROUP, solve_body, 0)
    lax.fori_loop(0, n_pairs // WKV_UPDATE_GROUP, update_body, 0)


def _wkv_scan(z, dm, w_up, a_up, w0, a0, k_k, k_a, r_k):
    bsz, t, n = z.shape
    C = WKV_CHUNK
    nc = t // C
    n_pairs = dm // LANES
    lora_block = (n - LORA_COLS) // LORA_COLS

    def tchunk(dd, cc):
        return jnp.where(dd == 0, cc, nc - 1 - cc)

    def dir_map(bb, dd, cc):
        return (dd, 0, 0)

    def const2(bb, dd, cc):
        return (0, 0)

    def out_map(bb, dd, cc):
        return (dd, bb, tchunk(dd, cc), 0)

    out_sds = jax.ShapeDtypeStruct((2, bsz, t, dm), BF16)
    kern = functools.partial(_wkv_kernel, d_model=dm)
    return pl.pallas_call(
        kern,
        grid=(bsz, 2, nc),
        in_specs=[
            pl.BlockSpec((1, C, 3 * dm), lambda bb, dd, cc: (bb, tchunk(dd, cc), 0)),
            pl.BlockSpec((1, C, LORA_COLS), lambda bb, dd, cc: (bb, tchunk(dd, cc), lora_block)),
            pl.BlockSpec((1, LORA_W, dm), dir_map),
            pl.BlockSpec((1, LORA_W, dm), dir_map),
            pl.BlockSpec((1, 1, dm), dir_map),
            pl.BlockSpec((1, 1, dm), dir_map),
            pl.BlockSpec((1, dm), const2),
            pl.BlockSpec((1, dm), const2),
            pl.BlockSpec((1, dm), const2),
        ],
        out_specs=[pl.BlockSpec((1, 1, C, dm), out_map), pl.BlockSpec((1, 1, C, dm), out_map)],
        out_shape=[out_sds, out_sds],
        scratch_shapes=[pltpu.VMEM((n_pairs, LANES, LANES), F32),
                        pltpu.VMEM((C, dm), F32), pltpu.VMEM((C, dm), F32), pltpu.VMEM((C, dm), F32),
                        pltpu.VMEM((n_pairs, C, LANES), BF16), pltpu.VMEM((n_pairs, 2 * C, LANES), BF16),
                        pltpu.VMEM((n_pairs, C, LANES), BF16), pltpu.VMEM((n_pairs, 2 * C, LANES), BF16),
                        pltpu.VMEM((n_pairs, 2 * C, LANES), BF16), pltpu.VMEM((n_pairs, C, LANES), BF16)],
        compiler_params=_params("arbitrary", "arbitrary", "arbitrary"),
        name="wkv_scan",
    )(z, z, w_up, a_up, w0, a0, k_k, k_a, r_k)


def _pool_kernel(p_ref, gate_ref, w_ref, scale_ref, o_ref, pad_ref, *, rows):
    g = pl.program_id(1)
    t, gi = p_ref.shape[1], p_ref.shape[2]
    zeros = jnp.zeros((POOL_PAD, gi), F32)
    pad_ref[0:POOL_PAD, :] = zeros
    pad_ref[POOL_PAD + t:POOL_PAD + t + POOL_PAD, :] = zeros
    pad_ref[POOL_PAD:POOL_PAD + t, :] = p_ref[0]
    w = w_ref[0]
    scale = scale_ref[...]

    for gidx, win in enumerate(POOL_WINDOWS):
        @pl.when(g == gidx)
        def _(win=win):
            half = win // 2

            def tile_body(i, carry):
                r0 = pl.multiple_of(i * rows, rows)
                n = rows + 2 * SUBLANES
                xt = pad_ref[pl.ds(r0 + POOL_PAD - SUBLANES, n), :]
                acc = xt
                step = 1
                while step < win:
                    acc = acc + pltpu.roll(acc, n - step, axis=0)
                    step *= 2
                if SUBLANES - half:
                    acc = pltpu.roll(acc, n - (SUBLANES - half), axis=0)
                acc = acc[0:rows]
                tt = r0 + lax.broadcasted_iota(jnp.int32, (rows, LANES), 0)
                cnt = (jnp.minimum(tt + (win - half), t) - jnp.maximum(tt - half, 0)).astype(F32)
                inv = 1.0 / cnt
                inv_full = jnp.concatenate([inv] * (gi // LANES), axis=1)
                dlt = acc * inv_full - xt[SUBLANES:SUBLANES + rows]
                out = jnp.dot(dlt.astype(BF16), w, preferred_element_type=F32) * scale
                gate = jax.nn.sigmoid(gate_ref[0, pl.ds(r0, rows), :])
                o_ref[0, pl.ds(r0, rows), :] = (gate * out).astype(o_ref.dtype)
                return carry

            lax.fori_loop(0, t // rows, tile_body, 0)


def _pool_branch(z, pool_col, gate_col, pool_w, pool_scale, rows=256):
    bsz, t, _ = z.shape
    ng, gi, go = pool_w.shape
    dm = ng * go
    kern = functools.partial(_pool_kernel, rows=rows)
    return pl.pallas_call(
        kern,
        grid=(bsz, ng),
        in_specs=[pl.BlockSpec((1, t, gi), lambda b, g: (b, 0, pool_col // gi + g)),
                  pl.BlockSpec((1, t, go), lambda b, g: (b, 0, gate_col // go + g)),
                  pl.BlockSpec((1, gi, go), lambda b, g: (g, 0, 0)),
                  pl.BlockSpec((1, go), lambda b, g: (0, g))],
        out_specs=pl.BlockSpec((1, t, go), lambda b, g: (b, 0, g)),
        out_shape=jax.ShapeDtypeStruct((bsz, t, dm), BF16),
        scratch_shapes=[pltpu.VMEM((t + 2 * POOL_PAD, gi), F32)],
        compiler_params=_params("arbitrary", "arbitrary"),
        name="pool_branch",
    )(z, z, pool_w, pool_scale)


def _merge_kernel(y_ref, bo_ref, lo_ref, gup_ref, gng_ref, gnb_ref, gate_ref, yb_ref, o_ref):
    gd_lo = 2 * LORA_W
    esum = _head_sum_matrix()
    inv_n = 1.0 / HEAD_SIZE
    gd = jax.nn.sigmoid(lo_ref[0, :, gd_lo:gd_lo + GATE_LORA_PAD]).astype(BF16)

    def group_body(gidx, carry):
        cols = [pl.ds(pl.multiple_of((gidx * MERGE_GROUP + j) * LANES, LANES), LANES) for j in range(MERGE_GROUP)]
        y = [y_ref[0, 0, :, cs].astype(F32) + y_ref[1, 0, :, cs].astype(F32) for cs in cols]
        mu = [jnp.dot(x.astype(BF16), esum, preferred_element_type=F32) * inv_n for x in y]
        g = [jnp.dot(gd, gup_ref[:, cs], preferred_element_type=F32) for cs in cols]
        yc = _gmap(lambda x, m: x - m, y, mu)
        var = [jnp.dot((x * x).astype(BF16), esum, preferred_element_type=F32) * inv_n for x in yc]
        for cs, x, vr, gi in zip(cols, yc, var, g):
            yn = x * lax.rsqrt(vr + GN_EPS) * gng_ref[:, cs] + gnb_ref[:, cs]
            yn = yn + bo_ref[0, 0, :, cs].astype(F32) + bo_ref[1, 0, :, cs].astype(F32)
            ya = jax.nn.sigmoid(gate_ref[0, :, cs]) * (yn * gi)
            o_ref[0, :, cs] = (ya + yb_ref[0, :, cs].astype(F32)).astype(o_ref.dtype)
        return carry

    lax.fori_loop(0, o_ref.shape[2] // (LANES * MERGE_GROUP), group_body, 0)


def _merge(y, bo, zs, zp, g_up, ln_g, ln_b, yb, tt=256):
    _, bsz, t, dm = y.shape
    n = zs.shape[-1]
    lora_block = (n - LORA_COLS) // LORA_COLS
    return pl.pallas_call(
        _merge_kernel,
        grid=(bsz, t // tt),
        in_specs=[pl.BlockSpec((2, 1, tt, dm), lambda b, i: (0, b, i, 0)),
                  pl.BlockSpec((2, 1, tt, dm), lambda b, i: (0, b, i, 0)),
                  pl.BlockSpec((1, tt, LORA_COLS), lambda b, i: (b, i, lora_block)),
                  pl.BlockSpec((GATE_LORA_PAD, dm), lambda b, i: (0, 0)),
                  pl.BlockSpec((1, dm), lambda b, i: (0, 0)),
                  pl.BlockSpec((1, dm), lambda b, i: (0, 0)),
                  pl.BlockSpec((1, tt, dm), lambda b, i: (b, i, 0)),
                  pl.BlockSpec((1, tt, dm), lambda b, i: (b, i, 0))],
        out_specs=pl.BlockSpec((1, tt, dm), lambda b, i: (b, i, 0)),
        out_shape=jax.ShapeDtypeStruct((bsz, t, dm), BF16),
        compiler_params=_params("arbitrary", "arbitrary"),
        name="wkv_merge",
    )(y, bo, zs, g_up, ln_g, ln_b, zp, yb)


def _xattn_kernel(q_ref, k_ref, v_ref, o_ref, *, head_dim):
    scale = head_dim ** -0.5
    nt_dims = (((1,), (1,)), ((), ()))
    for h in range(X_HEADS):
        cs = slice(h * head_dim, (h + 1) * head_dim)
        s = lax.dot_general(q_ref[0, :, cs], k_ref[0, :, cs], nt_dims, preferred_element_type=F32) * scale
        m = jnp.max(s, axis=-1, keepdims=True)
        e = jnp.exp(s - m)
        p = e / jnp.sum(e, axis=-1, keepdims=True)
        o_ref[0, :, cs] = jnp.dot(p.astype(BF16), v_ref[0, :, cs], preferred_element_type=F32).astype(o_ref.dtype)


def _xattn(q, k, v, tq=512):
    bsz, t, dm = q.shape
    m = k.shape[1]
    kern = functools.partial(_xattn_kernel, head_dim=dm // X_HEADS)
    return pl.pallas_call(
        kern,
        grid=(bsz, t // tq),
        in_specs=[pl.BlockSpec((1, tq, dm), lambda b, i: (b, i, 0)),
                  pl.BlockSpec((1, m, dm), lambda b, i: (b, 0, 0)),
                  pl.BlockSpec((1, m, dm), lambda b, i: (b, 0, 0))],
        out_specs=pl.BlockSpec((1, tq, dm), lambda b, i: (b, i, 0)),
        out_shape=jax.ShapeDtypeStruct((bsz, t, dm), BF16),
        compiler_params=_params("arbitrary", "arbitrary"),
        name="xattn",
    )(q, k, v)


def _pad_to(x, axis, size):
    pad = [(0, 0)] * x.ndim
    pad[axis] = (0, size - x.shape[axis])
    return jnp.pad(x, pad)


def _gain_tile(g):
    return jnp.broadcast_to(g[:, None], (g.shape[0], LANES))


def _in_proj_operands(p, dm):
    w_in, shift_w = p['w_in'], p['shift_w']
    gate_lora = p['g_up'].shape[0]
    c_rkv = 3 * dm
    c_lora = c_rkv + 2 * LORA_W + gate_lora
    pool_width = p['pool_w'].shape[0] * p['pool_w'].shape[1]
    c_pool = c_lora + pool_width
    wb = w_in.astype(BF16)
    w_shift = jnp.concatenate([wb[:, :c_rkv], _pad_to(wb[:, c_rkv:c_lora], 1, LORA_COLS)], axis=1)
    taps = jnp.concatenate([shift_w[:, :c_rkv], _pad_to(shift_w[:, c_rkv:c_lora], 1, LORA_COLS)], axis=1)
    w_plain = jnp.concatenate([wb[:, c_pool:], wb[:, c_lora:c_pool]], axis=1)
    return w_shift, taps, w_plain


def _trunk(xs, mem, lp, norm_final_g):
    t, dm = xs[0].shape[1:]
    rows = [x.shape[0] * t for x in xs]
    bsz = sum(x.shape[0] for x in xs)
    n_mem = mem.shape[1]
    m_tok = bsz * t
    hs = tuple(x.reshape(-1, dm) for x in xs)
    memf = mem.reshape(bsz * n_mem, dm)
    depth = lp['w_in'].shape[0]
    h = None
    for l in range(depth):
        p = {name: arr[l] for name, arr in lp.items()}
        w_shift, taps, w_plain = _in_proj_operands(p, dm)
        g_up = _pad_to(p['g_up'], 0, GATE_LORA_PAD).astype(BF16)

        if h is None:
            xn = _rmsnorm2(hs[0], hs[1], p['norm_mix_g'], BF16)
            res = hs
        else:
            xn = _rmsnorm(h, p['norm_mix_g'], BF16)
            res = h
        zs = _in_proj(xn, w_shift, taps, bsz, t)
        zp = _matmul(xn, w_plain, F32, PROJ_TM, PROJ_TN, name="in_plain").reshape(bsz, t, -1)

        w_up = jnp.stack([p['w_up_f'], p['w_up_b']]).astype(BF16)
        a_up = jnp.stack([p['a_up_f'], p['a_up_b']]).astype(BF16)
        w0 = jnp.stack([p['w0_f'], p['w0_b']]).reshape(2, 1, dm)
        a0 = jnp.stack([p['a0_f'], p['a0_b']]).reshape(2, 1, dm)
        y, bo = _wkv_scan(zs, dm, w_up, a_up, w0, a0,
                          p['k_k'].reshape(1, dm), p['k_a'].reshape(1, dm), p['r_k'].reshape(1, dm))
        yb = _pool_branch(zp, 2 * dm, dm, p['pool_w'].astype(BF16), p['pool_scale'].reshape(1, dm))
        merged = _merge(y, bo, zs, zp, g_up, p['ln_x_g'].reshape(1, dm), p['ln_x_b'].reshape(1, dm), yb)
        h, h_bf, scale = _matmul_stats(merged.reshape(m_tok, dm), p['w_out'].astype(BF16), PROJ_TM, PROJ_TN,
                                       residual=res, name="out_proj")

        mn = _rmsnorm(memf, p['norm_mem_g'], BF16)
        q = _matmul_wres(h_bf, p['xq'], BF16, PROJ_TM, PROJ_TN, scale=scale, gain=_gain_tile(p['norm_x_g']), name="xq")
        kx = _matmul_wres(mn, p['xk'], BF16, PROJ_TM, PROJ_TN, name="xk")
        vx = _matmul_wres(mn, p['xv'], BF16, PROJ_TM, PROJ_TN, name="xv")
        o = _xattn(q.reshape(bsz, t, dm), kx.reshape(bsz, n_mem, dm), vx.reshape(bsz, n_mem, dm))
        h, h_bf, scale = _matmul_stats(o.reshape(m_tok, dm), p['xo'].astype(BF16), PROJ_TM, PROJ_TN,
                                       residual=h, name="xo")

        hidden = p['ffn_w2'].shape[0]
        w13 = (p['norm_ffn_g'][:, None] * p['ffn_w13']).astype(BF16)
        act = _swiglu_up(h_bf, scale, w13, hidden, FFN_TM, FFN_TN)
        w2 = p['ffn_w2'].astype(BF16)
        if l + 1 < depth:
            h = _matmul(act, w2, F32, FFN_DOWN_TM, FFN_DOWN_TN, residual=h, name="ffn_down")

    outs, off = [], 0
    for x, nrow in zip(xs, rows):
        y = _matmul_res_norm(act, w2, h, norm_final_g, FFN_DOWN_TM, FFN_DOWN_TN,
                             off // FFN_DOWN_TM, nrow // FFN_DOWN_TM, name="ffn_down_norm")
        outs.append(y.reshape(x.shape))
        off += nrow
    return tuple(outs)


def kernel(x_prompt, x_sample, mem_prompt, mem_sample, norm_mix_g, w_in, shift_w, w0_f, w_up_f, w0_b, w_up_b, a0_f, a_up_f, a0_b, a_up_b, g_up, k_k, k_a, r_k, ln_x_g, ln_x_b, pool_w, pool_scale, w_out, norm_x_g, norm_mem_g, xq, xk, xv, xo, norm_ffn_g, ffn_w13, ffn_w2, norm_final_g):
    assert x_prompt.shape[1:] == x_sample.shape[1:] and mem_prompt.shape[1:] == mem_sample.shape[1:]
    t, dm = x_prompt.shape[1:]
    hidden = ffn_w2.shape[1]
    assert t % max(WKV_CHUNK, IN_ROW_CHUNK, 256) == 0 and dm % (LANES * WKV_SOLVE_GROUP) == 0
    assert all(x.shape[0] * t % max(PROJ_TM, FFN_TM) == 0 for x in (x_prompt, x_sample))
    assert hidden % FFN_TN == 0 and dm % max(PROJ_TN, FFN_DOWN_TN) == 0
    lp = {
        'norm_mix_g': norm_mix_g, 'w_in': w_in, 'shift_w': shift_w,
        'w0_f': w0_f, 'w_up_f': w_up_f, 'w0_b': w0_b, 'w_up_b': w_up_b,
        'a0_f': a0_f, 'a_up_f': a_up_f, 'a0_b': a0_b, 'a_up_b': a_up_b,
        'g_up': g_up, 'k_k': k_k, 'k_a': k_a, 'r_k': r_k.reshape(r_k.shape[0], -1),
        'ln_x_g': ln_x_g, 'ln_x_b': ln_x_b,
        'pool_w': pool_w, 'pool_scale': pool_scale, 'w_out': w_out,
        'norm_x_g': norm_x_g, 'norm_mem_g': norm_mem_g, 'xq': xq, 'xk': xk, 'xv': xv, 'xo': xo,
        'norm_ffn_g': norm_ffn_g, 'ffn_w13': ffn_w13, 'ffn_w2': ffn_w2,
    }
    mem = jnp.concatenate([mem_prompt, mem_sample], axis=0)
    return _trunk((x_prompt, x_sample), mem, lp, norm_final_g)
```

```python
import functools
import math

import jax
import jax.numpy as jnp
from jax import lax
from jax.experimental import pallas as pl
from jax.experimental.pallas import tpu as pltpu

F32 = jnp.float32
BF16 = jnp.bfloat16

LANES = 128
SUBLANES = 8
VMEM_LIMIT_BYTES = 56 * 1024 * 1024
VMEM_LIMIT_BYTES_MAX = 60000 * 1024

HEAD_SIZE = 64
HEAD_SHIFT = 6
X_HEADS = 4
POOL_WINDOWS = (2, 4, 8, 16)
POOL_PAD = 16
GN_EPS = 64e-5
NORM_EPS = 1e-6
DECAY_SCALE = -math.exp(-0.5)
WKV_CHUNK = 64
WKV_SOLVE_GROUP = 32
WKV_UPDATE_GROUP = 32
MERGE_GROUP = 16
LORA_W = 128
GATE_LORA_PAD = 512
LORA_COLS = 1024
PROJ_TM = 1024
PROJ_TN = 512
FFN_TM = 2048
FFN_TN = 256
FFN_DOWN_TM = 512
FFN_DOWN_TN = 256
IN_TN = 512
IN_ROW_CHUNK = 512


def _params(*semantics):
    return pltpu.CompilerParams(dimension_semantics=semantics, vmem_limit_bytes=VMEM_LIMIT_BYTES)


def _rms(x, g):
    ms = jnp.mean(x * x, axis=-1, keepdims=True)
    return x * lax.rsqrt(ms + NORM_EPS) * g


def _rmsnorm_kernel(x_ref, g_ref, o_ref):
    o_ref[...] = _rms(x_ref[...], g_ref[...]).astype(o_ref.dtype)


def _rmsnorm(x, g, out_dtype, tm=256, row_block_offset=0, n_row_blocks=None):
    m, d = x.shape
    nb = m // tm if n_row_blocks is None else n_row_blocks
    return pl.pallas_call(
        _rmsnorm_kernel,
        grid=(nb,),
        in_specs=[pl.BlockSpec((tm, d), lambda i: (i + row_block_offset, 0)),
                  pl.BlockSpec((1, d), lambda i: (0, 0))],
        out_specs=pl.BlockSpec((tm, d), lambda i: (i, 0)),
        out_shape=jax.ShapeDtypeStruct((nb * tm, d), out_dtype),
        compiler_params=_params("parallel"),
        name="rmsnorm",
    )(x, g.reshape(1, d))


def _rmsnorm2_kernel(xa_ref, xb_ref, g_ref, o_ref, *, na):
    x = jnp.where(pl.program_id(0) < na, xa_ref[...], xb_ref[...])
    o_ref[...] = _rms(x, g_ref[...]).astype(o_ref.dtype)


def _rmsnorm2(xa, xb, g, out_dtype, tm=256):
    d = xa.shape[1]
    na, nb = xa.shape[0] // tm, xb.shape[0] // tm
    return pl.pallas_call(
        functools.partial(_rmsnorm2_kernel, na=na),
        grid=(na + nb,),
        in_specs=[pl.BlockSpec((tm, d), lambda i: (jnp.minimum(i, na - 1), 0)),
                  pl.BlockSpec((tm, d), lambda i: (jnp.maximum(i - na, 0), 0)),
                  pl.BlockSpec((1, d), lambda i: (0, 0))],
        out_specs=pl.BlockSpec((tm, d), lambda i: (i, 0)),
        out_shape=jax.ShapeDtypeStruct(((na + nb) * tm, d), out_dtype),
        compiler_params=_params("arbitrary"),
        name="rmsnorm2",
    )(xa, xb, g.reshape(1, d))


def _matmul_kernel(x_ref, w_ref, o_ref):
    o_ref[...] = jnp.dot(x_ref[...], w_ref[...], preferred_element_type=F32).astype(o_ref.dtype)


def _matmul_res_kernel(x_ref, w_ref, r_ref, o_ref):
    acc = jnp.dot(x_ref[...], w_ref[...], preferred_element_type=F32)
    o_ref[...] = (r_ref[...] + acc).astype(o_ref.dtype)


def _matmul(x, w, out_dtype, tm, tn, residual=None, name="matmul"):
    m, k = x.shape
    n = w.shape[1]
    in_specs = [pl.BlockSpec((tm, k), lambda i, j: (i, 0)),
                pl.BlockSpec((k, tn), lambda i, j: (0, j))]
    args = [x, w]
    body = _matmul_kernel
    if residual is not None:
        in_specs.append(pl.BlockSpec((tm, tn), lambda i, j: (i, j)))
        args.append(residual)
        body = _matmul_res_kernel
    return pl.pallas_call(
        body,
        grid=(m // tm, n // tn),
        in_specs=in_specs,
        out_specs=pl.BlockSpec((tm, tn), lambda i, j: (i, j)),
        out_shape=jax.ShapeDtypeStruct((m, n), out_dtype),
        compiler_params=_params("arbitrary", "arbitrary"),
        name=name,
    )(*args)


def _matmul_res_norm_kernel(x_ref, w_ref, r_ref, g_ref, o_ref):
    j = pl.program_id(1)
    tn = w_ref.shape[1]
    acc = jnp.dot(x_ref[...], w_ref[...], preferred_element_type=F32)
    o_ref[:, pl.ds(pl.multiple_of(j * tn, tn), tn)] = r_ref[...] + acc

    @pl.when(j == pl.num_programs(1) - 1)
    def _():
        o_ref[...] = _rms(o_ref[...], g_ref[...])


def _matmul_res_norm(x, w, residual, g, tm, tn, row_block_offset, n_row_blocks, name):
    k = x.shape[1]
    n = w.shape[1]
    off = row_block_offset
    return pl.pallas_call(
        _matmul_res_norm_kernel,
        grid=(n_row_blocks, n // tn),
        in_specs=[pl.BlockSpec((tm, k), lambda i, j: (i + off, 0)),
                  pl.BlockSpec((k, tn), lambda i, j: (0, j)),
                  pl.BlockSpec((tm, tn), lambda i, j: (i + off, j)),
                  pl.BlockSpec((1, n), lambda i, j: (0, 0))],
        out_specs=pl.BlockSpec((tm, n), lambda i, j: (i, 0)),
        out_shape=jax.ShapeDtypeStruct((n_row_blocks * tm, n), F32),
        compiler_params=pltpu.CompilerParams(dimension_semantics=("arbitrary", "arbitrary"),
                                             vmem_limit_bytes=VMEM_LIMIT_BYTES_MAX),
        name=name,
    )(x, w, residual, g.reshape(1, n))


def _in_proj_kernel(x_ref, w_ref, taps_ref, o_ref):
    t = x_ref.shape[0]
    rc = IN_ROW_CHUNK
    w = w_ref[...]
    taps = taps_ref[...]
    ridx = lax.broadcasted_iota(jnp.int32, (rc, w.shape[1]), 0)
    zs = [jnp.dot(x_ref[c * rc:(c + 1) * rc, :], w, preferred_element_type=F32) for c in range(t // rc)]
    zero_row = jnp.zeros((1, w.shape[1]), F32)
    for c, z in enumerate(zs):
        prev_row = zs[c - 1][rc - 1:rc, :] if c > 0 else zero_row
        next_row = zs[c + 1][0:1, :] if c + 1 < len(zs) else zero_row
        zm1 = jnp.where(ridx == 0, prev_row, pltpu.roll(z, 1, axis=0))
        zp1 = jnp.where(ridx == rc - 1, next_row, pltpu.roll(z, rc - 1, axis=0))
        o_ref[0, c * rc:(c + 1) * rc, :] = zm1 * taps[0:1, :] + z * taps[1:2, :] + zp1 * taps[2:3, :]


def _in_proj(xn, w, taps, bsz, t):
    k = xn.shape[1]
    n = w.shape[1]
    return pl.pallas_call(
        _in_proj_kernel,
        grid=(bsz, n // IN_TN),
        in_specs=[pl.BlockSpec((t, k), lambda b, j: (b, 0), pipeline_mode=pl.Buffered(1)),
                  pl.BlockSpec((k, IN_TN), lambda b, j: (0, j)),
                  pl.BlockSpec((3, IN_TN), lambda b, j: (0, j))],
        out_specs=pl.BlockSpec((1, t, IN_TN), lambda b, j: (b, 0, j)),
        out_shape=jax.ShapeDtypeStruct((bsz, t, n), F32),
        compiler_params=_params("arbitrary", "arbitrary"),
        name="in_proj",
    )(xn, w, taps)


def _lane_tiled(x, width):
    return jnp.concatenate([x] * (width // LANES), axis=1)


def _matmul_stats_kernel(x_ref, w_ref, *rest, na, inv_d):
    *r_refs, o_ref, ob_ref, sc_ref, ssq_ref = rest
    j = pl.program_id(1)
    acc = jnp.dot(x_ref[...], w_ref[...], preferred_element_type=F32)
    res = r_refs[0][...] if len(r_refs) == 1 else jnp.where(pl.program_id(0) < na, r_refs[0][...], r_refs[1][...])
    h = res + acc
    o_ref[...] = h
    ob_ref[...] = h.astype(BF16)
    hh = h * h
    part = hh[:, 0:LANES]
    for c in range(1, hh.shape[1] // LANES):
        part = part + hh[:, c * LANES:(c + 1) * LANES]

    @pl.when(j == 0)
    def _():
        ssq_ref[...] = part

    @pl.when(j > 0)
    def _():
        ssq_ref[...] += part

    @pl.when(j == pl.num_programs(1) - 1)
    def _():
        ms = jnp.sum(ssq_ref[...], axis=-1, keepdims=True) * inv_d
        sc_ref[...] = jnp.broadcast_to(lax.rsqrt(ms + NORM_EPS), sc_ref.shape)


def _matmul_stats(x, w, tm, tn, residual, name):
    m, k = x.shape
    n = w.shape[1]
    in_specs = [pl.BlockSpec((tm, k), lambda i, j: (i, 0)),
                pl.BlockSpec((k, tn), lambda i, j: (0, j))]
    na = 0
    if isinstance(residual, tuple):
        ra, rb = residual
        na = ra.shape[0] // tm
        in_specs += [pl.BlockSpec((tm, tn), lambda i, j: (jnp.minimum(i, na - 1), j)),
                     pl.BlockSpec((tm, tn), lambda i, j: (jnp.maximum(i - na, 0), j))]
        res_args = [ra, rb]
    else:
        in_specs.append(pl.BlockSpec((tm, tn), lambda i, j: (i, j)))
        res_args = [residual]
    return pl.pallas_call(
        functools.partial(_matmul_stats_kernel, na=na, inv_d=1.0 / n),
        grid=(m // tm, n // tn),
        in_specs=in_specs,
        out_specs=[pl.BlockSpec((tm, tn), lambda i, j: (i, j)),
                   pl.BlockSpec((tm, tn), lambda i, j: (i, j)),
                   pl.BlockSpec((tm, LANES), lambda i, j: (i, 0))],
        out_shape=[jax.ShapeDtypeStruct((m, n), F32), jax.ShapeDtypeStruct((m, n), BF16),
                   jax.ShapeDtypeStruct((m, LANES), F32)],
        scratch_shapes=[pltpu.VMEM((tm, LANES), F32)],
        compiler_params=_params("arbitrary", "arbitrary"),
        name=name,
    )(x, w, *res_args)


def _cast_weight(w_ref, wb_ref, g_ref):
    w = w_ref[...]
    if g_ref is not None:
        w = w * _lane_tiled(g_ref[...], w.shape[1])
    wb_ref[...] = w.astype(BF16)


def _wres_kernel(*refs, normed):
    if normed:
        x_ref, sc_ref, g_ref, w_ref, o_ref, wb_ref = refs
    else:
        x_ref, w_ref, o_ref, wb_ref = refs
        sc_ref = g_ref = None

    @pl.when(pl.program_id(1) == 0)
    def _():
        _cast_weight(w_ref, wb_ref, g_ref)

    acc = jnp.dot(x_ref[...], wb_ref[...], preferred_element_type=F32)
    if normed:
        acc = acc * _lane_tiled(sc_ref[...], acc.shape[1])
    o_ref[...] = acc.astype(o_ref.dtype)


def _matmul_wres(x, w, out_dtype, tm, tn, scale=None, gain=None, name="matmul_wres"):
    m, k = x.shape
    n = w.shape[1]
    normed = scale is not None
    in_specs = [pl.BlockSpec((tm, k), lambda j, i: (i, 0))]
    args = [x]
    if normed:
        in_specs += [pl.BlockSpec((tm, LANES), lambda j, i: (i, 0)),
                     pl.BlockSpec((k, LANES), lambda j, i: (0, 0))]
        args += [scale, gain]
    in_specs.append(pl.BlockSpec((k, tn), lambda j, i: (0, j)))
    args.append(w)
    return pl.pallas_call(
        functools.partial(_wres_kernel, normed=normed),
        grid=(n // tn, m // tm),
        in_specs=in_specs,
        out_specs=pl.BlockSpec((tm, tn), lambda j, i: (i, j)),
        out_shape=jax.ShapeDtypeStruct((m, n), out_dtype),
        scratch_shapes=[pltpu.VMEM((k, tn), BF16)],
        compiler_params=_params("arbitrary", "arbitrary"),
        name=name,
    )(*args)


def _swiglu_kernel(x_ref, sc_ref, wg_ref, wu_ref, o_ref):
    x = x_ref[...]
    sc = _lane_tiled(sc_ref[...], o_ref.shape[1])
    gate = jnp.dot(x, wg_ref[...], preferred_element_type=F32) * sc
    up = jnp.dot(x, wu_ref[...], preferred_element_type=F32) * sc
    o_ref[...] = (gate * jax.nn.sigmoid(gate) * up).astype(o_ref.dtype)


def _swiglu_up(x, scale, w13, hidden, tm, tn):
    m, k = x.shape
    nb = hidden // tn
    return pl.pallas_call(
        _swiglu_kernel,
        grid=(m // tm, nb),
        in_specs=[pl.BlockSpec((tm, k), lambda i, j: (i, 0)),
                  pl.BlockSpec((tm, LANES), lambda i, j: (i, 0)),
                  pl.BlockSpec((k, tn), lambda i, j: (0, j)),
                  pl.BlockSpec((k, tn), lambda i, j: (0, j + nb))],
        out_specs=pl.BlockSpec((tm, tn), lambda i, j: (i, j)),
        out_shape=jax.ShapeDtypeStruct((m, hidden), BF16),
        compiler_params=_params("arbitrary", "arbitrary"),
        name="swiglu_up",
    )(x, scale, w13, w13)


def _head_sum_matrix():
    r = lax.broadcasted_iota(jnp.int32, (LANES, LANES), 0) >> HEAD_SHIFT
    c = lax.broadcasted_iota(jnp.int32, (LANES, LANES), 1) >> HEAD_SHIFT
    return jnp.where(r == c, 1.0, 0.0).astype(BF16)


def _split2(x):
    hi = x.astype(BF16)
    return hi, (x - hi.astype(F32)).astype(BF16)


def _gmap(f, *lists):
    return [f(*xs) for xs in zip(*lists)]


def _wkv_kernel(z_ref, lo_ref, wup_ref, aup_ref, w0_ref, a0_ref, kk_ref, ka_ref, rk_ref,
                y_ref, bo_ref, state_ref, lw_ref, cum_ref, icl_ref,
                tinv_ref, lrk_ref, rb_ref, lhs_ref, btk_ref, vb_ref, *, d_model):
    C = WKV_CHUNK
    d = pl.program_id(1)
    c = pl.program_id(2)
    sgn = 1 - 2 * d

    @pl.when(c == 0)
    def _():
        state_ref[...] = jnp.zeros_like(state_ref)

    wl = w0_ref[0] + jnp.dot(jnp.tanh(lo_ref[0, :, 0:LORA_W]).astype(BF16), wup_ref[0],
                             preferred_element_type=F32)
    lw = DECAY_SCALE * jax.nn.sigmoid(wl)
    lw_ref[...] = lw
    r2c = lax.broadcasted_iota(jnp.int32, (C, 2 * C), 0)
    c2c = lax.broadcasted_iota(jnp.int32, (C, 2 * C), 1) & (C - 1)
    tri2 = jnp.where((r2c - c2c) * sgn >= 0, 1.0, 0.0).astype(BF16)
    lw_hi, lw_lo = _split2(lw)
    cum_ref[...] = jnp.dot(tri2, jnp.concatenate([lw_hi, lw_lo], axis=0), preferred_element_type=F32)
    icl_ref[...] = jax.nn.sigmoid(
        a0_ref[0] + jnp.dot(lo_ref[0, :, LORA_W:2 * LORA_W].astype(BF16), aup_ref[0],
                            preferred_element_type=F32))

    row = lax.broadcasted_iota(jnp.int32, (C, LANES), 0)
    col = lax.broadcasted_iota(jnp.int32, (C, LANES), 1)
    colh = col & (HEAD_SIZE - 1)
    order = (row - colh) * sgn
    strict = order > 0
    incl = order >= 0
    eye2 = row == colh
    lane_lo = col < HEAD_SIZE
    esum = _head_sum_matrix()
    rr = lax.broadcasted_iota(jnp.int32, (LANES, LANES), 0) >> HEAD_SHIFT
    cc = lax.broadcasted_iota(jnp.int32, (LANES, LANES), 1) >> HEAD_SHIFT
    blockdiag = rr == cc

    def bd(x):
        zero = jnp.zeros_like(x)
        return jnp.concatenate([jnp.where(lane_lo, x, zero), jnp.where(lane_lo, zero, x)], axis=0)

    def pmul(x, y):
        return jnp.dot(x.astype(BF16), bd(y.astype(BF16)), preferred_element_type=F32)

    nt_dims = (((1,), (1,)), ((), ()))
    tn_dims = (((0,), (0,)), ((), ()))

    def lane_tile(p, offset=0):
        return pl.ds(pl.multiple_of(offset + p * LANES, LANES), LANES)

    def head_sums(xs):
        stacked = jnp.concatenate([x.astype(BF16) for x in xs], axis=0)
        sums = jnp.dot(stacked, esum, preferred_element_type=F32)
        return [sums[j * C:(j + 1) * C] for j in range(len(xs))]

    def solve_body(g, carry):
        pairs = [g * WKV_SOLVE_GROUP + j for j in range(WKV_SOLVE_GROUP)]
        cols = [lane_tile(p) for p in pairs]
        r = [z_ref[0, :, cs] for cs in cols]
        k = [z_ref[0, :, lane_tile(p, d_model)] for p in pairs]
        v = [z_ref[0, :, lane_tile(p, 2 * d_model)] for p in pairs]
        icl = [icl_ref[:, cs] for cs in cols]
        lwp = [lw_ref[:, cs] for cs in cols]
        cum = [cum_ref[:, cs] for cs in cols]

        q = [ki * kk_ref[:, cs] for ki, cs in zip(k, cols)]
        n2 = head_sums([qi * qi for qi in q])
        kd = [ki * (1.0 + (ic - 1.0) * ka_ref[:, cs]) for ki, ic, cs in zip(k, icl, cols)]
        bsum = head_sums([ri * kdi * rk_ref[:, cs] for ri, kdi, cs in zip(r, kd, cols)])
        for cs, bs, vi in zip(cols, bsum, v):
            bo_ref[0, 0, :, cs] = (bs * vi).astype(bo_ref.dtype)

        kk = [qi * lax.rsqrt(jnp.maximum(ni, 1e-12)) for qi, ni in zip(q, n2)]
        b = _gmap(lambda x, ic: x * ic, kk, icl)
        e_out = [jnp.exp(-x) for x in cum]
        at = _gmap(lambda x, cm, lw_: (-x * jnp.exp(cm - lw_)).astype(BF16), kk, cum, lwp)
        rt = _gmap(lambda x, cm: (x * jnp.exp(cm)).astype(BF16), r, cum)
        bt = _gmap(lambda x, e: (x * e).astype(BF16), b, e_out)
        kt = _gmap(lambda x, e: (x * e).astype(BF16), kd, e_out)

        lhs = _gmap(lambda a_, r_: jnp.concatenate([a_, r_], axis=0), at, rt)
        rhs_t = _gmap(lambda b_, k_: jnp.concatenate([bd(b_), bd(k_)], axis=0), bt, kt)
        pmat = _gmap(lambda l_, r_: lax.dot_general(l_, r_, nt_dims, preferred_element_type=F32), lhs, rhs_t)
        for p, l_, b_, k_, v_ in zip(pairs, lhs, bt, kt, v):
            lhs_ref[p] = l_
            btk_ref[p] = jnp.concatenate([b_, k_], axis=0)
            vb_ref[p] = v_.astype(BF16)
        lab = [jnp.where(strict, x[:C, :LANES], 0.0).astype(BF16) for x in pmat]
        for p, x in zip(pairs, pmat):
            rb_ref[p] = jnp.where(incl, x[C:, :LANES], 0.0).astype(BF16)
            lrk_ref[p] = jnp.concatenate([jnp.where(strict, x[:C, LANES:], 0.0).astype(BF16),
                                          jnp.where(incl, x[C:, LANES:], 0.0).astype(BF16)], axis=0)

        zero_b = jnp.zeros((C, LANES), BF16)
        ident = jnp.where(eye2, 1.0, 0.0).astype(BF16)
        first = (row >> 1) == (colh >> 1)
        tinv = [ident + jnp.where(first, x, zero_b) for x in lab]
        s = 2
        while s < C:
            sh = s.bit_length() - 1
            level = ((row >> (sh + 1)) == (colh >> (sh + 1))) & ((row >> sh) != (colh >> sh))
            off = [jnp.where(level, x, zero_b) for x in lab]
            tmp = _gmap(pmul, tinv, off)
            upd_t = _gmap(pmul, tmp, tinv)
            tinv = _gmap(lambda t_, x: t_ + x.astype(BF16), tinv, upd_t)
            s *= 2
        for p, x in zip(pairs, tinv):
            tinv_ref[p] = x
        return carry

    def update_body(g, carry):
        pairs = [g * WKV_UPDATE_GROUP + j for j in range(WKV_UPDATE_GROUP)]
        cols = [lane_tile(p) for p in pairs]
        h = [state_ref[p] for p in pairs]
        vb = [vb_ref[p] for p in pairs]
        hs = [jnp.dot(lhs_ref[p], h_.astype(BF16), preferred_element_type=F32)
              for p, h_ in zip(pairs, h)]
        lrkv = [jnp.dot(lrk_ref[p], bd(v_), preferred_element_type=F32) for p, v_ in zip(pairs, vb)]
        rhs_u = _gmap(lambda h_, x: (h_[:C] + x[:C]).astype(BF16), hs, lrkv)
        u = [jnp.dot(tinv_ref[p], bd(x), preferred_element_type=F32) for p, x in zip(pairs, rhs_u)]
        ub = [x.astype(BF16) for x in u]
        rbu = [jnp.dot(rb_ref[p], bd(x), preferred_element_type=F32) for p, x in zip(pairs, ub)]
        for cs, h_, xv, xu in zip(cols, hs, lrkv, rbu):
            y_ref[0, 0, :, cs] = (h_[C:] + xv[C:] + xu).astype(y_ref.dtype)
        upd = [lax.dot_general(btk_ref[p], jnp.concatenate([u_, v_], axis=0), tn_dims,
                               preferred_element_type=F32) for p, u_, v_ in zip(pairs, ub, vb)]
        for p, cs, h_, x in zip(pairs, cols, h, upd):
            tot = jnp.sum(lw_ref[:, cs], axis=0, keepdims=True)
            decay_rows = jnp.broadcast_to(jnp.exp(tot), (LANES, LANES)).T
            state_ref[p] = decay_rows * (h_ + jnp.where(blockdiag, x, 0.0))
        return carry

    n_pairs = d_model // LANES
    lax.fori_loop(0, n_pairs // WKV_SOLVE_GROUP, solve_body, 0)
    lax.fori_loop(0, n_pairs // WKV_UPDATE_GROUP, update_body, 0)


def _wkv_scan(z, dm, w_up, a_up, w0, a0, k_k, k_a, r_k):
    bsz, t, n = z.shape
    C = WKV_CHUNK
    nc = t // C
    n_pairs = dm // LANES
    lora_block = (n - LORA_COLS) // LORA_COLS

    def tchunk(dd, cc):
        return jnp.where(dd == 0, cc, nc - 1 - cc)

    def dir_map(bb, dd, cc):
        return (dd, 0, 0)

    def const2(bb, dd, cc):
        return (0, 0)

    def out_map(bb, dd, cc):
        return (dd, bb, tchunk(dd, cc), 0)

    out_sds = jax.ShapeDtypeStruct((2, bsz, t, dm), BF16)
    kern = functools.partial(_wkv_kernel, d_model=dm)
    return pl.pallas_call(
        kern,
        grid=(bsz, 2, nc),
        in_specs=[
            pl.BlockSpec((1, C, 3 * dm), lambda bb, dd, cc: (bb, tchunk(dd, cc), 0)),
            pl.BlockSpec((1, C, LORA_COLS), lambda bb, dd, cc: (bb, tchunk(dd, cc), lora_block)),
            pl.BlockSpec((1, LORA_W, dm), dir_map),
            pl.BlockSpec((1, LORA_W, dm), dir_map),
            pl.BlockSpec((1, 1, dm), dir_map),
            pl.BlockSpec((1, 1, dm), dir_map),
            pl.BlockSpec((1, dm), const2),
            pl.BlockSpec((1, dm), const2),
            pl.BlockSpec((1, dm), const2),
        ],
        out_specs=[pl.BlockSpec((1, 1, C, dm), out_map), pl.BlockSpec((1, 1, C, dm), out_map)],
        out_shape=[out_sds, out_sds],
        scratch_shapes=[pltpu.VMEM((n_pairs, LANES, LANES), F32),
                        pltpu.VMEM((C, dm), F32), pltpu.VMEM((C, dm), F32), pltpu.VMEM((C, dm), F32),
                        pltpu.VMEM((n_pairs, C, LANES), BF16), pltpu.VMEM((n_pairs, 2 * C, LANES), BF16),
                        pltpu.VMEM((n_pairs, C, LANES), BF16), pltpu.VMEM((n_pairs, 2 * C, LANES), BF16),
                        pltpu.VMEM((n_pairs, 2 * C, LANES), BF16), pltpu.VMEM((n_pairs, C, LANES), BF16)],
        compiler_params=_params("arbitrary", "arbitrary", "arbitrary"),
        name="wkv_scan",
    )(z, z, w_up, a_up, w0, a0, k_k, k_a, r_k)


def _pool_kernel(p_ref, gate_ref, w_ref, scale_ref, o_ref, pad_ref, *, rows):
    g = pl.program_id(1)
    t, gi = p_ref.shape[1], p_ref.shape[2]
    zeros = jnp.zeros((POOL_PAD, gi), F32)
    pad_ref[0:POOL_PAD, :] = zeros
    pad_ref[POOL_PAD + t:POOL_PAD + t + POOL_PAD, :] = zeros
    pad_ref[POOL_PAD:POOL_PAD + t, :] = p_ref[0]
    w = w_ref[0]
    scale = scale_ref[...]

    for gidx, win in enumerate(POOL_WINDOWS):
        @pl.when(g == gidx)
        def _(win=win):
            half = win // 2

            def tile_body(i, carry):
                r0 = pl.multiple_of(i * rows, rows)
                n = rows + 2 * SUBLANES
                xt = pad_ref[pl.ds(r0 + POOL_PAD - SUBLANES, n), :]
                acc = xt
                step = 1
                while step < win:
                    acc = acc + pltpu.roll(acc, n - step, axis=0)
                    step *= 2
                if SUBLANES - half:
                    acc = pltpu.roll(acc, n - (SUBLANES - half), axis=0)
                acc = acc[0:rows]
                tt = r0 + lax.broadcasted_iota(jnp.int32, (rows, LANES), 0)
                cnt = (jnp.minimum(tt + (win - half), t) - jnp.maximum(tt - half, 0)).astype(F32)
                inv = 1.0 / cnt
                inv_full = jnp.concatenate([inv] * (gi // LANES), axis=1)
                dlt = acc * inv_full - xt[SUBLANES:SUBLANES + rows]
                out = jnp.dot(dlt.astype(BF16), w, preferred_element_type=F32) * scale
                gate = jax.nn.sigmoid(gate_ref[0, pl.ds(r0, rows), :])
                o_ref[0, pl.ds(r0, rows), :] = (gate * out).astype(o_ref.dtype)
                return carry

            lax.fori_loop(0, t // rows, tile_body, 0)


def _pool_branch(z, pool_col, gate_col, pool_w, pool_scale, rows=256):
    bsz, t, _ = z.shape
    ng, gi, go = pool_w.shape
    dm = ng * go
    kern = functools.partial(_pool_kernel, rows=rows)
    return pl.pallas_call(
        kern,
        grid=(bsz, ng),
        in_specs=[pl.BlockSpec((1, t, gi), lambda b, g: (b, 0, pool_col // gi + g)),
                  pl.BlockSpec((1, t, go), lambda b, g: (b, 0, gate_col // go + g)),
                  pl.BlockSpec((1, gi, go), lambda b, g: (g, 0, 0)),
                  pl.BlockSpec((1, go), lambda b, g: (0, g))],
        out_specs=pl.BlockSpec((1, t, go), lambda b, g: (b, 0, g)),
        out_shape=jax.ShapeDtypeStruct((bsz, t, dm), BF16),
        scratch_shapes=[pltpu.VMEM((t + 2 * POOL_PAD, gi), F32)],
        compiler_params=_params("arbitrary", "arbitrary"),
        name="pool_branch",
    )(z, z, pool_w, pool_scale)


def _merge_kernel(y_ref, bo_ref, lo_ref, gup_ref, gng_ref, gnb_ref, gate_ref, yb_ref, o_ref):
    gd_lo = 2 * LORA_W
    esum = _head_sum_matrix()
    inv_n = 1.0 / HEAD_SIZE
    gd = jax.nn.sigmoid(lo_ref[0, :, gd_lo:gd_lo + GATE_LORA_PAD]).astype(BF16)

    def group_body(gidx, carry):
        cols = [pl.ds(pl.multiple_of((gidx * MERGE_GROUP + j) * LANES, LANES), LANES) for j in range(MERGE_GROUP)]
        y = [y_ref[0, 0, :, cs].astype(F32) + y_ref[1, 0, :, cs].astype(F32) for cs in cols]
        mu = [jnp.dot(x.astype(BF16), esum, preferred_element_type=F32) * inv_n for x in y]
        g = [jnp.dot(gd, gup_ref[:, cs], preferred_element_type=F32) for cs in cols]
        yc = _gmap(lambda x, m: x - m, y, mu)
        var = [jnp.dot((x * x).astype(BF16), esum, preferred_element_type=F32) * inv_n for x in yc]
        for cs, x, vr, gi in zip(cols, yc, var, g):
            yn = x * lax.rsqrt(vr + GN_EPS) * gng_ref[:, cs] + gnb_ref[:, cs]
            yn = yn + bo_ref[0, 0, :, cs].astype(F32) + bo_ref[1, 0, :, cs].astype(F32)
            ya = jax.nn.sigmoid(gate_ref[0, :, cs]) * (yn * gi)
            o_ref[0, :, cs] = (ya + yb_ref[0, :, cs].astype(F32)).astype(o_ref.dtype)
        return carry

    lax.fori_loop(0, o_ref.shape[2] // (LANES * MERGE_GROUP), group_body, 0)


def _merge(y, bo, zs, zp, g_up, ln_g, ln_b, yb, tt=256):
    _, bsz, t, dm = y.shape
    n = zs.shape[-1]
    lora_block = (n - LORA_COLS) // LORA_COLS
    return pl.pallas_call(
        _merge_kernel,
        grid=(bsz, t // tt),
        in_specs=[pl.BlockSpec((2, 1, tt, dm), lambda b, i: (0, b, i, 0)),
                  pl.BlockSpec((2, 1, tt, dm), lambda b, i: (0, b, i, 0)),
                  pl.BlockSpec((1, tt, LORA_COLS), lambda b, i: (b, i, lora_block)),
                  pl.BlockSpec((GATE_LORA_PAD, dm), lambda b, i: (0, 0)),
                  pl.BlockSpec((1, dm), lambda b, i: (0, 0)),
                  pl.BlockSpec((1, dm), lambda b, i: (0, 0)),
                  pl.BlockSpec((1, tt, dm), lambda b, i: (b, i, 0)),
                  pl.BlockSpec((1, tt, dm), lambda b, i: (b, i, 0))],
        out_specs=pl.BlockSpec((1, tt, dm), lambda b, i: (b, i, 0)),
        out_shape=jax.ShapeDtypeStruct((bsz, t, dm), BF16),
        compiler_params=_params("arbitrary", "arbitrary"),
        name="wkv_merge",
    )(y, bo, zs, g_up, ln_g, ln_b, zp, yb)


def _xattn_kernel(q_ref, k_ref, v_ref, o_ref, *, head_dim):
    scale = head_dim ** -0.5
    nt_dims = (((1,), (1,)), ((), ()))
    for h in range(X_HEADS):
        cs = slice(h * head_dim, (h + 1) * head_dim)
        s = lax.dot_general(q_ref[0, :, cs], k_ref[0, :, cs], nt_dims, preferred_element_type=F32) * scale
        m = jnp.max(s, axis=-1, keepdims=True)
        e = jnp.exp(s - m)
        p = e / jnp.sum(e, axis=-1, keepdims=True)
        o_ref[0, :, cs] = jnp.dot(p.astype(BF16), v_ref[0, :, cs], preferred_element_type=F32).astype(o_ref.dtype)


def _xattn(q, k, v, tq=512):
    bsz, t, dm = q.shape
    m = k.shape[1]
    kern = functools.partial(_xattn_kernel, head_dim=dm // X_HEADS)
    return pl.pallas_call(
        kern,
        grid=(bsz, t // tq),
        in_specs=[pl.BlockSpec((1, tq, dm), lambda b, i: (b, i, 0)),
                  pl.BlockSpec((1, m, dm), lambda b, i: (b, 0, 0)),
                  pl.BlockSpec((1, m, dm), lambda b, i: (b, 0, 0))],
        out_specs=pl.BlockSpec((1, tq, dm), lambda b, i: (b, i, 0)),
        out_shape=jax.ShapeDtypeStruct((bsz, t, dm), BF16),
        compiler_params=_params("arbitrary", "arbitrary"),
        name="xattn",
    )(q, k, v)


def _pad_to(x, axis, size):
    pad = [(0, 0)] * x.ndim
    pad[axis] = (0, size - x.shape[axis])
    return jnp.pad(x, pad)


def _gain_tile(g):
    return jnp.broadcast_to(g[:, None], (g.shape[0], LANES))


def _in_proj_operands(p, dm):
    w_in, shift_w = p['w_in'], p['shift_w']
    gate_lora = p['g_up'].shape[0]
    c_rkv = 3 * dm
    c_lora = c_rkv + 2 * LORA_W + gate_lora
    pool_width = p['pool_w'].shape[0] * p['pool_w'].shape[1]
    c_pool = c_lora + pool_width
    wb = w_in.astype(BF16)
    w_shift = jnp.concatenate([wb[:, :c_rkv], _pad_to(wb[:, c_rkv:c_lora], 1, LORA_COLS)], axis=1)
    taps = jnp.concatenate([shift_w[:, :c_rkv], _pad_to(shift_w[:, c_rkv:c_lora], 1, LORA_COLS)], axis=1)
    w_plain = jnp.concatenate([wb[:, c_pool:], wb[:, c_lora:c_pool]], axis=1)
    return w_shift, taps, w_plain


def _trunk(xs, mem, lp, norm_final_g):
    t, dm = xs[0].shape[1:]
    rows = [x.shape[0] * t for x in xs]
    bsz = sum(x.shape[0] for x in xs)
    n_mem = mem.shape[1]
    m_tok = bsz * t
    hs = tuple(x.reshape(-1, dm) for x in xs)
    memf = mem.reshape(bsz * n_mem, dm)
    depth = lp['w_in'].shape[0]
    h = None
    for l in range(depth):
        p = {name: arr[l] for name, arr in lp.items()}
        w_shift, taps, w_plain = _in_proj_operands(p, dm)
        g_up = _pad_to(p['g_up'], 0, GATE_LORA_PAD).astype(BF16)

        if h is None:
            xn = _rmsnorm2(hs[0], hs[1], p['norm_mix_g'], BF16)
            res = hs
        else:
            xn = _rmsnorm(h, p['norm_mix_g'], BF16)
            res = h
        zs = _in_proj(xn, w_shift, taps, bsz, t)
        zp = _matmul(xn, w_plain, F32, PROJ_TM, 2 * PROJ_TN, name="in_plain").reshape(bsz, t, -1)

        w_up = jnp.stack([p['w_up_f'], p['w_up_b']]).astype(BF16)
        a_up = jnp.stack([p['a_up_f'], p['a_up_b']]).astype(BF16)
        w0 = jnp.stack([p['w0_f'], p['w0_b']]).reshape(2, 1, dm)
        a0 = jnp.stack([p['a0_f'], p['a0_b']]).reshape(2, 1, dm)
        y, bo = _wkv_scan(zs, dm, w_up, a_up, w0, a0,
                          p['k_k'].reshape(1, dm), p['k_a'].reshape(1, dm), p['r_k'].reshape(1, dm))
        yb = _pool_branch(zp, 2 * dm, dm, p['pool_w'].astype(BF16), p['pool_scale'].reshape(1, dm))
        merged = _merge(y, bo, zs, zp, g_up, p['ln_x_g'].reshape(1, dm), p['ln_x_b'].reshape(1, dm), yb)
        h, h_bf, scale = _matmul_stats(merged.reshape(m_tok, dm), p['w_out'].astype(BF16), PROJ_TM, PROJ_TN,
                                       residual=res, name="out_proj")

        mn = _rmsnorm(memf, p['norm_mem_g'], BF16)
        q = _matmul_wres(h_bf, p['xq'], BF16, PROJ_TM, PROJ_TN, scale=scale, gain=_gain_tile(p['norm_x_g']), name="xq")
        kx = _matmul_wres(mn, p['xk'], BF16, PROJ_TM, PROJ_TN, name="xk")
        vx = _matmul_wres(mn, p['xv'], BF16, PROJ_TM, PROJ_TN, name="xv")
        o = _xattn(q.reshape(bsz, t, dm), kx.reshape(bsz, n_mem, dm), vx.reshape(bsz, n_mem, dm))
        h, h_bf, scale = _matmul_stats(o.reshape(m_tok, dm), p['xo'].astype(BF16), PROJ_TM, PROJ_TN,
                                       residual=h, name="xo")

        hidden = p['ffn_w2'].shape[0]
        w13 = (p['norm_ffn_g'][:, None] * p['ffn_w13']).astype(BF16)
        act = _swiglu_up(h_bf, scale, w13, hidden, FFN_TM, FFN_TN)
        w2 = p['ffn_w2'].astype(BF16)
        if l + 1 < depth:
            h = _matmul(act, w2, F32, FFN_DOWN_TM, FFN_DOWN_TN, residual=h, name="ffn_down")

    outs, off = [], 0
    for x, nrow in zip(xs, rows):
        y = _matmul_res_norm(act, w2, h, norm_final_g, FFN_DOWN_TM, FFN_DOWN_TN,
                             off // FFN_DOWN_TM, nrow // FFN_DOWN_TM, name="ffn_down_norm")
        outs.append(y.reshape(x.shape))
        off += nrow
    return tuple(outs)


def kernel(x_prompt, x_sample, mem_prompt, mem_sample, norm_mix_g, w_in, shift_w, w0_f, w_up_f, w0_b, w_up_b, a0_f, a_up_f, a0_b, a_up_b, g_up, k_k, k_a, r_k, ln_x_g, ln_x_b, pool_w, pool_scale, w_out, norm_x_g, norm_mem_g, xq, xk, xv, xo, norm_ffn_g, ffn_w13, ffn_w2, norm_final_g):
    assert x_prompt.shape[1:] == x_sample.shape[1:] and mem_prompt.shape[1:] == mem_sample.shape[1:]
    t, dm = x_prompt.shape[1:]
    hidden = ffn_w2.shape[1]
    assert t % max(WKV_CHUNK, IN_ROW_CHUNK, 256) == 0 and dm % (LANES * WKV_SOLVE_GROUP) == 0
    assert all(x.shape[0] * t % max(PROJ_TM, FFN_TM) == 0 for x in (x_prompt, x_sample))
    assert hidden % FFN_TN == 0 and dm % max(PROJ_TN, FFN_DOWN_TN) == 0
    lp = {
        'norm_mix_g': norm_mix_g, 'w_in': w_in, 'shift_w': shift_w,
        'w0_f': w0_f, 'w_up_f': w_up_f, 'w0_b': w0_b, 'w_up_b': w_up_b,
        'a0_f': a0_f, 'a_up_f': a_up_f, 'a0_b': a0_b, 'a_up_b': a_up_b,
        'g_up': g_up, 'k_k': k_k, 'k_a': k_a, 'r_k': r_k.reshape(r_k.shape[0], -1),
        'ln_x_g': ln_x_g, 'ln_x_b': ln_x_b,
        'pool_w': pool_w, 'pool_scale': pool_scale, 'w_out': w_out,
        'norm_x_g': norm_x_g, 'norm_mem_g': norm_mem_g, 'xq': xq, 'xk': xk, 'xv': xv, 'xo': xo,
        'norm_ffn_g': norm_ffn_g, 'ffn_w13': ffn_w13, 'ffn_w2': ffn_w2,
    }
    mem = jnp.concatenate([mem_prompt, mem_sample], axis=0)
    return _trunk((x_prompt, x_sample), mem, lp, norm_final_g)
```

```python
import functools
import math

import jax
import jax.numpy as jnp
from jax import lax
from jax.experimental import pallas as pl
from jax.experimental.pallas import tpu as pltpu

F32 = jnp.float32
BF16 = jnp.bfloat16

LANES = 128
SUBLANES = 8
VMEM_LIMIT_BYTES = 56 * 1024 * 1024
VMEM_LIMIT_BYTES_MAX = 60000 * 1024

HEAD_SIZE = 64
HEAD_SHIFT = 6
X_HEADS = 4
POOL_WINDOWS = (2, 4, 8, 16)
POOL_PAD = 16
GN_EPS = 64e-5
NORM_EPS = 1e-6
DECAY_SCALE = -math.exp(-0.5)
WKV_CHUNK = 64
WKV_SOLVE_GROUP = 32
WKV_UPDATE_GROUP = 32
MERGE_GROUP = 16
LORA_W = 128
GATE_LORA_PAD = 512
LORA_COLS = 1024
PROJ_TM = 1024
PROJ_TN = 512
FFN_TM = 2048
FFN_TN = 256
FFN_DOWN_TM = 512
FFN_DOWN_TN = 256
IN_TN = 512
IN_ROW_CHUNK = 512


def _params(*semantics):
    return pltpu.CompilerParams(dimension_semantics=semantics, vmem_limit_bytes=VMEM_LIMIT_BYTES)


def _rms(x, g):
    ms = jnp.mean(x * x, axis=-1, keepdims=True)
    return x * lax.rsqrt(ms + NORM_EPS) * g


def _rmsnorm_kernel(x_ref, g_ref, o_ref):
    o_ref[...] = _rms(x_ref[...], g_ref[...]).astype(o_ref.dtype)


def _rmsnorm(x, g, out_dtype, tm=256, row_block_offset=0, n_row_blocks=None):
    m, d = x.shape
    nb = m // tm if n_row_blocks is None else n_row_blocks
    return pl.pallas_call(
        _rmsnorm_kernel,
        grid=(nb,),
        in_specs=[pl.BlockSpec((tm, d), lambda i: (i + row_block_offset, 0)),
                  pl.BlockSpec((1, d), lambda i: (0, 0))],
        out_specs=pl.BlockSpec((tm, d), lambda i: (i, 0)),
        out_shape=jax.ShapeDtypeStruct((nb * tm, d), out_dtype),
        compiler_params=_params("parallel"),
        name="rmsnorm",
    )(x, g.reshape(1, d))


def _rmsnorm2_kernel(xa_ref, xb_ref, g_ref, o_ref, *, na):
    x = jnp.where(pl.program_id(0) < na, xa_ref[...], xb_ref[...])
    o_ref[...] = _rms(x, g_ref[...]).astype(o_ref.dtype)


def _rmsnorm2(xa, xb, g, out_dtype, tm=256):
    d = xa.shape[1]
    na, nb = xa.shape[0] // tm, xb.shape[0] // tm
    return pl.pallas_call(
        functools.partial(_rmsnorm2_kernel, na=na),
        grid=(na + nb,),
        in_specs=[pl.BlockSpec((tm, d), lambda i: (jnp.minimum(i, na - 1), 0)),
                  pl.BlockSpec((tm, d), lambda i: (jnp.maximum(i - na, 0), 0)),
                  pl.BlockSpec((1, d), lambda i: (0, 0))],
        out_specs=pl.BlockSpec((tm, d), lambda i: (i, 0)),
        out_shape=jax.ShapeDtypeStruct(((na + nb) * tm, d), out_dtype),
        compiler_params=_params("arbitrary"),
        name="rmsnorm2",
    )(xa, xb, g.reshape(1, d))


def _matmul_kernel(x_ref, w_ref, o_ref):
    o_ref[...] = jnp.dot(x_ref[...], w_ref[...], preferred_element_type=F32).astype(o_ref.dtype)


def _matmul_res_kernel(x_ref, w_ref, r_ref, o_ref):
    acc = jnp.dot(x_ref[...], w_ref[...], preferred_element_type=F32)
    o_ref[...] = (r_ref[...] + acc).astype(o_ref.dtype)


def _matmul(x, w, out_dtype, tm, tn, residual=None, name="matmul"):
    m, k = x.shape
    n = w.shape[1]
    in_specs = [pl.BlockSpec((tm, k), lambda i, j: (i, 0)),
                pl.BlockSpec((k, tn), lambda i, j: (0, j))]
    args = [x, w]
    body = _matmul_kernel
    if residual is not None:
        in_specs.append(pl.BlockSpec((tm, tn), lambda i, j: (i, j)))
        args.append(residual)
        body = _matmul_res_kernel
    return pl.pallas_call(
        body,
        grid=(m // tm, n // tn),
        in_specs=in_specs,
        out_specs=pl.BlockSpec((tm, tn), lambda i, j: (i, j)),
        out_shape=jax.ShapeDtypeStruct((m, n), out_dtype),
        compiler_params=_params("arbitrary", "arbitrary"),
        name=name,
    )(*args)


def _matmul_res_norm_kernel(x_ref, w_ref, r_ref, g_ref, o_ref):
    j = pl.program_id(1)
    tn = w_ref.shape[1]
    acc = jnp.dot(x_ref[...], w_ref[...], preferred_element_type=F32)
    o_ref[:, pl.ds(pl.multiple_of(j * tn, tn), tn)] = r_ref[...] + acc

    @pl.when(j == pl.num_programs(1) - 1)
    def _():
        o_ref[...] = _rms(o_ref[...], g_ref[...])


def _matmul_res_norm(x, w, residual, g, tm, tn, row_block_offset, n_row_blocks, name):
    k = x.shape[1]
    n = w.shape[1]
    off = row_block_offset
    return pl.pallas_call(
        _matmul_res_norm_kernel,
        grid=(n_row_blocks, n // tn),
        in_specs=[pl.BlockSpec((tm, k), lambda i, j: (i + off, 0)),
                  pl.BlockSpec((k, tn), lambda i, j: (0, j)),
                  pl.BlockSpec((tm, tn), lambda i, j: (i + off, j)),
                  pl.BlockSpec((1, n), lambda i, j: (0, 0))],
        out_specs=pl.BlockSpec((tm, n), lambda i, j: (i, 0)),
        out_shape=jax.ShapeDtypeStruct((n_row_blocks * tm, n), F32),
        compiler_params=pltpu.CompilerParams(dimension_semantics=("arbitrary", "arbitrary"),
                                             vmem_limit_bytes=VMEM_LIMIT_BYTES_MAX),
        name=name,
    )(x, w, residual, g.reshape(1, n))


def _in_proj_kernel(x_ref, w_ref, taps_ref, o_ref):
    t = x_ref.shape[0]
    rc = IN_ROW_CHUNK
    w = w_ref[...]
    taps = taps_ref[...]
    ridx = lax.broadcasted_iota(jnp.int32, (rc, w.shape[1]), 0)
    zs = [jnp.dot(x_ref[c * rc:(c + 1) * rc, :], w, preferred_element_type=F32) for c in range(t // rc)]
    zero_row = jnp.zeros((1, w.shape[1]), F32)
    for c, z in enumerate(zs):
        prev_row = zs[c - 1][rc - 1:rc, :] if c > 0 else zero_row
        next_row = zs[c + 1][0:1, :] if c + 1 < len(zs) else zero_row
        zm1 = jnp.where(ridx == 0, prev_row, pltpu.roll(z, 1, axis=0))
        zp1 = jnp.where(ridx == rc - 1, next_row, pltpu.roll(z, rc - 1, axis=0))
        o_ref[0, c * rc:(c + 1) * rc, :] = zm1 * taps[0:1, :] + z * taps[1:2, :] + zp1 * taps[2:3, :]


def _in_proj(xn, w, taps, bsz, t):
    k = xn.shape[1]
    n = w.shape[1]
    return pl.pallas_call(
        _in_proj_kernel,
        grid=(bsz, n // IN_TN),
        in_specs=[pl.BlockSpec((t, k), lambda b, j: (b, 0), pipeline_mode=pl.Buffered(1)),
                  pl.BlockSpec((k, IN_TN), lambda b, j: (0, j)),
                  pl.BlockSpec((3, IN_TN), lambda b, j: (0, j))],
        out_specs=pl.BlockSpec((1, t, IN_TN), lambda b, j: (b, 0, j)),
        out_shape=jax.ShapeDtypeStruct((bsz, t, n), F32),
        compiler_params=_params("arbitrary", "arbitrary"),
        name="in_proj",
    )(xn, w, taps)


def _lane_tiled(x, width):
    return jnp.concatenate([x] * (width // LANES), axis=1)


def _matmul_stats_kernel(x_ref, w_ref, *rest, na, inv_d):
    *r_refs, o_ref, ob_ref, sc_ref, ssq_ref = rest
    j = pl.program_id(1)
    acc = jnp.dot(x_ref[...], w_ref[...], preferred_element_type=F32)
    res = r_refs[0][...] if len(r_refs) == 1 else jnp.where(pl.program_id(0) < na, r_refs[0][...], r_refs[1][...])
    h = res + acc
    o_ref[...] = h
    ob_ref[...] = h.astype(BF16)
    hh = h * h
    part = hh[:, 0:LANES]
    for c in range(1, hh.shape[1] // LANES):
        part = part + hh[:, c * LANES:(c + 1) * LANES]

    @pl.when(j == 0)
    def _():
        ssq_ref[...] = part

    @pl.when(j > 0)
    def _():
        ssq_ref[...] += part

    @pl.when(j == pl.num_programs(1) - 1)
    def _():
        ms = jnp.sum(ssq_ref[...], axis=-1, keepdims=True) * inv_d
        sc_ref[...] = jnp.broadcast_to(lax.rsqrt(ms + NORM_EPS), sc_ref.shape)


def _matmul_stats(x, w, tm, tn, residual, name):
    m, k = x.shape
    n = w.shape[1]
    in_specs = [pl.BlockSpec((tm, k), lambda i, j: (i, 0)),
                pl.BlockSpec((k, tn), lambda i, j: (0, j))]
    na = 0
    if isinstance(residual, tuple):
        ra, rb = residual
        na = ra.shape[0] // tm
        in_specs += [pl.BlockSpec((tm, tn), lambda i, j: (jnp.minimum(i, na - 1), j)),
                     pl.BlockSpec((tm, tn), lambda i, j: (jnp.maximum(i - na, 0), j))]
        res_args = [ra, rb]
    else:
        in_specs.append(pl.BlockSpec((tm, tn), lambda i, j: (i, j)))
        res_args = [residual]
    return pl.pallas_call(
        functools.partial(_matmul_stats_kernel, na=na, inv_d=1.0 / n),
        grid=(m // tm, n // tn),
        in_specs=in_specs,
        out_specs=[pl.BlockSpec((tm, tn), lambda i, j: (i, j)),
                   pl.BlockSpec((tm, tn), lambda i, j: (i, j)),
                   pl.BlockSpec((tm, LANES), lambda i, j: (i, 0))],
        out_shape=[jax.ShapeDtypeStruct((m, n), F32), jax.ShapeDtypeStruct((m, n), BF16),
                   jax.ShapeDtypeStruct((m, LANES), F32)],
        scratch_shapes=[pltpu.VMEM((tm, LANES), F32)],
        compiler_params=_params("arbitrary", "arbitrary"),
        name=name,
    )(x, w, *res_args)


def _cast_weight(w_ref, wb_ref, g_ref):
    w = w_ref[...]
    if g_ref is not None:
        w = w * _lane_tiled(g_ref[...], w.shape[1])
    wb_ref[...] = w.astype(BF16)


def _wres_kernel(*refs, normed):
    if normed:
        x_ref, sc_ref, g_ref, w_ref, o_ref, wb_ref = refs
    else:
        x_ref, w_ref, o_ref, wb_ref = refs
        sc_ref = g_ref = None

    @pl.when(pl.program_id(1) == 0)
    def _():
        _cast_weight(w_ref, wb_ref, g_ref)

    acc = jnp.dot(x_ref[...], wb_ref[...], preferred_element_type=F32)
    if normed:
        acc = acc * _lane_tiled(sc_ref[...], acc.shape[1])
    o_ref[...] = acc.astype(o_ref.dtype)


def _matmul_wres(x, w, out_dtype, tm, tn, scale=None, gain=None, name="matmul_wres"):
    m, k = x.shape
    n = w.shape[1]
    normed = scale is not None
    in_specs = [pl.BlockSpec((tm, k), lambda j, i: (i, 0))]
    args = [x]
    if normed:
        in_specs += [pl.BlockSpec((tm, LANES), lambda j, i: (i, 0)),
                     pl.BlockSpec((k, LANES), lambda j, i: (0, 0))]
        args += [scale, gain]
    in_specs.append(pl.BlockSpec((k, tn), lambda j, i: (0, j)))
    args.append(w)
    return pl.pallas_call(
        functools.partial(_wres_kernel, normed=normed),
        grid=(n // tn, m // tm),
        in_specs=in_specs,
        out_specs=pl.BlockSpec((tm, tn), lambda j, i: (i, j)),
        out_shape=jax.ShapeDtypeStruct((m, n), out_dtype),
        scratch_shapes=[pltpu.VMEM((k, tn), BF16)],
        compiler_params=_params("arbitrary", "arbitrary"),
        name=name,
    )(*args)


def _swiglu_kernel(x_ref, sc_ref, wg_ref, wu_ref, o_ref):
    x = x_ref[...]
    sc = _lane_tiled(sc_ref[...], o_ref.shape[1])
    gate = jnp.dot(x, wg_ref[...], preferred_element_type=F32) * sc
    up = jnp.dot(x, wu_ref[...], preferred_element_type=F32) * sc
    o_ref[...] = (gate * jax.nn.sigmoid(gate) * up).astype(o_ref.dtype)


def _swiglu_up(x, scale, w13, hidden, tm, tn):
    m, k = x.shape
    nb = hidden // tn
    return pl.pallas_call(
        _swiglu_kernel,
        grid=(m // tm, nb),
        in_specs=[pl.BlockSpec((tm, k), lambda i, j: (i, 0)),
                  pl.BlockSpec((tm, LANES), lambda i, j: (i, 0)),
                  pl.BlockSpec((k, tn), lambda i, j: (0, j)),
                  pl.BlockSpec((k, tn), lambda i, j: (0, j + nb))],
        out_specs=pl.BlockSpec((tm, tn), lambda i, j: (i, j)),
        out_shape=jax.ShapeDtypeStruct((m, hidden), BF16),
        compiler_params=_params("arbitrary", "arbitrary"),
        name="swiglu_up",
    )(x, scale, w13, w13)


def _head_sum_matrix():
    r = lax.broadcasted_iota(jnp.int32, (LANES, LANES), 0) >> HEAD_SHIFT
    c = lax.broadcasted_iota(jnp.int32, (LANES, LANES), 1) >> HEAD_SHIFT
    return jnp.where(r == c, 1.0, 0.0).astype(BF16)


def _split2(x):
    hi = x.astype(BF16)
    return hi, (x - hi.astype(F32)).astype(BF16)


def _gmap(f, *lists):
    return [f(*xs) for xs in zip(*lists)]


def _wkv_rates_kernel(lo_ref, wup_ref, aup_ref, w0_ref, a0_ref, lw_ref, icl_ref):
    wl = w0_ref[0] + jnp.dot(jnp.tanh(lo_ref[:, 0:LORA_W]).astype(BF16), wup_ref[0], preferred_element_type=F32)
    lw_ref[0] = DECAY_SCALE * jax.nn.sigmoid(wl)
    icl_ref[0] = jax.nn.sigmoid(
        a0_ref[0] + jnp.dot(lo_ref[:, LORA_W:2 * LORA_W].astype(BF16), aup_ref[0], preferred_element_type=F32))


def _wkv_rates(zs2d, dm, w_up, a_up, w0, a0, tm=512):
    m, n = zs2d.shape
    lora_block = (n - LORA_COLS) // LORA_COLS
    sds = jax.ShapeDtypeStruct((2, m, dm), F32)
    return pl.pallas_call(
        _wkv_rates_kernel,
        grid=(2, m // tm),
        in_specs=[pl.BlockSpec((tm, LORA_COLS), lambda d, i: (i, lora_block)),
                  pl.BlockSpec((1, LORA_W, dm), lambda d, i: (d, 0, 0)),
                  pl.BlockSpec((1, LORA_W, dm), lambda d, i: (d, 0, 0)),
                  pl.BlockSpec((1, 1, dm), lambda d, i: (d, 0, 0)),
                  pl.BlockSpec((1, 1, dm), lambda d, i: (d, 0, 0))],
        out_specs=[pl.BlockSpec((1, tm, dm), lambda d, i: (d, i, 0)),
                   pl.BlockSpec((1, tm, dm), lambda d, i: (d, i, 0))],
        out_shape=[sds, sds],
        compiler_params=_params("arbitrary", "arbitrary"),
        name="wkv_rates",
    )(zs2d, w_up, a_up, w0, a0)


def _wkv_kernel(z_ref, lwin_ref, iclin_ref, kk_ref, ka_ref, rk_ref,
                y_ref, bo_ref, state_ref, cum_ref,
                tinv_ref, lrk_ref, rb_ref, lhs_ref, btk_ref, vb_ref, *, d_model):
    C = WKV_CHUNK
    d = pl.program_id(1)
    c = pl.program_id(2)
    sgn = 1 - 2 * d

    @pl.when(c == 0)
    def _():
        state_ref[...] = jnp.zeros_like(state_ref)

    r2c = lax.broadcasted_iota(jnp.int32, (C, 2 * C), 0)
    c2c = lax.broadcasted_iota(jnp.int32, (C, 2 * C), 1) & (C - 1)
    tri2 = jnp.where((r2c - c2c) * sgn >= 0, 1.0, 0.0).astype(BF16)
    lw_hi, lw_lo = _split2(lwin_ref[0, 0])
    cum_ref[...] = jnp.dot(tri2, jnp.concatenate([lw_hi, lw_lo], axis=0), preferred_element_type=F32)

    row = lax.broadcasted_iota(jnp.int32, (C, LANES), 0)
    col = lax.broadcasted_iota(jnp.int32, (C, LANES), 1)
    colh = col & (HEAD_SIZE - 1)
    order = (row - colh) * sgn
    strict = order > 0
    incl = order >= 0
    eye2 = row == colh
    lane_lo = col < HEAD_SIZE
    esum = _head_sum_matrix()
    rr = lax.broadcasted_iota(jnp.int32, (LANES, LANES), 0) >> HEAD_SHIFT
    cc = lax.broadcasted_iota(jnp.int32, (LANES, LANES), 1) >> HEAD_SHIFT
    blockdiag = rr == cc

    def bd(x):
        zero = jnp.zeros_like(x)
        return jnp.concatenate([jnp.where(lane_lo, x, zero), jnp.where(lane_lo, zero, x)], axis=0)

    def pmul(x, y):
        return jnp.dot(x.astype(BF16), bd(y.astype(BF16)), preferred_element_type=F32)

    nt_dims = (((1,), (1,)), ((), ()))
    tn_dims = (((0,), (0,)), ((), ()))

    def lane_tile(p, offset=0):
        return pl.ds(pl.multiple_of(offset + p * LANES, LANES), LANES)

    def head_sums(xs):
        stacked = jnp.concatenate([x.astype(BF16) for x in xs], axis=0)
        sums = jnp.dot(stacked, esum, preferred_element_type=F32)
        return [sums[j * C:(j + 1) * C] for j in range(len(xs))]

    def solve_body(g, carry):
        pairs = [g * WKV_SOLVE_GROUP + j for j in range(WKV_SOLVE_GROUP)]
        cols = [lane_tile(p) for p in pairs]
        r = [z_ref[0, :, cs] for cs in cols]
        k = [z_ref[0, :, lane_tile(p, d_model)] for p in pairs]
        v = [z_ref[0, :, lane_tile(p, 2 * d_model)] for p in pairs]
        icl = [iclin_ref[0, 0, :, cs] for cs in cols]
        lwp = [lwin_ref[0, 0, :, cs] for cs in cols]
        cum = [cum_ref[:, cs] for cs in cols]

        q = [ki * kk_ref[:, cs] for ki, cs in zip(k, cols)]
        n2 = head_sums([qi * qi for qi in q])
        kd = [ki * (1.0 + (ic - 1.0) * ka_ref[:, cs]) for ki, ic, cs in zip(k, icl, cols)]
        bsum = head_sums([ri * kdi * rk_ref[:, cs] for ri, kdi, cs in zip(r, kd, cols)])
        for cs, bs, vi in zip(cols, bsum, v):
            bo_ref[0, 0, :, cs] = (bs * vi).astype(bo_ref.dtype)

        kk = [qi * lax.rsqrt(jnp.maximum(ni, 1e-12)) for qi, ni in zip(q, n2)]
        b = _gmap(lambda x, ic: x * ic, kk, icl)
        e_out = [jnp.exp(-x) for x in cum]
        at = _gmap(lambda x, cm, lw_: (-x * jnp.exp(cm - lw_)).astype(BF16), kk, cum, lwp)
        rt = _gmap(lambda x, cm: (x * jnp.exp(cm)).astype(BF16), r, cum)
        bt = _gmap(lambda x, e: (x * e).astype(BF16), b, e_out)
        kt = _gmap(lambda x, e: (x * e).astype(BF16), kd, e_out)

        lhs = _gmap(lambda a_, r_: jnp.concatenate([a_, r_], axis=0), at, rt)
        rhs_t = _gmap(lambda b_, k_: jnp.concatenate([bd(b_), bd(k_)], axis=0), bt, kt)
        pmat = _gmap(lambda l_, r_: lax.dot_general(l_, r_, nt_dims, preferred_element_type=F32), lhs, rhs_t)
        for p, l_, b_, k_, v_ in zip(pairs, lhs, bt, kt, v):
            lhs_ref[p] = l_
            btk_ref[p] = jnp.concatenate([b_, k_], axis=0)
            vb_ref[p] = v_.astype(BF16)
        lab = [jnp.where(strict, x[:C, :LANES], 0.0).astype(BF16) for x in pmat]
        for p, x in zip(pairs, pmat):
            rb_ref[p] = jnp.where(incl, x[C:, :LANES], 0.0).astype(BF16)
            lrk_ref[p] = jnp.concatenate([jnp.where(strict, x[:C, LANES:], 0.0).astype(BF16),
                                          jnp.where(incl, x[C:, LANES:], 0.0).astype(BF16)], axis=0)

        zero_b = jnp.zeros((C, LANES), BF16)
        ident = jnp.where(eye2, 1.0, 0.0).astype(BF16)
        first = (row >> 1) == (colh >> 1)
        tinv = [ident + jnp.where(first, x, zero_b) for x in lab]
        s = 2
        while s < C:
            sh = s.bit_length() - 1
            level = ((row >> (sh + 1)) == (colh >> (sh + 1))) & ((row >> sh) != (colh >> sh))
            off = [jnp.where(level, x, zero_b) for x in lab]
            tmp = _gmap(pmul, tinv, off)
            upd_t = _gmap(pmul, tmp, tinv)
            tinv = _gmap(lambda t_, x: t_ + x.astype(BF16), tinv, upd_t)
            s *= 2
        for p, x in zip(pairs, tinv):
            tinv_ref[p] = x
        return carry

    def update_body(g, carry):
        pairs = [g * WKV_UPDATE_GROUP + j for j in range(WKV_UPDATE_GROUP)]
        cols = [lane_tile(p) for p in pairs]
        h = [state_ref[p] for p in pairs]
        vb = [vb_ref[p] for p in pairs]
        hs = [jnp.dot(lhs_ref[p], h_.astype(BF16), preferred_element_type=F32)
              for p, h_ in zip(pairs, h)]
        lrkv = [jnp.dot(lrk_ref[p], bd(v_), preferred_element_type=F32) for p, v_ in zip(pairs, vb)]
        rhs_u = _gmap(lambda h_, x: (h_[:C] + x[:C]).astype(BF16), hs, lrkv)
        u = [jnp.dot(tinv_ref[p], bd(x), preferred_element_type=F32) for p, x in zip(pairs, rhs_u)]
        ub = [x.astype(BF16) for x in u]
        rbu = [jnp.dot(rb_ref[p], bd(x), preferred_element_type=F32) for p, x in zip(pairs, ub)]
        for cs, h_, xv, xu in zip(cols, hs, lrkv, rbu):
            y_ref[0, 0, :, cs] = (h_[C:] + xv[C:] + xu).astype(y_ref.dtype)
        upd = [lax.dot_general(btk_ref[p], jnp.concatenate([u_, v_], axis=0), tn_dims,
                               preferred_element_type=F32) for p, u_, v_ in zip(pairs, ub, vb)]
        for p, cs, h_, x in zip(pairs, cols, h, upd):
            tot = jnp.sum(lwin_ref[0, 0, :, cs], axis=0, keepdims=True)
            decay_rows = jnp.broadcast_to(jnp.exp(tot), (LANES, LANES)).T
            state_ref[p] = decay_rows * (h_ + jnp.where(blockdiag, x, 0.0))
        return carry

    n_pairs = d_model // LANES
    lax.fori_loop(0, n_pairs // WKV_SOLVE_GROUP, solve_body, 0)
    lax.fori_loop(0, n_pairs // WKV_UPDATE_GROUP, update_body, 0)


def _wkv_scan(z, dm, lw, icl, k_k, k_a, r_k):
    bsz, t, n = z.shape
    C = WKV_CHUNK
    nc = t // C
    n_pairs = dm // LANES

    def tchunk(dd, cc):
        return jnp.where(dd == 0, cc, nc - 1 - cc)

    def const2(bb, dd, cc):
        return (0, 0)

    def out_map(bb, dd, cc):
        return (dd, bb, tchunk(dd, cc), 0)

    out_sds = jax.ShapeDtypeStruct((2, bsz, t, dm), BF16)
    kern = functools.partial(_wkv_kernel, d_model=dm)
    return pl.pallas_call(
        kern,
        grid=(bsz, 2, nc),
        in_specs=[
            pl.BlockSpec((1, C, 3 * dm), lambda bb, dd, cc: (bb, tchunk(dd, cc), 0)),
            pl.BlockSpec((1, 1, C, dm), out_map),
            pl.BlockSpec((1, 1, C, dm), out_map),
            pl.BlockSpec((1, dm), const2),
            pl.BlockSpec((1, dm), const2),
            pl.BlockSpec((1, dm), const2),
        ],
        out_specs=[pl.BlockSpec((1, 1, C, dm), out_map), pl.BlockSpec((1, 1, C, dm), out_map)],
        out_shape=[out_sds, out_sds],
        scratch_shapes=[pltpu.VMEM((n_pairs, LANES, LANES), F32), pltpu.VMEM((C, dm), F32),
                        pltpu.VMEM((n_pairs, C, LANES), BF16), pltpu.VMEM((n_pairs, 2 * C, LANES), BF16),
                        pltpu.VMEM((n_pairs, C, LANES), BF16), pltpu.VMEM((n_pairs, 2 * C, LANES), BF16),
                        pltpu.VMEM((n_pairs, 2 * C, LANES), BF16), pltpu.VMEM((n_pairs, C, LANES), BF16)],
        compiler_params=_params("arbitrary", "arbitrary", "arbitrary"),
        name="wkv_scan",
    )(z, lw, icl, k_k, k_a, r_k)


def _pool_kernel(p_ref, gate_ref, w_ref, scale_ref, o_ref, pad_ref, *, rows):
    g = pl.program_id(1)
    t, gi = p_ref.shape[1], p_ref.shape[2]
    zeros = jnp.zeros((POOL_PAD, gi), F32)
    pad_ref[0:POOL_PAD, :] = zeros
    pad_ref[POOL_PAD + t:POOL_PAD + t + POOL_PAD, :] = zeros
    pad_ref[POOL_PAD:POOL_PAD + t, :] = p_ref[0]
    w = w_ref[0]
    scale = scale_ref[...]

    for gidx, win in enumerate(POOL_WINDOWS):
        @pl.when(g == gidx)
        def _(win=win):
            half = win // 2

            def tile_body(i, carry):
                r0 = pl.multiple_of(i * rows, rows)
                n = rows + 2 * SUBLANES
                xt = pad_ref[pl.ds(r0 + POOL_PAD - SUBLANES, n), :]
                acc = xt
                step = 1
                while step < win:
                    acc = acc + pltpu.roll(acc, n - step, axis=0)
                    step *= 2
                if SUBLANES - half:
                    acc = pltpu.roll(acc, n - (SUBLANES - half), axis=0)
                acc = acc[0:rows]
                tt = r0 + lax.broadcasted_iota(jnp.int32, (rows, LANES), 0)
                cnt = (jnp.minimum(tt + (win - half), t) - jnp.maximum(tt - half, 0)).astype(F32)
                inv = 1.0 / cnt
                inv_full = jnp.concatenate([inv] * (gi // LANES), axis=1)
                dlt = acc * inv_full - xt[SUBLANES:SUBLANES + rows]
                out = jnp.dot(dlt.astype(BF16), w, preferred_element_type=F32) * scale
                gate = jax.nn.sigmoid(gate_ref[0, pl.ds(r0, rows), :])
                o_ref[0, pl.ds(r0, rows), :] = (gate * out).astype(o_ref.dtype)
                return carry

            lax.fori_loop(0, t // rows, tile_body, 0)


def _pool_branch(z, pool_col, gate_col, pool_w, pool_scale, rows=256):
    bsz, t, _ = z.shape
    ng, gi, go = pool_w.shape
    dm = ng * go
    kern = functools.partial(_pool_kernel, rows=rows)
    return pl.pallas_call(
        kern,
        grid=(bsz, ng),
        in_specs=[pl.BlockSpec((1, t, gi), lambda b, g: (b, 0, pool_col // gi + g)),
                  pl.BlockSpec((1, t, go), lambda b, g: (b, 0, gate_col // go + g)),
                  pl.BlockSpec((1, gi, go), lambda b, g: (g, 0, 0)),
                  pl.BlockSpec((1, go), lambda b, g: (0, g))],
        out_specs=pl.BlockSpec((1, t, go), lambda b, g: (b, 0, g)),
        out_shape=jax.ShapeDtypeStruct((bsz, t, dm), BF16),
        scratch_shapes=[pltpu.VMEM((t + 2 * POOL_PAD, gi), F32)],
        compiler_params=_params("arbitrary", "arbitrary"),
        name="pool_branch",
    )(z, z, pool_w, pool_scale)


def _merge_kernel(y_ref, bo_ref, lo_ref, gup_ref, gng_ref, gnb_ref, gate_ref, yb_ref, o_ref):
    gd_lo = 2 * LORA_W
    esum = _head_sum_matrix()
    inv_n = 1.0 / HEAD_SIZE
    gd = jax.nn.sigmoid(lo_ref[0, :, gd_lo:gd_lo + GATE_LORA_PAD]).astype(BF16)

    def group_body(gidx, carry):
        cols = [pl.ds(pl.multiple_of((gidx * MERGE_GROUP + j) * LANES, LANES), LANES) for j in range(MERGE_GROUP)]
        y = [y_ref[0, 0, :, cs].astype(F32) + y_ref[1, 0, :, cs].astype(F32) for cs in cols]
        mu = [jnp.dot(x.astype(BF16), esum, preferred_element_type=F32) * inv_n for x in y]
        g = [jnp.dot(gd, gup_ref[:, cs], preferred_element_type=F32) for cs in cols]
        yc = _gmap(lambda x, m: x - m, y, mu)
        var = [jnp.dot((x * x).astype(BF16), esum, preferred_element_type=F32) * inv_n for x in yc]
        for cs, x, vr, gi in zip(cols, yc, var, g):
            yn = x * lax.rsqrt(vr + GN_EPS) * gng_ref[:, cs] + gnb_ref[:, cs]
            yn = yn + bo_ref[0, 0, :, cs].astype(F32) + bo_ref[1, 0, :, cs].astype(F32)
            ya = jax.nn.sigmoid(gate_ref[0, :, cs]) * (yn * gi)
            o_ref[0, :, cs] = (ya + yb_ref[0, :, cs].astype(F32)).astype(o_ref.dtype)
        return carry

    lax.fori_loop(0, o_ref.shape[2] // (LANES * MERGE_GROUP), group_body, 0)


def _merge(y, bo, zs, zp, g_up, ln_g, ln_b, yb, tt=256):
    _, bsz, t, dm = y.shape
    n = zs.shape[-1]
    lora_block = (n - LORA_COLS) // LORA_COLS
    return pl.pallas_call(
        _merge_kernel,
        grid=(bsz, t // tt),
        in_specs=[pl.BlockSpec((2, 1, tt, dm), lambda b, i: (0, b, i, 0)),
                  pl.BlockSpec((2, 1, tt, dm), lambda b, i: (0, b, i, 0)),
                  pl.BlockSpec((1, tt, LORA_COLS), lambda b, i: (b, i, lora_block)),
                  pl.BlockSpec((GATE_LORA_PAD, dm), lambda b, i: (0, 0)),
                  pl.BlockSpec((1, dm), lambda b, i: (0, 0)),
                  pl.BlockSpec((1, dm), lambda b, i: (0, 0)),
                  pl.BlockSpec((1, tt, dm), lambda b, i: (b, i, 0)),
                  pl.BlockSpec((1, tt, dm), lambda b, i: (b, i, 0))],
        out_specs=pl.BlockSpec((1, tt, dm), lambda b, i: (b, i, 0)),
        out_shape=jax.ShapeDtypeStruct((bsz, t, dm), BF16),
        compiler_params=_params("arbitrary", "arbitrary"),
        name="wkv_merge",
    )(y, bo, zs, g_up, ln_g, ln_b, zp, yb)


def _xattn_kernel(q_ref, k_ref, v_ref, o_ref, *, head_dim):
    scale = head_dim ** -0.5
    nt_dims = (((1,), (1,)), ((), ()))
    for h in range(X_HEADS):
        cs = slice(h * head_dim, (h + 1) * head_dim)
        s = lax.dot_general(q_ref[0, :, cs], k_ref[0, :, cs], nt_dims, preferred_element_type=F32) * scale
        m = jnp.max(s, axis=-1, keepdims=True)
        e = jnp.exp(s - m)
        p = e / jnp.sum(e, axis=-1, keepdims=True)
        o_ref[0, :, cs] = jnp.dot(p.astype(BF16), v_ref[0, :, cs], preferred_element_type=F32).astype(o_ref.dtype)


def _xattn(q, k, v, tq=512):
    bsz, t, dm = q.shape
    m = k.shape[1]
    kern = functools.partial(_xattn_kernel, head_dim=dm // X_HEADS)
    return pl.pallas_call(
        kern,
        grid=(bsz, t // tq),
        in_specs=[pl.BlockSpec((1, tq, dm), lambda b, i: (b, i, 0)),
                  pl.BlockSpec((1, m, dm), lambda b, i: (b, 0, 0)),
                  pl.BlockSpec((1, m, dm), lambda b, i: (b, 0, 0))],
        out_specs=pl.BlockSpec((1, tq, dm), lambda b, i: (b, i, 0)),
        out_shape=jax.ShapeDtypeStruct((bsz, t, dm), BF16),
        compiler_params=_params("arbitrary", "arbitrary"),
        name="xattn",
    )(q, k, v)


def _pad_to(x, axis, size):
    pad = [(0, 0)] * x.ndim
    pad[axis] = (0, size - x.shape[axis])
    return jnp.pad(x, pad)


def _gain_tile(g):
    return jnp.broadcast_to(g[:, None], (g.shape[0], LANES))


def _in_proj_operands(p, dm):
    w_in, shift_w = p['w_in'], p['shift_w']
    gate_lora = p['g_up'].shape[0]
    c_rkv = 3 * dm
    c_lora = c_rkv + 2 * LORA_W + gate_lora
    pool_width = p['pool_w'].shape[0] * p['pool_w'].shape[1]
    c_pool = c_lora + pool_width
    wb = w_in.astype(BF16)
    w_shift = jnp.concatenate([wb[:, :c_rkv], _pad_to(wb[:, c_rkv:c_lora], 1, LORA_COLS)], axis=1)
    taps = jnp.concatenate([shift_w[:, :c_rkv], _pad_to(shift_w[:, c_rkv:c_lora], 1, LORA_COLS)], axis=1)
    w_plain = jnp.concatenate([wb[:, c_pool:], wb[:, c_lora:c_pool]], axis=1)
    return w_shift, taps, w_plain


def _trunk(xs, mem, lp, norm_final_g):
    t, dm = xs[0].shape[1:]
    rows = [x.shape[0] * t for x in xs]
    bsz = sum(x.shape[0] for x in xs)
    n_mem = mem.shape[1]
    m_tok = bsz * t
    hs = tuple(x.reshape(-1, dm) for x in xs)
    memf = mem.reshape(bsz * n_mem, dm)
    depth = lp['w_in'].shape[0]
    h = None
    for l in range(depth):
        p = {name: arr[l] for name, arr in lp.items()}
        w_shift, taps, w_plain = _in_proj_operands(p, dm)
        g_up = _pad_to(p['g_up'], 0, GATE_LORA_PAD).astype(BF16)

        if h is None:
            xn = _rmsnorm2(hs[0], hs[1], p['norm_mix_g'], BF16)
            res = hs
        else:
            xn = _rmsnorm(h, p['norm_mix_g'], BF16)
            res = h
        zs = _in_proj(xn, w_shift, taps, bsz, t)
        zp = _matmul(xn, w_plain, F32, PROJ_TM, 2 * PROJ_TN, name="in_plain").reshape(bsz, t, -1)

        w_up = jnp.stack([p['w_up_f'], p['w_up_b']]).astype(BF16)
        a_up = jnp.stack([p['a_up_f'], p['a_up_b']]).astype(BF16)
        w0 = jnp.stack([p['w0_f'], p['w0_b']]).reshape(2, 1, dm)
        a0 = jnp.stack([p['a0_f'], p['a0_b']]).reshape(2, 1, dm)
        lw, icl = _wkv_rates(zs.reshape(m_tok, -1), dm, w_up, a_up, w0, a0)
        y, bo = _wkv_scan(zs, dm, lw.reshape(2, bsz, t, dm), icl.reshape(2, bsz, t, dm),
                          p['k_k'].reshape(1, dm), p['k_a'].reshape(1, dm), p['r_k'].reshape(1, dm))
        yb = _pool_branch(zp, 2 * dm, dm, p['pool_w'].astype(BF16), p['pool_scale'].reshape(1, dm))
        merged = _merge(y, bo, zs, zp, g_up, p['ln_x_g'].reshape(1, dm), p['ln_x_b'].reshape(1, dm), yb)
        h, h_bf, scale = _matmul_stats(merged.reshape(m_tok, dm), p['w_out'].astype(BF16), PROJ_TM, PROJ_TN,
                                       residual=res, name="out_proj")

        mn = _rmsnorm(memf, p['norm_mem_g'], BF16)
        q = _matmul_wres(h_bf, p['xq'], BF16, PROJ_TM, PROJ_TN, scale=scale, gain=_gain_tile(p['norm_x_g']), name="xq")
        kx = _matmul_wres(mn, p['xk'], BF16, PROJ_TM, PROJ_TN, name="xk")
        vx = _matmul_wres(mn, p['xv'], BF16, PROJ_TM, PROJ_TN, name="xv")
        o = _xattn(q.reshape(bsz, t, dm), kx.reshape(bsz, n_mem, dm), vx.reshape(bsz, n_mem, dm))
        h, h_bf, scale = _matmul_stats(o.reshape(m_tok, dm), p['xo'].astype(BF16), PROJ_TM, PROJ_TN,
                                       residual=h, name="xo")

        hidden = p['ffn_w2'].shape[0]
        w13 = (p['norm_ffn_g'][:, None] * p['ffn_w13']).astype(BF16)
        act = _swiglu_up(h_bf, scale, w13, hidden, FFN_TM, FFN_TN)
        w2 = p['ffn_w2'].astype(BF16)
        if l + 1 < depth:
            h = _matmul(act, w2, F32, FFN_DOWN_TM, FFN_DOWN_TN, residual=h, name="ffn_down")

    outs, off = [], 0
    for x, nrow in zip(xs, rows):
        y = _matmul_res_norm(act, w2, h, norm_final_g, FFN_DOWN_TM, FFN_DOWN_TN,
                             off // FFN_DOWN_TM, nrow // FFN_DOWN_TM, name="ffn_down_norm")
        outs.append(y.reshape(x.shape))
        off += nrow
    return tuple(outs)


def kernel(x_prompt, x_sample, mem_prompt, mem_sample, norm_mix_g, w_in, shift_w, w0_f, w_up_f, w0_b, w_up_b, a0_f, a_up_f, a0_b, a_up_b, g_up, k_k, k_a, r_k, ln_x_g, ln_x_b, pool_w, pool_scale, w_out, norm_x_g, norm_mem_g, xq, xk, xv, xo, norm_ffn_g, ffn_w13, ffn_w2, norm_final_g):
    assert x_prompt.shape[1:] == x_sample.shape[1:] and mem_prompt.shape[1:] == mem_sample.shape[1:]
    t, dm = x_prompt.shape[1:]
    hidden = ffn_w2.shape[1]
    assert t % max(WKV_CHUNK, IN_ROW_CHUNK, 256) == 0 and dm % (LANES * WKV_SOLVE_GROUP) == 0
    assert all(x.shape[0] * t % max(PROJ_TM, FFN_TM) == 0 for x in (x_prompt, x_sample))
    assert hidden % FFN_TN == 0 and dm % max(PROJ_TN, FFN_DOWN_TN) == 0
    lp = {
        'norm_mix_g': norm_mix_g, 'w_in': w_in, 'shift_w': shift_w,
        'w0_f': w0_f, 'w_up_f': w_up_f, 'w0_b': w0_b, 'w_up_b': w_up_b,
        'a0_f': a0_f, 'a_up_f': a_up_f, 'a0_b': a0_b, 'a_up_b': a_up_b,
        'g_up': g_up, 'k_k': k_k, 'k_a': k_a, 'r_k': r_k.reshape(r_k.shape[0], -1),
        'ln_x_g': ln_x_g, 'ln_x_b': ln_x_b,
        'pool_w': pool_w, 'pool_scale': pool_scale, 'w_out': w_out,
        'norm_x_g': norm_x_g, 'norm_mem_g': norm_mem_g, 'xq': xq, 'xk': xk, 'xv': xv, 'xo': xo,
        'norm_ffn_g': norm_ffn_g, 'ffn_w13': ffn_w13, 'ffn_w2': ffn_w2,
    }
    mem = jnp.concatenate([mem_prompt, mem_sample], axis=0)
    return _trunk((x_prompt, x_sample), mem, lp, norm_final_g)
```

```python
import functools
import math

import jax
import jax.numpy as jnp
from jax import lax
from jax.experimental import pallas as pl
from jax.experimental.pallas import tpu as pltpu

F32 = jnp.float32
BF16 = jnp.bfloat16

LANES = 128
SUBLANES = 8
VMEM_LIMIT_BYTES = 56 * 1024 * 1024
VMEM_LIMIT_BYTES_MAX = 60000 * 1024

HEAD_SIZE = 64
HEAD_SHIFT = 6
X_HEADS = 4
POOL_WINDOWS = (2, 4, 8, 16)
POOL_PAD = 16
GN_EPS = 64e-5
NORM_EPS = 1e-6
DECAY_SCALE = -math.exp(-0.5)
WKV_CHUNK = 64
WKV_SOLVE_GROUP = 32
WKV_UPDATE_GROUP = 32
MERGE_GROUP = 16
LORA_W = 128
GATE_LORA_PAD = 512
LORA_COLS = 1024
PROJ_TM = 1024
PROJ_TN = 512
FFN_TM = 2048
FFN_TN = 256
FFN_DOWN_TM = 512
FFN_DOWN_TN = 256
IN_TN = 512
IN_ROW_CHUNK = 512


def _params(*semantics):
    return pltpu.CompilerParams(dimension_semantics=semantics, vmem_limit_bytes=VMEM_LIMIT_BYTES)


def _rms(x, g):
    ms = jnp.mean(x * x, axis=-1, keepdims=True)
    return x * lax.rsqrt(ms + NORM_EPS) * g


def _rmsnorm_kernel(x_ref, g_ref, o_ref):
    o_ref[...] = _rms(x_ref[...], g_ref[...]).astype(o_ref.dtype)


def _rmsnorm(x, g, out_dtype, tm=256, row_block_offset=0, n_row_blocks=None):
    m, d = x.shape
    nb = m // tm if n_row_blocks is None else n_row_blocks
    return pl.pallas_call(
        _rmsnorm_kernel,
        grid=(nb,),
        in_specs=[pl.BlockSpec((tm, d), lambda i: (i + row_block_offset, 0)),
                  pl.BlockSpec((1, d), lambda i: (0, 0))],
        out_specs=pl.BlockSpec((tm, d), lambda i: (i, 0)),
        out_shape=jax.ShapeDtypeStruct((nb * tm, d), out_dtype),
        compiler_params=_params("parallel"),
        name="rmsnorm",
    )(x, g.reshape(1, d))


def _rmsnorm2_kernel(xa_ref, xb_ref, g_ref, o_ref, *, na):
    x = jnp.where(pl.program_id(0) < na, xa_ref[...], xb_ref[...])
    o_ref[...] = _rms(x, g_ref[...]).astype(o_ref.dtype)


def _rmsnorm2(xa, xb, g, out_dtype, tm=256):
    d = xa.shape[1]
    na, nb = xa.shape[0] // tm, xb.shape[0] // tm
    return pl.pallas_call(
        functools.partial(_rmsnorm2_kernel, na=na),
        grid=(na + nb,),
        in_specs=[pl.BlockSpec((tm, d), lambda i: (jnp.minimum(i, na - 1), 0)),
                  pl.BlockSpec((tm, d), lambda i: (jnp.maximum(i - na, 0), 0)),
                  pl.BlockSpec((1, d), lambda i: (0, 0))],
        out_specs=pl.BlockSpec((tm, d), lambda i: (i, 0)),
        out_shape=jax.ShapeDtypeStruct(((na + nb) * tm, d), out_dtype),
        compiler_params=_params("arbitrary"),
        name="rmsnorm2",
    )(xa, xb, g.reshape(1, d))


def _matmul_kernel(x_ref, w_ref, o_ref):
    o_ref[...] = jnp.dot(x_ref[...], w_ref[...], preferred_element_type=F32).astype(o_ref.dtype)


def _matmul_res_kernel(x_ref, w_ref, r_ref, o_ref):
    acc = jnp.dot(x_ref[...], w_ref[...], preferred_element_type=F32)
    o_ref[...] = (r_ref[...] + acc).astype(o_ref.dtype)


def _matmul(x, w, out_dtype, tm, tn, residual=None, name="matmul"):
    m, k = x.shape
    n = w.shape[1]
    in_specs = [pl.BlockSpec((tm, k), lambda i, j: (i, 0)),
                pl.BlockSpec((k, tn), lambda i, j: (0, j))]
    args = [x, w]
    body = _matmul_kernel
    if residual is not None:
        in_specs.append(pl.BlockSpec((tm, tn), lambda i, j: (i, j)))
        args.append(residual)
        body = _matmul_res_kernel
    return pl.pallas_call(
        body,
        grid=(m // tm, n // tn),
        in_specs=in_specs,
        out_specs=pl.BlockSpec((tm, tn), lambda i, j: (i, j)),
        out_shape=jax.ShapeDtypeStruct((m, n), out_dtype),
        compiler_params=_params("arbitrary", "arbitrary"),
        name=name,
    )(*args)


def _matmul_res_norm_kernel(x_ref, w_ref, r_ref, g_ref, o_ref):
    j = pl.program_id(1)
    tn = w_ref.shape[1]
    acc = jnp.dot(x_ref[...], w_ref[...], preferred_element_type=F32)
    o_ref[:, pl.ds(pl.multiple_of(j * tn, tn), tn)] = r_ref[...] + acc

    @pl.when(j == pl.num_programs(1) - 1)
    def _():
        o_ref[...] = _rms(o_ref[...], g_ref[...])


def _matmul_res_norm(x, w, residual, g, tm, tn, row_block_offset, n_row_blocks, name):
    k = x.shape[1]
    n = w.shape[1]
    off = row_block_offset
    return pl.pallas_call(
        _matmul_res_norm_kernel,
        grid=(n_row_blocks, n // tn),
        in_specs=[pl.BlockSpec((tm, k), lambda i, j: (i + off, 0)),
                  pl.BlockSpec((k, tn), lambda i, j: (0, j)),
                  pl.BlockSpec((tm, tn), lambda i, j: (i + off, j)),
                  pl.BlockSpec((1, n), lambda i, j: (0, 0))],
        out_specs=pl.BlockSpec((tm, n), lambda i, j: (i, 0)),
        out_shape=jax.ShapeDtypeStruct((n_row_blocks * tm, n), F32),
        compiler_params=pltpu.CompilerParams(dimension_semantics=("arbitrary", "arbitrary"),
                                             vmem_limit_bytes=VMEM_LIMIT_BYTES_MAX),
        name=name,
    )(x, w, residual, g.reshape(1, n))


def _in_proj_kernel(x_ref, w_ref, taps_ref, o_ref):
    t = x_ref.shape[0]
    rc = IN_ROW_CHUNK
    w = w_ref[...]
    taps = taps_ref[...]
    ridx = lax.broadcasted_iota(jnp.int32, (rc, w.shape[1]), 0)
    zs = [jnp.dot(x_ref[c * rc:(c + 1) * rc, :], w, preferred_element_type=F32) for c in range(t // rc)]
    zero_row = jnp.zeros((1, w.shape[1]), F32)
    for c, z in enumerate(zs):
        prev_row = zs[c - 1][rc - 1:rc, :] if c > 0 else zero_row
        next_row = zs[c + 1][0:1, :] if c + 1 < len(zs) else zero_row
        zm1 = jnp.where(ridx == 0, prev_row, pltpu.roll(z, 1, axis=0))
        zp1 = jnp.where(ridx == rc - 1, next_row, pltpu.roll(z, rc - 1, axis=0))
        o_ref[0, c * rc:(c + 1) * rc, :] = zm1 * taps[0:1, :] + z * taps[1:2, :] + zp1 * taps[2:3, :]


def _in_proj(xn, w, taps, bsz, t):
    k = xn.shape[1]
    n = w.shape[1]
    return pl.pallas_call(
        _in_proj_kernel,
        grid=(bsz, n // IN_TN),
        in_specs=[pl.BlockSpec((t, k), lambda b, j: (b, 0), pipeline_mode=pl.Buffered(1)),
                  pl.BlockSpec((k, IN_TN), lambda b, j: (0, j)),
                  pl.BlockSpec((3, IN_TN), lambda b, j: (0, j))],
        out_specs=pl.BlockSpec((1, t, IN_TN), lambda b, j: (b, 0, j)),
        out_shape=jax.ShapeDtypeStruct((bsz, t, n), F32),
        compiler_params=_params("arbitrary", "arbitrary"),
        name="in_proj",
    )(xn, w, taps)


def _lane_tiled(x, width):
    return jnp.concatenate([x] * (width // LANES), axis=1)


def _matmul_stats_kernel(x_ref, w_ref, *rest, na, inv_d):
    *r_refs, o_ref, ob_ref, sc_ref, ssq_ref = rest
    j = pl.program_id(1)
    acc = jnp.dot(x_ref[...], w_ref[...], preferred_element_type=F32)
    res = r_refs[0][...] if len(r_refs) == 1 else jnp.where(pl.program_id(0) < na, r_refs[0][...], r_refs[1][...])
    h = res + acc
    o_ref[...] = h
    ob_ref[...] = h.astype(BF16)
    hh = h * h
    part = hh[:, 0:LANES]
    for c in range(1, hh.shape[1] // LANES):
        part = part + hh[:, c * LANES:(c + 1) * LANES]

    @pl.when(j == 0)
    def _():
        ssq_ref[...] = part

    @pl.when(j > 0)
    def _():
        ssq_ref[...] += part

    @pl.when(j == pl.num_programs(1) - 1)
    def _():
        ms = jnp.sum(ssq_ref[...], axis=-1, keepdims=True) * inv_d
        sc_ref[...] = jnp.broadcast_to(lax.rsqrt(ms + NORM_EPS), sc_ref.shape)


def _matmul_stats(x, w, tm, tn, residual, name):
    m, k = x.shape
    n = w.shape[1]
    in_specs = [pl.BlockSpec((tm, k), lambda i, j: (i, 0)),
                pl.BlockSpec((k, tn), lambda i, j: (0, j))]
    na = 0
    if isinstance(residual, tuple):
        ra, rb = residual
        na = ra.shape[0] // tm
        in_specs += [pl.BlockSpec((tm, tn), lambda i, j: (jnp.minimum(i, na - 1), j)),
                     pl.BlockSpec((tm, tn), lambda i, j: (jnp.maximum(i - na, 0), j))]
        res_args = [ra, rb]
    else:
        in_specs.append(pl.BlockSpec((tm, tn), lambda i, j: (i, j)))
        res_args = [residual]
    return pl.pallas_call(
        functools.partial(_matmul_stats_kernel, na=na, inv_d=1.0 / n),
        grid=(m // tm, n // tn),
        in_specs=in_specs,
        out_specs=[pl.BlockSpec((tm, tn), lambda i, j: (i, j)),
                   pl.BlockSpec((tm, tn), lambda i, j: (i, j)),
                   pl.BlockSpec((tm, LANES), lambda i, j: (i, 0))],
        out_shape=[jax.ShapeDtypeStruct((m, n), F32), jax.ShapeDtypeStruct((m, n), BF16),
                   jax.ShapeDtypeStruct((m, LANES), F32)],
        scratch_shapes=[pltpu.VMEM((tm, LANES), F32)],
        compiler_params=_params("arbitrary", "arbitrary"),
        name=name,
    )(x, w, *res_args)


def _cast_weight(w_ref, wb_ref, g_ref):
    w = w_ref[...]
    if g_ref is not None:
        w = w * _lane_tiled(g_ref[...], w.shape[1])
    wb_ref[...] = w.astype(BF16)


def _wres_kernel(*refs, normed):
    if normed:
        x_ref, sc_ref, g_ref, w_ref, o_ref, wb_ref = refs
    else:
        x_ref, w_ref, o_ref, wb_ref = refs
        sc_ref = g_ref = None

    @pl.when(pl.program_id(1) == 0)
    def _():
        _cast_weight(w_ref, wb_ref, g_ref)

    acc = jnp.dot(x_ref[...], wb_ref[...], preferred_element_type=F32)
    if normed:
        acc = acc * _lane_tiled(sc_ref[...], acc.shape[1])
    o_ref[...] = acc.astype(o_ref.dtype)


def _matmul_wres(x, w, out_dtype, tm, tn, scale=None, gain=None, name="matmul_wres"):
    m, k = x.shape
    n = w.shape[1]
    normed = scale is not None
    in_specs = [pl.BlockSpec((tm, k), lambda j, i: (i, 0))]
    args = [x]
    if normed:
        in_specs += [pl.BlockSpec((tm, LANES), lambda j, i: (i, 0)),
                     pl.BlockSpec((k, LANES), lambda j, i: (0, 0))]
        args += [scale, gain]
    in_specs.append(pl.BlockSpec((k, tn), lambda j, i: (0, j)))
    args.append(w)
    return pl.pallas_call(
        functools.partial(_wres_kernel, normed=normed),
        grid=(n // tn, m // tm),
        in_specs=in_specs,
        out_specs=pl.BlockSpec((tm, tn), lambda j, i: (i, j)),
        out_shape=jax.ShapeDtypeStruct((m, n), out_dtype),
        scratch_shapes=[pltpu.VMEM((k, tn), BF16)],
        compiler_params=_params("arbitrary", "arbitrary"),
        name=name,
    )(*args)


def _swiglu_kernel(x_ref, sc_ref, wg_ref, wu_ref, o_ref):
    x = x_ref[...]
    sc = _lane_tiled(sc_ref[...], o_ref.shape[1])
    gate = jnp.dot(x, wg_ref[...], preferred_element_type=F32) * sc
    up = jnp.dot(x, wu_ref[...], preferred_element_type=F32) * sc
    o_ref[...] = (gate * jax.nn.sigmoid(gate) * up).astype(o_ref.dtype)


def _swiglu_up(x, scale, w13, hidden, tm, tn):
    m, k = x.shape
    nb = hidden // tn
    return pl.pallas_call(
        _swiglu_kernel,
        grid=(m // tm, nb),
        in_specs=[pl.BlockSpec((tm, k), lambda i, j: (i, 0)),
                  pl.BlockSpec((tm, LANES), lambda i, j: (i, 0)),
                  pl.BlockSpec((k, tn), lambda i, j: (0, j)),
                  pl.BlockSpec((k, tn), lambda i, j: (0, j + nb))],
        out_specs=pl.BlockSpec((tm, tn), lambda i, j: (i, j)),
        out_shape=jax.ShapeDtypeStruct((m, hidden), BF16),
        compiler_params=_params("arbitrary", "arbitrary"),
        name="swiglu_up",
    )(x, scale, w13, w13)


def _head_sum_matrix():
    r = lax.broadcasted_iota(jnp.int32, (LANES, LANES), 0) >> HEAD_SHIFT
    c = lax.broadcasted_iota(jnp.int32, (LANES, LANES), 1) >> HEAD_SHIFT
    return jnp.where(r == c, 1.0, 0.0).astype(BF16)


def _split2(x):
    hi = x.astype(BF16)
    return hi, (x - hi.astype(F32)).astype(BF16)


def _gmap(f, *lists):
    return [f(*xs) for xs in zip(*lists)]


def _wkv_rates_kernel(lo_ref, wup_ref, aup_ref, w0_ref, a0_ref, lw_ref, icl_ref):
    wl = w0_ref[0] + jnp.dot(jnp.tanh(lo_ref[:, 0:LORA_W]).astype(BF16), wup_ref[0], preferred_element_type=F32)
    lw_ref[0] = DECAY_SCALE * jax.nn.sigmoid(wl)
    icl_ref[0] = jax.nn.sigmoid(
        a0_ref[0] + jnp.dot(lo_ref[:, LORA_W:2 * LORA_W].astype(BF16), aup_ref[0], preferred_element_type=F32)
    ).astype(icl_ref.dtype)


def _wkv_rates(zs2d, dm, w_up, a_up, w0, a0, tm=512):
    m, n = zs2d.shape
    lora_block = (n - LORA_COLS) // LORA_COLS
    return pl.pallas_call(
        _wkv_rates_kernel,
        grid=(2, m // tm),
        in_specs=[pl.BlockSpec((tm, LORA_COLS), lambda d, i: (i, lora_block)),
                  pl.BlockSpec((1, LORA_W, dm), lambda d, i: (d, 0, 0)),
                  pl.BlockSpec((1, LORA_W, dm), lambda d, i: (d, 0, 0)),
                  pl.BlockSpec((1, 1, dm), lambda d, i: (d, 0, 0)),
                  pl.BlockSpec((1, 1, dm), lambda d, i: (d, 0, 0))],
        out_specs=[pl.BlockSpec((1, tm, dm), lambda d, i: (d, i, 0)),
                   pl.BlockSpec((1, tm, dm), lambda d, i: (d, i, 0))],
        out_shape=[jax.ShapeDtypeStruct((2, m, dm), F32), jax.ShapeDtypeStruct((2, m, dm), BF16)],
        compiler_params=_params("arbitrary", "arbitrary"),
        name="wkv_rates",
    )(zs2d, w_up, a_up, w0, a0)


def _wkv_kernel(z_ref, lwin_ref, iclin_ref, kk_ref, ka_ref, rk_ref,
                y_ref, bo_ref, state_ref, cum_ref,
                tinv_ref, lrk_ref, rb_ref, lhs_ref, btk_ref, vb_ref, *, d_model):
    C = WKV_CHUNK
    d = pl.program_id(1)
    c = pl.program_id(2)
    sgn = 1 - 2 * d

    @pl.when(c == 0)
    def _():
        state_ref[...] = jnp.zeros_like(state_ref)

    r2c = lax.broadcasted_iota(jnp.int32, (C, 2 * C), 0)
    c2c = lax.broadcasted_iota(jnp.int32, (C, 2 * C), 1) & (C - 1)
    tri2 = jnp.where((r2c - c2c) * sgn >= 0, 1.0, 0.0).astype(BF16)
    lw_hi, lw_lo = _split2(lwin_ref[0, 0])
    cum_ref[...] = jnp.dot(tri2, jnp.concatenate([lw_hi, lw_lo], axis=0), preferred_element_type=F32)

    row = lax.broadcasted_iota(jnp.int32, (C, LANES), 0)
    col = lax.broadcasted_iota(jnp.int32, (C, LANES), 1)
    colh = col & (HEAD_SIZE - 1)
    order = (row - colh) * sgn
    strict = order > 0
    incl = order >= 0
    eye2 = row == colh
    lane_lo = col < HEAD_SIZE
    esum = _head_sum_matrix()
    rr = lax.broadcasted_iota(jnp.int32, (LANES, LANES), 0) >> HEAD_SHIFT
    cc = lax.broadcasted_iota(jnp.int32, (LANES, LANES), 1) >> HEAD_SHIFT
    blockdiag = rr == cc

    def bd(x):
        zero = jnp.zeros_like(x)
        return jnp.concatenate([jnp.where(lane_lo, x, zero), jnp.where(lane_lo, zero, x)], axis=0)

    def pmul(x, y):
        return jnp.dot(x.astype(BF16), bd(y.astype(BF16)), preferred_element_type=F32)

    nt_dims = (((1,), (1,)), ((), ()))
    tn_dims = (((0,), (0,)), ((), ()))

    def lane_tile(p, offset=0):
        return pl.ds(pl.multiple_of(offset + p * LANES, LANES), LANES)

    def head_sums(xs):
        stacked = jnp.concatenate([x.astype(BF16) for x in xs], axis=0)
        sums = jnp.dot(stacked, esum, preferred_element_type=F32)
        return [sums[j * C:(j + 1) * C] for j in range(len(xs))]

    def solve_body(g, carry):
        pairs = [g * WKV_SOLVE_GROUP + j for j in range(WKV_SOLVE_GROUP)]
        cols = [lane_tile(p) for p in pairs]
        r = [z_ref[0, :, cs] for cs in cols]
        k = [z_ref[0, :, lane_tile(p, d_model)] for p in pairs]
        v = [z_ref[0, :, lane_tile(p, 2 * d_model)] for p in pairs]
        icl = [iclin_ref[0, 0, :, cs].astype(F32) for cs in cols]
        lwp = [lwin_ref[0, 0, :, cs] for cs in cols]
        cum = [cum_ref[:, cs] for cs in cols]

        q = [ki * kk_ref[:, cs] for ki, cs in zip(k, cols)]
        n2 = head_sums([qi * qi for qi in q])
        kd = [ki * (1.0 + (ic - 1.0) * ka_ref[:, cs]) for ki, ic, cs in zip(k, icl, cols)]
        bsum = head_sums([ri * kdi * rk_ref[:, cs] for ri, kdi, cs in zip(r, kd, cols)])
        for cs, bs, vi in zip(cols, bsum, v):
            bo_ref[0, 0, :, cs] = (bs * vi).astype(bo_ref.dtype)

        kk = [qi * lax.rsqrt(jnp.maximum(ni, 1e-12)) for qi, ni in zip(q, n2)]
        b = _gmap(lambda x, ic: x * ic, kk, icl)
        e_out = [jnp.exp(-x) for x in cum]
        at = _gmap(lambda x, cm, lw_: (-x * jnp.exp(cm - lw_)).astype(BF16), kk, cum, lwp)
        rt = _gmap(lambda x, cm: (x * jnp.exp(cm)).astype(BF16), r, cum)
        bt = _gmap(lambda x, e: (x * e).astype(BF16), b, e_out)
        kt = _gmap(lambda x, e: (x * e).astype(BF16), kd, e_out)

        lhs = _gmap(lambda a_, r_: jnp.concatenate([a_, r_], axis=0), at, rt)
        rhs_t = _gmap(lambda b_, k_: jnp.concatenate([bd(b_), bd(k_)], axis=0), bt, kt)
        pmat = _gmap(lambda l_, r_: lax.dot_general(l_, r_, nt_dims, preferred_element_type=F32), lhs, rhs_t)
        for p, l_, b_, k_, v_ in zip(pairs, lhs, bt, kt, v):
            lhs_ref[p] = l_
            btk_ref[p] = jnp.concatenate([b_, k_], axis=0)
            vb_ref[p] = v_.astype(BF16)
        lab = [jnp.where(strict, x[:C, :LANES], 0.0).astype(BF16) for x in pmat]
        for p, x in zip(pairs, pmat):
            rb_ref[p] = jnp.where(incl, x[C:, :LANES], 0.0).astype(BF16)
            lrk_ref[p] = jnp.concatenate([jnp.where(strict, x[:C, LANES:], 0.0).astype(BF16),
                                          jnp.where(incl, x[C:, LANES:], 0.0).astype(BF16)], axis=0)

        zero_b = jnp.zeros((C, LANES), BF16)
        ident = jnp.where(eye2, 1.0, 0.0).astype(BF16)
        first = (row >> 1) == (colh >> 1)
        tinv = [ident + jnp.where(first, x, zero_b) for x in lab]
        s = 2
        while s < C:
            sh = s.bit_length() - 1
            level = ((row >> (sh + 1)) == (colh >> (sh + 1))) & ((row >> sh) != (colh >> sh))
            off = [jnp.where(level, x, zero_b) for x in lab]
            tmp = _gmap(pmul, tinv, off)
            upd_t = _gmap(pmul, tmp, tinv)
            tinv = _gmap(lambda t_, x: t_ + x.astype(BF16), tinv, upd_t)
            s *= 2
        for p, x in zip(pairs, tinv):
            tinv_ref[p] = x
        return carry

    def update_body(g, carry):
        pairs = [g * WKV_UPDATE_GROUP + j for j in range(WKV_UPDATE_GROUP)]
        cols = [lane_tile(p) for p in pairs]
        h = [state_ref[p] for p in pairs]
        vb = [vb_ref[p] for p in pairs]
        hs = [jnp.dot(lhs_ref[p], h_.astype(BF16), preferred_element_type=F32)
              for p, h_ in zip(pairs, h)]
        lrkv = [jnp.dot(lrk_ref[p], bd(v_), preferred_element_type=F32) for p, v_ in zip(pairs, vb)]
        rhs_u = _gmap(lambda h_, x: (h_[:C] + x[:C]).astype(BF16), hs, lrkv)
        u = [jnp.dot(tinv_ref[p], bd(x), preferred_element_type=F32) for p, x in zip(pairs, rhs_u)]
        ub = [x.astype(BF16) for x in u]
        rbu = [jnp.dot(rb_ref[p], bd(x), preferred_element_type=F32) for p, x in zip(pairs, ub)]
        for cs, h_, xv, xu in zip(cols, hs, lrkv, rbu):
            y_ref[0, 0, :, cs] = (h_[C:] + xv[C:] + xu).astype(y_ref.dtype)
        upd = [lax.dot_general(btk_ref[p], jnp.concatenate([u_, v_], axis=0), tn_dims,
                               preferred_element_type=F32) for p, u_, v_ in zip(pairs, ub, vb)]
        for p, cs, h_, x in zip(pairs, cols, h, upd):
            tot = jnp.sum(lwin_ref[0, 0, :, cs], axis=0, keepdims=True)
            decay_rows = jnp.broadcast_to(jnp.exp(tot), (LANES, LANES)).T
            state_ref[p] = decay_rows * (h_ + jnp.where(blockdiag, x, 0.0))
        return carry

    n_pairs = d_model // LANES
    lax.fori_loop(0, n_pairs // WKV_SOLVE_GROUP, solve_body, 0)
    lax.fori_loop(0, n_pairs // WKV_UPDATE_GROUP, update_body, 0)


def _wkv_scan(z, dm, lw, icl, k_k, k_a, r_k):
    bsz, t, n = z.shape
    C = WKV_CHUNK
    nc = t // C
    n_pairs = dm // LANES

    def tchunk(dd, cc):
        return jnp.where(dd == 0, cc, nc - 1 - cc)

    def const2(bb, dd, cc):
        return (0, 0)

    def out_map(bb, dd, cc):
        return (dd, bb, tchunk(dd, cc), 0)

    out_sds = jax.ShapeDtypeStruct((2, bsz, t, dm), BF16)
    kern = functools.partial(_wkv_kernel, d_model=dm)
    return pl.pallas_call(
        kern,
        grid=(bsz, 2, nc),
        in_specs=[
            pl.BlockSpec((1, C, 3 * dm), lambda bb, dd, cc: (bb, tchunk(dd, cc), 0)),
            pl.BlockSpec((1, 1, C, dm), out_map),
            pl.BlockSpec((1, 1, C, dm), out_map),
            pl.BlockSpec((1, dm), const2),
            pl.BlockSpec((1, dm), const2),
            pl.BlockSpec((1, dm), const2),
        ],
        out_specs=[pl.BlockSpec((1, 1, C, dm), out_map), pl.BlockSpec((1, 1, C, dm), out_map)],
        out_shape=[out_sds, out_sds],
        scratch_shapes=[pltpu.VMEM((n_pairs, LANES, LANES), F32), pltpu.VMEM((C, dm), F32),
                        pltpu.VMEM((n_pairs, C, LANES), BF16), pltpu.VMEM((n_pairs, 2 * C, LANES), BF16),
                        pltpu.VMEM((n_pairs, C, LANES), BF16), pltpu.VMEM((n_pairs, 2 * C, LANES), BF16),
                        pltpu.VMEM((n_pairs, 2 * C, LANES), BF16), pltpu.VMEM((n_pairs, C, LANES), BF16)],
        compiler_params=_params("arbitrary", "arbitrary", "arbitrary"),
        name="wkv_scan",
    )(z, lw, icl, k_k, k_a, r_k)


def _pool_kernel(p_ref, gate_ref, w_ref, scale_ref, o_ref, pad_ref, *, rows):
    g = pl.program_id(1)
    t, gi = p_ref.shape[1], p_ref.shape[2]
    zeros = jnp.zeros((POOL_PAD, gi), F32)
    pad_ref[0:POOL_PAD, :] = zeros
    pad_ref[POOL_PAD + t:POOL_PAD + t + POOL_PAD, :] = zeros
    pad_ref[POOL_PAD:POOL_PAD + t, :] = p_ref[0]
    w = w_ref[0]
    scale = scale_ref[...]

    for gidx, win in enumerate(POOL_WINDOWS):
        @pl.when(g == gidx)
        def _(win=win):
            half = win // 2

            def tile_body(i, carry):
                r0 = pl.multiple_of(i * rows, rows)
                n = rows + 2 * SUBLANES
                xt = pad_ref[pl.ds(r0 + POOL_PAD - SUBLANES, n), :]
                acc = xt
                step = 1
                while step < win:
                    acc = acc + pltpu.roll(acc, n - step, axis=0)
                    step *= 2
                if SUBLANES - half:
                    acc = pltpu.roll(acc, n - (SUBLANES - half), axis=0)
                acc = acc[0:rows]
                tt = r0 + lax.broadcasted_iota(jnp.int32, (rows, LANES), 0)
                cnt = (jnp.minimum(tt + (win - half), t) - jnp.maximum(tt - half, 0)).astype(F32)
                inv = 1.0 / cnt
                inv_full = jnp.concatenate([inv] * (gi // LANES), axis=1)
                dlt = acc * inv_full - xt[SUBLANES:SUBLANES + rows]
                out = jnp.dot(dlt.astype(BF16), w, preferred_element_type=F32) * scale
                gate = jax.nn.sigmoid(gate_ref[0, pl.ds(r0, rows), :])
                o_ref[0, pl.ds(r0, rows), :] = (gate * out).astype(o_ref.dtype)
                return carry

            lax.fori_loop(0, t // rows, tile_body, 0)


def _pool_branch(z, pool_col, gate_col, pool_w, pool_scale, rows=256):
    bsz, t, _ = z.shape
    ng, gi, go = pool_w.shape
    dm = ng * go
    kern = functools.partial(_pool_kernel, rows=rows)
    return pl.pallas_call(
        kern,
        grid=(bsz, ng),
        in_specs=[pl.BlockSpec((1, t, gi), lambda b, g: (b, 0, pool_col // gi + g)),
                  pl.BlockSpec((1, t, go), lambda b, g: (b, 0, gate_col // go + g)),
                  pl.BlockSpec((1, gi, go), lambda b, g: (g, 0, 0)),
                  pl.BlockSpec((1, go), lambda b, g: (0, g))],
        out_specs=pl.BlockSpec((1, t, go), lambda b, g: (b, 0, g)),
        out_shape=jax.ShapeDtypeStruct((bsz, t, dm), BF16),
        scratch_shapes=[pltpu.VMEM((t + 2 * POOL_PAD, gi), F32)],
        compiler_params=_params("arbitrary", "arbitrary"),
        name="pool_branch",
    )(z, z, pool_w, pool_scale)


def _merge_kernel(y_ref, bo_ref, lo_ref, gup_ref, gng_ref, gnb_ref, gate_ref, yb_ref, o_ref):
    gd_lo = 2 * LORA_W
    esum = _head_sum_matrix()
    inv_n = 1.0 / HEAD_SIZE
    gd = jax.nn.sigmoid(lo_ref[0, :, gd_lo:gd_lo + GATE_LORA_PAD]).astype(BF16)

    def group_body(gidx, carry):
        cols = [pl.ds(pl.multiple_of((gidx * MERGE_GROUP + j) * LANES, LANES), LANES) for j in range(MERGE_GROUP)]
        y = [y_ref[0, 0, :, cs].astype(F32) + y_ref[1, 0, :, cs].astype(F32) for cs in cols]
        mu = [jnp.dot(x.astype(BF16), esum, preferred_element_type=F32) * inv_n for x in y]
        g = [jnp.dot(gd, gup_ref[:, cs], preferred_element_type=F32) for cs in cols]
        yc = _gmap(lambda x, m: x - m, y, mu)
        var = [jnp.dot((x * x).astype(BF16), esum, preferred_element_type=F32) * inv_n for x in yc]
        for cs, x, vr, gi in zip(cols, yc, var, g):
            yn = x * lax.rsqrt(vr + GN_EPS) * gng_ref[:, cs] + gnb_ref[:, cs]
            yn = yn + bo_ref[0, 0, :, cs].astype(F32) + bo_ref[1, 0, :, cs].astype(F32)
            ya = jax.nn.sigmoid(gate_ref[0, :, cs]) * (yn * gi)
            o_ref[0, :, cs] = (ya + yb_ref[0, :, cs].astype(F32)).astype(o_ref.dtype)
        return carry

    lax.fori_loop(0, o_ref.shape[2] // (LANES * MERGE_GROUP), group_body, 0)


def _merge(y, bo, zs, zp, g_up, ln_g, ln_b, yb, tt=256):
    _, bsz, t, dm = y.shape
    n = zs.shape[-1]
    lora_block = (n - LORA_COLS) // LORA_COLS
    return pl.pallas_call(
        _merge_kernel,
        grid=(bsz, t // tt),
        in_specs=[pl.BlockSpec((2, 1, tt, dm), lambda b, i: (0, b, i, 0)),
                  pl.BlockSpec((2, 1, tt, dm), lambda b, i: (0, b, i, 0)),
                  pl.BlockSpec((1, tt, LORA_COLS), lambda b, i: (b, i, lora_block)),
                  pl.BlockSpec((GATE_LORA_PAD, dm), lambda b, i: (0, 0)),
                  pl.BlockSpec((1, dm), lambda b, i: (0, 0)),
                  pl.BlockSpec((1, dm), lambda b, i: (0, 0)),
                  pl.BlockSpec((1, tt, dm), lambda b, i: (b, i, 0)),
                  pl.BlockSpec((1, tt, dm), lambda b, i: (b, i, 0))],
        out_specs=pl.BlockSpec((1, tt, dm), lambda b, i: (b, i, 0)),
        out_shape=jax.ShapeDtypeStruct((bsz, t, dm), BF16),
        compiler_params=_params("arbitrary", "arbitrary"),
        name="wkv_merge",
    )(y, bo, zs, g_up, ln_g, ln_b, zp, yb)


def _xattn_kernel(q_ref, k_ref, v_ref, o_ref, *, head_dim):
    scale = head_dim ** -0.5
    nt_dims = (((1,), (1,)), ((), ()))
    for h in range(X_HEADS):
        cs = slice(h * head_dim, (h + 1) * head_dim)
        s = lax.dot_general(q_ref[0, :, cs], k_ref[0, :, cs], nt_dims, preferred_element_type=F32) * scale
        m = jnp.max(s, axis=-1, keepdims=True)
        e = jnp.exp(s - m)
        p = e / jnp.sum(e, axis=-1, keepdims=True)
        o_ref[0, :, cs] = jnp.dot(p.astype(BF16), v_ref[0, :, cs], preferred_element_type=F32).astype(o_ref.dtype)


def _xattn(q, k, v, tq=512):
    bsz, t, dm = q.shape
    m = k.shape[1]
    kern = functools.partial(_xattn_kernel, head_dim=dm // X_HEADS)
    return pl.pallas_call(
        kern,
        grid=(bsz, t // tq),
        in_specs=[pl.BlockSpec((1, tq, dm), lambda b, i: (b, i, 0)),
                  pl.BlockSpec((1, m, dm), lambda b, i: (b, 0, 0)),
                  pl.BlockSpec((1, m, dm), lambda b, i: (b, 0, 0))],
        out_specs=pl.BlockSpec((1, tq, dm), lambda b, i: (b, i, 0)),
        out_shape=jax.ShapeDtypeStruct((bsz, t, dm), BF16),
        compiler_params=_params("arbitrary", "arbitrary"),
        name="xattn",
    )(q, k, v)


def _pad_to(x, axis, size):
    pad = [(0, 0)] * x.ndim
    pad[axis] = (0, size - x.shape[axis])
    return jnp.pad(x, pad)


def _gain_tile(g):
    return jnp.broadcast_to(g[:, None], (g.shape[0], LANES))


def _in_proj_operands(p, dm):
    w_in, shift_w = p['w_in'], p['shift_w']
    gate_lora = p['g_up'].shape[0]
    c_rkv = 3 * dm
    c_lora = c_rkv + 2 * LORA_W + gate_lora
    pool_width = p['pool_w'].shape[0] * p['pool_w'].shape[1]
    c_pool = c_lora + pool_width
    wb = w_in.astype(BF16)
    w_shift = jnp.concatenate([wb[:, :c_rkv], _pad_to(wb[:, c_rkv:c_lora], 1, LORA_COLS)], axis=1)
    taps = jnp.concatenate([shift_w[:, :c_rkv], _pad_to(shift_w[:, c_rkv:c_lora], 1, LORA_COLS)], axis=1)
    w_plain = jnp.concatenate([wb[:, c_pool:], wb[:, c_lora:c_pool]], axis=1)
    return w_shift, taps, w_plain


def _trunk(xs, mem, lp, norm_final_g):
    t, dm = xs[0].shape[1:]
    rows = [x.shape[0] * t for x in xs]
    bsz = sum(x.shape[0] for x in xs)
    n_mem = mem.shape[1]
    m_tok = bsz * t
    hs = tuple(x.reshape(-1, dm) for x in xs)
    memf = mem.reshape(bsz * n_mem, dm)
    depth = lp['w_in'].shape[0]
    h = None
    for l in range(depth):
        p = {name: arr[l] for name, arr in lp.items()}
        w_shift, taps, w_plain = _in_proj_operands(p, dm)
        g_up = _pad_to(p['g_up'], 0, GATE_LORA_PAD).astype(BF16)

        if h is None:
            xn = _rmsnorm2(hs[0], hs[1], p['norm_mix_g'], BF16)
            res = hs
        else:
            xn = _rmsnorm(h, p['norm_mix_g'], BF16)
            res = h
        zs = _in_proj(xn, w_shift, taps, bsz, t)
        zp = _matmul(xn, w_plain, F32, PROJ_TM, 2 * PROJ_TN, name="in_plain").reshape(bsz, t, -1)

        w_up = jnp.stack([p['w_up_f'], p['w_up_b']]).astype(BF16)
        a_up = jnp.stack([p['a_up_f'], p['a_up_b']]).astype(BF16)
        w0 = jnp.stack([p['w0_f'], p['w0_b']]).reshape(2, 1, dm)
        a0 = jnp.stack([p['a0_f'], p['a0_b']]).reshape(2, 1, dm)
        lw, icl = _wkv_rates(zs.reshape(m_tok, -1), dm, w_up, a_up, w0, a0)
        y, bo = _wkv_scan(zs, dm, lw.reshape(2, bsz, t, dm), icl.reshape(2, bsz, t, dm),
                          p['k_k'].reshape(1, dm), p['k_a'].reshape(1, dm), p['r_k'].reshape(1, dm))
        yb = _pool_branch(zp, 2 * dm, dm, p['pool_w'].astype(BF16), p['pool_scale'].reshape(1, dm))
        merged = _merge(y, bo, zs, zp, g_up, p['ln_x_g'].reshape(1, dm), p['ln_x_b'].reshape(1, dm), yb)
        h, h_bf, scale = _matmul_stats(merged.reshape(m_tok, dm), p['w_out'].astype(BF16), PROJ_TM, PROJ_TN,
                                       residual=res, name="out_proj")

        mn = _rmsnorm(memf, p['norm_mem_g'], BF16)
        q = _matmul_wres(h_bf, p['xq'], BF16, PROJ_TM, PROJ_TN, scale=scale, gain=_gain_tile(p['norm_x_g']), name="xq")
        kx = _matmul_wres(mn, p['xk'], BF16, PROJ_TM, PROJ_TN, name="xk")
        vx = _matmul_wres(mn, p['xv'], BF16, PROJ_TM, PROJ_TN, name="xv")
        o = _xattn(q.reshape(bsz, t, dm), kx.reshape(bsz, n_mem, dm), vx.reshape(bsz, n_mem, dm))
        h, h_bf, scale = _matmul_stats(o.reshape(m_tok, dm), p['xo'].astype(BF16), PROJ_TM, PROJ_TN,
                                       residual=h, name="xo")

        hidden = p['ffn_w2'].shape[0]
        w13 = (p['norm_ffn_g'][:, None] * p['ffn_w13']).astype(BF16)
        act = _swiglu_up(h_bf, scale, w13, hidden, FFN_TM, FFN_TN)
        w2 = p['ffn_w2'].astype(BF16)
        if l + 1 < depth:
            h = _matmul(act, w2, F32, FFN_DOWN_TM, FFN_DOWN_TN, residual=h, name="ffn_down")

    outs, off = [], 0
    for x, nrow in zip(xs, rows):
        y = _matmul_res_norm(act, w2, h, norm_final_g, FFN_DOWN_TM, FFN_DOWN_TN,
                             off // FFN_DOWN_TM, nrow // FFN_DOWN_TM, name="ffn_down_norm")
        outs.append(y.reshape(x.shape))
        off += nrow
    return tuple(outs)


def kernel(x_prompt, x_sample, mem_prompt, mem_sample, norm_mix_g, w_in, shift_w, w0_f, w_up_f, w0_b, w_up_b, a0_f, a_up_f, a0_b, a_up_b, g_up, k_k, k_a, r_k, ln_x_g, ln_x_b, pool_w, pool_scale, w_out, norm_x_g, norm_mem_g, xq, xk, xv, xo, norm_ffn_g, ffn_w13, ffn_w2, norm_final_g):
    assert x_prompt.shape[1:] == x_sample.shape[1:] and mem_prompt.shape[1:] == mem_sample.shape[1:]
    t, dm = x_prompt.shape[1:]
    hidden = ffn_w2.shape[1]
    assert t % max(WKV_CHUNK, IN_ROW_CHUNK, 256) == 0 and dm % (LANES * WKV_SOLVE_GROUP) == 0
    assert all(x.shape[0] * t % max(PROJ_TM, FFN_TM) == 0 for x in (x_prompt, x_sample))
    assert hidden % FFN_TN == 0 and dm % max(PROJ_TN, FFN_DOWN_TN) == 0
    lp = {
        'norm_mix_g': norm_mix_g, 'w_in': w_in, 'shift_w': shift_w,
        'w0_f': w0_f, 'w_up_f': w_up_f, 'w0_b': w0_b, 'w_up_b': w_up_b,
        'a0_f': a0_f, 'a_up_f': a_up_f, 'a0_b': a0_b, 'a_up_b': a_up_b,
        'g_up': g_up, 'k_k': k_k, 'k_a': k_a, 'r_k': r_k.reshape(r_k.shape[0], -1),
        'ln_x_g': ln_x_g, 'ln_x_b': ln_x_b,
        'pool_w': pool_w, 'pool_scale': pool_scale, 'w_out': w_out,
        'norm_x_g': norm_x_g, 'norm_mem_g': norm_mem_g, 'xq': xq, 'xk': xk, 'xv': xv, 'xo': xo,
        'norm_ffn_g': norm_ffn_g, 'ffn_w13': ffn_w13, 'ffn_w2': ffn_w2,
    }
    mem = jnp.concatenate([mem_prompt, mem_sample], axis=0)
    return _trunk((x_prompt, x_sample), mem, lp, norm_final_g)
```
